```python
import math
import jax, jax.numpy as jnp
from jax import lax
import numpy as np

D_MODEL = 1024
BATCH = 1
SEQ = 16384
DEPTH = 1
DEC_BATCH = 32
DEC_SEQ = 2048
PAST_LEN = 128

D_HYENA = D_MODEL // 2
D_S5 = D_MODEL - D_HYENA
D_MIX = D_HYENA + D_S5
D_IN = 3 * D_HYENA + D_S5
HYENA_ORDER = 2
SHORT_CONV = 3
FILTER_EMB = 33
FILTER_BANDS = (FILTER_EMB - 1) // 2
FILTER_WIDTH = 64
DECAY_TARGET = 1e-2
DECAY_FAST_PCT = 0.3
DECAY_SLOW_PCT = 1.5
DECAY_MAX = math.log(DECAY_TARGET) / DECAY_FAST_PCT
DECAY_MIN = math.log(DECAY_TARGET) / DECAY_SLOW_PCT
S5_GROUP = 16
S5_N_GROUPS = D_S5 // S5_GROUP
S5_STATE = 64
S5_DIRS = 2
LAMBDA_RE_MAX = -1e-4
N_EXPERTS = 32
TOP_K = 4
D_FF = D_MODEL
SWIGLU_LIMIT = 7.0
SWIGLU_ALPHA = 1.702
MOE_BLOCK = 512
RMS_EPS = 1e-6

kernel_name = 'hymba_hyena_s5_moe_encoder'


def rmsnorm(x, g):
    xf = x.astype(jnp.float32)
    y = xf * lax.rsqrt(jnp.mean(xf * xf, axis=-1, keepdims=True) + RMS_EPS)
    return (y * g.astype(jnp.float32)).astype(x.dtype)


def hyena_filter_spectrum(L, w1, b1, w2, b2, w3, b3, freq):
    f32 = jnp.float32
    t = jnp.linspace(0.0, 1.0, L, dtype=f32)[:, None]
    w = 2.0 * math.pi * jnp.arange(L, dtype=f32) / L
    bands = jnp.linspace(1e-4, FILTER_BANDS - 1, FILTER_BANDS, dtype=f32)
    ang = w[:, None] * bands[None, :]
    z = jnp.concatenate([t, jnp.cos(ang), -jnp.sin(ang)], axis=-1)
    freq = freq.astype(f32)
    h = jnp.sin(freq[0] * (z @ w1.astype(f32) + b1.astype(f32)))
    h = jnp.sin(freq[1] * (h @ w2.astype(f32) + b2.astype(f32)))
    h = h @ w3.astype(f32) + b3.astype(f32)
    h = h.reshape(L, 2, HYENA_ORDER, D_HYENA)
    deltas = jnp.abs(jnp.linspace(DECAY_MIN, DECAY_MAX, D_HYENA, dtype=f32))
    h = h * jnp.exp(-t * deltas[None, :])[:, None, None, :]
    h_fwd, h_bwd = h[:, 0], h[:, 1]
    k = jnp.concatenate([h_fwd, jnp.zeros((1, HYENA_ORDER, D_HYENA), f32), h_bwd[1:][::-1]], axis=0)
    k = k / (jnp.sum(jnp.abs(k), axis=0, keepdims=True) + 1e-6)
    return jnp.fft.rfft(k, n=2 * L, axis=0)


def hyena_branch(z, short_w, short_b, k_hat, filt_bias):
    L = z.shape[1]
    zp = jnp.pad(z, ((0, 0), (1, 1), (0, 0)))
    z = zp[:, :-2] * short_w[0] + zp[:, 1:-1] * short_w[1] + zp[:, 2:] * short_w[2] + short_b
    v, x1, x2 = jnp.split(z, 3, axis=-1)
    bias = filt_bias.astype(jnp.float32)
    for o, gate in enumerate((x1, x2)):
        vf = v.astype(jnp.float32)
        conv = jnp.fft.irfft(jnp.fft.rfft(vf, n=2 * L, axis=1) * k_hat[None, :, o], n=2 * L, axis=1)[:, :L]
        v = (gate.astype(jnp.float32) * (conv + vf * bias[o])).astype(z.dtype)
    return v


def _ssm_combine(e_i, e_j):
    a_i, b_i = e_i
    a_j, b_j = e_j
    return a_j * a_i, a_j * b_i + b_j


def s5_direction(u, lam_re, lam_im, log_dt, b_re, b_im, c_re, c_im, reverse):
    f32 = jnp.float32
    lam = jnp.minimum(lam_re.astype(f32), LAMBDA_RE_MAX) + 1j * lam_im.astype(f32)
    dt = jnp.exp(log_dt.astype(f32))[:, None]
    lam_bar = jnp.exp(lam * dt)
    b_bar = ((lam_bar - 1.0) / lam)[:, :, None] * (b_re.astype(f32) + 1j * b_im.astype(f32))
    bu = jnp.einsum('blgh,gph->blgp', u, b_bar)
    a = jnp.broadcast_to(lam_bar, bu.shape)
    _, xs = lax.associative_scan(_ssm_combine, (a, bu), axis=1, reverse=reverse)
    c = c_re.astype(f32) + 1j * c_im.astype(f32)
    return jnp.einsum('blgp,ghp->blgh', xs, c).real


def s5_branch(z, lam_re, lam_im, log_dt, b_re, b_im, c_re, c_im, d_skip, w_glu, b_glu):
    B, L, _ = z.shape
    u = z.astype(jnp.float32).reshape(B, L, S5_N_GROUPS, S5_GROUP)
    y = s5_direction(u, lam_re[0], lam_im[0], log_dt[0], b_re[0], b_im[0], c_re[0], c_im[0], False) \
        + s5_direction(u, lam_re[1], lam_im[1], log_dt[1], b_re[1], b_im[1], c_re[1], c_im[1], True)
    y = (y + u * d_skip.astype(jnp.float32).reshape(S5_N_GROUPS, S5_GROUP)).reshape(B, L, D_S5)
    g = jax.nn.gelu(y)
    return g * jax.nn.sigmoid(g @ w_glu.astype(jnp.float32) + b_glu.astype(jnp.float32))


def moe(h, w_router, b_router, w_gate, b_gate, w_up, b_up, w_down, b_down):
    T, D = h.shape
    logits = h.astype(jnp.float32) @ w_router.astype(jnp.float32) + b_router.astype(jnp.float32)
    top_logit, top_idx = lax.top_k(logits, TOP_K)
    gates = jax.nn.softmax(top_logit, axis=-1)
    n_assign = T * TOP_K
    expert = top_idx.reshape(-1).astype(jnp.int32)
    token = jnp.repeat(jnp.arange(T, dtype=jnp.int32), TOP_K)
    weight = gates.reshape(-1)
    order = jnp.argsort(expert)
    expert_s, token_s, weight_s = expert[order], token[order], weight[order]
    counts = jnp.bincount(expert, length=N_EXPERTS)
    padded = (counts + MOE_BLOCK - 1) // MOE_BLOCK * MOE_BLOCK
    start = jnp.cumsum(counts) - counts
    pend = jnp.cumsum(padded)
    pstart = pend - padded
    dest = pstart[expert_s] + jnp.arange(n_assign, dtype=jnp.int32) - start[expert_s]
    n_blocks = -(-n_assign // MOE_BLOCK) + N_EXPERTS
    n_slots = n_blocks * MOE_BLOCK
    slot_token = jnp.zeros((n_slots,), jnp.int32).at[dest].set(token_s)
    slot_weight = jnp.zeros((n_slots,), jnp.float32).at[dest].set(weight_s)
    block_expert = jnp.minimum(
        jnp.searchsorted(pend, jnp.arange(n_blocks, dtype=jnp.int32) * MOE_BLOCK, side='right'),
        N_EXPERTS - 1)
    xb = h[slot_token].reshape(n_blocks, MOE_BLOCK, D)

    def expert_block(args):
        xe, e = args
        gt = jnp.minimum(xe @ w_gate[e] + b_gate[e], SWIGLU_LIMIT)
        up = jnp.clip(xe @ w_up[e] + b_up[e], -SWIGLU_LIMIT, SWIGLU_LIMIT)
        act = (up + 1.0) * (gt * jax.nn.sigmoid(SWIGLU_ALPHA * gt))
        return act @ w_down[e] + b_down[e]

    yb = lax.map(expert_block, (xb, block_expert)).reshape(n_slots, D)
    return jnp.zeros_like(h).at[slot_token].add(yb * slot_weight[:, None].astype(h.dtype))


def encoder_layer(x, g_mix, w_in, b_in, short_w, short_b, filt_w1, filt_b1, filt_w2, filt_b2,
                  filt_w3, filt_b3, filt_freq, filt_bias, s5_lam_re, s5_lam_im, s5_log_dt,
                  s5_b_re, s5_b_im, s5_c_re, s5_c_im, s5_d, w_glu, b_glu, g_hyena, g_s5, w_out,
                  g_ffn, w_router, b_router, w_gate, b_gate, w_up, b_up, w_down, b_down):
    B, L, D = x.shape
    h = rmsnorm(x, g_mix)
    proj = h @ w_in + b_in
    k_hat = hyena_filter_spectrum(L, filt_w1, filt_b1, filt_w2, filt_b2, filt_w3, filt_b3, filt_freq)
    y_a = hyena_branch(proj[..., :3 * D_HYENA], short_w, short_b, k_hat, filt_bias)
    y_b = s5_branch(proj[..., 3 * D_HYENA:], s5_lam_re, s5_lam_im, s5_log_dt, s5_b_re, s5_b_im,
                    s5_c_re, s5_c_im, s5_d, w_glu, b_glu).astype(x.dtype)
    mixed = jnp.concatenate([rmsnorm(y_a, g_hyena), rmsnorm(y_b, g_s5)], axis=-1) @ w_out
    x = x + mixed
    hf = rmsnorm(x, g_ffn).reshape(B * L, D)
    x = x + moe(hf, w_router, b_router, w_gate, b_gate, w_up, b_up, w_down, b_down).reshape(B, L, D)
    return x


def setup_inputs(seed: int = 0) -> dict:
    key = jax.random.key(seed)
    ks = jax.random.split(key, 40)
    f32 = jnp.float32

    def nrm(k, shape, scale):
        return scale * jax.random.normal(k, shape, f32)

    G, P, H = S5_N_GROUPS, S5_STATE, S5_GROUP
    n = jnp.arange(P, dtype=f32)
    lam_re = -0.5 + nrm(ks[15], (DEPTH, S5_DIRS, G, P), 0.01)
    lam_im = math.pi * n + nrm(ks[16], (DEPTH, S5_DIRS, G, P), 0.01)
    log_dt = jax.random.uniform(ks[17], (DEPTH, S5_DIRS, G), f32, math.log(1e-3), math.log(1e-1))
    return {
        'x_prompt': nrm(ks[0], (BATCH, SEQ, D_MODEL), 1.0),
        'x_sample': nrm(ks[1], (DEC_BATCH, DEC_SEQ, D_MODEL), 1.0),
        'g_mix': 1.0 + nrm(ks[2], (DEPTH, D_MODEL), 0.02),
        'w_in': nrm(ks[3], (DEPTH, D_MODEL, D_IN), D_MODEL ** -0.5),
        'b_in': nrm(ks[4], (DEPTH, D_IN), 0.02),
        'short_w': nrm(ks[5], (DEPTH, SHORT_CONV, 3 * D_HYENA), SHORT_CONV ** -0.5),
        'short_b': nrm(ks[6], (DEPTH, 3 * D_HYENA), 0.02),
        'filt_w1': nrm(ks[7], (DEPTH, FILTER_EMB, FILTER_WIDTH), FILTER_EMB ** -0.5),
        'filt_b1': nrm(ks[8], (DEPTH, FILTER_WIDTH), 0.1),
        'filt_w2': nrm(ks[9], (DEPTH, FILTER_WIDTH, FILTER_WIDTH), FILTER_WIDTH ** -0.5),
        'filt_b2': nrm(ks[10], (DEPTH, FILTER_WIDTH), 0.1),
        'filt_w3': nrm(ks[11], (DEPTH, FILTER_WIDTH, 2 * HYENA_ORDER * D_HYENA), FILTER_WIDTH ** -0.5),
        'filt_b3': nrm(ks[12], (DEPTH, 2 * HYENA_ORDER * D_HYENA), 0.02),
        'filt_freq': 1.0 + nrm(ks[13], (DEPTH, 2, FILTER_WIDTH), 0.05),
        'filt_bias': nrm(ks[14], (DEPTH, HYENA_ORDER, D_HYENA), 1.0),
        's5_lam_re': lam_re,
        's5_lam_im': lam_im,
        's5_log_dt': log_dt,
        's5_b_re': nrm(ks[18], (DEPTH, S5_DIRS, G, P, H), (2.0 * H) ** -0.5),
        's5_b_im': nrm(ks[19], (DEPTH, S5_DIRS, G, P, H), (2.0 * H) ** -0.5),
        's5_c_re': nrm(ks[20], (DEPTH, S5_DIRS, G, H, P), (2.0 * P) ** -0.5),
        's5_c_im': nrm(ks[21], (DEPTH, S5_DIRS, G, H, P), (2.0 * P) ** -0.5),
        's5_d': nrm(ks[22], (DEPTH, D_S5), 1.0),
        'w_glu': nrm(ks[23], (DEPTH, D_S5, D_S5), D_S5 ** -0.5),
        'b_glu': nrm(ks[24], (DEPTH, D_S5), 0.02),
        'g_hyena': 1.0 + nrm(ks[25], (DEPTH, D_HYENA), 0.02),
        'g_s5': 1.0 + nrm(ks[26], (DEPTH, D_S5), 0.02),
        'w_out': nrm(ks[27], (DEPTH, D_MIX, D_MODEL), D_MIX ** -0.5),
        'g_ffn': 1.0 + nrm(ks[28], (DEPTH, D_MODEL), 0.02),
        'w_router': nrm(ks[29], (DEPTH, D_MODEL, N_EXPERTS), D_MODEL ** -0.5),
        'b_router': nrm(ks[30], (DEPTH, N_EXPERTS), 0.01),
        'w_gate': nrm(ks[31], (DEPTH, N_EXPERTS, D_MODEL, D_FF), D_MODEL ** -0.5),
        'b_gate': nrm(ks[32], (DEPTH, N_EXPERTS, D_FF), 0.02),
        'w_up': nrm(ks[33], (DEPTH, N_EXPERTS, D_MODEL, D_FF), D_MODEL ** -0.5),
        'b_up': nrm(ks[34], (DEPTH, N_EXPERTS, D_FF), 0.02),
        'w_down': nrm(ks[35], (DEPTH, N_EXPERTS, D_FF, D_MODEL), D_FF ** -0.5),
        'b_down': nrm(ks[36], (DEPTH, N_EXPERTS, D_MODEL), 0.02),
        'g_final': 1.0 + nrm(ks[37], (D_MODEL,), 0.02),
    }


def reference(x_prompt, x_sample, g_mix, w_in, b_in, short_w, short_b, filt_w1, filt_b1, filt_w2,
              filt_b2, filt_w3, filt_b3, filt_freq, filt_bias, s5_lam_re, s5_lam_im, s5_log_dt,
              s5_b_re, s5_b_im, s5_c_re, s5_c_im, s5_d, w_glu, b_glu, g_hyena, g_s5, w_out, g_ffn,
              w_router, b_router, w_gate, b_gate, w_up, b_up, w_down, b_down, g_final):
    def trunk(x):
        for i in range(DEPTH):
            x = encoder_layer(x, g_mix[i], w_in[i], b_in[i], short_w[i], short_b[i], filt_w1[i],
                              filt_b1[i], filt_w2[i], filt_b2[i], filt_w3[i], filt_b3[i],
                              filt_freq[i], filt_bias[i], s5_lam_re[i], s5_lam_im[i],
                              s5_log_dt[i], s5_b_re[i], s5_b_im[i], s5_c_re[i], s5_c_im[i],
                              s5_d[i], w_glu[i], b_glu[i], g_hyena[i], g_s5[i], w_out[i],
                              g_ffn[i], w_router[i], b_router[i], w_gate[i], b_gate[i],
                              w_up[i], b_up[i], w_down[i], b_down[i])
        return rmsnorm(x, g_final)

    y_prompt = trunk(x_prompt)
    y_sample = trunk(x_sample)
    return (y_prompt, y_sample)
```

```python
import functools
import math

import numpy as np
import jax
import jax.numpy as jnp
from jax import lax
from jax.experimental import pallas as pl
from jax.experimental.pallas import tpu as pltpu

F32 = jnp.float32
BF16 = jnp.bfloat16
I32 = jnp.int32
HIGHEST = lax.Precision.HIGHEST

D_MODEL = 1024
C_HY = 512
C_S5 = 512
D_IN = 3 * C_HY + C_S5
FILTER_BANDS = 16
FILTER_WIDTH = 64
Z_PAD = 128
DECAY_MAX = math.log(1e-2) / 0.3
DECAY_MIN = math.log(1e-2) / 1.5
S5_G, S5_H, S5_P = 32, 16, 64
S5_CHUNK = 16
S5_UW = S5_CHUNK * S5_H
S5_SW = 2 * S5_P
LAMBDA_RE_MAX = -1e-4
N_EXPERTS = 32
TOP_K = 4
SWIGLU_LIMIT = 7.0
SWIGLU_ALPHA = 1.702
MOE_BLOCK = 512
RMS_EPS = 1e-6
LANES = 128
SUBLANES = 8
VMEM_LIMIT_BYTES = 56 * 1024 * 1024


def _cparams(*sem):
    return pltpu.CompilerParams(dimension_semantics=sem, vmem_limit_bytes=VMEM_LIMIT_BYTES)


def _rms(x, g):
    return x * lax.rsqrt(jnp.mean(x * x, axis=-1, keepdims=True) + RMS_EPS) * g


def _sigmoid(x):
    return 1.0 / (1.0 + jnp.exp(-x))


def _dot(a, b):
    return jnp.dot(a, b, preferred_element_type=F32)


def _inproj_kernel(x_ref, xp_ref, xn_ref, g_ref, w_ref, b_ref, sw_ref, sb_ref,
                   v_ref, x1_ref, x2_ref, u_ref, *, tm, seq):
    i = pl.program_id(0)
    g = g_ref[...]
    h = _rms(x_ref[...], g).astype(BF16)
    proj = _dot(h, w_ref[...]) + b_ref[...]
    u_ref[...] = proj[:, 3 * C_HY:]
    z = proj[:, :3 * C_HY]
    halo = jnp.concatenate([xp_ref[...], xn_ref[...]], axis=0)
    zh = _dot(_rms(halo, g).astype(BF16), w_ref[:, :3 * C_HY]) + b_ref[:, :3 * C_HY]
    row0 = i * tm
    has_prev = lax.rem(row0, seq) != 0
    has_next = lax.rem(row0 + tm, seq) != 0
    zp = jnp.where(has_prev, zh[SUBLANES - 1:SUBLANES], 0.0)
    zn = jnp.where(has_next, zh[SUBLANES:SUBLANES + 1], 0.0)
    rid = lax.broadcasted_iota(I32, (tm, 1), 0)
    zm1 = jnp.where(rid == 0, zp, pltpu.roll(z, 1, 0))
    zp1 = jnp.where(rid == tm - 1, zn, pltpu.roll(z, tm - 1, 0))
    sw = sw_ref[...]
    o = zm1 * sw[0:1] + z * sw[1:2] + zp1 * sw[2:3] + sb_ref[...]
    v_ref[...] = o[:, :C_HY]
    x1_ref[...] = o[:, C_HY:2 * C_HY]
    x2_ref[...] = o[:, 2 * C_HY:]


def _inproj(x2d, seq, g_mix, w_in_bf, b_in, short_w, short_b, tm=512):
    T = x2d.shape[0]
    nb8 = T // SUBLANES
    tb = tm // SUBLANES
    out = jax.ShapeDtypeStruct((T, C_HY), F32)
    const = lambda i: (0, 0)
    return pl.pallas_call(
        functools.partial(_inproj_kernel, tm=tm, seq=seq),
        grid=(T // tm,),
        in_specs=[
            pl.BlockSpec((tm, D_MODEL), lambda i: (i, 0)),
            pl.BlockSpec((SUBLANES, D_MODEL), lambda i: (jnp.maximum(i * tb - 1, 0), 0)),
            pl.BlockSpec((SUBLANES, D_MODEL), lambda i: (jnp.minimum((i + 1) * tb, nb8 - 1), 0)),
            pl.BlockSpec((1, D_MODEL), const),
            pl.BlockSpec((D_MODEL, D_IN), const),
            pl.BlockSpec((1, D_IN), const),
            pl.BlockSpec((3, 3 * C_HY), const),
            pl.BlockSpec((1, 3 * C_HY), const),
        ],
        out_specs=[pl.BlockSpec((tm, C_HY), lambda i: (i, 0))] * 4,
        out_shape=[out] * 4,
        compiler_params=_cparams("arbitrary"),
        name="inproj",
    )(x2d, x2d, x2d, g_mix, w_in_bf, b_in, short_w, short_b)


def _filter_kernel(z_ref, w1_ref, b1_ref, w2_ref, b2_ref, w3_ref, b3_ref, fr_ref, dl_ref,
                   k_ref, s_ref, *, tr, seq):
    i = pl.program_id(0)
    z = z_ref[...]
    fr = fr_ref[...]
    h = jnp.sin(fr[0:1] * (jnp.dot(z, w1_ref[...], precision=HIGHEST,
                                   preferred_element_type=F32) + b1_ref[...]))
    h = jnp.sin(fr[1:2] * (jnp.dot(h, w2_ref[...], precision=HIGHEST,
                                   preferred_element_type=F32) + b2_ref[...]))
    h = jnp.dot(h, w3_ref[...], precision=HIGHEST, preferred_element_type=F32) + b3_ref[...]
    h = h * jnp.exp(-z[:, 0:1] * dl_ref[...])
    rid = i * tr + lax.broadcasted_iota(I32, (tr, 1), 0)
    h = jnp.where(rid == seq, 0.0, h)
    k_ref[...] = h

    @pl.when(i == 0)
    def _():
        s_ref[...] = jnp.zeros_like(s_ref)

    s_ref[...] += jnp.sum(jnp.abs(h), axis=0, keepdims=True)


def _filter_taps(seq, w1, b1, w2, b2, w3, b3, freq, tr=512):
    L = seq
    N = 2 * L
    t = jnp.linspace(0.0, 1.0, L, dtype=F32)[:, None]
    w = 2.0 * math.pi * jnp.arange(L, dtype=F32) / L
    bands = jnp.linspace(1e-4, FILTER_BANDS - 1, FILTER_BANDS, dtype=F32)
    ang = w[:, None] * bands[None, :]
    z = jnp.concatenate([t, jnp.cos(ang), -jnp.sin(ang)], axis=-1)
    z = jnp.pad(z, ((0, 0), (0, Z_PAD - z.shape[1])))
    n = jnp.arange(N)
    zc = z[jnp.where(n < L, n, jnp.where(n == L, 0, N - n))]
    w1p = jnp.pad(w1, ((0, Z_PAD - w1.shape[0]), (0, 0)))
    w3d = w3.reshape(FILTER_WIDTH, 2, 2 * C_HY).transpose(1, 0, 2)
    b3d = b3.reshape(2, 1, 2 * C_HY)
    deltas = jnp.abs(jnp.linspace(DECAY_MIN, DECAY_MAX, C_HY, dtype=F32))
    dl = jnp.tile(deltas, 2)[None, :]
    half = (N // tr) // 2
    const = lambda i: (0, 0)
    taps, sums = pl.pallas_call(
        functools.partial(_filter_kernel, tr=tr, seq=L),
        grid=(N // tr,),
        in_specs=[
            pl.BlockSpec((tr, Z_PAD), lambda i: (i, 0)),
            pl.BlockSpec((Z_PAD, FILTER_WIDTH), const),
            pl.BlockSpec((1, FILTER_WIDTH), const),
            pl.BlockSpec((FILTER_WIDTH, FILTER_WIDTH), const),
            pl.BlockSpec((1, FILTER_WIDTH), const),
            pl.BlockSpec((None, FILTER_WIDTH, 2 * C_HY), lambda i: (i // half, 0, 0)),
            pl.BlockSpec((None, 1, 2 * C_HY), lambda i: (i // half, 0, 0)),
            pl.BlockSpec((2, FILTER_WIDTH), const),
            pl.BlockSpec((1, 2 * C_HY), const),
        ],
        out_specs=[pl.BlockSpec((tr, 2 * C_HY), lambda i: (i, 0)),
                   pl.BlockSpec((1, 2 * C_HY), const)],
        out_shape=[jax.ShapeDtypeStruct((N, 2 * C_HY), F32),
                   jax.ShapeDtypeStruct((1, 2 * C_HY), F32)],
        compiler_params=_cparams("arbitrary"),
        name="hyena_filter",
    )(zc, w1p, b1[None, :], w2, b2[None, :], w3d, b3d, freq, dl)
    return taps, sums


def _split_n(N):
    n1 = {32768: 128, 4096: 64}.get(N)
    if n1 is None:
        n1 = 1 << (int(math.log2(N)) // 2)
    return n1, N // n1


def _dft_constants(N1, N2, seq):
    N = N1 * N2
    k1 = np.arange(N1)[:, None]
    n1 = np.arange(N1)[None, :]
    ang = 2.0 * np.pi * ((k1 * n1) % N1) / N1
    fa_full = np.concatenate([np.cos(ang), -np.sin(ang)], axis=0)
    fa_half = fa_full[:, :N1 // 2]
    fa_inv = np.concatenate([np.cos(ang), -np.sin(ang)], axis=1)[:N1 // 2] / N
    k2 = np.arange(N2)[:, None]
    n2 = np.arange(N2)[None, :]
    a2 = 2.0 * np.pi * ((k2 * n2) % N2) / N2
    cr, ci = np.cos(a2), -np.sin(a2)
    f2 = np.block([[cr, -ci], [ci, cr]])
    g2 = np.block([[cr, ci], [-ci, cr]])
    to = lambda a: jnp.asarray(a.astype(np.float32)).astype(BF16)
    kk = lax.broadcasted_iota(I32, (N1, N2, LANES), 0)
    nn = lax.broadcasted_iota(I32, (N1, N2, LANES), 1)
    ta = (2.0 * math.pi / N) * lax.rem(kk * nn, N).astype(F32)
    tw = jnp.stack([jnp.cos(ta), -jnp.sin(ta)], axis=0)
    return dict(fa_full=to(fa_full), fa_half=to(fa_half), fa_inv=to(fa_inv), f2=to(f2), g2=to(g2),
                tw=tw)


def _dft_a_kernel(f_ref, x_ref, o_ref):
    o_ref[...] = _dot(f_ref[...], x_ref[...].astype(BF16))


def _dft_a(f, x, cb=4096):
    Bt, K, cols = x.shape
    R = f.shape[0]
    return pl.pallas_call(
        _dft_a_kernel,
        grid=(Bt, cols // cb),
        in_specs=[pl.BlockSpec((R, K), lambda b, j: (0, 0)),
                  pl.BlockSpec((None, K, cb), lambda b, j: (b, 0, j))],
        out_specs=pl.BlockSpec((None, R, cb), lambda b, j: (b, 0, j)),
        out_shape=jax.ShapeDtypeStruct((Bt, R, cols), F32),
        compiler_params=_cparams("arbitrary", "arbitrary"),
        name="dft_a",
    )(f, x)


def _dft_a_inv_kernel(f_ref, z_ref, gate_ref, v_ref, bias_ref, o_ref):
    y = _dot(f_ref[...], z_ref[...].astype(BF16))
    o_ref[...] = gate_ref[...] * (y + v_ref[...] * bias_ref[...])


def _dft_a_inv(f, z, gate, v, bias_row, cb=4096):
    Bt, R2, cols = z.shape
    K = f.shape[0]
    blk = pl.BlockSpec((None, K, cb), lambda b, j: (b, 0, j))
    return pl.pallas_call(
        _dft_a_inv_kernel,
        grid=(Bt, cols // cb),
        in_specs=[pl.BlockSpec((K, R2), lambda b, j: (0, 0)),
                  pl.BlockSpec((None, R2, cb), lambda b, j: (b, 0, j)),
                  blk, blk,
                  pl.BlockSpec((1, cb), lambda b, j: (0, 0))],
        out_specs=blk,
        out_shape=jax.ShapeDtypeStruct((Bt, K, cols), F32),
        compiler_params=_cparams("arbitrary", "arbitrary"),
        name="dft_a_inv",
    )(f, z, gate, v, bias_row)


def _dft_c_kernel(a_ref, k_ref, tw_ref, f_ref, g_ref, o_ref, *, bb, n2):
    rep = C_HY // LANES
    twr = jnp.tile(tw_ref[0], (1, rep))
    twi = jnp.tile(tw_ref[1], (1, rep))
    kr = k_ref[0]
    ki = k_ref[1]
    for j in range(bb):
        ar = a_ref[j, 0]
        ai = a_ref[j, 1]
        pr = ar * twr - ai * twi
        pi = ar * twi + ai * twr
        x = _dot(f_ref[...], jnp.concatenate([pr, pi], axis=0).astype(BF16))
        xr = x[:n2]
        xi = x[n2:]
        yr = xr * kr - xi * ki
        yi = xr * ki + xi * kr
        zz = _dot(g_ref[...], jnp.concatenate([yr, yi], axis=0).astype(BF16))
        zr = zz[:n2]
        zi = zz[n2:]
        o_ref[j, 0] = zr * twr + zi * twi
        o_ref[j, 1] = zi * twr - zr * twi


def _dft_c(a5, khat, order, tw, f2, g2, bb):
    Bt, _, N1, N2, _ = a5.shape
    return pl.pallas_call(
        functools.partial(_dft_c_kernel, bb=bb, n2=N2),
        grid=(N1, Bt // bb),
        in_specs=[pl.BlockSpec((bb, 2, None, N2, C_HY), lambda k, b: (b, 0, k, 0, 0)),
                  pl.BlockSpec((2, None, N2, C_HY), lambda k, b: (0, k, 0, order)),
                  pl.BlockSpec((2, None, N2, LANES), lambda k, b: (0, k, 0, 0)),
                  pl.BlockSpec((2 * N2, 2 * N2), lambda k, b: (0, 0)),
                  pl.BlockSpec((2 * N2, 2 * N2), lambda k, b: (0, 0))],
        out_specs=pl.BlockSpec((bb, 2, None, N2, C_HY), lambda k, b: (b, 0, k, 0, 0)),
        out_shape=jax.ShapeDtypeStruct(a5.shape, F32),
        compiler_params=_cparams("arbitrary", "arbitrary"),
        name="dft_c",
    )(a5, khat, tw, f2, g2)


def _dft_c_filter_kernel(a_ref, tw_ref, f_ref, s_ref, o_ref, *, n2):
    rep = 2 * C_HY // LANES
    twr = jnp.tile(tw_ref[0], (1, rep))
    twi = jnp.tile(tw_ref[1], (1, rep))
    ar = a_ref[0]
    ai = a_ref[1]
    pr = ar * twr - ai * twi
    pi = ar * twi + ai * twr
    x = _dot(f_ref[...], jnp.concatenate([pr, pi], axis=0).astype(BF16))
    sc = 1.0 / (s_ref[...] + 1e-6)
    o_ref[0] = x[:n2] * sc
    o_ref[1] = x[n2:] * sc


def _dft_c_filter(a4, tw, f2, sums):
    _, N1, N2, W = a4.shape
    return pl.pallas_call(
        functools.partial(_dft_c_filter_kernel, n2=N2),
        grid=(N1,),
        in_specs=[pl.BlockSpec((2, None, N2, W), lambda k: (0, k, 0, 0)),
                  pl.BlockSpec((2, None, N2, LANES), lambda k: (0, k, 0, 0)),
                  pl.BlockSpec((2 * N2, 2 * N2), lambda k: (0, 0)),
                  pl.BlockSpec((1, W), lambda k: (0, 0))],
        out_specs=pl.BlockSpec((2, None, N2, W), lambda k: (0, k, 0, 0)),
        out_shape=jax.ShapeDtypeStruct(a4.shape, F32),
        compiler_params=_cparams("arbitrary"),
        name="dft_c_filter",
    )(a4, tw, f2, sums)


def _hyena(v, x1, x2, B, seq, filt, filt_bias, bb):
    L = seq
    N = 2 * L
    N1, N2 = _split_n(N)
    cst = _dft_constants(N1, N2, L)
    taps, sums = _filter_taps(L, *filt)
    W = 2 * C_HY
    ka = _dft_a(cst["fa_full"], taps.reshape(1, N1, N2 * W))
    khat = _dft_c_filter(ka.reshape(2, N1, N2, W), cst["tw"], cst["f2"], sums)
    cols = N2 * C_HY
    cb = min(4096, cols)
    view = lambda a: a.reshape(B, N1 // 2, cols)
    cur = view(v)
    for o, gate in enumerate((x1, x2)):
        a = _dft_a(cst["fa_half"], cur, cb)
        z = _dft_c(a.reshape(B, 2, N1, N2, C_HY), khat, o, cst["tw"], cst["f2"], cst["g2"], bb)
        bias_row = jnp.tile(filt_bias[o], cb // C_HY)[None, :]
        cur = _dft_a_inv(cst["fa_inv"], z.reshape(B, 2 * N1, cols), view(gate), cur, bias_row, cb)
    return cur.reshape(B * L, C_HY)


def _s5_operators(lam_re, lam_im, log_dt, b_re, b_im, c_re, c_im, d_skip):
    Tc, G, H, P = S5_CHUNK, S5_G, S5_H, S5_P
    lam = jnp.minimum(lam_re.astype(F32), LAMBDA_RE_MAX) + 1j * lam_im.astype(F32)
    dt = jnp.exp(log_dt.astype(F32))[..., None]
    lam_dt = lam * dt
    lam_bar = jnp.exp(lam_dt)
    b_bar = ((lam_bar - 1.0) / lam)[..., None] * (b_re.astype(F32) + 1j * b_im.astype(F32))
    c = c_re.astype(F32) + 1j * c_im.astype(F32)
    tau = jnp.arange(Tc + 1, dtype=F32)
    pw = jnp.exp(lam_dt[None] * tau[:, None, None, None])
    kk = jnp.einsum('dghp,tdgp,dgpk->dtghk', c, pw[:Tc], b_bar).real
    i = jnp.arange(Tc)
    lag = i[:, None] - i[None, :]
    kf = jnp.where((lag >= 0)[:, :, None, None, None], kk[0][jnp.clip(lag, 0, Tc - 1)], 0.0)
    kb = jnp.where((lag <= 0)[:, :, None, None, None], kk[1][jnp.clip(-lag, 0, Tc - 1)], 0.0)
    m = kf + kb
    eye = (lag == 0)[:, :, None, None, None] * jnp.eye(H, dtype=F32)[None, None, None]
    m = m + eye * d_skip.astype(F32).reshape(G, H)[None, None, :, :, None]
    w_intra = m.transpose(2, 1, 4, 0, 3).reshape(G, Tc * H, Tc * H)
    clf = c[0][None] * pw[1:Tc + 1, 0][:, :, None, :]
    clb = c[1][None] * pw[Tc - i, 1][:, :, None, :]
    st2y = lambda cl: jnp.concatenate([cl.real, -cl.imag], axis=-1).transpose(1, 3, 0, 2) \
        .reshape(G, 2 * P, Tc * H)
    w_xf, w_xb = st2y(clf), st2y(clb)
    sf = pw[Tc - 1 - i, 0][:, :, :, None] * b_bar[0][None]
    sb = pw[i, 1][:, :, :, None] * b_bar[1][None]
    u2s = lambda s: jnp.concatenate([s.real, s.imag], axis=2).transpose(1, 0, 3, 2) \
        .reshape(G, Tc * H, 2 * P)
    w_state = jnp.concatenate([u2s(sf), u2s(sb)], axis=-1)
    lam16 = pw[Tc]
    sc_a = jnp.concatenate([lam16.real, lam16.real], axis=-1).reshape(2, G * 2 * P)
    sc_b = jnp.concatenate([-lam16.imag, lam16.imag], axis=-1).reshape(2, G * 2 * P)
    return dict(w_intra=w_intra.astype(BF16), w_xf=w_xf.astype(BF16), w_xb=w_xb.astype(BF16),
                w_state=w_state.astype(BF16), sc_a=sc_a, sc_b=sc_b)


def _s5_state_kernel(u_ref, w_ref, sf_ref, sb_ref, *, gb):
    for g in range(gb):
        r = _dot(u_ref[:, g * S5_UW:(g + 1) * S5_UW], w_ref[g])
        sf_ref[:, g * S5_SW:(g + 1) * S5_SW] = r[:, :S5_SW]
        sb_ref[:, g * S5_SW:(g + 1) * S5_SW] = r[:, S5_SW:]


def _s5_state(u2, w_state, tr, gb=4):
    R = u2.shape[0]
    out = jax.ShapeDtypeStruct((R, S5_G * S5_SW), F32)
    return pl.pallas_call(
        functools.partial(_s5_state_kernel, gb=gb),
        grid=(S5_G // gb, R // tr),
        in_specs=[pl.BlockSpec((tr, gb * S5_UW), lambda g, r: (r, g)),
                  pl.BlockSpec((gb, S5_UW, 2 * S5_SW), lambda g, r: (g, 0, 0))],
        out_specs=[pl.BlockSpec((tr, gb * S5_SW), lambda g, r: (r, g))] * 2,
        out_shape=[out, out],
        compiler_params=_cparams("arbitrary", "arbitrary"),
        name="s5_state",
    )(u2, w_state)


def _s5_scan_kernel(sf_ref, sb_ref, a_ref, b_ref, xf_ref, xb_ref, *, nch, rows):
    lb = sf_ref.shape[1]
    lane = lax.broadcasted_iota(I32, (rows, lb), 1)
    first = (lane & S5_P) == 0

    def swap(x):
        return jnp.where(first, pltpu.roll(x, lb - S5_P, 1), pltpu.roll(x, S5_P, 1))

    af, ab = a_ref[0], a_ref[1]
    bf, bk = b_ref[0], b_ref[1]

    def body(c, carry):
        xf, xb = carry
        rf = pl.ds(pl.multiple_of(c * rows, rows), rows)
        xf_ref[rf, :] = xf
        xf = af * xf + bf * swap(xf) + sf_ref[rf, :]
        rb = pl.ds(pl.multiple_of((nch - 1 - c) * rows, rows), rows)
        xb_ref[rb, :] = xb
        xb = ab * xb + bk * swap(xb) + sb_ref[rb, :]
        return xf, xb

    zero = jnp.zeros((rows, lb), F32)
    lax.fori_loop(0, nch, body, (zero, zero))


def _s5_scan(sf, sb, sc_a, sc_b, nch, rows, lb):
    R, lanes = sf.shape
    crow = sc_a.shape[1]
    blk = pl.BlockSpec((R, lb), lambda j: (0, j))
    cblk = pl.BlockSpec((2, crow, lb), lambda j: (0, 0, j))
    out = jax.ShapeDtypeStruct((R, lanes), F32)
    return pl.pallas_call(
        functools.partial(_s5_scan_kernel, nch=nch, rows=rows),
        grid=(lanes // lb,),
        in_specs=[blk, blk, cblk, cblk],
        out_specs=[blk, blk],
        out_shape=[out, out],
        compiler_params=_cparams("arbitrary"),
        name="s5_scan",
    )(sf, sb, sc_a, sc_b)


def _s5_out_kernel(u_ref, xf_ref, xb_ref, wm_ref, wf_ref, wb_ref, y_ref, *, gb):
    for g in range(gb):
        us = slice(g * S5_UW, (g + 1) * S5_UW)
        ss = slice(g * S5_SW, (g + 1) * S5_SW)
        acc = _dot(u_ref[:, us], wm_ref[g])
        acc += _dot(xf_ref[:, ss].astype(BF16), wf_ref[g])
        acc += _dot(xb_ref[:, ss].astype(BF16), wb_ref[g])
        y_ref[:, us] = acc


def _s5_out(u2, xf, xb, ops, tr, gb=4):
    R = u2.shape[0]
    return pl.pallas_call(
        functools.partial(_s5_out_kernel, gb=gb),
        grid=(S5_G // gb, R // tr),
        in_specs=[pl.BlockSpec((tr, gb * S5_UW), lambda g, r: (r, g)),
                  pl.BlockSpec((tr, gb * S5_SW), lambda g, r: (r, g)),
                  pl.BlockSpec((tr, gb * S5_SW), lambda g, r: (r, g)),
                  pl.BlockSpec((gb, S5_UW, S5_UW), lambda g, r: (g, 0, 0)),
                  pl.BlockSpec((gb, S5_SW, S5_UW), lambda g, r: (g, 0, 0)),
                  pl.BlockSpec((gb, S5_SW, S5_UW), lambda g, r: (g, 0, 0))],
        out_specs=pl.BlockSpec((tr, gb * S5_UW), lambda g, r: (r, g)),
        out_shape=jax.ShapeDtypeStruct((R, S5_G * S5_UW), F32),
        compiler_params=_cparams("arbitrary", "arbitrary"),
        name="s5_out",
    )(u2, xf, xb, ops["w_intra"], ops["w_xf"], ops["w_xb"])


def _s5(u, B, seq, ops):
    Tc, G, H = S5_CHUNK, S5_G, S5_H
    nch = seq // Tc
    R = nch * B
    u2 = u.astype(BF16).reshape(B, nch, Tc, G, H).transpose(1, 0, 3, 2, 4).reshape(R, G * S5_UW)
    tr = min(512, R)
    sf, sb = _s5_state(u2, ops["w_state"], tr)
    lanes = G * S5_SW
    if B >= SUBLANES:
        rows, lb = B, 256
        xf, xb = _s5_scan(sf, sb, ops["sc_a"][:, None, :], ops["sc_b"][:, None, :], nch, rows, lb)
    else:
        assert B == 1
        rows, lb = SUBLANES, LANES
        fold = lambda a: a.reshape(nch * rows, lanes // rows)
        cf = lambda a: a.reshape(2, rows, lanes // rows)
        xf, xb = _s5_scan(fold(sf), fold(sb), cf(ops["sc_a"]), cf(ops["sc_b"]), nch, rows, lb)
        xf, xb = xf.reshape(R, lanes), xb.reshape(R, lanes)
    y2 = _s5_out(u2, xf, xb, ops, tr)
    return y2.reshape(nch, B, G, Tc, H).transpose(1, 0, 3, 2, 4).reshape(B * seq, C_S5)


def _mix_kernel(x_ref, ya_ref, yb_ref, wglu_ref, bglu_ref, gh_ref, gs_ref, wout_ref, gffn_ref,
                wr_ref, br_ref, x1_ref, hf_ref, route_ref, *, tm):
    g = jax.nn.gelu(yb_ref[...])
    yb = g * _sigmoid(_dot(g.astype(BF16), wglu_ref[...]) + bglu_ref[...])
    na = _rms(ya_ref[...], gh_ref[...]).astype(BF16)
    nb = _rms(yb, gs_ref[...]).astype(BF16)
    mixed = _dot(na, wout_ref[:C_HY, :]) + _dot(nb, wout_ref[C_HY:, :])
    x1 = x_ref[...] + mixed
    x1_ref[...] = x1
    hf = _rms(x1, gffn_ref[...])
    hf_ref[...] = hf
    logits = jnp.dot(hf, wr_ref[...], precision=HIGHEST, preferred_element_type=F32) + br_ref[...]
    lane = lax.broadcasted_iota(I32, (tm, LANES), 1)
    neg = jnp.float32(-jnp.inf)
    l = jnp.where(lane < N_EXPERTS, logits, neg)
    vals, idxs = [], []
    for _ in range(TOP_K):
        m = jnp.max(l, axis=-1, keepdims=True)
        idx = jnp.min(jnp.where(l == m, lane, LANES), axis=-1, keepdims=True)
        vals.append(m)
        idxs.append(idx)
        l = jnp.where(lane == idx, neg, l)
    es = [jnp.exp(v - vals[0]) for v in vals]
    den = es[0] + es[1] + es[2] + es[3]
    route = jnp.zeros((tm, LANES), F32)
    for k in range(TOP_K):
        route = jnp.where(lane == k, es[k] / den, route)
        route = jnp.where(lane == TOP_K + k, idxs[k].astype(F32), route)
    route_ref[...] = route


def _mix(x2d, ya, ybp, w_glu_bf, b_glu, g_hyena, g_s5, w_out_bf, g_ffn, w_router_p, b_router_p, tm=256):
    T = x2d.shape[0]
    const = lambda i: (0, 0)
    row = lambda w: pl.BlockSpec((tm, w), lambda i: (i, 0))
    return pl.pallas_call(
        functools.partial(_mix_kernel, tm=tm),
        grid=(T // tm,),
        in_specs=[row(D_MODEL), row(C_HY), row(C_S5),
                  pl.BlockSpec((C_S5, C_S5), const), pl.BlockSpec((1, C_S5), const),
                  pl.BlockSpec((1, C_HY), const), pl.BlockSpec((1, C_S5), const),
                  pl.BlockSpec((D_MODEL, D_MODEL), const), pl.BlockSpec((1, D_MODEL), const),
                  pl.BlockSpec((D_MODEL, LANES), const), pl.BlockSpec((1, LANES), const)],
        out_specs=[row(D_MODEL), row(D_MODEL), row(LANES)],
        out_shape=[jax.ShapeDtypeStruct((T, D_MODEL), F32), jax.ShapeDtypeStruct((T, D_MODEL), F32),
                   jax.ShapeDtypeStruct((T, LANES), F32)],
        compiler_params=_cparams("arbitrary"),
        name="mix",
    )(x2d, ya, ybp, w_glu_bf, b_glu, g_hyena, g_s5, w_out_bf, g_ffn, w_router_p, b_router_p)


def _route_kernel(r_ref, dest_ref, cnt_ref, pst_ref, carry, pstart, *, tm):
    p = pl.program_id(0)
    i = pl.program_id(1)
    lane = lax.broadcasted_iota(I32, (tm, LANES), 1)
    r = r_ref[...]
    eids = [r[:, TOP_K + k:TOP_K + k + 1].astype(I32) for k in range(TOP_K)]
    oh = jnp.zeros((tm, LANES), F32)
    for e in eids:
        oh += (lane == e).astype(F32)
    tot = jnp.sum(oh, axis=0, keepdims=True)

    @pl.when((p == 0) & (i == 0))
    def _():
        carry[...] = jnp.zeros_like(carry)

    @pl.when((p == 1) & (i == 0))
    def _():
        cnt = carry[...]
        cnt_ref[...] = cnt
        padded = jnp.floor((cnt + (MOE_BLOCK - 1)) * (1.0 / MOE_BLOCK)) * MOE_BLOCK
        a = lax.broadcasted_iota(I32, (LANES, LANES), 0)
        b = lax.broadcasted_iota(I32, (LANES, LANES), 1)
        excl = jnp.dot(jnp.broadcast_to(padded, (SUBLANES, LANES)), (a < b).astype(F32),
                       precision=HIGHEST, preferred_element_type=F32)
        pstart[...] = excl[0:1]
        pst_ref[...] = excl[0:1]
        carry[...] = jnp.zeros_like(carry)

    @pl.when(p == 1)
    def _():
        a = lax.broadcasted_iota(I32, (tm, tm), 0)
        b = lax.broadcasted_iota(I32, (tm, tm), 1)
        before = _dot((b < a).astype(BF16), oh.astype(BF16))
        base = before + carry[...] + pstart[...]
        out = jnp.zeros((tm, LANES), F32)
        for k, e in enumerate(eids):
            d = jnp.sum(jnp.where(lane == e, base, 0.0), axis=-1, keepdims=True)
            out = jnp.where(lane == k, d, out)
        dest_ref[...] = out.astype(I32)

    carry[...] += tot


def _route(route, tm=512):
    T = route.shape[0]
    one = jax.ShapeDtypeStruct((1, LANES), F32)
    return pl.pallas_call(
        functools.partial(_route_kernel, tm=tm),
        grid=(2, T // tm),
        in_specs=[pl.BlockSpec((tm, LANES), lambda p, i: (i, 0))],
        out_specs=[pl.BlockSpec((tm, LANES), lambda p, i: (i * p, 0)),
                   pl.BlockSpec((1, LANES), lambda p, i: (0, 0)),
                   pl.BlockSpec((1, LANES), lambda p, i: (0, 0))],
        out_shape=[jax.ShapeDtypeStruct((T, LANES), I32), one, one],
        scratch_shapes=[pltpu.VMEM((1, LANES), F32), pltpu.VMEM((1, LANES), F32)],
        compiler_params=_cparams("arbitrary", "arbitrary"),
        name="route",
    )(route)


ZERO_ROWS = 64


def _dispatch_kernel(padpos_ref, padcnt_ref, nu_ref, dest_ref, h_ref, xs_ref, zbuf, sem, zsem, *, td, nblk):
    i = pl.program_id(0)

    @pl.when(i == 0)
    def _():
        zbuf[...] = jnp.zeros_like(zbuf)

        def per_expert(e, carry):
            off = padpos_ref[e]

            def zero_row(r):
                return pltpu.make_async_copy(zbuf.at[pl.ds(0, 1)], xs_ref.at[pl.ds(off + r, 1)], zsem)

            def z_issue(r, c):
                zero_row(r).start()
                return c

            def z_drain(r, c):
                zero_row(r).wait()
                return c

            lax.fori_loop(0, padcnt_ref[e], z_issue, 0)
            lax.fori_loop(0, padcnt_ref[e], z_drain, 0)
            return carry

        lax.fori_loop(0, N_EXPERTS, per_expert, 0)

        def zero_piece(j):
            row = pl.multiple_of(j * ZERO_ROWS, ZERO_ROWS)
            return pltpu.make_async_copy(zbuf, xs_ref.at[pl.ds(row, ZERO_ROWS)], zsem)

        per_blk = MOE_BLOCK // ZERO_ROWS

        def t_issue(j, c):
            zero_piece(j).start()
            return c

        def t_drain(j, c):
            zero_piece(j).wait()
            return c

        lax.fori_loop(nu_ref[0] * per_blk, nblk * per_blk, t_issue, 0)
        lax.fori_loop(nu_ref[0] * per_blk, nblk * per_blk, t_drain, 0)

    def row_copy(r, k):
        return pltpu.make_async_copy(h_ref.at[pl.ds(r, 1)], xs_ref.at[pl.ds(dest_ref[r * TOP_K + k], 1)],
                                     sem)

    def issue(r, carry):
        for k in range(TOP_K):
            row_copy(r, k).start()
        return carry

    def drain(r, carry):
        for k in range(TOP_K):
            row_copy(r, k).wait()
        return carry

    lax.fori_loop(0, td, issue, 0)
    lax.fori_loop(0, td, drain, 0)


def _dispatch(hf, dest_flat, padpos, padcnt, n_used, n_slots, td=128):
    T = hf.shape[0]
    grid_spec = pltpu.PrefetchScalarGridSpec(
        num_scalar_prefetch=3,
        grid=(T // td,),
        in_specs=[pl.BlockSpec((td * TOP_K,), lambda i, *_: (i,), memory_space=pltpu.SMEM),
                  pl.BlockSpec((td, D_MODEL), lambda i, *_: (i, 0))],
        out_specs=pl.BlockSpec(memory_space=pl.ANY),
        scratch_shapes=[pltpu.VMEM((ZERO_ROWS, D_MODEL), F32),
                        pltpu.SemaphoreType.DMA(()), pltpu.SemaphoreType.DMA(())],
    )
    return pl.pallas_call(
        functools.partial(_dispatch_kernel, td=td, nblk=n_slots // MOE_BLOCK),
        grid_spec=grid_spec,
        out_shape=jax.ShapeDtypeStruct((n_slots, D_MODEL), F32),
        compiler_params=_cparams("arbitrary"),
        name="dispatch",
    )(padpos, padcnt, n_used, dest_flat, hf)


def _moe_kernel(be_ref, nu_ref, x_ref, wg_ref, bg_ref, wu_ref, bu_ref, wd_ref, bd_ref, o_ref):
    used = pl.program_id(0) < nu_ref[0]

    @pl.when(jnp.logical_not(used))
    def _():
        o_ref[...] = jnp.zeros_like(o_ref)

    @pl.when(used)
    def _():
        x = x_ref[...].astype(BF16)
        gt = jnp.minimum(_dot(x, wg_ref[...]) + bg_ref[...], SWIGLU_LIMIT)
        up = jnp.clip(_dot(x, wu_ref[...]) + bu_ref[...], -SWIGLU_LIMIT, SWIGLU_LIMIT)
        act = (up + 1.0) * (gt * _sigmoid(SWIGLU_ALPHA * gt))
        o_ref[...] = _dot(act.astype(BF16), wd_ref[...]) + bd_ref[...]


def _moe(xs, block_expert, n_used, wg, bg, wu, bu, wd, bd):
    n_slots = xs.shape[0]
    nblk = n_slots // MOE_BLOCK
    blk = lambda i, be, nu: (jnp.minimum(i, nu[0] - 1), 0)
    exp = lambda i, be, nu: (be[jnp.minimum(i, nu[0] - 1)], 0, 0)
    wspec = pl.BlockSpec((None, D_MODEL, D_MODEL), exp)
    bspec = pl.BlockSpec((None, 1, D_MODEL), exp)
    grid_spec = pltpu.PrefetchScalarGridSpec(
        num_scalar_prefetch=2,
        grid=(nblk,),
        in_specs=[pl.BlockSpec((MOE_BLOCK, D_MODEL), blk), wspec, bspec, wspec, bspec, wspec, bspec],
        out_specs=pl.BlockSpec((MOE_BLOCK, D_MODEL), lambda i, be, nu: (i, 0)),
    )
    return pl.pallas_call(
        _moe_kernel,
        grid_spec=grid_spec,
        out_shape=jax.ShapeDtypeStruct((n_slots, D_MODEL), F32),
        compiler_params=_cparams("arbitrary"),
        name="moe",
    )(block_expert, n_used, xs, wg, bg, wu, bu, wd, bd)


def _combine_kernel(dest_ref, x1_ref, r_ref, g_ref, ys_ref, o_ref, buf, sem, *, tc):
    def row_copy(r, k):
        return pltpu.make_async_copy(ys_ref.at[pl.ds(dest_ref[r * TOP_K + k], 1)],
                                     buf.at[k, pl.ds(r, 1)], sem)

    def issue(r, carry):
        for k in range(TOP_K):
            row_copy(r, k).start()
        return carry

    def drain(r, carry):
        for k in range(TOP_K):
            row_copy(r, k).wait()
        return carry

    lax.fori_loop(0, tc, issue, 0)
    lax.fori_loop(0, tc, drain, 0)
    gates = r_ref[...]
    acc = x1_ref[...]
    for k in range(TOP_K):
        acc += gates[:, k:k + 1] * buf[k]
    o_ref[...] = _rms(acc, g_ref[...])


def _combine(x1, route, dest_flat, ys, g_final, tc=128):
    T = x1.shape[0]
    return pl.pallas_call(
        functools.partial(_combine_kernel, tc=tc),
        grid=(T // tc,),
        in_specs=[pl.BlockSpec((tc * TOP_K,), lambda i: (i,), memory_space=pltpu.SMEM),
                  pl.BlockSpec((tc, D_MODEL), lambda i: (i, 0)),
                  pl.BlockSpec((tc, LANES), lambda i: (i, 0)),
                  pl.BlockSpec((1, D_MODEL), lambda i: (0, 0)),
                  pl.BlockSpec(memory_space=pl.ANY)],
        out_specs=pl.BlockSpec((tc, D_MODEL), lambda i: (i, 0)),
        out_shape=jax.ShapeDtypeStruct((T, D_MODEL), F32),
        scratch_shapes=[pltpu.VMEM((TOP_K, tc, D_MODEL), F32), pltpu.SemaphoreType.DMA(())],
        compiler_params=_cparams("arbitrary"),
        name="combine",
    )(dest_flat, x1, route, g_final, ys)


def _moe_layer(x1, hf, route, ew, g_final):
    T = x1.shape[0]
    n_assign = T * TOP_K
    nblk = n_assign // MOE_BLOCK + N_EXPERTS
    n_slots = nblk * MOE_BLOCK
    dest, cnt, pst = _route(route)
    cnt_i = cnt[0, :N_EXPERTS].astype(I32)
    pstart = pst[0, :N_EXPERTS].astype(I32)
    padded = (cnt_i + MOE_BLOCK - 1) // MOE_BLOCK * MOE_BLOCK
    pend = pstart + padded
    block_expert = jnp.minimum(
        jnp.sum(jnp.arange(nblk, dtype=I32)[:, None] * MOE_BLOCK >= pend[None, :], axis=1),
        N_EXPERTS - 1).astype(I32)
    n_used = (pend[-1] // MOE_BLOCK).reshape(1).astype(I32)
    dest_flat = dest[:, :TOP_K].reshape(n_assign)
    xs = _dispatch(hf, dest_flat, pstart + cnt_i, padded - cnt_i, n_used, n_slots)
    ys = _moe(xs, block_expert, n_used, *ew)
    return _combine(x1, route, dest_flat, ys, g_final)


def _trunk(x, p, bb):
    B, L, D = x.shape
    T = B * L
    x2d = x.reshape(T, D)
    v, x1g, x2g, u = _inproj(x2d, L, p["g_mix"], p["w_in"], p["b_in"], p["short_w"], p["short_b"])
    ya = _hyena(v, x1g, x2g, B, L, p["filt"], p["filt_bias"], bb)
    ybp = _s5(u, B, L, p["s5"])
    x1, hf, route = _mix(x2d, ya, ybp, p["w_glu"], p["b_glu"], p["g_hyena"], p["g_s5"], p["w_out"],
                         p["g_ffn"], p["w_router"], p["b_router"])
    y = _moe_layer(x1, hf, route, p["experts"], p["g_final"])
    return y.reshape(B, L, D)


def kernel(x_prompt, x_sample, g_mix, w_in, b_in, short_w, short_b, filt_w1, filt_b1, filt_w2, filt_b2, filt_w3, filt_b3, filt_freq, filt_bias, s5_lam_re, s5_lam_im, s5_log_dt, s5_b_re, s5_b_im, s5_c_re, s5_c_im, s5_d, w_glu, b_glu, g_hyena, g_s5, w_out, g_ffn, w_router, b_router, w_gate, b_gate, w_up, b_up, w_down, b_down, g_final):
    assert g_mix.shape[0] == 1, "one encoder layer"
    row = lambda a: a[0][None, :].astype(F32)
    p = dict(
        g_mix=row(g_mix), w_in=w_in[0].astype(BF16), b_in=row(b_in),
        short_w=short_w[0].astype(F32), short_b=row(short_b),
        filt=(filt_w1[0], filt_b1[0], filt_w2[0], filt_b2[0], filt_w3[0], filt_b3[0], filt_freq[0]),
        filt_bias=filt_bias[0].astype(F32),
        s5=_s5_operators(s5_lam_re[0], s5_lam_im[0], s5_log_dt[0], s5_b_re[0], s5_b_im[0],
                         s5_c_re[0], s5_c_im[0], s5_d[0]),
        w_glu=w_glu[0].astype(BF16), b_glu=row(b_glu), g_hyena=row(g_hyena), g_s5=row(g_s5),
        w_out=w_out[0].astype(BF16), g_ffn=row(g_ffn),
        w_router=jnp.pad(w_router[0].astype(F32), ((0, 0), (0, LANES - N_EXPERTS))),
        b_router=jnp.pad(b_router[0].astype(F32), (0, LANES - N_EXPERTS))[None, :],
        experts=(w_gate[0].astype(BF16), b_gate[0][:, None, :], w_up[0].astype(BF16),
                 b_up[0][:, None, :], w_down[0].astype(BF16), b_down[0][:, None, :]),
        g_final=g_final[None, :].astype(F32),
    )
    y_prompt = _trunk(x_prompt, p, bb=1)
    y_sample = _trunk(x_sample, p, bb=8)
    return (y_prompt, y_sample)
```

```python
import functools
import math

import numpy as np
import jax
import jax.numpy as jnp
from jax import lax
from jax.experimental import pallas as pl
from jax.experimental.pallas import tpu as pltpu

F32 = jnp.float32
BF16 = jnp.bfloat16
I32 = jnp.int32
HIGHEST = lax.Precision.HIGHEST

LANES = 128
SUBLANES = 8
D_MODEL = 1024
C_HY = 512
C_S5 = 512
D_IN = 3 * C_HY + C_S5
Q_HY = C_HY // LANES
FILTER_BANDS = 16
FILTER_WIDTH = 64
Z_PAD = 128
DECAY_MAX = math.log(1e-2) / 0.3
DECAY_MIN = math.log(1e-2) / 1.5
S5_G, S5_H, S5_P = 32, 16, 64
S5_CHUNK = 16
S5_UW = S5_CHUNK * S5_H
S5_SW = 2 * S5_P
S5_GB = LANES // S5_H
S5_NB = S5_G // S5_GB
S5_KW = S5_CHUNK * LANES
S5_XW = S5_GB * S5_SW
LAMBDA_RE_MAX = -1e-4
N_EXPERTS = 32
TOP_K = 4
SWIGLU_LIMIT = 7.0
SWIGLU_ALPHA = 1.702
MOE_BLOCK = 512
RMS_EPS = 1e-6
VMEM_LIMIT_BYTES = 56 * 1024 * 1024


def _cparams(*sem):
    return pltpu.CompilerParams(dimension_semantics=sem, vmem_limit_bytes=VMEM_LIMIT_BYTES)


def _rms(x, g):
    return x * lax.rsqrt(jnp.mean(x * x, axis=-1, keepdims=True) + RMS_EPS) * g


def _sigmoid(x):
    return 1.0 / (1.0 + jnp.exp(-x))


def _dot(a, b):
    return jnp.dot(a, b, preferred_element_type=F32)


def _lane_block(x, q):
    return x[:, q * LANES:(q + 1) * LANES]


def _as_rows(ref):
    return ref.reshape(math.prod(ref.shape[:-1]), LANES)


def _inproj_kernel(x_ref, xp_ref, xn_ref, g_ref, w_ref, b_ref, sw_ref, sb_ref,
                   v_ref, x1_ref, x2_ref, u_ref, *, tm, seq):
    i = pl.program_id(0)
    g = g_ref[...]
    h = _rms(x_ref[...], g).astype(BF16)
    proj = _dot(h, w_ref[...]) + b_ref[...]
    u_ref[...] = proj[:, 3 * C_HY:]
    z = proj[:, :3 * C_HY]
    halo = jnp.concatenate([xp_ref[...], xn_ref[...]], axis=0)
    zh = _dot(_rms(halo, g).astype(BF16), w_ref[:, :3 * C_HY]) + b_ref[:, :3 * C_HY]
    row0 = i * tm
    has_prev = lax.rem(row0, seq) != 0
    has_next = lax.rem(row0 + tm, seq) != 0
    zp = jnp.where(has_prev, zh[SUBLANES - 1:SUBLANES], 0.0)
    zn = jnp.where(has_next, zh[SUBLANES:SUBLANES + 1], 0.0)
    rid = lax.broadcasted_iota(I32, (tm, 1), 0)
    zm1 = jnp.where(rid == 0, zp, pltpu.roll(z, 1, 0))
    zp1 = jnp.where(rid == tm - 1, zn, pltpu.roll(z, tm - 1, 0))
    sw = sw_ref[...]
    o = zm1 * sw[0:1] + z * sw[1:2] + zp1 * sw[2:3] + sb_ref[...]
    for q in range(Q_HY):
        v_ref[q] = _lane_block(o, q)
        x1_ref[q] = _lane_block(o, Q_HY + q)
        x2_ref[q] = _lane_block(o, 2 * Q_HY + q)


def _inproj(x2d, seq, g_mix, w_in_bf, b_in, short_w, short_b, tm=512):
    T = x2d.shape[0]
    nb8 = T // SUBLANES
    tb = tm // SUBLANES
    hy = jax.ShapeDtypeStruct((Q_HY, T, LANES), F32)
    hy_spec = pl.BlockSpec((Q_HY, tm, LANES), lambda i: (0, i, 0))
    const = lambda i: (0, 0)
    return pl.pallas_call(
        functools.partial(_inproj_kernel, tm=tm, seq=seq),
        grid=(T // tm,),
        in_specs=[
            pl.BlockSpec((tm, D_MODEL), lambda i: (i, 0)),
            pl.BlockSpec((SUBLANES, D_MODEL), lambda i: (jnp.maximum(i * tb - 1, 0), 0)),
            pl.BlockSpec((SUBLANES, D_MODEL), lambda i: (jnp.minimum((i + 1) * tb, nb8 - 1), 0)),
            pl.BlockSpec((1, D_MODEL), const),
            pl.BlockSpec((D_MODEL, D_IN), const),
            pl.BlockSpec((1, D_IN), const),
            pl.BlockSpec((3, 3 * C_HY), const),
            pl.BlockSpec((1, 3 * C_HY), const),
        ],
        out_specs=[hy_spec, hy_spec, hy_spec, pl.BlockSpec((tm, C_S5), lambda i: (i, 0))],
        out_shape=[hy, hy, hy, jax.ShapeDtypeStruct((T, C_S5), F32)],
        compiler_params=_cparams("arbitrary"),
        name="inproj",
    )(x2d, x2d, x2d, g_mix, w_in_bf, b_in, short_w, short_b)


def _filter_kernel(z_ref, w1_ref, b1_ref, w2_ref, b2_ref, w3_ref, b3_ref, fr_ref, dl_ref,
                   k_ref, s_ref, *, tr, seq):
    i = pl.program_id(0)
    z = z_ref[...]
    fr = fr_ref[...]
    h = jnp.sin(fr[0:1] * (jnp.dot(z, w1_ref[...], precision=HIGHEST,
                                   preferred_element_type=F32) + b1_ref[...]))
    h = jnp.sin(fr[1:2] * (jnp.dot(h, w2_ref[...], precision=HIGHEST,
                                   preferred_element_type=F32) + b2_ref[...]))
    h = jnp.dot(h, w3_ref[...], precision=HIGHEST, preferred_element_type=F32) + b3_ref[...]
    h = h * jnp.exp(-z[:, 0:1] * dl_ref[...])
    rid = i * tr + lax.broadcasted_iota(I32, (tr, 1), 0)
    h = jnp.where(rid == seq, 0.0, h)
    for q in range(2 * Q_HY):
        k_ref[q] = _lane_block(h, q)

    @pl.when(i == 0)
    def _():
        s_ref[...] = jnp.zeros_like(s_ref)

    s_ref[...] += jnp.sum(jnp.abs(h), axis=0, keepdims=True)


def _filter_taps(seq, w1, b1, w2, b2, w3, b3, freq, tr=512):
    L = seq
    N = 2 * L
    n = jnp.arange(N, dtype=I32)
    pos = jnp.where(n < L, n, jnp.where(n == L, 0, N - n)).astype(F32)
    t = (pos * (1.0 / (L - 1)))[:, None]
    w = 2.0 * math.pi * pos / L
    bands = jnp.linspace(1e-4, FILTER_BANDS - 1, FILTER_BANDS, dtype=F32)
    ang = w[:, None] * bands[None, :]
    zc = jnp.concatenate([t, jnp.cos(ang), -jnp.sin(ang),
                          jnp.zeros((N, Z_PAD - 1 - 2 * FILTER_BANDS), F32)], axis=-1)
    w1p = jnp.pad(w1, ((0, Z_PAD - w1.shape[0]), (0, 0)))
    w3d = w3.reshape(FILTER_WIDTH, 2, 2 * C_HY).transpose(1, 0, 2)
    b3d = b3.reshape(2, 1, 2 * C_HY)
    deltas = jnp.abs(jnp.linspace(DECAY_MIN, DECAY_MAX, C_HY, dtype=F32))
    dl = jnp.tile(deltas, 2)[None, :]
    half = (N // tr) // 2
    const = lambda i: (0, 0)
    taps, sums = pl.pallas_call(
        functools.partial(_filter_kernel, tr=tr, seq=L),
        grid=(N // tr,),
        in_specs=[
            pl.BlockSpec((tr, Z_PAD), lambda i: (i, 0)),
            pl.BlockSpec((Z_PAD, FILTER_WIDTH), const),
            pl.BlockSpec((1, FILTER_WIDTH), const),
            pl.BlockSpec((FILTER_WIDTH, FILTER_WIDTH), const),
            pl.BlockSpec((1, FILTER_WIDTH), const),
            pl.BlockSpec((None, FILTER_WIDTH, 2 * C_HY), lambda i: (i // half, 0, 0)),
            pl.BlockSpec((None, 1, 2 * C_HY), lambda i: (i // half, 0, 0)),
            pl.BlockSpec((2, FILTER_WIDTH), const),
            pl.BlockSpec((1, 2 * C_HY), const),
        ],
        out_specs=[pl.BlockSpec((2 * Q_HY, tr, LANES), lambda i: (0, i, 0)),
                   pl.BlockSpec((1, 2 * C_HY), const)],
        out_shape=[jax.ShapeDtypeStruct((2 * Q_HY, N, LANES), F32),
                   jax.ShapeDtypeStruct((1, 2 * C_HY), F32)],
        compiler_params=_cparams("arbitrary"),
        name="hyena_filter",
    )(zc, w1p, b1[None, :], w2, b2[None, :], w3d, b3d, freq, dl)
    return taps, sums


def _split_n(N):
    n1 = {32768: 128, 4096: 64}.get(N)
    if n1 is None:
        n1 = 1 << (int(math.log2(N)) // 2)
    return n1, N // n1


def _dft_constants(N1, N2):
    N = N1 * N2
    k1 = np.arange(N1)[:, None]
    n1 = np.arange(N1)[None, :]
    ang = 2.0 * np.pi * ((k1 * n1) % N1) / N1
    fa_full = np.concatenate([np.cos(ang), -np.sin(ang)], axis=0)
    fa_half = fa_full[:, :N1 // 2]
    fa_inv = np.concatenate([np.cos(ang), -np.sin(ang)], axis=1)[:N1 // 2] / N
    k2 = np.arange(N2)[:, None]
    n2 = np.arange(N2)[None, :]
    a2 = 2.0 * np.pi * ((k2 * n2) % N2) / N2
    cr, ci = np.cos(a2), -np.sin(a2)
    f2 = np.block([[cr, -ci], [ci, cr]])
    g2 = np.block([[cr, ci], [-ci, cr]])
    to = lambda a: jnp.asarray(a.astype(np.float32)).astype(BF16)
    kk = lax.broadcasted_iota(I32, (N1, N2, LANES), 0)
    nn = lax.broadcasted_iota(I32, (N1, N2, LANES), 1)
    ta = (2.0 * math.pi / N) * lax.rem(kk * nn, N).astype(F32)
    tw = jnp.stack([jnp.cos(ta), -jnp.sin(ta)], axis=0)
    return dict(fa_full=to(fa_full), fa_half=to(fa_half), fa_inv=to(fa_inv), f2=to(f2), g2=to(g2),
                tw=tw)


def _dft_a_kernel(f_ref, x_ref, o_ref, *, rows):
    nq = x_ref.shape[0]
    x2d = _as_rows(x_ref)
    f = f_ref[...]
    for j in range(SUBLANES):
        xs = jnp.concatenate(
            [x2d[pl.ds(q * rows * SUBLANES + j, rows, stride=SUBLANES), :] for q in range(nq)], axis=1)
        r = _dot(f, xs.astype(BF16))
        for q in range(nq):
            o_ref[q, j] = _lane_block(r, q)


def _dft_a(f, x):
    Q, Bt, K, N2, _ = x.shape
    R = f.shape[0]
    return pl.pallas_call(
        functools.partial(_dft_a_kernel, rows=K),
        grid=(Bt, N2 // SUBLANES, Q // Q_HY),
        in_specs=[pl.BlockSpec((R, K), lambda b, j, w: (0, 0)),
                  pl.BlockSpec((Q_HY, None, K, SUBLANES, LANES), lambda b, j, w: (w, b, 0, j, 0))],
        out_specs=pl.BlockSpec((Q_HY, None, SUBLANES, R, LANES), lambda b, j, w: (w, b, j, 0, 0)),
        out_shape=jax.ShapeDtypeStruct((Q, Bt, N2, R, LANES), F32),
        compiler_params=_cparams("arbitrary", "arbitrary", "arbitrary"),
        name="dft_a",
    )(f, x)


def _dft_a_inv_kernel(f_ref, z_ref, gate_ref, v_ref, bias_ref, o_ref, *, rows):
    g2d = _as_rows(gate_ref)
    v2d = _as_rows(v_ref)
    o2d = _as_rows(o_ref)
    f = f_ref[...]
    bias = bias_ref[...]
    for j in range(SUBLANES):
        zj = jnp.concatenate([z_ref[q, j] for q in range(Q_HY)], axis=1)
        y = _dot(f, zj.astype(BF16))
        for q in range(Q_HY):
            sl = pl.ds(q * rows * SUBLANES + j, rows, stride=SUBLANES)
            o2d[sl, :] = g2d[sl, :] * (_lane_block(y, q) + v2d[sl, :] * _lane_block(bias, q))


def _dft_a_inv(f, z, gate, v, bias_row):
    Q, Bt, N2, R2, _ = z.shape
    K = f.shape[0]
    nat = pl.BlockSpec((Q, None, K, SUBLANES, LANES), lambda b, j: (0, b, 0, j, 0))
    return pl.pallas_call(
        functools.partial(_dft_a_inv_kernel, rows=K),
        grid=(Bt, N2 // SUBLANES),
        in_specs=[pl.BlockSpec((K, R2), lambda b, j: (0, 0)),
                  pl.BlockSpec((Q, None, SUBLANES, R2, LANES), lambda b, j: (0, b, j, 0, 0)),
                  nat, nat,
                  pl.BlockSpec((1, C_HY), lambda b, j: (0, 0))],
        out_specs=nat,
        out_shape=jax.ShapeDtypeStruct((Q, Bt, K, N2, LANES), F32),
        compiler_params=_cparams("arbitrary", "arbitrary"),
        name="dft_a_inv",
    )(f, z, gate, v, bias_row)


C_STEP_Q = 2
K1_STEP = SUBLANES


def _dft_c_kernel(a_ref, k_ref, tw_ref, f_ref, g_ref, z_ref, *, bb, n2):
    per = n2 * 2 * K1_STEP
    a2d = _as_rows(a_ref)
    z2d = _as_rows(z_ref)
    cols = [(b, q) for b in range(bb) for q in range(C_STEP_Q)]

    def rows(b, q, kk, im):
        return pl.ds((q * bb + b) * per + im * K1_STEP + kk, n2, stride=2 * K1_STEP)

    for kk in range(K1_STEP):
        twr = jnp.tile(tw_ref[0, kk], (1, len(cols)))
        twi = jnp.tile(tw_ref[1, kk], (1, len(cols)))
        kr = jnp.tile(k_ref[0, kk], (1, bb))
        ki = jnp.tile(k_ref[1, kk], (1, bb))
        ar = jnp.concatenate([a2d[rows(b, q, kk, 0), :] for b, q in cols], axis=1)
        ai = jnp.concatenate([a2d[rows(b, q, kk, 1), :] for b, q in cols], axis=1)
        pr = ar * twr - ai * twi
        pi = ar * twi + ai * twr
        x = _dot(f_ref[...], jnp.concatenate([pr, pi], axis=0).astype(BF16))
        xr = x[:n2]
        xi = x[n2:]
        yr = xr * kr - xi * ki
        yi = xr * ki + xi * kr
        zz = _dot(g_ref[...], jnp.concatenate([yr, yi], axis=0).astype(BF16))
        zr = zz[:n2]
        zi = zz[n2:]
        outr = zr * twr + zi * twi
        outi = zi * twr - zr * twi
        for c, (b, q) in enumerate(cols):
            z2d[rows(b, q, kk, 0), :] = _lane_block(outr, c)
            z2d[rows(b, q, kk, 1), :] = _lane_block(outi, c)


def _dft_c(a6, khat, order, tw, f2, g2, bb):
    Q, Bt, N2, _, N1, _ = a6.shape
    cw = C_STEP_Q * LANES
    ablk = pl.BlockSpec((C_STEP_Q, bb, N2, 2, K1_STEP, LANES), lambda g, c, b: (c, b, 0, 0, g, 0))
    return pl.pallas_call(
        functools.partial(_dft_c_kernel, bb=bb, n2=N2),
        grid=(N1 // K1_STEP, Q // C_STEP_Q, Bt // bb),
        in_specs=[ablk,
                  pl.BlockSpec((2, K1_STEP, N2, cw), lambda g, c, b: (0, g, 0, order * (C_HY // cw) + c)),
                  pl.BlockSpec((2, K1_STEP, N2, LANES), lambda g, c, b: (0, g, 0, 0)),
                  pl.BlockSpec((2 * N2, 2 * N2), lambda g, c, b: (0, 0)),
                  pl.BlockSpec((2 * N2, 2 * N2), lambda g, c, b: (0, 0))],
        out_specs=ablk,
        out_shape=jax.ShapeDtypeStruct(a6.shape, F32),
        compiler_params=_cparams("arbitrary", "arbitrary", "arbitrary"),
        name="dft_c",
    )(a6, khat, tw, f2, g2)


def _dft_c_filter_kernel(a_ref, tw_ref, f_ref, s_ref, o_ref, *, n2):
    per = n2 * 2 * K1_STEP
    a2d = _as_rows(a_ref)
    sc = 1.0 / (s_ref[...] + 1e-6)
    for kk in range(K1_STEP):
        twr = jnp.tile(tw_ref[0, kk], (1, C_STEP_Q))
        twi = jnp.tile(tw_ref[1, kk], (1, C_STEP_Q))
        ar = jnp.concatenate([a2d[pl.ds(q * per + kk, n2, stride=2 * K1_STEP), :]
                              for q in range(C_STEP_Q)], axis=1)
        ai = jnp.concatenate([a2d[pl.ds(q * per + K1_STEP + kk, n2, stride=2 * K1_STEP), :]
                              for q in range(C_STEP_Q)], axis=1)
        pr = ar * twr - ai * twi
        pi = ar * twi + ai * twr
        x = _dot(f_ref[...], jnp.concatenate([pr, pi], axis=0).astype(BF16))
        o_ref[0, kk] = x[:n2] * sc
        o_ref[1, kk] = x[n2:] * sc


def _dft_c_filter(a5, tw, f2, sums):
    Q, N2, _, N1, _ = a5.shape
    cw = C_STEP_Q * LANES
    return pl.pallas_call(
        functools.partial(_dft_c_filter_kernel, n2=N2),
        grid=(N1 // K1_STEP, Q // C_STEP_Q),
        in_specs=[pl.BlockSpec((C_STEP_Q, N2, 2, K1_STEP, LANES), lambda g, c: (c, 0, 0, g, 0)),
                  pl.BlockSpec((2, K1_STEP, N2, LANES), lambda g, c: (0, g, 0, 0)),
                  pl.BlockSpec((2 * N2, 2 * N2), lambda g, c: (0, 0)),
                  pl.BlockSpec((1, cw), lambda g, c: (0, c))],
        out_specs=pl.BlockSpec((2, K1_STEP, N2, cw), lambda g, c: (0, g, 0, c)),
        out_shape=jax.ShapeDtypeStruct((2, N1, N2, Q * LANES), F32),
        compiler_params=_cparams("arbitrary", "arbitrary"),
        name="dft_c_filter",
    )(a5, tw, f2, sums)


def _hyena(v, x1, x2, B, seq, filt, filt_bias, bb):
    L = seq
    N = 2 * L
    N1, N2 = _split_n(N)
    cst = _dft_constants(N1, N2)
    taps, sums = _filter_taps(L, *filt)
    ka = _dft_a(cst["fa_full"], taps.reshape(2 * Q_HY, 1, N1, N2, LANES))
    khat = _dft_c_filter(ka.reshape(2 * Q_HY, N2, 2, N1, LANES), cst["tw"], cst["f2"], sums)
    nat = lambda a: a.reshape(Q_HY, B, N1 // 2, N2, LANES)
    cur = nat(v)
    for o, gate in enumerate((x1, x2)):
        a = _dft_a(cst["fa_half"], cur)
        z = _dft_c(a.reshape(Q_HY, B, N2, 2, N1, LANES), khat, o, cst["tw"], cst["f2"], cst["g2"], bb)
        cur = _dft_a_inv(cst["fa_inv"], z.reshape(Q_HY, B, N2, 2 * N1, LANES), nat(gate), cur,
                         filt_bias[o][None, :])
    return cur.reshape(Q_HY, B * L, LANES)


def _s5_operators(lam_re, lam_im, log_dt, b_re, b_im, c_re, c_im, d_skip):
    Tc, G, H, P = S5_CHUNK, S5_G, S5_H, S5_P
    lam = jnp.minimum(lam_re.astype(F32), LAMBDA_RE_MAX) + 1j * lam_im.astype(F32)
    dt = jnp.exp(log_dt.astype(F32))[..., None]
    lam_dt = lam * dt
    lam_bar = jnp.exp(lam_dt)
    b_bar = ((lam_bar - 1.0) / lam)[..., None] * (b_re.astype(F32) + 1j * b_im.astype(F32))
    c = c_re.astype(F32) + 1j * c_im.astype(F32)
    tau = jnp.arange(Tc + 1, dtype=F32)
    pw = jnp.exp(lam_dt[None] * tau[:, None, None, None])
    kk = jnp.einsum('dghp,tdgp,dgpk->dtghk', c, pw[:Tc], b_bar).real
    i = jnp.arange(Tc)
    lag = i[:, None] - i[None, :]
    kf = jnp.where((lag >= 0)[:, :, None, None, None], kk[0][jnp.clip(lag, 0, Tc - 1)], 0.0)
    kb = jnp.where((lag <= 0)[:, :, None, None, None], kk[1][jnp.clip(-lag, 0, Tc - 1)], 0.0)
    m = kf + kb
    eye = (lag == 0)[:, :, None, None, None] * jnp.eye(H, dtype=F32)[None, None, None]
    m = m + eye * d_skip.astype(F32).reshape(G, H)[None, None, :, :, None]
    clf = c[0][None] * pw[1:Tc + 1, 0][:, :, None, :]
    clb = c[1][None] * pw[Tc - i, 1][:, :, None, :]
    sf = pw[Tc - 1 - i, 0][:, :, :, None] * b_bar[0][None]
    sb = pw[i, 1][:, :, :, None] * b_bar[1][None]
    eye_g = jnp.eye(S5_GB, dtype=F32)
    blk = lambda a: a.reshape((a.shape[0], S5_NB, S5_GB) + a.shape[2:])
    w_intra = jnp.einsum('ijqakh,ab->qjahibk', m.reshape(Tc, Tc, S5_NB, S5_GB, H, H), eye_g) \
        .reshape(S5_NB, S5_KW, S5_KW)
    st2y = lambda cl: jnp.einsum('iqahs,ab->qasibh',
                                 blk(jnp.concatenate([cl.real, -cl.imag], axis=-1)), eye_g) \
        .reshape(S5_NB, S5_XW, S5_KW)
    u2s = lambda s: jnp.einsum('jqash,ab->qjahbs',
                               blk(jnp.concatenate([s.real, s.imag], axis=2)), eye_g) \
        .reshape(S5_NB, S5_KW, S5_XW)
    w_state = jnp.concatenate([u2s(sf), u2s(sb)], axis=-1)
    lam16 = pw[Tc]
    sc_a = jnp.concatenate([lam16.real, lam16.real], axis=-1).reshape(2, G * 2 * P)
    sc_b = jnp.concatenate([-lam16.imag, lam16.imag], axis=-1).reshape(2, G * 2 * P)
    return dict(w_intra=w_intra.astype(BF16), w_xf=st2y(clf).astype(BF16),
                w_xb=st2y(clb).astype(BF16), w_state=w_state.astype(BF16), sc_a=sc_a, sc_b=sc_b)


def _s5_rows(i, cl, B, ncc):
    return pl.ds(S5_CHUNK * cl + i, B, stride=S5_CHUNK * ncc)


def _s5_chunk_inputs(u2d, B, ncc):
    cols = []
    for i in range(S5_CHUNK):
        if B == 1:
            cols.append(u2d[pl.ds(i, ncc, stride=S5_CHUNK), :])
        else:
            cols.append(jnp.concatenate([u2d[_s5_rows(i, cl, B, ncc), :] for cl in range(ncc)], axis=0))
    return jnp.concatenate(cols, axis=1).astype(BF16)


def _s5_state_kernel(u_ref, w_ref, sf_ref, sb_ref, *, B, ncc):
    lhs = _s5_chunk_inputs(_as_rows(u_ref), B, ncc)
    r = _dot(lhs, w_ref[...])
    sf_ref[...] = r[:, :S5_XW]
    sb_ref[...] = r[:, S5_XW:]


def _s5_state(u3, w_state, ncc):
    B, L, _ = u3.shape
    nch = L // S5_CHUNK
    out = jax.ShapeDtypeStruct((nch * B, S5_NB * S5_XW), F32)
    return pl.pallas_call(
        functools.partial(_s5_state_kernel, B=B, ncc=ncc),
        grid=(S5_NB, nch // ncc),
        in_specs=[pl.BlockSpec((B, S5_CHUNK * ncc, LANES), lambda q, t: (0, t, q)),
                  pl.BlockSpec((None, S5_KW, 2 * S5_XW), lambda q, t: (q, 0, 0))],
        out_specs=[pl.BlockSpec((ncc * B, S5_XW), lambda q, t: (t, q))] * 2,
        out_shape=[out, out],
        compiler_params=_cparams("arbitrary", "arbitrary"),
        name="s5_state",
    )(u3, w_state)


def _s5_scan_kernel(sf_ref, sb_ref, a_ref, b_ref, xf_ref, xb_ref, *, nch, rows):
    lb = sf_ref.shape[1]
    vr = max(rows, SUBLANES)
    lane = lax.broadcasted_iota(I32, (vr, lb), 1)
    first = (lane & S5_P) == 0

    def swap(x):
        return jnp.where(first, pltpu.roll(x, lb - S5_P, 1), pltpu.roll(x, S5_P, 1))

    af, ab = a_ref[0], a_ref[1]
    bf, bk = b_ref[0], b_ref[1]

    def body(c, carry):
        xf, xb = carry
        rf = pl.ds(pl.multiple_of(c * rows, rows), rows)
        xf_ref[rf, :] = xf[:rows]
        xf = af * xf + bf * swap(xf) + sf_ref[rf, :]
        rb = pl.ds(pl.multiple_of((nch - 1 - c) * rows, rows), rows)
        xb_ref[rb, :] = xb[:rows]
        xb = ab * xb + bk * swap(xb) + sb_ref[rb, :]
        return xf, xb

    zero = jnp.zeros((vr, lb), F32)
    lax.fori_loop(0, nch, body, (zero, zero))


def _s5_scan(sf, sb, sc_a, sc_b, nch, rows, lb):
    R, lanes = sf.shape
    blk = pl.BlockSpec((R, lb), lambda j: (0, j))
    cblk = pl.BlockSpec((2, 1, lb), lambda j: (0, 0, j))
    out = jax.ShapeDtypeStruct((R, lanes), F32)
    return pl.pallas_call(
        functools.partial(_s5_scan_kernel, nch=nch, rows=rows),
        grid=(lanes // lb,),
        in_specs=[blk, blk, cblk, cblk],
        out_specs=[blk, blk],
        out_shape=[out, out],
        compiler_params=_cparams("arbitrary"),
        name="s5_scan",
    )(sf, sb, sc_a[:, None, :], sc_b[:, None, :])


def _s5_out_kernel(u_ref, xf_ref, xb_ref, wm_ref, wf_ref, wb_ref, y_ref, *, B, ncc):
    lhs = _s5_chunk_inputs(_as_rows(u_ref), B, ncc)
    acc = _dot(lhs, wm_ref[...])
    acc += _dot(xf_ref[...].astype(BF16), wf_ref[...])
    acc += _dot(xb_ref[...].astype(BF16), wb_ref[...])
    y2d = _as_rows(y_ref)
    for i in range(S5_CHUNK):
        piece = _lane_block(acc, i)
        if B == 1:
            y2d[pl.ds(i, ncc, stride=S5_CHUNK), :] = piece
        else:
            for cl in range(ncc):
                y2d[_s5_rows(i, cl, B, ncc), :] = piece[cl * B:(cl + 1) * B]


def _s5_out(u3, xf, xb, ops, ncc):
    B, L, _ = u3.shape
    nch = L // S5_CHUNK
    tok = pl.BlockSpec((B, S5_CHUNK * ncc, LANES), lambda q, t: (0, t, q))
    st = pl.BlockSpec((ncc * B, S5_XW), lambda q, t: (t, q))
    return pl.pallas_call(
        functools.partial(_s5_out_kernel, B=B, ncc=ncc),
        grid=(S5_NB, nch // ncc),
        in_specs=[tok, st, st,
                  pl.BlockSpec((None, S5_KW, S5_KW), lambda q, t: (q, 0, 0)),
                  pl.BlockSpec((None, S5_XW, S5_KW), lambda q, t: (q, 0, 0)),
                  pl.BlockSpec((None, S5_XW, S5_KW), lambda q, t: (q, 0, 0))],
        out_specs=tok,
        out_shape=jax.ShapeDtypeStruct(u3.shape, F32),
        compiler_params=_cparams("arbitrary", "arbitrary"),
        name="s5_out",
    )(u3, xf, xb, ops["w_intra"], ops["w_xf"], ops["w_xb"])


def _s5(u, B, seq, ops):
    nch = seq // S5_CHUNK
    u3 = u.reshape(B, seq, C_S5)
    ncc = min(nch, 256 if B == 1 else 256 // B)
    sf, sb = _s5_state(u3, ops["w_state"], ncc)
    lb = 1024 if B == 1 else 256
    xf, xb = _s5_scan(sf, sb, ops["sc_a"], ops["sc_b"], nch, B, lb)
    return _s5_out(u3, xf, xb, ops, ncc).reshape(B * seq, C_S5)


def _mix_kernel(x_ref, ya_ref, yb_ref, wglu_ref, bglu_ref, gh_ref, gs_ref, wout_ref, gffn_ref,
                wr_ref, br_ref, x1_ref, hf_ref, route_ref, *, tm):
    g = jax.nn.gelu(yb_ref[...])
    yb = g * _sigmoid(_dot(g.astype(BF16), wglu_ref[...]) + bglu_ref[...])
    ya = jnp.concatenate([ya_ref[q] for q in range(Q_HY)], axis=1)
    na = _rms(ya, gh_ref[...]).astype(BF16)
    nb = _rms(yb, gs_ref[...]).astype(BF16)
    mixed = _dot(na, wout_ref[:C_HY, :]) + _dot(nb, wout_ref[C_HY:, :])
    x1 = x_ref[...] + mixed
    x1_ref[...] = x1
    hf = _rms(x1, gffn_ref[...])
    hf_ref[...] = hf
    logits = jnp.dot(hf, wr_ref[...], precision=HIGHEST, preferred_element_type=F32) + br_ref[...]
    lane = lax.broadcasted_iota(I32, (tm, LANES), 1)
    neg = jnp.float32(-jnp.inf)
    l = jnp.where(lane < N_EXPERTS, logits, neg)
    vals, idxs = [], []
    for _ in range(TOP_K):
        m = jnp.max(l, axis=-1, keepdims=True)
        idx = jnp.min(jnp.where(l == m, lane, LANES), axis=-1, keepdims=True)
        vals.append(m)
        idxs.append(idx)
        l = jnp.where(lane == idx, neg, l)
    es = [jnp.exp(v - vals[0]) for v in vals]
    den = es[0] + es[1] + es[2] + es[3]
    route = jnp.zeros((tm, LANES), F32)
    for k in range(TOP_K):
        route = jnp.where(lane == k, es[k] / den, route)
        route = jnp.where(lane == TOP_K + k, idxs[k].astype(F32), route)
    route_ref[...] = route


def _mix(x2d, ya4, ybp, w_glu_bf, b_glu, g_hyena, g_s5, w_out_bf, g_ffn, w_router_p, b_router_p, tm=256):
    T = x2d.shape[0]
    const = lambda i: (0, 0)
    row = lambda w: pl.BlockSpec((tm, w), lambda i: (i, 0))
    return pl.pallas_call(
        functools.partial(_mix_kernel, tm=tm),
        grid=(T // tm,),
        in_specs=[row(D_MODEL), pl.BlockSpec((Q_HY, tm, LANES), lambda i: (0, i, 0)), row(C_S5),
                  pl.BlockSpec((C_S5, C_S5), const), pl.BlockSpec((1, C_S5), const),
                  pl.BlockSpec((1, C_HY), const), pl.BlockSpec((1, C_S5), const),
                  pl.BlockSpec((D_MODEL, D_MODEL), const), pl.BlockSpec((1, D_MODEL), const),
                  pl.BlockSpec((D_MODEL, LANES), const), pl.BlockSpec((1, LANES), const)],
        out_specs=[row(D_MODEL), row(D_MODEL), row(LANES)],
        out_shape=[jax.ShapeDtypeStruct((T, D_MODEL), F32), jax.ShapeDtypeStruct((T, D_MODEL), F32),
                   jax.ShapeDtypeStruct((T, LANES), F32)],
        compiler_params=_cparams("arbitrary"),
        name="mix",
    )(x2d, ya4, ybp, w_glu_bf, b_glu, g_hyena, g_s5, w_out_bf, g_ffn, w_router_p, b_router_p)


def _route_kernel(r_ref, dest_ref, cnt_ref, pst_ref, carry, pstart, *, tm):
    p = pl.program_id(0)
    i = pl.program_id(1)
    lane = lax.broadcasted_iota(I32, (tm, LANES), 1)
    r = r_ref[...]
    eids = [r[:, TOP_K + k:TOP_K + k + 1].astype(I32) for k in range(TOP_K)]
    oh = jnp.zeros((tm, LANES), F32)
    for e in eids:
        oh += (lane == e).astype(F32)
    tot = jnp.sum(oh, axis=0, keepdims=True)

    @pl.when((p == 0) & (i == 0))
    def _():
        carry[...] = jnp.zeros_like(carry)

    @pl.when((p == 1) & (i == 0))
    def _():
        cnt = carry[...]
        cnt_ref[...] = cnt
        padded = jnp.floor((cnt + (MOE_BLOCK - 1)) * (1.0 / MOE_BLOCK)) * MOE_BLOCK
        a = lax.broadcasted_iota(I32, (LANES, LANES), 0)
        b = lax.broadcasted_iota(I32, (LANES, LANES), 1)
        excl = jnp.dot(jnp.broadcast_to(padded, (SUBLANES, LANES)), (a < b).astype(F32),
                       precision=HIGHEST, preferred_element_type=F32)
        pstart[...] = excl[0:1]
        pst_ref[...] = excl[0:1]
        carry[...] = jnp.zeros_like(carry)

    @pl.when(p == 1)
    def _():
        a = lax.broadcasted_iota(I32, (tm, tm), 0)
        b = lax.broadcasted_iota(I32, (tm, tm), 1)
        before = _dot((b < a).astype(BF16), oh.astype(BF16))
        base = before + carry[...] + pstart[...]
        out = jnp.zeros((tm, LANES), F32)
        for k, e in enumerate(eids):
            d = jnp.sum(jnp.where(lane == e, base, 0.0), axis=-1, keepdims=True)
            out = jnp.where(lane == k, d, out)
        dest_ref[...] = out.astype(I32)

    carry[...] += tot


def _route(route, tm=512):
    T = route.shape[0]
    one = jax.ShapeDtypeStruct((1, LANES), F32)
    return pl.pallas_call(
        functools.partial(_route_kernel, tm=tm),
        grid=(2, T // tm),
        in_specs=[pl.BlockSpec((tm, LANES), lambda p, i: (i, 0))],
        out_specs=[pl.BlockSpec((tm, LANES), lambda p, i: (i * p, 0)),
                   pl.BlockSpec((1, LANES), lambda p, i: (0, 0)),
                   pl.BlockSpec((1, LANES), lambda p, i: (0, 0))],
        out_shape=[jax.ShapeDtypeStruct((T, LANES), I32), one, one],
        scratch_shapes=[pltpu.VMEM((1, LANES), F32), pltpu.VMEM((1, LANES), F32)],
        compiler_params=_cparams("arbitrary", "arbitrary"),
        name="route",
    )(route)


ZERO_ROWS = 64
DMA_UNROLL = 4


def _dispatch_kernel(padpos_ref, padcnt_ref, nu_ref, dest_ref, h_ref, xs_ref, zbuf, sem, zsem,
                     *, td, nblk, nsteps):
    i = pl.program_id(0)

    @pl.when(i == 0)
    def _():
        zbuf[...] = jnp.zeros_like(zbuf)

        def per_expert(e, carry):
            off = padpos_ref[e]

            def zero_row(r):
                return pltpu.make_async_copy(zbuf.at[pl.ds(0, 1)], xs_ref.at[pl.ds(off + r, 1)], zsem)

            def z_issue(r, c):
                zero_row(r).start()
                return c

            def z_drain(r, c):
                zero_row(r).wait()
                return c

            lax.fori_loop(0, padcnt_ref[e], z_issue, 0)
            lax.fori_loop(0, padcnt_ref[e], z_drain, 0)
            return carry

        lax.fori_loop(0, N_EXPERTS, per_expert, 0)

        def zero_piece(j):
            row = pl.multiple_of(j * ZERO_ROWS, ZERO_ROWS)
            return pltpu.make_async_copy(zbuf, xs_ref.at[pl.ds(row, ZERO_ROWS)], zsem)

        per_blk = MOE_BLOCK // ZERO_ROWS

        def t_issue(j, c):
            zero_piece(j).start()
            return c

        def t_drain(j, c):
            zero_piece(j).wait()
            return c

        lax.fori_loop(nu_ref[0] * per_blk, nblk * per_blk, t_issue, 0)
        lax.fori_loop(nu_ref[0] * per_blk, nblk * per_blk, t_drain, 0)

    slot = lax.rem(i, 2)

    def issue(r, carry):
        for k in range(TOP_K):
            pltpu.make_async_copy(h_ref.at[pl.ds(i * td + r, 1)],
                                  xs_ref.at[pl.ds(dest_ref[r * TOP_K + k], 1)], sem.at[slot]).start()
        return carry

    lax.fori_loop(0, td, issue, 0, unroll=DMA_UNROLL)

    def wait_step(s):
        for _ in range(TOP_K):
            pltpu.make_async_copy(h_ref.at[pl.ds(0, td)], xs_ref.at[pl.ds(0, td)], sem.at[s]).wait()

    @pl.when(i > 0)
    def _():
        wait_step(1 - slot)

    @pl.when(i == nsteps - 1)
    def _():
        wait_step(slot)


def _dispatch(hf, dest_flat, padpos, padcnt, n_used, n_slots, td=128):
    T = hf.shape[0]
    nsteps = T // td
    grid_spec = pltpu.PrefetchScalarGridSpec(
        num_scalar_prefetch=3,
        grid=(nsteps,),
        in_specs=[pl.BlockSpec((td * TOP_K,), lambda i, *_: (i,), memory_space=pltpu.SMEM),
                  pl.BlockSpec(memory_space=pl.ANY)],
        out_specs=pl.BlockSpec(memory_space=pl.ANY),
        scratch_shapes=[pltpu.VMEM((ZERO_ROWS, D_MODEL), F32),
                        pltpu.SemaphoreType.DMA((2,)), pltpu.SemaphoreType.DMA(())],
    )
    return pl.pallas_call(
        functools.partial(_dispatch_kernel, td=td, nblk=n_slots // MOE_BLOCK, nsteps=nsteps),
        grid_spec=grid_spec,
        out_shape=jax.ShapeDtypeStruct((n_slots, D_MODEL), F32),
        compiler_params=_cparams("arbitrary"),
        name="dispatch",
    )(padpos, padcnt, n_used, dest_flat, hf)


def _moe_kernel(be_ref, nu_ref, x_ref, wg_ref, bg_ref, wu_ref, bu_ref, wd_ref, bd_ref, o_ref):
    used = pl.program_id(0) < nu_ref[0]

    @pl.when(jnp.logical_not(used))
    def _():
        o_ref[...] = jnp.zeros_like(o_ref)

    @pl.when(used)
    def _():
        x = x_ref[...].astype(BF16)
        gt = jnp.minimum(_dot(x, wg_ref[...]) + bg_ref[...], SWIGLU_LIMIT)
        up = jnp.clip(_dot(x, wu_ref[...]) + bu_ref[...], -SWIGLU_LIMIT, SWIGLU_LIMIT)
        act = (up + 1.0) * (gt * _sigmoid(SWIGLU_ALPHA * gt))
        o_ref[...] = _dot(act.astype(BF16), wd_ref[...]) + bd_ref[...]


def _moe(xs, block_expert, n_used, wg, bg, wu, bu, wd, bd):
    n_slots = xs.shape[0]
    nblk = n_slots // MOE_BLOCK
    blk = lambda i, be, nu: (jnp.minimum(i, nu[0] - 1), 0)
    exp = lambda i, be, nu: (be[jnp.minimum(i, nu[0] - 1)], 0, 0)
    wspec = pl.BlockSpec((None, D_MODEL, D_MODEL), exp)
    bspec = pl.BlockSpec((None, 1, D_MODEL), exp)
    grid_spec = pltpu.PrefetchScalarGridSpec(
        num_scalar_prefetch=2,
        grid=(nblk,),
        in_specs=[pl.BlockSpec((MOE_BLOCK, D_MODEL), blk), wspec, bspec, wspec, bspec, wspec, bspec],
        out_specs=pl.BlockSpec((MOE_BLOCK, D_MODEL), lambda i, be, nu: (i, 0)),
    )
    return pl.pallas_call(
        _moe_kernel,
        grid_spec=grid_spec,
        out_shape=jax.ShapeDtypeStruct((n_slots, D_MODEL), F32),
        compiler_params=_cparams("arbitrary"),
        name="moe",
    )(block_expert, n_used, xs, wg, bg, wu, bu, wd, bd)


def _combine_kernel(dcur_ref, dnext_ref, x1_ref, r_ref, g_ref, ys_ref, o_ref, buf, sem, *, tc, nsteps):
    i = pl.program_id(0)
    slot = lax.rem(i, 2)

    def gather(d_ref, s):
        def body(r, carry):
            for k in range(TOP_K):
                pltpu.make_async_copy(ys_ref.at[pl.ds(d_ref[r * TOP_K + k], 1)],
                                      buf.at[s, k, pl.ds(r, 1)], sem.at[s]).start()
            return carry

        lax.fori_loop(0, tc, body, 0, unroll=DMA_UNROLL)

    @pl.when(i == 0)
    def _():
        gather(dcur_ref, slot)

    @pl.when(i + 1 < nsteps)
    def _():
        gather(dnext_ref, 1 - slot)

    for k in range(TOP_K):
        pltpu.make_async_copy(ys_ref.at[pl.ds(0, tc)], buf.at[slot, k], sem.at[slot]).wait()
    gates = r_ref[...]
    acc = x1_ref[...]
    for k in range(TOP_K):
        acc += gates[:, k:k + 1] * buf[slot, k]
    o_ref[...] = _rms(acc, g_ref[...])


def _combine(x1, route, dest_flat, ys, g_final, tc=128):
    T = x1.shape[0]
    nsteps = T // tc
    return pl.pallas_call(
        functools.partial(_combine_kernel, tc=tc, nsteps=nsteps),
        grid=(nsteps,),
        in_specs=[pl.BlockSpec((tc * TOP_K,), lambda i: (i,), memory_space=pltpu.SMEM),
                  pl.BlockSpec((tc * TOP_K,), lambda i: (jnp.minimum(i + 1, nsteps - 1),),
                               memory_space=pltpu.SMEM),
                  pl.BlockSpec((tc, D_MODEL), lambda i: (i, 0)),
                  pl.BlockSpec((tc, LANES), lambda i: (i, 0)),
                  pl.BlockSpec((1, D_MODEL), lambda i: (0, 0)),
                  pl.BlockSpec(memory_space=pl.ANY)],
        out_specs=pl.BlockSpec((tc, D_MODEL), lambda i: (i, 0)),
        out_shape=jax.ShapeDtypeStruct((T, D_MODEL), F32),
        scratch_shapes=[pltpu.VMEM((2, TOP_K, tc, D_MODEL), F32), pltpu.SemaphoreType.DMA((2,))],
        compiler_params=_cparams("arbitrary"),
        name="combine",
    )(dest_flat, dest_flat, x1, route, g_final, ys)


def _moe_layer(x1, hf, route, ew, g_final):
    T = x1.shape[0]
    n_assign = T * TOP_K
    nblk = n_assign // MOE_BLOCK + N_EXPERTS
    n_slots = nblk * MOE_BLOCK
    dest, cnt, pst = _route(route)
    cnt_i = cnt[0, :N_EXPERTS].astype(I32)
    pstart = pst[0, :N_EXPERTS].astype(I32)
    padded = (cnt_i + MOE_BLOCK - 1) // MOE_BLOCK * MOE_BLOCK
    pend = pstart + padded
    block_expert = jnp.minimum(
        jnp.sum(jnp.arange(nblk, dtype=I32)[:, None] * MOE_BLOCK >= pend[None, :], axis=1),
        N_EXPERTS - 1).astype(I32)
    n_used = (pend[-1] // MOE_BLOCK).reshape(1).astype(I32)
    dest_flat = dest[:, :TOP_K].reshape(n_assign)
    xs = _dispatch(hf, dest_flat, pstart + cnt_i, padded - cnt_i, n_used, n_slots)
    ys = _moe(xs, block_expert, n_used, *ew)
    return _combine(x1, route, dest_flat, ys, g_final)


def _trunk(x, p, bb):
    B, L, D = x.shape
    T = B * L
    x2d = x.reshape(T, D)
    v, x1g, x2g, u = _inproj(x2d, L, p["g_mix"], p["w_in"], p["b_in"], p["short_w"], p["short_b"])
    ya = _hyena(v, x1g, x2g, B, L, p["filt"], p["filt_bias"], bb)
    ybp = _s5(u, B, L, p["s5"])
    x1, hf, route = _mix(x2d, ya, ybp, p["w_glu"], p["b_glu"], p["g_hyena"], p["g_s5"], p["w_out"],
                         p["g_ffn"], p["w_router"], p["b_router"])
    y = _moe_layer(x1, hf, route, p["experts"], p["g_final"])
    return y.reshape(B, L, D)


def kernel(x_prompt, x_sample, g_mix, w_in, b_in, short_w, short_b, filt_w1, filt_b1, filt_w2, filt_b2, filt_w3, filt_b3, filt_freq, filt_bias, s5_lam_re, s5_lam_im, s5_log_dt, s5_b_re, s5_b_im, s5_c_re, s5_c_im, s5_d, w_glu, b_glu, g_hyena, g_s5, w_out, g_ffn, w_router, b_router, w_gate, b_gate, w_up, b_up, w_down, b_down, g_final):
    assert g_mix.shape[0] == 1, "one encoder layer"
    row = lambda a: a[0][None, :].astype(F32)
    p = dict(
        g_mix=row(g_mix), w_in=w_in[0].astype(BF16), b_in=row(b_in),
        short_w=short_w[0].astype(F32), short_b=row(short_b),
        filt=(filt_w1[0], filt_b1[0], filt_w2[0], filt_b2[0], filt_w3[0], filt_b3[0], filt_freq[0]),
        filt_bias=filt_bias[0].astype(F32),
        s5=_s5_operators(s5_lam_re[0], s5_lam_im[0], s5_log_dt[0], s5_b_re[0], s5_b_im[0],
                         s5_c_re[0], s5_c_im[0], s5_d[0]),
        w_glu=w_glu[0].astype(BF16), b_glu=row(b_glu), g_hyena=row(g_hyena), g_s5=row(g_s5),
        w_out=w_out[0].astype(BF16), g_ffn=row(g_ffn),
        w_router=jnp.pad(w_router[0].astype(F32), ((0, 0), (0, LANES - N_EXPERTS))),
        b_router=jnp.pad(b_router[0].astype(F32), (0, LANES - N_EXPERTS))[None, :],
        experts=(w_gate[0].astype(BF16), b_gate[0][:, None, :], w_up[0].astype(BF16),
                 b_up[0][:, None, :], w_down[0].astype(BF16), b_down[0][:, None, :]),
        g_final=g_final[None, :].astype(F32),
    )
    y_prompt = _trunk(x_prompt, p, bb=1)
    y_sample = _trunk(x_sample, p, bb=4)
    return (y_prompt, y_sample)
```

```python
import functools
import math

import numpy as np
import jax
import jax.numpy as jnp
from jax import lax
from jax.experimental import pallas as pl
from jax.experimental.pallas import tpu as pltpu

F32 = jnp.float32
BF16 = jnp.bfloat16
I32 = jnp.int32
HIGHEST = lax.Precision.HIGHEST

LANES = 128
SUBLANES = 8
D_MODEL = 1024
C_HY = 512
C_S5 = 512
D_IN = 3 * C_HY + C_S5
Q_HY = C_HY // LANES
FILTER_BANDS = 16
FILTER_WIDTH = 64
Z_PAD = 128
DECAY_MAX = math.log(1e-2) / 0.3
DECAY_MIN = math.log(1e-2) / 1.5
S5_G, S5_H, S5_P = 32, 16, 64
S5_CHUNK = 16
S5_UW = S5_CHUNK * S5_H
S5_SW = 2 * S5_P
S5_GB = LANES // S5_H
S5_NB = S5_G // S5_GB
S5_KW = S5_CHUNK * LANES
S5_XW = S5_GB * S5_SW
S5_STEP_ROWS = 512
LAMBDA_RE_MAX = -1e-4
N_EXPERTS = 32
TOP_K = 4
SWIGLU_LIMIT = 7.0
SWIGLU_ALPHA = 1.702
MOE_BLOCK = 512
RMS_EPS = 1e-6
VMEM_LIMIT_BYTES = 56 * 1024 * 1024


def _cparams(*sem):
    return pltpu.CompilerParams(dimension_semantics=sem, vmem_limit_bytes=VMEM_LIMIT_BYTES)


def _rms(x, g):
    return x * lax.rsqrt(jnp.mean(x * x, axis=-1, keepdims=True) + RMS_EPS) * g


def _sigmoid(x):
    return 1.0 / (1.0 + jnp.exp(-x))


def _dot(a, b):
    return jnp.dot(a, b, preferred_element_type=F32)


def _lane_block(x, q):
    return x[:, q * LANES:(q + 1) * LANES]


def _as_rows(ref):
    return ref.reshape(math.prod(ref.shape[:-1]), LANES)


def _inproj_kernel(x_ref, xp_ref, xn_ref, g_ref, w_ref, b_ref, sw_ref, sb_ref,
                   v_ref, x1_ref, x2_ref, u_ref, *, tm, seq):
    i = pl.program_id(0)
    g = g_ref[...]
    h = _rms(x_ref[...], g).astype(BF16)
    proj = _dot(h, w_ref[...]) + b_ref[...]
    u_ref[...] = proj[:, 3 * C_HY:]
    z = proj[:, :3 * C_HY]
    halo = jnp.concatenate([xp_ref[...], xn_ref[...]], axis=0)
    zh = _dot(_rms(halo, g).astype(BF16), w_ref[:, :3 * C_HY]) + b_ref[:, :3 * C_HY]
    row0 = i * tm
    has_prev = lax.rem(row0, seq) != 0
    has_next = lax.rem(row0 + tm, seq) != 0
    zp = jnp.where(has_prev, zh[SUBLANES - 1:SUBLANES], 0.0)
    zn = jnp.where(has_next, zh[SUBLANES:SUBLANES + 1], 0.0)
    rid = lax.broadcasted_iota(I32, (tm, 1), 0)
    zm1 = jnp.where(rid == 0, zp, pltpu.roll(z, 1, 0))
    zp1 = jnp.where(rid == tm - 1, zn, pltpu.roll(z, tm - 1, 0))
    sw = sw_ref[...]
    o = zm1 * sw[0:1] + z * sw[1:2] + zp1 * sw[2:3] + sb_ref[...]
    for q in range(Q_HY):
        v_ref[q] = _lane_block(o, q)
        x1_ref[q] = _lane_block(o, Q_HY + q)
        x2_ref[q] = _lane_block(o, 2 * Q_HY + q)


def _inproj(x2d, seq, g_mix, w_in_bf, b_in, short_w, short_b, tm=512):
    T = x2d.shape[0]
    nb8 = T // SUBLANES
    tb = tm // SUBLANES
    hy = jax.ShapeDtypeStruct((Q_HY, T, LANES), F32)
    hy_spec = pl.BlockSpec((Q_HY, tm, LANES), lambda i: (0, i, 0))
    const = lambda i: (0, 0)
    return pl.pallas_call(
        functools.partial(_inproj_kernel, tm=tm, seq=seq),
        grid=(T // tm,),
        in_specs=[
            pl.BlockSpec((tm, D_MODEL), lambda i: (i, 0)),
            pl.BlockSpec((SUBLANES, D_MODEL), lambda i: (jnp.maximum(i * tb - 1, 0), 0)),
            pl.BlockSpec((SUBLANES, D_MODEL), lambda i: (jnp.minimum((i + 1) * tb, nb8 - 1), 0)),
            pl.BlockSpec((1, D_MODEL), const),
            pl.BlockSpec((D_MODEL, D_IN), const),
            pl.BlockSpec((1, D_IN), const),
            pl.BlockSpec((3, 3 * C_HY), const),
            pl.BlockSpec((1, 3 * C_HY), const),
        ],
        out_specs=[hy_spec, hy_spec, hy_spec, pl.BlockSpec((tm, C_S5), lambda i: (i, 0))],
        out_shape=[hy, hy, hy, jax.ShapeDtypeStruct((T, C_S5), F32)],
        compiler_params=_cparams("arbitrary"),
        name="inproj",
    )(x2d, x2d, x2d, g_mix, w_in_bf, b_in, short_w, short_b)


def _filter_kernel(z_ref, w1_ref, b1_ref, w2_ref, b2_ref, w3_ref, b3_ref, fr_ref, dl_ref,
                   k_ref, s_ref, *, tr, seq):
    i = pl.program_id(0)
    z = z_ref[...]
    fr = fr_ref[...]
    h = jnp.sin(fr[0:1] * (jnp.dot(z, w1_ref[...], precision=HIGHEST,
                                   preferred_element_type=F32) + b1_ref[...]))
    h = jnp.sin(fr[1:2] * (jnp.dot(h, w2_ref[...], precision=HIGHEST,
                                   preferred_element_type=F32) + b2_ref[...]))
    h = jnp.dot(h, w3_ref[...], precision=HIGHEST, preferred_element_type=F32) + b3_ref[...]
    h = h * jnp.exp(-z[:, 0:1] * dl_ref[...])
    rid = i * tr + lax.broadcasted_iota(I32, (tr, 1), 0)
    h = jnp.where(rid == seq, 0.0, h)
    for q in range(2 * Q_HY):
        k_ref[q] = _lane_block(h, q)

    @pl.when(i == 0)
    def _():
        s_ref[...] = jnp.zeros_like(s_ref)

    s_ref[...] += jnp.sum(jnp.abs(h), axis=0, keepdims=True)


def _filter_taps(seq, w1, b1, w2, b2, w3, b3, freq, tr=512):
    L = seq
    N = 2 * L
    n = jnp.arange(N, dtype=I32)
    pos = jnp.where(n < L, n, jnp.where(n == L, 0, N - n)).astype(F32)
    t = (pos * (1.0 / (L - 1)))[:, None]
    w = 2.0 * math.pi * pos / L
    bands = jnp.linspace(1e-4, FILTER_BANDS - 1, FILTER_BANDS, dtype=F32)
    ang = w[:, None] * bands[None, :]
    zc = jnp.concatenate([t, jnp.cos(ang), -jnp.sin(ang),
                          jnp.zeros((N, Z_PAD - 1 - 2 * FILTER_BANDS), F32)], axis=-1)
    w1p = jnp.pad(w1, ((0, Z_PAD - w1.shape[0]), (0, 0)))
    w3d = w3.reshape(FILTER_WIDTH, 2, 2 * C_HY).transpose(1, 0, 2)
    b3d = b3.reshape(2, 1, 2 * C_HY)
    deltas = jnp.abs(jnp.linspace(DECAY_MIN, DECAY_MAX, C_HY, dtype=F32))
    dl = jnp.tile(deltas, 2)[None, :]
    half = (N // tr) // 2
    const = lambda i: (0, 0)
    taps, sums = pl.pallas_call(
        functools.partial(_filter_kernel, tr=tr, seq=L),
        grid=(N // tr,),
        in_specs=[
            pl.BlockSpec((tr, Z_PAD), lambda i: (i, 0)),
            pl.BlockSpec((Z_PAD, FILTER_WIDTH), const),
            pl.BlockSpec((1, FILTER_WIDTH), const),
            pl.BlockSpec((FILTER_WIDTH, FILTER_WIDTH), const),
            pl.BlockSpec((1, FILTER_WIDTH), const),
            pl.BlockSpec((None, FILTER_WIDTH, 2 * C_HY), lambda i: (i // half, 0, 0)),
            pl.BlockSpec((None, 1, 2 * C_HY), lambda i: (i // half, 0, 0)),
            pl.BlockSpec((2, FILTER_WIDTH), const),
            pl.BlockSpec((1, 2 * C_HY), const),
        ],
        out_specs=[pl.BlockSpec((2 * Q_HY, tr, LANES), lambda i: (0, i, 0)),
                   pl.BlockSpec((1, 2 * C_HY), const)],
        out_shape=[jax.ShapeDtypeStruct((2 * Q_HY, N, LANES), F32),
                   jax.ShapeDtypeStruct((1, 2 * C_HY), F32)],
        compiler_params=_cparams("arbitrary"),
        name="hyena_filter",
    )(zc, w1p, b1[None, :], w2, b2[None, :], w3d, b3d, freq, dl)
    return taps, sums


def _split_n(N):
    n1 = {32768: 128, 4096: 64}.get(N)
    if n1 is None:
        n1 = 1 << (int(math.log2(N)) // 2)
    return n1, N // n1


def _dft_constants(N1, N2):
    N = N1 * N2
    k1 = np.arange(N1)[:, None]
    n1 = np.arange(N1)[None, :]
    ang = 2.0 * np.pi * ((k1 * n1) % N1) / N1
    fa_full = np.concatenate([np.cos(ang), -np.sin(ang)], axis=0)
    fa_half = fa_full[:, :N1 // 2]
    fa_inv = np.concatenate([np.cos(ang), -np.sin(ang)], axis=1)[:N1 // 2] / N
    k2 = np.arange(N2)[:, None]
    n2 = np.arange(N2)[None, :]
    a2 = 2.0 * np.pi * ((k2 * n2) % N2) / N2
    cr, ci = np.cos(a2), -np.sin(a2)
    f2 = np.block([[cr, -ci], [ci, cr]])
    g2 = np.block([[cr, ci], [-ci, cr]])
    to = lambda a: jnp.asarray(a.astype(np.float32)).astype(BF16)
    kk = lax.broadcasted_iota(I32, (N1, N2, LANES), 0)
    nn = lax.broadcasted_iota(I32, (N1, N2, LANES), 1)
    ta = (2.0 * math.pi / N) * lax.rem(kk * nn, N).astype(F32)
    tw = jnp.stack([jnp.cos(ta), -jnp.sin(ta)], axis=0)
    return dict(fa_full=to(fa_full), fa_half=to(fa_half), fa_inv=to(fa_inv), f2=to(f2), g2=to(g2),
                tw=tw)


def _dft_a_kernel(f_ref, x_ref, o_ref, *, rows):
    nq = x_ref.shape[0]
    x2d = _as_rows(x_ref)
    f = f_ref[...]
    for j in range(SUBLANES):
        xs = jnp.concatenate(
            [x2d[pl.ds(q * rows * SUBLANES + j, rows, stride=SUBLANES), :] for q in range(nq)], axis=1)
        r = _dot(f, xs.astype(BF16))
        for q in range(nq):
            o_ref[q, j] = _lane_block(r, q)


def _dft_a(f, x):
    Q, Bt, K, N2, _ = x.shape
    R = f.shape[0]
    return pl.pallas_call(
        functools.partial(_dft_a_kernel, rows=K),
        grid=(Bt, N2 // SUBLANES, Q // Q_HY),
        in_specs=[pl.BlockSpec((R, K), lambda b, j, w: (0, 0)),
                  pl.BlockSpec((Q_HY, None, K, SUBLANES, LANES), lambda b, j, w: (w, b, 0, j, 0))],
        out_specs=pl.BlockSpec((Q_HY, None, SUBLANES, R, LANES), lambda b, j, w: (w, b, j, 0, 0)),
        out_shape=jax.ShapeDtypeStruct((Q, Bt, N2, R, LANES), F32),
        compiler_params=_cparams("arbitrary", "arbitrary", "arbitrary"),
        name="dft_a",
    )(f, x)


def _dft_a_inv_kernel(f_ref, z_ref, gate_ref, v_ref, bias_ref, o_ref, *, rows):
    g2d = _as_rows(gate_ref)
    v2d = _as_rows(v_ref)
    o2d = _as_rows(o_ref)
    f = f_ref[...]
    bias = bias_ref[...]
    for j in range(SUBLANES):
        zj = jnp.concatenate([z_ref[q, j] for q in range(Q_HY)], axis=1)
        y = _dot(f, zj.astype(BF16))
        for q in range(Q_HY):
            sl = pl.ds(q * rows * SUBLANES + j, rows, stride=SUBLANES)
            o2d[sl, :] = g2d[sl, :] * (_lane_block(y, q) + v2d[sl, :] * _lane_block(bias, q))


def _dft_a_inv(f, z, gate, v, bias_row):
    Q, Bt, N2, R2, _ = z.shape
    K = f.shape[0]
    nat = pl.BlockSpec((Q, None, K, SUBLANES, LANES), lambda b, j: (0, b, 0, j, 0))
    return pl.pallas_call(
        functools.partial(_dft_a_inv_kernel, rows=K),
        grid=(Bt, N2 // SUBLANES),
        in_specs=[pl.BlockSpec((K, R2), lambda b, j: (0, 0)),
                  pl.BlockSpec((Q, None, SUBLANES, R2, LANES), lambda b, j: (0, b, j, 0, 0)),
                  nat, nat,
                  pl.BlockSpec((1, C_HY), lambda b, j: (0, 0))],
        out_specs=nat,
        out_shape=jax.ShapeDtypeStruct((Q, Bt, K, N2, LANES), F32),
        compiler_params=_cparams("arbitrary", "arbitrary"),
        name="dft_a_inv",
    )(f, z, gate, v, bias_row)


C_STEP_Q = 2
K1_STEP = SUBLANES


def _dft_c_kernel(a_ref, k_ref, tw_ref, f_ref, g_ref, z_ref, *, bb, n2):
    per = n2 * 2 * K1_STEP
    a2d = _as_rows(a_ref)
    z2d = _as_rows(z_ref)
    cols = [(b, q) for b in range(bb) for q in range(C_STEP_Q)]

    def rows(b, q, kk, im):
        return pl.ds((q * bb + b) * per + im * K1_STEP + kk, n2, stride=2 * K1_STEP)

    for kk in range(K1_STEP):
        twr = jnp.tile(tw_ref[0, kk], (1, len(cols)))
        twi = jnp.tile(tw_ref[1, kk], (1, len(cols)))
        kr = jnp.tile(k_ref[0, kk], (1, bb))
        ki = jnp.tile(k_ref[1, kk], (1, bb))
        ar = jnp.concatenate([a2d[rows(b, q, kk, 0), :] for b, q in cols], axis=1)
        ai = jnp.concatenate([a2d[rows(b, q, kk, 1), :] for b, q in cols], axis=1)
        pr = ar * twr - ai * twi
        pi = ar * twi + ai * twr
        x = _dot(f_ref[...], jnp.concatenate([pr, pi], axis=0).astype(BF16))
        xr = x[:n2]
        xi = x[n2:]
        yr = xr * kr - xi * ki
        yi = xr * ki + xi * kr
        zz = _dot(g_ref[...], jnp.concatenate([yr, yi], axis=0).astype(BF16))
        zr = zz[:n2]
        zi = zz[n2:]
        outr = zr * twr + zi * twi
        outi = zi * twr - zr * twi
        for c, (b, q) in enumerate(cols):
            z2d[rows(b, q, kk, 0), :] = _lane_block(outr, c)
            z2d[rows(b, q, kk, 1), :] = _lane_block(outi, c)


def _dft_c(a6, khat, order, tw, f2, g2, bb):
    Q, Bt, N2, _, N1, _ = a6.shape
    cw = C_STEP_Q * LANES
    ablk = pl.BlockSpec((C_STEP_Q, bb, N2, 2, K1_STEP, LANES), lambda g, c, b: (c, b, 0, 0, g, 0))
    return pl.pallas_call(
        functools.partial(_dft_c_kernel, bb=bb, n2=N2),
        grid=(N1 // K1_STEP, Q // C_STEP_Q, Bt // bb),
        in_specs=[ablk,
                  pl.BlockSpec((2, K1_STEP, N2, cw), lambda g, c, b: (0, g, 0, order * (C_HY // cw) + c)),
                  pl.BlockSpec((2, K1_STEP, N2, LANES), lambda g, c, b: (0, g, 0, 0)),
                  pl.BlockSpec((2 * N2, 2 * N2), lambda g, c, b: (0, 0)),
                  pl.BlockSpec((2 * N2, 2 * N2), lambda g, c, b: (0, 0))],
        out_specs=ablk,
        out_shape=jax.ShapeDtypeStruct(a6.shape, F32),
        compiler_params=_cparams("arbitrary", "arbitrary", "arbitrary"),
        name="dft_c",
    )(a6, khat, tw, f2, g2)


def _dft_c_filter_kernel(a_ref, tw_ref, f_ref, s_ref, o_ref, *, n2):
    per = n2 * 2 * K1_STEP
    a2d = _as_rows(a_ref)
    sc = 1.0 / (s_ref[...] + 1e-6)
    for kk in range(K1_STEP):
        twr = jnp.tile(tw_ref[0, kk], (1, C_STEP_Q))
        twi = jnp.tile(tw_ref[1, kk], (1, C_STEP_Q))
        ar = jnp.concatenate([a2d[pl.ds(q * per + kk, n2, stride=2 * K1_STEP), :]
                              for q in range(C_STEP_Q)], axis=1)
        ai = jnp.concatenate([a2d[pl.ds(q * per + K1_STEP + kk, n2, stride=2 * K1_STEP), :]
                              for q in range(C_STEP_Q)], axis=1)
        pr = ar * twr - ai * twi
        pi = ar * twi + ai * twr
        x = _dot(f_ref[...], jnp.concatenate([pr, pi], axis=0).astype(BF16))
        o_ref[0, kk] = x[:n2] * sc
        o_ref[1, kk] = x[n2:] * sc


def _dft_c_filter(a5, tw, f2, sums):
    Q, N2, _, N1, _ = a5.shape
    cw = C_STEP_Q * LANES
    return pl.pallas_call(
        functools.partial(_dft_c_filter_kernel, n2=N2),
        grid=(N1 // K1_STEP, Q // C_STEP_Q),
        in_specs=[pl.BlockSpec((C_STEP_Q, N2, 2, K1_STEP, LANES), lambda g, c: (c, 0, 0, g, 0)),
                  pl.BlockSpec((2, K1_STEP, N2, LANES), lambda g, c: (0, g, 0, 0)),
                  pl.BlockSpec((2 * N2, 2 * N2), lambda g, c: (0, 0)),
                  pl.BlockSpec((1, cw), lambda g, c: (0, c))],
        out_specs=pl.BlockSpec((2, K1_STEP, N2, cw), lambda g, c: (0, g, 0, c)),
        out_shape=jax.ShapeDtypeStruct((2, N1, N2, Q * LANES), F32),
        compiler_params=_cparams("arbitrary", "arbitrary"),
        name="dft_c_filter",
    )(a5, tw, f2, sums)


def _hyena(v, x1, x2, B, seq, filt, filt_bias, bb):
    L = seq
    N = 2 * L
    N1, N2 = _split_n(N)
    cst = _dft_constants(N1, N2)
    taps, sums = _filter_taps(L, *filt)
    ka = _dft_a(cst["fa_full"], taps.reshape(2 * Q_HY, 1, N1, N2, LANES))
    khat = _dft_c_filter(ka.reshape(2 * Q_HY, N2, 2, N1, LANES), cst["tw"], cst["f2"], sums)
    nat = lambda a: a.reshape(Q_HY, B, N1 // 2, N2, LANES)
    cur = nat(v)
    for o, gate in enumerate((x1, x2)):
        a = _dft_a(cst["fa_half"], cur)
        z = _dft_c(a.reshape(Q_HY, B, N2, 2, N1, LANES), khat, o, cst["tw"], cst["f2"], cst["g2"], bb)
        cur = _dft_a_inv(cst["fa_inv"], z.reshape(Q_HY, B, N2, 2 * N1, LANES), nat(gate), cur,
                         filt_bias[o][None, :])
    return cur.reshape(Q_HY, B * L, LANES)


def _s5_operators(lam_re, lam_im, log_dt, b_re, b_im, c_re, c_im, d_skip):
    Tc, G, H, P = S5_CHUNK, S5_G, S5_H, S5_P
    lam = jnp.minimum(lam_re.astype(F32), LAMBDA_RE_MAX) + 1j * lam_im.astype(F32)
    dt = jnp.exp(log_dt.astype(F32))[..., None]
    lam_dt = lam * dt
    lam_bar = jnp.exp(lam_dt)
    b_bar = ((lam_bar - 1.0) / lam)[..., None] * (b_re.astype(F32) + 1j * b_im.astype(F32))
    c = c_re.astype(F32) + 1j * c_im.astype(F32)
    tau = jnp.arange(Tc + 1, dtype=F32)
    pw = jnp.exp(lam_dt[None] * tau[:, None, None, None])
    kk = jnp.einsum('dghp,tdgp,dgpk->dtghk', c, pw[:Tc], b_bar).real
    i = jnp.arange(Tc)
    lag = i[:, None] - i[None, :]
    kf = jnp.where((lag >= 0)[:, :, None, None, None], kk[0][jnp.clip(lag, 0, Tc - 1)], 0.0)
    kb = jnp.where((lag <= 0)[:, :, None, None, None], kk[1][jnp.clip(-lag, 0, Tc - 1)], 0.0)
    m = kf + kb
    eye = (lag == 0)[:, :, None, None, None] * jnp.eye(H, dtype=F32)[None, None, None]
    m = m + eye * d_skip.astype(F32).reshape(G, H)[None, None, :, :, None]
    clf = c[0][None] * pw[1:Tc + 1, 0][:, :, None, :]
    clb = c[1][None] * pw[Tc - i, 1][:, :, None, :]
    sf = pw[Tc - 1 - i, 0][:, :, :, None] * b_bar[0][None]
    sb = pw[i, 1][:, :, :, None] * b_bar[1][None]
    w_intra = m.transpose(2, 1, 4, 0, 3).reshape(G, Tc * H, Tc * H)
    st2y = lambda cl: jnp.concatenate([cl.real, -cl.imag], axis=-1).transpose(1, 3, 0, 2) \
        .reshape(G, 2 * P, Tc * H)
    u2s = lambda s: jnp.concatenate([s.real, s.imag], axis=2).transpose(1, 0, 3, 2) \
        .reshape(G, Tc * H, 2 * P)
    w_state = jnp.concatenate([u2s(sf), u2s(sb)], axis=-1)
    lam16 = pw[Tc]
    sc_a = jnp.concatenate([lam16.real, lam16.real], axis=-1).reshape(2, G * 2 * P)
    sc_b = jnp.concatenate([-lam16.imag, lam16.imag], axis=-1).reshape(2, G * 2 * P)
    col = np.arange(S5_KW)
    spread = (np.arange(S5_UW)[:, None] == (col // LANES * S5_H + col % S5_H)[None, :])
    return dict(w_intra=w_intra.astype(BF16), w_xf=st2y(clf).astype(BF16), w_xb=st2y(clb).astype(BF16),
                w_state=w_state.astype(BF16), sc_a=sc_a, sc_b=sc_b,
                spread=jnp.asarray(spread.astype(np.float32)).astype(BF16))


def _s5_rows(i, cl, B, ncc):
    return pl.ds(S5_CHUNK * cl + i, B, stride=S5_CHUNK * ncc)


def _s5_chunk_inputs(u2d, B, ncc):
    cols = []
    for i in range(S5_CHUNK):
        if B == 1:
            cols.append(u2d[pl.ds(i, ncc, stride=S5_CHUNK), :])
        else:
            cols.append(jnp.concatenate([u2d[_s5_rows(i, cl, B, ncc), :] for cl in range(ncc)], axis=0))
    return jnp.concatenate(cols, axis=1).astype(BF16)


def _s5_state_kernel(u_ref, w_ref, sf_ref, sb_ref, wblk, *, B, ncc):
    @pl.when(pl.program_id(1) == 0)
    def _():
        wblk[...] = jnp.zeros_like(wblk)
        for a in range(S5_GB):
            for j in range(S5_CHUNK):
                r0 = j * LANES + a * S5_H
                for d in range(2):
                    c0 = d * S5_XW + a * S5_SW
                    wblk[r0:r0 + S5_H, c0:c0 + S5_SW] = \
                        w_ref[a, j * S5_H:(j + 1) * S5_H, d * S5_SW:(d + 1) * S5_SW]

    lhs = _s5_chunk_inputs(_as_rows(u_ref), B, ncc)
    r = _dot(lhs, wblk[...])
    sf_ref[...] = r[:, :S5_XW]
    sb_ref[...] = r[:, S5_XW:]


def _s5_state(u3, w_state, ncc):
    B, L, _ = u3.shape
    nch = L // S5_CHUNK
    out = jax.ShapeDtypeStruct((nch * B, S5_NB * S5_XW), F32)
    return pl.pallas_call(
        functools.partial(_s5_state_kernel, B=B, ncc=ncc),
        grid=(S5_NB, nch // ncc),
        in_specs=[pl.BlockSpec((B, S5_CHUNK * ncc, LANES), lambda q, t: (0, t, q)),
                  pl.BlockSpec((S5_GB, S5_UW, 2 * S5_SW), lambda q, t: (q, 0, 0))],
        out_specs=[pl.BlockSpec((ncc * B, S5_XW), lambda q, t: (t, q))] * 2,
        out_shape=[out, out],
        scratch_shapes=[pltpu.VMEM((S5_KW, 2 * S5_XW), BF16)],
        compiler_params=_cparams("arbitrary", "arbitrary"),
        name="s5_state",
    )(u3, w_state)


def _s5_scan_kernel(sf_ref, sb_ref, a_ref, b_ref, xf_ref, xb_ref, *, nch, rows):
    lb = sf_ref.shape[1]
    vr = max(rows, SUBLANES)
    lane = lax.broadcasted_iota(I32, (vr, lb), 1)
    first = (lane & S5_P) == 0

    def swap(x):
        return jnp.where(first, pltpu.roll(x, lb - S5_P, 1), pltpu.roll(x, S5_P, 1))

    af, ab = a_ref[0], a_ref[1]
    bf, bk = b_ref[0], b_ref[1]

    def body(c, carry):
        xf, xb = carry
        rf = pl.ds(pl.multiple_of(c * rows, rows), rows)
        xf_ref[rf, :] = xf[:rows]
        xf = af * xf + bf * swap(xf) + sf_ref[rf, :]
        rb = pl.ds(pl.multiple_of((nch - 1 - c) * rows, rows), rows)
        xb_ref[rb, :] = xb[:rows]
        xb = ab * xb + bk * swap(xb) + sb_ref[rb, :]
        return xf, xb

    zero = jnp.zeros((vr, lb), F32)
    lax.fori_loop(0, nch, body, (zero, zero))


def _s5_scan(sf, sb, sc_a, sc_b, nch, rows, lb):
    R, lanes = sf.shape
    blk = pl.BlockSpec((R, lb), lambda j: (0, j))
    cblk = pl.BlockSpec((2, 1, lb), lambda j: (0, 0, j))
    out = jax.ShapeDtypeStruct((R, lanes), F32)
    return pl.pallas_call(
        functools.partial(_s5_scan_kernel, nch=nch, rows=rows),
        grid=(lanes // lb,),
        in_specs=[blk, blk, cblk, cblk],
        out_specs=[blk, blk],
        out_shape=[out, out],
        compiler_params=_cparams("arbitrary"),
        name="s5_scan",
    )(sf, sb, sc_a[:, None, :], sc_b[:, None, :])


def _s5_out_kernel(u_ref, xf_ref, xb_ref, wm_ref, wf_ref, wb_ref, e_ref, y_ref, wm_blk, wf_blk, wb_blk,
                   *, B, ncc):
    @pl.when(pl.program_id(1) == 0)
    def _():
        lane = lax.broadcasted_iota(I32, (1, S5_KW), 1)
        slot = (lane // S5_H) % S5_GB
        e = e_ref[...]
        for a in range(S5_GB):
            own = slot == a
            ex = jnp.where(own, _dot(wm_ref[a], e), 0.0).astype(BF16)
            for j in range(S5_CHUNK):
                r0 = j * LANES + a * S5_H
                wm_blk[r0:r0 + S5_H, :] = ex[j * S5_H:(j + 1) * S5_H, :]
            wf_blk[a * S5_SW:(a + 1) * S5_SW, :] = jnp.where(own, _dot(wf_ref[a], e), 0.0).astype(BF16)
            wb_blk[a * S5_SW:(a + 1) * S5_SW, :] = jnp.where(own, _dot(wb_ref[a], e), 0.0).astype(BF16)

    lhs = _s5_chunk_inputs(_as_rows(u_ref), B, ncc)
    acc = _dot(lhs, wm_blk[...])
    acc += _dot(xf_ref[...].astype(BF16), wf_blk[...])
    acc += _dot(xb_ref[...].astype(BF16), wb_blk[...])
    y2d = _as_rows(y_ref)
    for i in range(S5_CHUNK):
        piece = _lane_block(acc, i)
        if B == 1:
            y2d[pl.ds(i, ncc, stride=S5_CHUNK), :] = piece
        else:
            for cl in range(ncc):
                y2d[_s5_rows(i, cl, B, ncc), :] = piece[cl * B:(cl + 1) * B]


def _s5_out(u3, xf, xb, ops, ncc):
    B, L, _ = u3.shape
    nch = L // S5_CHUNK
    tok = pl.BlockSpec((B, S5_CHUNK * ncc, LANES), lambda q, t: (0, t, q))
    st = pl.BlockSpec((ncc * B, S5_XW), lambda q, t: (t, q))
    return pl.pallas_call(
        functools.partial(_s5_out_kernel, B=B, ncc=ncc),
        grid=(S5_NB, nch // ncc),
        in_specs=[tok, st, st,
                  pl.BlockSpec((S5_GB, S5_UW, S5_UW), lambda q, t: (q, 0, 0)),
                  pl.BlockSpec((S5_GB, S5_SW, S5_UW), lambda q, t: (q, 0, 0)),
                  pl.BlockSpec((S5_GB, S5_SW, S5_UW), lambda q, t: (q, 0, 0)),
                  pl.BlockSpec((S5_UW, S5_KW), lambda q, t: (0, 0))],
        out_specs=tok,
        out_shape=jax.ShapeDtypeStruct(u3.shape, F32),
        scratch_shapes=[pltpu.VMEM((S5_KW, S5_KW), BF16), pltpu.VMEM((S5_XW, S5_KW), BF16),
                        pltpu.VMEM((S5_XW, S5_KW), BF16)],
        compiler_params=_cparams("arbitrary", "arbitrary"),
        name="s5_out",
    )(u3, xf, xb, ops["w_intra"], ops["w_xf"], ops["w_xb"], ops["spread"])


def _s5(u, B, seq, ops):
    nch = seq // S5_CHUNK
    u3 = u.reshape(B, seq, C_S5)
    ncc = min(nch, S5_STEP_ROWS // B)
    sf, sb = _s5_state(u3, ops["w_state"], ncc)
    lb = 1024 if B == 1 else 256
    xf, xb = _s5_scan(sf, sb, ops["sc_a"], ops["sc_b"], nch, B, lb)
    return _s5_out(u3, xf, xb, ops, ncc).reshape(B * seq, C_S5)


def _mix_kernel(x_ref, ya_ref, yb_ref, wglu_ref, bglu_ref, gh_ref, gs_ref, wout_ref, gffn_ref,
                wr_ref, br_ref, x1_ref, hf_ref, route_ref, *, tm):
    g = jax.nn.gelu(yb_ref[...])
    yb = g * _sigmoid(_dot(g.astype(BF16), wglu_ref[...]) + bglu_ref[...])
    ya = jnp.concatenate([ya_ref[q] for q in range(Q_HY)], axis=1)
    na = _rms(ya, gh_ref[...]).astype(BF16)
    nb = _rms(yb, gs_ref[...]).astype(BF16)
    mixed = _dot(na, wout_ref[:C_HY, :]) + _dot(nb, wout_ref[C_HY:, :])
    x1 = x_ref[...] + mixed
    x1_ref[...] = x1
    hf = _rms(x1, gffn_ref[...])
    hf_ref[...] = hf
    logits = jnp.dot(hf, wr_ref[...], precision=HIGHEST, preferred_element_type=F32) + br_ref[...]
    lane = lax.broadcasted_iota(I32, (tm, LANES), 1)
    neg = jnp.float32(-jnp.inf)
    l = jnp.where(lane < N_EXPERTS, logits, neg)
    vals, idxs = [], []
    for _ in range(TOP_K):
        m = jnp.max(l, axis=-1, keepdims=True)
        idx = jnp.min(jnp.where(l == m, lane, LANES), axis=-1, keepdims=True)
        vals.append(m)
        idxs.append(idx)
        l = jnp.where(lane == idx, neg, l)
    es = [jnp.exp(v - vals[0]) for v in vals]
    den = es[0] + es[1] + es[2] + es[3]
    route = jnp.zeros((tm, LANES), F32)
    for k in range(TOP_K):
        route = jnp.where(lane == k, es[k] / den, route)
        route = jnp.where(lane == TOP_K + k, idxs[k].astype(F32), route)
    route_ref[...] = route


def _mix(x2d, ya4, ybp, w_glu_bf, b_glu, g_hyena, g_s5, w_out_bf, g_ffn, w_router_p, b_router_p, tm=256):
    T = x2d.shape[0]
    const = lambda i: (0, 0)
    row = lambda w: pl.BlockSpec((tm, w), lambda i: (i, 0))
    return pl.pallas_call(
        functools.partial(_mix_kernel, tm=tm),
        grid=(T // tm,),
        in_specs=[row(D_MODEL), pl.BlockSpec((Q_HY, tm, LANES), lambda i: (0, i, 0)), row(C_S5),
                  pl.BlockSpec((C_S5, C_S5), const), pl.BlockSpec((1, C_S5), const),
                  pl.BlockSpec((1, C_HY), const), pl.BlockSpec((1, C_S5), const),
                  pl.BlockSpec((D_MODEL, D_MODEL), const), pl.BlockSpec((1, D_MODEL), const),
                  pl.BlockSpec((D_MODEL, LANES), const), pl.BlockSpec((1, LANES), const)],
        out_specs=[row(D_MODEL), row(D_MODEL), row(LANES)],
        out_shape=[jax.ShapeDtypeStruct((T, D_MODEL), F32), jax.ShapeDtypeStruct((T, D_MODEL), F32),
                   jax.ShapeDtypeStruct((T, LANES), F32)],
        compiler_params=_cparams("arbitrary"),
        name="mix",
    )(x2d, ya4, ybp, w_glu_bf, b_glu, g_hyena, g_s5, w_out_bf, g_ffn, w_router_p, b_router_p)


def _route_kernel(r_ref, dest_ref, cnt_ref, pst_ref, carry, pstart, *, tm):
    p = pl.program_id(0)
    i = pl.program_id(1)
    lane = lax.broadcasted_iota(I32, (tm, LANES), 1)
    r = r_ref[...]
    eids = [r[:, TOP_K + k:TOP_K + k + 1].astype(I32) for k in range(TOP_K)]
    oh = jnp.zeros((tm, LANES), F32)
    for e in eids:
        oh += (lane == e).astype(F32)
    tot = jnp.sum(oh, axis=0, keepdims=True)

    @pl.when((p == 0) & (i == 0))
    def _():
        carry[...] = jnp.zeros_like(carry)

    @pl.when((p == 1) & (i == 0))
    def _():
        cnt = carry[...]
        cnt_ref[...] = cnt
        padded = jnp.floor((cnt + (MOE_BLOCK - 1)) * (1.0 / MOE_BLOCK)) * MOE_BLOCK
        a = lax.broadcasted_iota(I32, (LANES, LANES), 0)
        b = lax.broadcasted_iota(I32, (LANES, LANES), 1)
        excl = jnp.dot(jnp.broadcast_to(padded, (SUBLANES, LANES)), (a < b).astype(F32),
                       precision=HIGHEST, preferred_element_type=F32)
        pstart[...] = excl[0:1]
        pst_ref[...] = excl[0:1]
        carry[...] = jnp.zeros_like(carry)

    @pl.when(p == 1)
    def _():
        a = lax.broadcasted_iota(I32, (tm, tm), 0)
        b = lax.broadcasted_iota(I32, (tm, tm), 1)
        before = _dot((b < a).astype(BF16), oh.astype(BF16))
        base = before + carry[...] + pstart[...]
        out = jnp.zeros((tm, LANES), F32)
        for k, e in enumerate(eids):
            d = jnp.sum(jnp.where(lane == e, base, 0.0), axis=-1, keepdims=True)
            out = jnp.where(lane == k, d, out)
        dest_ref[...] = out.astype(I32)

    carry[...] += tot


def _route(route, tm=512):
    T = route.shape[0]
    one = jax.ShapeDtypeStruct((1, LANES), F32)
    return pl.pallas_call(
        functools.partial(_route_kernel, tm=tm),
        grid=(2, T // tm),
        in_specs=[pl.BlockSpec((tm, LANES), lambda p, i: (i, 0))],
        out_specs=[pl.BlockSpec((tm, LANES), lambda p, i: (i * p, 0)),
                   pl.BlockSpec((1, LANES), lambda p, i: (0, 0)),
                   pl.BlockSpec((1, LANES), lambda p, i: (0, 0))],
        out_shape=[jax.ShapeDtypeStruct((T, LANES), I32), one, one],
        scratch_shapes=[pltpu.VMEM((1, LANES), F32), pltpu.VMEM((1, LANES), F32)],
        compiler_params=_cparams("arbitrary", "arbitrary"),
        name="route",
    )(route)


ZERO_ROWS = 64
DMA_UNROLL = 4


RING = 3


def _dispatch_kernel(padpos_ref, padcnt_ref, nu_ref, dest_ref, h_ref, xs_ref, zbuf, hbuf, sem, lsem, zsem,
                     *, td, nblk, nsteps):
    i = pl.program_id(0)

    @pl.when(i == 0)
    def _():
        zbuf[...] = jnp.zeros_like(zbuf)

        def per_expert(e, carry):
            off = padpos_ref[e]

            def zero_row(r):
                return pltpu.make_async_copy(zbuf.at[pl.ds(0, 1)], xs_ref.at[pl.ds(off + r, 1)], zsem)

            def z_issue(r, c):
                zero_row(r).start()
                return c

            def z_drain(r, c):
                zero_row(r).wait()
                return c

            lax.fori_loop(0, padcnt_ref[e], z_issue, 0)
            lax.fori_loop(0, padcnt_ref[e], z_drain, 0)
            return carry

        lax.fori_loop(0, N_EXPERTS, per_expert, 0)

        def zero_piece(j):
            row = pl.multiple_of(j * ZERO_ROWS, ZERO_ROWS)
            return pltpu.make_async_copy(zbuf, xs_ref.at[pl.ds(row, ZERO_ROWS)], zsem)

        per_blk = MOE_BLOCK // ZERO_ROWS

        def t_issue(j, c):
            zero_piece(j).start()
            return c

        def t_drain(j, c):
            zero_piece(j).wait()
            return c

        lax.fori_loop(nu_ref[0] * per_blk, nblk * per_blk, t_issue, 0)
        lax.fori_loop(nu_ref[0] * per_blk, nblk * per_blk, t_drain, 0)

    def load(step):
        s = lax.rem(step, RING)
        return pltpu.make_async_copy(h_ref.at[pl.ds(step * td, td)], hbuf.at[s], lsem.at[s])

    def wait_rows(step):
        s = lax.rem(step, RING)
        for _ in range(TOP_K):
            pltpu.make_async_copy(hbuf.at[s], xs_ref.at[pl.ds(0, td)], sem.at[s]).wait()

    @pl.when(i == 0)
    def _():
        load(0).start()
        if nsteps > 1:
            load(1).start()

    load(i).wait()
    slot = lax.rem(i, RING)

    def issue(r, carry):
        for k in range(TOP_K):
            pltpu.make_async_copy(hbuf.at[slot, pl.ds(r, 1)],
                                  xs_ref.at[pl.ds(dest_ref[r * TOP_K + k], 1)], sem.at[slot]).start()
        return carry

    lax.fori_loop(0, td, issue, 0, unroll=DMA_UNROLL)

    @pl.when(i > 0)
    def _():
        wait_rows(i - 1)

    @pl.when(i + 2 < nsteps)
    def _():
        load(i + 2).start()

    @pl.when(i == nsteps - 1)
    def _():
        wait_rows(i)


def _dispatch(hf, dest_flat, padpos, padcnt, n_used, n_slots, td=128):
    T = hf.shape[0]
    nsteps = T // td
    grid_spec = pltpu.PrefetchScalarGridSpec(
        num_scalar_prefetch=3,
        grid=(nsteps,),
        in_specs=[pl.BlockSpec((td * TOP_K,), lambda i, *_: (i,), memory_space=pltpu.SMEM),
                  pl.BlockSpec(memory_space=pl.ANY)],
        out_specs=pl.BlockSpec(memory_space=pl.ANY),
        scratch_shapes=[pltpu.VMEM((ZERO_ROWS, D_MODEL), F32), pltpu.VMEM((RING, td, D_MODEL), F32),
                        pltpu.SemaphoreType.DMA((RING,)), pltpu.SemaphoreType.DMA((RING,)),
                        pltpu.SemaphoreType.DMA(())],
    )
    return pl.pallas_call(
        functools.partial(_dispatch_kernel, td=td, nblk=n_slots // MOE_BLOCK, nsteps=nsteps),
        grid_spec=grid_spec,
        out_shape=jax.ShapeDtypeStruct((n_slots, D_MODEL), F32),
        compiler_params=_cparams("arbitrary"),
        name="dispatch",
    )(padpos, padcnt, n_used, dest_flat, hf)


def _moe_kernel(be_ref, nu_ref, x_ref, wg_ref, bg_ref, wu_ref, bu_ref, wd_ref, bd_ref, o_ref):
    used = pl.program_id(0) < nu_ref[0]

    @pl.when(jnp.logical_not(used))
    def _():
        o_ref[...] = jnp.zeros_like(o_ref)

    @pl.when(used)
    def _():
        x = x_ref[...].astype(BF16)
        gt = jnp.minimum(_dot(x, wg_ref[...]) + bg_ref[...], SWIGLU_LIMIT)
        up = jnp.clip(_dot(x, wu_ref[...]) + bu_ref[...], -SWIGLU_LIMIT, SWIGLU_LIMIT)
        act = (up + 1.0) * (gt * _sigmoid(SWIGLU_ALPHA * gt))
        o_ref[...] = _dot(act.astype(BF16), wd_ref[...]) + bd_ref[...]


def _moe(xs, block_expert, n_used, wg, bg, wu, bu, wd, bd):
    n_slots = xs.shape[0]
    nblk = n_slots // MOE_BLOCK
    blk = lambda i, be, nu: (jnp.minimum(i, nu[0] - 1), 0)
    exp = lambda i, be, nu: (be[jnp.minimum(i, nu[0] - 1)], 0, 0)
    wspec = pl.BlockSpec((None, D_MODEL, D_MODEL), exp)
    bspec = pl.BlockSpec((None, 1, D_MODEL), exp)
    grid_spec = pltpu.PrefetchScalarGridSpec(
        num_scalar_prefetch=2,
        grid=(nblk,),
        in_specs=[pl.BlockSpec((MOE_BLOCK, D_MODEL), blk), wspec, bspec, wspec, bspec, wspec, bspec],
        out_specs=pl.BlockSpec((MOE_BLOCK, D_MODEL), lambda i, be, nu: (i, 0)),
    )
    return pl.pallas_call(
        _moe_kernel,
        grid_spec=grid_spec,
        out_shape=jax.ShapeDtypeStruct((n_slots, D_MODEL), F32),
        compiler_params=_cparams("arbitrary"),
        name="moe",
    )(block_expert, n_used, xs, wg, bg, wu, bu, wd, bd)


def _combine_kernel(dcur_ref, dnext_ref, x1_ref, r_ref, g_ref, ys_ref, o_ref, buf, sem, *, tc, nsteps):
    i = pl.program_id(0)
    slot = lax.rem(i, 2)

    def gather(d_ref, s):
        def body(r, carry):
            for k in range(TOP_K):
                pltpu.make_async_copy(ys_ref.at[pl.ds(d_ref[r * TOP_K + k], 1)],
                                      buf.at[s, k, pl.ds(r, 1)], sem.at[s]).start()
            return carry

        lax.fori_loop(0, tc, body, 0, unroll=DMA_UNROLL)

    @pl.when(i == 0)
    def _():
        gather(dcur_ref, slot)

    @pl.when(i + 1 < nsteps)
    def _():
        gather(dnext_ref, 1 - slot)

    for k in range(TOP_K):
        pltpu.make_async_copy(ys_ref.at[pl.ds(0, tc)], buf.at[slot, k], sem.at[slot]).wait()
    gates = r_ref[...]
    acc = x1_ref[...]
    for k in range(TOP_K):
        acc += gates[:, k:k + 1] * buf[slot, k]
    o_ref[...] = _rms(acc, g_ref[...])


def _combine(x1, route, dest_flat, ys, g_final, tc=128):
    T = x1.shape[0]
    nsteps = T // tc
    return pl.pallas_call(
        functools.partial(_combine_kernel, tc=tc, nsteps=nsteps),
        grid=(nsteps,),
        in_specs=[pl.BlockSpec((tc * TOP_K,), lambda i: (i,), memory_space=pltpu.SMEM),
                  pl.BlockSpec((tc * TOP_K,), lambda i: (jnp.minimum(i + 1, nsteps - 1),),
                               memory_space=pltpu.SMEM),
                  pl.BlockSpec((tc, D_MODEL), lambda i: (i, 0)),
                  pl.BlockSpec((tc, LANES), lambda i: (i, 0)),
                  pl.BlockSpec((1, D_MODEL), lambda i: (0, 0)),
                  pl.BlockSpec(memory_space=pl.ANY)],
        out_specs=pl.BlockSpec((tc, D_MODEL), lambda i: (i, 0)),
        out_shape=jax.ShapeDtypeStruct((T, D_MODEL), F32),
        scratch_shapes=[pltpu.VMEM((2, TOP_K, tc, D_MODEL), F32), pltpu.SemaphoreType.DMA((2,))],
        compiler_params=_cparams("arbitrary"),
        name="combine",
    )(dest_flat, dest_flat, x1, route, g_final, ys)


def _moe_layer(x1, hf, route, ew, g_final):
    T = x1.shape[0]
    n_assign = T * TOP_K
    nblk = n_assign // MOE_BLOCK + N_EXPERTS
    n_slots = nblk * MOE_BLOCK
    dest, cnt, pst = _route(route)
    cnt_i = cnt[0, :N_EXPERTS].astype(I32)
    pstart = pst[0, :N_EXPERTS].astype(I32)
    padded = (cnt_i + MOE_BLOCK - 1) // MOE_BLOCK * MOE_BLOCK
    pend = pstart + padded
    block_expert = jnp.minimum(
        jnp.sum(jnp.arange(nblk, dtype=I32)[:, None] * MOE_BLOCK >= pend[None, :], axis=1),
        N_EXPERTS - 1).astype(I32)
    n_used = (pend[-1] // MOE_BLOCK).reshape(1).astype(I32)
    dest_flat = dest[:, :TOP_K].reshape(n_assign)
    xs = _dispatch(hf, dest_flat, pstart + cnt_i, padded - cnt_i, n_used, n_slots)
    ys = _moe(xs, block_expert, n_used, *ew)
    return _combine(x1, route, dest_flat, ys, g_final)


def _trunk(x, p, bb):
    B, L, D = x.shape
    T = B * L
    x2d = x.reshape(T, D)
    v, x1g, x2g, u = _inproj(x2d, L, p["g_mix"], p["w_in"], p["b_in"], p["short_w"], p["short_b"])
    ya = _hyena(v, x1g, x2g, B, L, p["filt"], p["filt_bias"], bb)
    ybp = _s5(u, B, L, p["s5"])
    x1, hf, route = _mix(x2d, ya, ybp, p["w_glu"], p["b_glu"], p["g_hyena"], p["g_s5"], p["w_out"],
                         p["g_ffn"], p["w_router"], p["b_router"])
    y = _moe_layer(x1, hf, route, p["experts"], p["g_final"])
    return y.reshape(B, L, D)


def kernel(x_prompt, x_sample, g_mix, w_in, b_in, short_w, short_b, filt_w1, filt_b1, filt_w2, filt_b2, filt_w3, filt_b3, filt_freq, filt_bias, s5_lam_re, s5_lam_im, s5_log_dt, s5_b_re, s5_b_im, s5_c_re, s5_c_im, s5_d, w_glu, b_glu, g_hyena, g_s5, w_out, g_ffn, w_router, b_router, w_gate, b_gate, w_up, b_up, w_down, b_down, g_final):
    assert g_mix.shape[0] == 1, "one encoder layer"
    row = lambda a: a[0][None, :].astype(F32)
    p = dict(
        g_mix=row(g_mix), w_in=w_in[0].astype(BF16), b_in=row(b_in),
        short_w=short_w[0].astype(F32), short_b=row(short_b),
        filt=(filt_w1[0], filt_b1[0], filt_w2[0], filt_b2[0], filt_w3[0], filt_b3[0], filt_freq[0]),
        filt_bias=filt_bias[0].astype(F32),
        s5=_s5_operators(s5_lam_re[0], s5_lam_im[0], s5_log_dt[0], s5_b_re[0], s5_b_im[0],
                         s5_c_re[0], s5_c_im[0], s5_d[0]),
        w_glu=w_glu[0].astype(BF16), b_glu=row(b_glu), g_hyena=row(g_hyena), g_s5=row(g_s5),
        w_out=w_out[0].astype(BF16), g_ffn=row(g_ffn),
        w_router=jnp.pad(w_router[0].astype(F32), ((0, 0), (0, LANES - N_EXPERTS))),
        b_router=jnp.pad(b_router[0].astype(F32), (0, LANES - N_EXPERTS))[None, :],
        experts=(w_gate[0].astype(BF16), b_gate[0][:, None, :], w_up[0].astype(BF16),
                 b_up[0][:, None, :], w_down[0].astype(BF16), b_down[0][:, None, :]),
        g_final=g_final[None, :].astype(F32),
    )
    y_prompt = _trunk(x_prompt, p, bb=1)
    y_sample = _trunk(x_sample, p, bb=4)
    return (y_prompt, y_sample)
```

```python
import functools
import math

import numpy as np
import jax
import jax.numpy as jnp
from jax import lax
from jax.experimental import pallas as pl
from jax.experimental.pallas import tpu as pltpu

F32 = jnp.float32
BF16 = jnp.bfloat16
I32 = jnp.int32
HIGHEST = lax.Precision.HIGHEST

LANES = 128
SUBLANES = 8
D_MODEL = 1024
C_HY = 512
C_S5 = 512
D_IN = 3 * C_HY + C_S5
Q_HY = C_HY // LANES
FILTER_BANDS = 16
FILTER_WIDTH = 64
Z_PAD = 128
DECAY_MAX = math.log(1e-2) / 0.3
DECAY_MIN = math.log(1e-2) / 1.5
S5_G, S5_H, S5_P = 32, 16, 64
S5_CHUNK = 16
S5_UW = S5_CHUNK * S5_H
S5_SW = 2 * S5_P
S5_GB = LANES // S5_H
S5_NB = S5_G // S5_GB
S5_KW = S5_CHUNK * LANES
S5_XW = S5_GB * S5_SW
S5_STEP_ROWS = 512
LAMBDA_RE_MAX = -1e-4
N_EXPERTS = 32
TOP_K = 4
SWIGLU_LIMIT = 7.0
SWIGLU_ALPHA = 1.702
MOE_BLOCK = 512
RMS_EPS = 1e-6
VMEM_LIMIT_BYTES = 56 * 1024 * 1024


def _cparams(*sem):
    return pltpu.CompilerParams(dimension_semantics=sem, vmem_limit_bytes=VMEM_LIMIT_BYTES)


def _rms(x, g):
    return x * lax.rsqrt(jnp.mean(x * x, axis=-1, keepdims=True) + RMS_EPS) * g


def _sigmoid(x):
    return 1.0 / (1.0 + jnp.exp(-x))


def _dot(a, b):
    return jnp.dot(a, b, preferred_element_type=F32)


def _split_bf16(w):
    hi = w.astype(BF16)
    return hi, (w - hi.astype(F32)).astype(BF16)


def _lane_block(x, q):
    return x[:, q * LANES:(q + 1) * LANES]


def _as_rows(ref):
    return ref.reshape(math.prod(ref.shape[:-1]), LANES)


def _inproj_kernel(x_ref, xp_ref, xn_ref, g_ref, w_ref, b_ref, sw_ref, sb_ref,
                   v_ref, x1_ref, x2_ref, u_ref, *, tm, seq):
    i = pl.program_id(0)
    g = g_ref[...]
    h = _rms(x_ref[...], g).astype(BF16)
    proj = _dot(h, w_ref[...]) + b_ref[...]
    u_ref[...] = proj[:, 3 * C_HY:]
    z = proj[:, :3 * C_HY]
    halo = jnp.concatenate([xp_ref[...], xn_ref[...]], axis=0)
    zh = _dot(_rms(halo, g).astype(BF16), w_ref[:, :3 * C_HY]) + b_ref[:, :3 * C_HY]
    row0 = i * tm
    has_prev = lax.rem(row0, seq) != 0
    has_next = lax.rem(row0 + tm, seq) != 0
    zp = jnp.where(has_prev, zh[SUBLANES - 1:SUBLANES], 0.0)
    zn = jnp.where(has_next, zh[SUBLANES:SUBLANES + 1], 0.0)
    rid = lax.broadcasted_iota(I32, (tm, 1), 0)
    zm1 = jnp.where(rid == 0, zp, pltpu.roll(z, 1, 0))
    zp1 = jnp.where(rid == tm - 1, zn, pltpu.roll(z, tm - 1, 0))
    sw = sw_ref[...]
    o = zm1 * sw[0:1] + z * sw[1:2] + zp1 * sw[2:3] + sb_ref[...]
    for q in range(Q_HY):
        v_ref[q] = _lane_block(o, q)
        x1_ref[q] = _lane_block(o, Q_HY + q)
        x2_ref[q] = _lane_block(o, 2 * Q_HY + q)


def _inproj(x2d, seq, g_mix, w_in_bf, b_in, short_w, short_b, tm=512):
    T = x2d.shape[0]
    nb8 = T // SUBLANES
    tb = tm // SUBLANES
    hy = jax.ShapeDtypeStruct((Q_HY, T, LANES), F32)
    hy_spec = pl.BlockSpec((Q_HY, tm, LANES), lambda i: (0, i, 0))
    const = lambda i: (0, 0)
    return pl.pallas_call(
        functools.partial(_inproj_kernel, tm=tm, seq=seq),
        grid=(T // tm,),
        in_specs=[
            pl.BlockSpec((tm, D_MODEL), lambda i: (i, 0)),
            pl.BlockSpec((SUBLANES, D_MODEL), lambda i: (jnp.maximum(i * tb - 1, 0), 0)),
            pl.BlockSpec((SUBLANES, D_MODEL), lambda i: (jnp.minimum((i + 1) * tb, nb8 - 1), 0)),
            pl.BlockSpec((1, D_MODEL), const),
            pl.BlockSpec((D_MODEL, D_IN), const),
            pl.BlockSpec((1, D_IN), const),
            pl.BlockSpec((3, 3 * C_HY), const),
            pl.BlockSpec((1, 3 * C_HY), const),
        ],
        out_specs=[hy_spec, hy_spec, hy_spec, pl.BlockSpec((tm, C_S5), lambda i: (i, 0))],
        out_shape=[hy, hy, hy, jax.ShapeDtypeStruct((T, C_S5), F32)],
        compiler_params=_cparams("arbitrary"),
        name="inproj",
    )(x2d, x2d, x2d, g_mix, w_in_bf, b_in, short_w, short_b)


def _filter_kernel(z_ref, w1_ref, b1_ref, w2_ref, b2_ref, w3_ref, b3_ref, fr_ref, dl_ref,
                   k_ref, s_ref, *, tr, seq):
    i = pl.program_id(0)
    z = z_ref[...]
    fr = fr_ref[...]
    h = jnp.sin(fr[0:1] * (jnp.dot(z, w1_ref[...], precision=HIGHEST,
                                   preferred_element_type=F32) + b1_ref[...]))
    h = jnp.sin(fr[1:2] * (jnp.dot(h, w2_ref[...], precision=HIGHEST,
                                   preferred_element_type=F32) + b2_ref[...]))
    h = jnp.dot(h, w3_ref[...], precision=HIGHEST, preferred_element_type=F32) + b3_ref[...]
    h = h * jnp.exp(-z[:, 0:1] * dl_ref[...])
    rid = i * tr + lax.broadcasted_iota(I32, (tr, 1), 0)
    h = jnp.where(rid == seq, 0.0, h)
    for q in range(2 * Q_HY):
        k_ref[q] = _lane_block(h, q)

    @pl.when(i == 0)
    def _():
        s_ref[...] = jnp.zeros_like(s_ref)

    s_ref[...] += jnp.sum(jnp.abs(h), axis=0, keepdims=True)


def _filter_taps(seq, w1, b1, w2, b2, w3, b3, freq, tr=512):
    L = seq
    N = 2 * L
    n = jnp.arange(N, dtype=I32)
    pos = jnp.where(n < L, n, jnp.where(n == L, 0, N - n)).astype(F32)
    t = (pos * (1.0 / (L - 1)))[:, None]
    w = 2.0 * math.pi * pos / L
    bands = jnp.linspace(1e-4, FILTER_BANDS - 1, FILTER_BANDS, dtype=F32)
    ang = w[:, None] * bands[None, :]
    zc = jnp.concatenate([t, jnp.cos(ang), -jnp.sin(ang),
                          jnp.zeros((N, Z_PAD - 1 - 2 * FILTER_BANDS), F32)], axis=-1)
    w1p = jnp.pad(w1, ((0, Z_PAD - w1.shape[0]), (0, 0)))
    w3d = w3.reshape(FILTER_WIDTH, 2, 2 * C_HY).transpose(1, 0, 2)
    b3d = b3.reshape(2, 1, 2 * C_HY)
    deltas = jnp.abs(jnp.linspace(DECAY_MIN, DECAY_MAX, C_HY, dtype=F32))
    dl = jnp.tile(deltas, 2)[None, :]
    half = (N // tr) // 2
    const = lambda i: (0, 0)
    taps, sums = pl.pallas_call(
        functools.partial(_filter_kernel, tr=tr, seq=L),
        grid=(N // tr,),
        in_specs=[
            pl.BlockSpec((tr, Z_PAD), lambda i: (i, 0)),
            pl.BlockSpec((Z_PAD, FILTER_WIDTH), const),
            pl.BlockSpec((1, FILTER_WIDTH), const),
            pl.BlockSpec((FILTER_WIDTH, FILTER_WIDTH), const),
            pl.BlockSpec((1, FILTER_WIDTH), const),
            pl.BlockSpec((None, FILTER_WIDTH, 2 * C_HY), lambda i: (i // half, 0, 0)),
            pl.BlockSpec((None, 1, 2 * C_HY), lambda i: (i // half, 0, 0)),
            pl.BlockSpec((2, FILTER_WIDTH), const),
            pl.BlockSpec((1, 2 * C_HY), const),
        ],
        out_specs=[pl.BlockSpec((2 * Q_HY, tr, LANES), lambda i: (0, i, 0)),
                   pl.BlockSpec((1, 2 * C_HY), const)],
        out_shape=[jax.ShapeDtypeStruct((2 * Q_HY, N, LANES), F32),
                   jax.ShapeDtypeStruct((1, 2 * C_HY), F32)],
        compiler_params=_cparams("arbitrary"),
        name="hyena_filter",
    )(zc, w1p, b1[None, :], w2, b2[None, :], w3d, b3d, freq, dl)
    return taps, sums


def _split_n(N):
    n1 = {32768: 128, 4096: 64}.get(N)
    if n1 is None:
        n1 = 1 << (int(math.log2(N)) // 2)
    return n1, N // n1


def _dft_constants(N1, N2):
    N = N1 * N2
    k1 = np.arange(N1)[:, None]
    n1 = np.arange(N1)[None, :]
    ang = 2.0 * np.pi * ((k1 * n1) % N1) / N1
    fa_full = np.concatenate([np.cos(ang), -np.sin(ang)], axis=0)
    fa_half = fa_full[:, :N1 // 2]
    fa_inv = np.concatenate([np.cos(ang), -np.sin(ang)], axis=1)[:N1 // 2] / N
    k2 = np.arange(N2)[:, None]
    n2 = np.arange(N2)[None, :]
    a2 = 2.0 * np.pi * ((k2 * n2) % N2) / N2
    cr, ci = np.cos(a2), -np.sin(a2)
    f2 = np.block([[cr, -ci], [ci, cr]])
    g2 = np.block([[cr, ci], [-ci, cr]])
    to = lambda a: jnp.asarray(a.astype(np.float32)).astype(BF16)
    kk = lax.broadcasted_iota(I32, (N1, N2, LANES), 0)
    nn = lax.broadcasted_iota(I32, (N1, N2, LANES), 1)
    ta = (2.0 * math.pi / N) * lax.rem(kk * nn, N).astype(F32)
    tw = jnp.stack([jnp.cos(ta), -jnp.sin(ta)], axis=0)
    return dict(fa_full=to(fa_full), fa_half=to(fa_half), fa_inv=to(fa_inv), f2=to(f2), g2=to(g2),
                tw=tw)


def _dft_a_kernel(f_ref, x_ref, o_ref, *, rows):
    nq = x_ref.shape[0]
    x2d = _as_rows(x_ref)
    f = f_ref[...]
    for j in range(SUBLANES):
        xs = jnp.concatenate(
            [x2d[pl.ds(q * rows * SUBLANES + j, rows, stride=SUBLANES), :] for q in range(nq)], axis=1)
        r = _dot(f, xs.astype(BF16))
        for q in range(nq):
            o_ref[q, j] = _lane_block(r, q)


def _dft_a(f, x):
    Q, Bt, K, N2, _ = x.shape
    R = f.shape[0]
    return pl.pallas_call(
        functools.partial(_dft_a_kernel, rows=K),
        grid=(Bt, N2 // SUBLANES, Q // Q_HY),
        in_specs=[pl.BlockSpec((R, K), lambda b, j, w: (0, 0)),
                  pl.BlockSpec((Q_HY, None, K, SUBLANES, LANES), lambda b, j, w: (w, b, 0, j, 0))],
        out_specs=pl.BlockSpec((Q_HY, None, SUBLANES, R, LANES), lambda b, j, w: (w, b, j, 0, 0)),
        out_shape=jax.ShapeDtypeStruct((Q, Bt, N2, R, LANES), F32),
        compiler_params=_cparams("arbitrary", "arbitrary", "arbitrary"),
        name="dft_a",
    )(f, x)


def _dft_a_inv_kernel(f_ref, z_ref, gate_ref, v_ref, bias_ref, o_ref, *, rows):
    g2d = _as_rows(gate_ref)
    v2d = _as_rows(v_ref)
    o2d = _as_rows(o_ref)
    f = f_ref[...]
    bias = bias_ref[...]
    for j in range(SUBLANES):
        zj = jnp.concatenate([z_ref[q, j] for q in range(Q_HY)], axis=1)
        y = _dot(f, zj.astype(BF16))
        for q in range(Q_HY):
            sl = pl.ds(q * rows * SUBLANES + j, rows, stride=SUBLANES)
            o2d[sl, :] = g2d[sl, :] * (_lane_block(y, q) + v2d[sl, :] * _lane_block(bias, q))


def _dft_a_inv(f, z, gate, v, bias_row):
    Q, Bt, N2, R2, _ = z.shape
    K = f.shape[0]
    nat = pl.BlockSpec((Q, None, K, SUBLANES, LANES), lambda b, j: (0, b, 0, j, 0))
    return pl.pallas_call(
        functools.partial(_dft_a_inv_kernel, rows=K),
        grid=(Bt, N2 // SUBLANES),
        in_specs=[pl.BlockSpec((K, R2), lambda b, j: (0, 0)),
                  pl.BlockSpec((Q, None, SUBLANES, R2, LANES), lambda b, j: (0, b, j, 0, 0)),
                  nat, nat,
                  pl.BlockSpec((1, C_HY), lambda b, j: (0, 0))],
        out_specs=nat,
        out_shape=jax.ShapeDtypeStruct((Q, Bt, K, N2, LANES), F32),
        compiler_params=_cparams("arbitrary", "arbitrary"),
        name="dft_a_inv",
    )(f, z, gate, v, bias_row)


C_STEP_Q = 2
K1_STEP = SUBLANES


def _dft_c_kernel(a_ref, k_ref, tw_ref, f_ref, g_ref, z_ref, *, bb, n2):
    per = n2 * 2 * K1_STEP
    a2d = _as_rows(a_ref)
    z2d = _as_rows(z_ref)
    cols = [(b, q) for b in range(bb) for q in range(C_STEP_Q)]

    def rows(b, q, kk, im):
        return pl.ds((q * bb + b) * per + im * K1_STEP + kk, n2, stride=2 * K1_STEP)

    for kk in range(K1_STEP):
        twr = jnp.tile(tw_ref[0, kk], (1, len(cols)))
        twi = jnp.tile(tw_ref[1, kk], (1, len(cols)))
        kr = jnp.tile(k_ref[0, kk], (1, bb))
        ki = jnp.tile(k_ref[1, kk], (1, bb))
        ar = jnp.concatenate([a2d[rows(b, q, kk, 0), :] for b, q in cols], axis=1)
        ai = jnp.concatenate([a2d[rows(b, q, kk, 1), :] for b, q in cols], axis=1)
        pr = ar * twr - ai * twi
        pi = ar * twi + ai * twr
        x = _dot(f_ref[...], jnp.concatenate([pr, pi], axis=0).astype(BF16))
        xr = x[:n2]
        xi = x[n2:]
        yr = xr * kr - xi * ki
        yi = xr * ki + xi * kr
        zz = _dot(g_ref[...], jnp.concatenate([yr, yi], axis=0).astype(BF16))
        zr = zz[:n2]
        zi = zz[n2:]
        outr = zr * twr + zi * twi
        outi = zi * twr - zr * twi
        for c, (b, q) in enumerate(cols):
            z2d[rows(b, q, kk, 0), :] = _lane_block(outr, c)
            z2d[rows(b, q, kk, 1), :] = _lane_block(outi, c)


def _dft_c(a6, khat, order, tw, f2, g2, bb):
    Q, Bt, N2, _, N1, _ = a6.shape
    cw = C_STEP_Q * LANES
    ablk = pl.BlockSpec((C_STEP_Q, bb, N2, 2, K1_STEP, LANES), lambda g, c, b: (c, b, 0, 0, g, 0))
    return pl.pallas_call(
        functools.partial(_dft_c_kernel, bb=bb, n2=N2),
        grid=(N1 // K1_STEP, Q // C_STEP_Q, Bt // bb),
        in_specs=[ablk,
                  pl.BlockSpec((2, K1_STEP, N2, cw), lambda g, c, b: (0, g, 0, order * (C_HY // cw) + c)),
                  pl.BlockSpec((2, K1_STEP, N2, LANES), lambda g, c, b: (0, g, 0, 0)),
                  pl.BlockSpec((2 * N2, 2 * N2), lambda g, c, b: (0, 0)),
                  pl.BlockSpec((2 * N2, 2 * N2), lambda g, c, b: (0, 0))],
        out_specs=ablk,
        out_shape=jax.ShapeDtypeStruct(a6.shape, F32),
        compiler_params=_cparams("arbitrary", "arbitrary", "arbitrary"),
        name="dft_c",
    )(a6, khat, tw, f2, g2)


def _dft_c_filter_kernel(a_ref, tw_ref, f_ref, s_ref, o_ref, *, n2):
    per = n2 * 2 * K1_STEP
    a2d = _as_rows(a_ref)
    sc = 1.0 / (s_ref[...] + 1e-6)
    for kk in range(K1_STEP):
        twr = jnp.tile(tw_ref[0, kk], (1, C_STEP_Q))
        twi = jnp.tile(tw_ref[1, kk], (1, C_STEP_Q))
        ar = jnp.concatenate([a2d[pl.ds(q * per + kk, n2, stride=2 * K1_STEP), :]
                              for q in range(C_STEP_Q)], axis=1)
        ai = jnp.concatenate([a2d[pl.ds(q * per + K1_STEP + kk, n2, stride=2 * K1_STEP), :]
                              for q in range(C_STEP_Q)], axis=1)
        pr = ar * twr - ai * twi
        pi = ar * twi + ai * twr
        x = _dot(f_ref[...], jnp.concatenate([pr, pi], axis=0).astype(BF16))
        o_ref[0, kk] = x[:n2] * sc
        o_ref[1, kk] = x[n2:] * sc


def _dft_c_filter(a5, tw, f2, sums):
    Q, N2, _, N1, _ = a5.shape
    cw = C_STEP_Q * LANES
    return pl.pallas_call(
        functools.partial(_dft_c_filter_kernel, n2=N2),
        grid=(N1 // K1_STEP, Q // C_STEP_Q),
        in_specs=[pl.BlockSpec((C_STEP_Q, N2, 2, K1_STEP, LANES), lambda g, c: (c, 0, 0, g, 0)),
                  pl.BlockSpec((2, K1_STEP, N2, LANES), lambda g, c: (0, g, 0, 0)),
                  pl.BlockSpec((2 * N2, 2 * N2), lambda g, c: (0, 0)),
                  pl.BlockSpec((1, cw), lambda g, c: (0, c))],
        out_specs=pl.BlockSpec((2, K1_STEP, N2, cw), lambda g, c: (0, g, 0, c)),
        out_shape=jax.ShapeDtypeStruct((2, N1, N2, Q * LANES), F32),
        compiler_params=_cparams("arbitrary", "arbitrary"),
        name="dft_c_filter",
    )(a5, tw, f2, sums)


def _hyena(v, x1, x2, B, seq, filt, filt_bias, bb):
    L = seq
    N = 2 * L
    N1, N2 = _split_n(N)
    cst = _dft_constants(N1, N2)
    taps, sums = _filter_taps(L, *filt)
    ka = _dft_a(cst["fa_full"], taps.reshape(2 * Q_HY, 1, N1, N2, LANES))
    khat = _dft_c_filter(ka.reshape(2 * Q_HY, N2, 2, N1, LANES), cst["tw"], cst["f2"], sums)
    nat = lambda a: a.reshape(Q_HY, B, N1 // 2, N2, LANES)
    cur = nat(v)
    for o, gate in enumerate((x1, x2)):
        a = _dft_a(cst["fa_half"], cur)
        z = _dft_c(a.reshape(Q_HY, B, N2, 2, N1, LANES), khat, o, cst["tw"], cst["f2"], cst["g2"], bb)
        cur = _dft_a_inv(cst["fa_inv"], z.reshape(Q_HY, B, N2, 2 * N1, LANES), nat(gate), cur,
                         filt_bias[o][None, :])
    return cur.reshape(Q_HY, B * L, LANES)


def _s5_operators(lam_re, lam_im, log_dt, b_re, b_im, c_re, c_im, d_skip):
    Tc, G, H, P = S5_CHUNK, S5_G, S5_H, S5_P
    lam = jnp.minimum(lam_re.astype(F32), LAMBDA_RE_MAX) + 1j * lam_im.astype(F32)
    dt = jnp.exp(log_dt.astype(F32))[..., None]
    lam_dt = lam * dt
    lam_bar = jnp.exp(lam_dt)
    b_bar = ((lam_bar - 1.0) / lam)[..., None] * (b_re.astype(F32) + 1j * b_im.astype(F32))
    c = c_re.astype(F32) + 1j * c_im.astype(F32)
    tau = jnp.arange(Tc + 1, dtype=F32)
    pw = jnp.exp(lam_dt[None] * tau[:, None, None, None])
    kk = jnp.einsum('dghp,tdgp,dgpk->dtghk', c, pw[:Tc], b_bar).real
    i = jnp.arange(Tc)
    lag = i[:, None] - i[None, :]
    kf = jnp.where((lag >= 0)[:, :, None, None, None], kk[0][jnp.clip(lag, 0, Tc - 1)], 0.0)
    kb = jnp.where((lag <= 0)[:, :, None, None, None], kk[1][jnp.clip(-lag, 0, Tc - 1)], 0.0)
    m = kf + kb
    eye = (lag == 0)[:, :, None, None, None] * jnp.eye(H, dtype=F32)[None, None, None]
    m = m + eye * d_skip.astype(F32).reshape(G, H)[None, None, :, :, None]
    clf = c[0][None] * pw[1:Tc + 1, 0][:, :, None, :]
    clb = c[1][None] * pw[Tc - i, 1][:, :, None, :]
    sf = pw[Tc - 1 - i, 0][:, :, :, None] * b_bar[0][None]
    sb = pw[i, 1][:, :, :, None] * b_bar[1][None]
    w_intra = m.transpose(2, 1, 4, 0, 3).reshape(G, Tc * H, Tc * H)
    st2y = lambda cl: jnp.concatenate([cl.real, -cl.imag], axis=-1).transpose(1, 3, 0, 2) \
        .reshape(G, 2 * P, Tc * H)
    u2s = lambda s: jnp.concatenate([s.real, s.imag], axis=2).transpose(1, 0, 3, 2) \
        .reshape(G, Tc * H, 2 * P)
    w_state = jnp.concatenate([u2s(sf), u2s(sb)], axis=-1)
    pair = lambda a: a.reshape(2, G // 2, 2 * P)
    lam16 = jnp.concatenate([pair(pw[Tc].real), pair(pw[Tc].imag)], axis=-1).reshape(2, G * 2 * P)
    col = np.arange(S5_KW)
    spread = (np.arange(S5_UW)[:, None] == (col // LANES * S5_H + col % S5_H)[None, :])
    return dict(w_intra=w_intra.astype(BF16), w_xf=st2y(clf).astype(BF16), w_xb=st2y(clb).astype(BF16),
                w_state=w_state.astype(BF16), lam16=lam16,
                spread=jnp.asarray(spread.astype(np.float32)).astype(BF16))


def _s5_rows(i, cl, B, ncc):
    return pl.ds(S5_CHUNK * cl + i, B, stride=S5_CHUNK * ncc)


def _s5_chunk_inputs(u2d, B, ncc):
    cols = []
    for i in range(S5_CHUNK):
        if B == 1:
            cols.append(u2d[pl.ds(i, ncc, stride=S5_CHUNK), :])
        else:
            cols.append(jnp.concatenate([u2d[_s5_rows(i, cl, B, ncc), :] for cl in range(ncc)], axis=0))
    return jnp.concatenate(cols, axis=1).astype(BF16)


def _s5_state_lane(a, part):
    return (a // 2) * 2 * S5_SW + part * S5_SW + (a % 2) * S5_P


def _s5_state_kernel(u_ref, w_ref, sf_ref, sb_ref, wblk, *, B, ncc):
    @pl.when(pl.program_id(1) == 0)
    def _():
        wblk[...] = jnp.zeros_like(wblk)
        for a in range(S5_GB):
            for j in range(S5_CHUNK):
                r0 = j * LANES + a * S5_H
                for d in range(2):
                    for part in range(2):
                        c0 = d * S5_XW + _s5_state_lane(a, part)
                        s0 = d * S5_SW + part * S5_P
                        wblk[r0:r0 + S5_H, c0:c0 + S5_P] = w_ref[a, j * S5_H:(j + 1) * S5_H, s0:s0 + S5_P]

    lhs = _s5_chunk_inputs(_as_rows(u_ref), B, ncc)
    r = _dot(lhs, wblk[...])
    sf_ref[...] = r[:, :S5_XW]
    sb_ref[...] = r[:, S5_XW:]


def _s5_state(u3, w_state, ncc):
    B, L, _ = u3.shape
    nch = L // S5_CHUNK
    out = jax.ShapeDtypeStruct((nch * B, S5_NB * S5_XW), F32)
    return pl.pallas_call(
        functools.partial(_s5_state_kernel, B=B, ncc=ncc),
        grid=(S5_NB, nch // ncc),
        in_specs=[pl.BlockSpec((B, S5_CHUNK * ncc, LANES), lambda q, t: (0, t, q)),
                  pl.BlockSpec((S5_GB, S5_UW, 2 * S5_SW), lambda q, t: (q, 0, 0))],
        out_specs=[pl.BlockSpec((ncc * B, S5_XW), lambda q, t: (t, q))] * 2,
        out_shape=[out, out],
        scratch_shapes=[pltpu.VMEM((S5_KW, 2 * S5_XW), BF16)],
        compiler_params=_cparams("arbitrary", "arbitrary"),
        name="s5_state",
    )(u3, w_state)


def _s5_scan_kernel(sf_ref, sb_ref, lam_ref, xf_ref, xb_ref, *, nch, rows):
    lb = sf_ref.shape[1]
    vr = max(rows, SUBLANES)
    lam_f, lam_b = lam_ref[0], lam_ref[1]

    def advance(x, lam, inc):
        out = []
        for u in range(0, lb, 2 * S5_SW):
            re, im = x[:, u:u + S5_SW], x[:, u + S5_SW:u + 2 * S5_SW]
            lr, li = lam[:, u:u + S5_SW], lam[:, u + S5_SW:u + 2 * S5_SW]
            out += [lr * re - li * im, lr * im + li * re]
        return jnp.concatenate(out, axis=1) + inc

    def body(c, carry):
        xf, xb = carry
        rf = pl.ds(pl.multiple_of(c * rows, rows), rows)
        xf_ref[rf, :] = xf[:rows]
        xf = advance(xf, lam_f, sf_ref[rf, :])
        rb = pl.ds(pl.multiple_of((nch - 1 - c) * rows, rows), rows)
        xb_ref[rb, :] = xb[:rows]
        xb = advance(xb, lam_b, sb_ref[rb, :])
        return xf, xb

    zero = jnp.zeros((vr, lb), F32)
    lax.fori_loop(0, nch, body, (zero, zero))


def _s5_scan(sf, sb, lam16, nch, rows, lb):
    R, lanes = sf.shape
    blk = pl.BlockSpec((R, lb), lambda j: (0, j))
    out = jax.ShapeDtypeStruct((R, lanes), F32)
    return pl.pallas_call(
        functools.partial(_s5_scan_kernel, nch=nch, rows=rows),
        grid=(lanes // lb,),
        in_specs=[blk, blk, pl.BlockSpec((2, 1, lb), lambda j: (0, 0, j))],
        out_specs=[blk, blk],
        out_shape=[out, out],
        compiler_params=_cparams("arbitrary"),
        name="s5_scan",
    )(sf, sb, lam16[:, None, :])


def _s5_out_kernel(u_ref, xf_ref, xb_ref, wm_ref, wf_ref, wb_ref, e_ref, y_ref, wm_blk, wf_blk, wb_blk,
                   *, B, ncc):
    @pl.when(pl.program_id(1) == 0)
    def _():
        lane = lax.broadcasted_iota(I32, (1, S5_KW), 1)
        slot = (lane // S5_H) % S5_GB
        e = e_ref[...]
        for a in range(S5_GB):
            own = slot == a
            ex = jnp.where(own, _dot(wm_ref[a], e), 0.0).astype(BF16)
            for j in range(S5_CHUNK):
                r0 = j * LANES + a * S5_H
                wm_blk[r0:r0 + S5_H, :] = ex[j * S5_H:(j + 1) * S5_H, :]
            for src, dst in ((wf_ref, wf_blk), (wb_ref, wb_blk)):
                ex = jnp.where(own, _dot(src[a], e), 0.0).astype(BF16)
                for part in range(2):
                    r0 = _s5_state_lane(a, part)
                    dst[r0:r0 + S5_P, :] = ex[part * S5_P:(part + 1) * S5_P, :]

    lhs = _s5_chunk_inputs(_as_rows(u_ref), B, ncc)
    acc = _dot(lhs, wm_blk[...])
    acc += _dot(xf_ref[...].astype(BF16), wf_blk[...])
    acc += _dot(xb_ref[...].astype(BF16), wb_blk[...])
    y2d = _as_rows(y_ref)
    for i in range(S5_CHUNK):
        piece = _lane_block(acc, i)
        if B == 1:
            y2d[pl.ds(i, ncc, stride=S5_CHUNK), :] = piece
        else:
            for cl in range(ncc):
                y2d[_s5_rows(i, cl, B, ncc), :] = piece[cl * B:(cl + 1) * B]


def _s5_out(u3, xf, xb, ops, ncc):
    B, L, _ = u3.shape
    nch = L // S5_CHUNK
    tok = pl.BlockSpec((B, S5_CHUNK * ncc, LANES), lambda q, t: (0, t, q))
    st = pl.BlockSpec((ncc * B, S5_XW), lambda q, t: (t, q))
    return pl.pallas_call(
        functools.partial(_s5_out_kernel, B=B, ncc=ncc),
        grid=(S5_NB, nch // ncc),
        in_specs=[tok, st, st,
                  pl.BlockSpec((S5_GB, S5_UW, S5_UW), lambda q, t: (q, 0, 0)),
                  pl.BlockSpec((S5_GB, S5_SW, S5_UW), lambda q, t: (q, 0, 0)),
                  pl.BlockSpec((S5_GB, S5_SW, S5_UW), lambda q, t: (q, 0, 0)),
                  pl.BlockSpec((S5_UW, S5_KW), lambda q, t: (0, 0))],
        out_specs=tok,
        out_shape=jax.ShapeDtypeStruct(u3.shape, F32),
        scratch_shapes=[pltpu.VMEM((S5_KW, S5_KW), BF16), pltpu.VMEM((S5_XW, S5_KW), BF16),
                        pltpu.VMEM((S5_XW, S5_KW), BF16)],
        compiler_params=_cparams("arbitrary", "arbitrary"),
        name="s5_out",
    )(u3, xf, xb, ops["w_intra"], ops["w_xf"], ops["w_xb"], ops["spread"])


def _s5(u, B, seq, ops):
    nch = seq // S5_CHUNK
    u3 = u.reshape(B, seq, C_S5)
    ncc = min(nch, S5_STEP_ROWS // B)
    sf, sb = _s5_state(u3, ops["w_state"], ncc)
    lb = 1024 if B == 1 else 256
    xf, xb = _s5_scan(sf, sb, ops["lam16"], nch, B, lb)
    return _s5_out(u3, xf, xb, ops, ncc).reshape(B * seq, C_S5)


def _mix_kernel(x_ref, ya_ref, yb_ref, wglu_ref, bglu_ref, gh_ref, gs_ref, wout_ref, gffn_ref,
                wrh_ref, wrl_ref, br_ref, x1_ref, hf_ref, route_ref, *, tm):
    g = jax.nn.gelu(yb_ref[...])
    yb = g * _sigmoid(_dot(g.astype(BF16), wglu_ref[...]) + bglu_ref[...])
    ya = jnp.concatenate([ya_ref[q] for q in range(Q_HY)], axis=1)
    na = _rms(ya, gh_ref[...]).astype(BF16)
    nb = _rms(yb, gs_ref[...]).astype(BF16)
    mixed = _dot(na, wout_ref[:C_HY, :]) + _dot(nb, wout_ref[C_HY:, :])
    x1 = x_ref[...] + mixed
    x1_ref[...] = x1
    hf = _rms(x1, gffn_ref[...])
    hf_ref[...] = hf
    hf_hi = hf.astype(BF16)
    hf_lo = (hf - hf_hi.astype(F32)).astype(BF16)
    logits = (_dot(hf_hi, wrh_ref[...]) + _dot(hf_lo, wrh_ref[...]) + _dot(hf_hi, wrl_ref[...])
              + br_ref[...])
    lane = lax.broadcasted_iota(I32, (tm, LANES), 1)
    neg = jnp.float32(-jnp.inf)
    l = jnp.where(lane < N_EXPERTS, logits, neg)
    vals, idxs = [], []
    for _ in range(TOP_K):
        m = jnp.max(l, axis=-1, keepdims=True)
        idx = jnp.min(jnp.where(l == m, lane, LANES), axis=-1, keepdims=True)
        vals.append(m)
        idxs.append(idx)
        l = jnp.where(lane == idx, neg, l)
    es = [jnp.exp(v - vals[0]) for v in vals]
    den = es[0] + es[1] + es[2] + es[3]
    route = jnp.zeros((tm, LANES), F32)
    for k in range(TOP_K):
        route = jnp.where(lane == k, es[k] / den, route)
        route = jnp.where(lane == TOP_K + k, idxs[k].astype(F32), route)
    route_ref[...] = route


def _mix(x2d, ya4, ybp, w_glu_bf, b_glu, g_hyena, g_s5, w_out_bf, g_ffn, w_router_p, b_router_p, tm=512):
    T = x2d.shape[0]
    const = lambda i: (0, 0)
    row = lambda w: pl.BlockSpec((tm, w), lambda i: (i, 0))
    return pl.pallas_call(
        functools.partial(_mix_kernel, tm=tm),
        grid=(T // tm,),
        in_specs=[row(D_MODEL), pl.BlockSpec((Q_HY, tm, LANES), lambda i: (0, i, 0)), row(C_S5),
                  pl.BlockSpec((C_S5, C_S5), const), pl.BlockSpec((1, C_S5), const),
                  pl.BlockSpec((1, C_HY), const), pl.BlockSpec((1, C_S5), const),
                  pl.BlockSpec((D_MODEL, D_MODEL), const), pl.BlockSpec((1, D_MODEL), const),
                  pl.BlockSpec((D_MODEL, LANES), const), pl.BlockSpec((D_MODEL, LANES), const),
                  pl.BlockSpec((1, LANES), const)],
        out_specs=[row(D_MODEL), row(D_MODEL), row(LANES)],
        out_shape=[jax.ShapeDtypeStruct((T, D_MODEL), F32), jax.ShapeDtypeStruct((T, D_MODEL), F32),
                   jax.ShapeDtypeStruct((T, LANES), F32)],
        compiler_params=_cparams("arbitrary"),
        name="mix",
    )(x2d, ya4, ybp, w_glu_bf, b_glu, g_hyena, g_s5, w_out_bf, g_ffn, w_router_p[0], w_router_p[1],
      b_router_p)


def _route_kernel(r_ref, dest_ref, cnt_ref, pst_ref, carry, pstart, *, tm):
    p = pl.program_id(0)
    i = pl.program_id(1)
    lane = lax.broadcasted_iota(I32, (tm, LANES), 1)
    r = r_ref[...]
    eids = [r[:, TOP_K + k:TOP_K + k + 1].astype(I32) for k in range(TOP_K)]
    oh = jnp.zeros((tm, LANES), F32)
    for e in eids:
        oh += (lane == e).astype(F32)
    tot = jnp.sum(oh, axis=0, keepdims=True)

    @pl.when((p == 0) & (i == 0))
    def _():
        carry[...] = jnp.zeros_like(carry)

    @pl.when((p == 1) & (i == 0))
    def _():
        cnt = carry[...]
        cnt_ref[...] = cnt
        padded = jnp.floor((cnt + (MOE_BLOCK - 1)) * (1.0 / MOE_BLOCK)) * MOE_BLOCK
        a = lax.broadcasted_iota(I32, (LANES, LANES), 0)
        b = lax.broadcasted_iota(I32, (LANES, LANES), 1)
        excl = jnp.dot(jnp.broadcast_to(padded, (SUBLANES, LANES)), (a < b).astype(F32),
                       precision=HIGHEST, preferred_element_type=F32)
        pstart[...] = excl[0:1]
        pst_ref[...] = excl[0:1]
        carry[...] = jnp.zeros_like(carry)

    @pl.when(p == 1)
    def _():
        a = lax.broadcasted_iota(I32, (tm, tm), 0)
        b = lax.broadcasted_iota(I32, (tm, tm), 1)
        before = _dot((b < a).astype(BF16), oh.astype(BF16))
        base = before + carry[...] + pstart[...]
        out = jnp.zeros((tm, LANES), F32)
        for k, e in enumerate(eids):
            d = jnp.sum(jnp.where(lane == e, base, 0.0), axis=-1, keepdims=True)
            out = jnp.where(lane == k, d, out)
        dest_ref[...] = out.astype(I32)

    carry[...] += tot


def _route(route, tm=512):
    T = route.shape[0]
    one = jax.ShapeDtypeStruct((1, LANES), F32)
    return pl.pallas_call(
        functools.partial(_route_kernel, tm=tm),
        grid=(2, T // tm),
        in_specs=[pl.BlockSpec((tm, LANES), lambda p, i: (i, 0))],
        out_specs=[pl.BlockSpec((tm, LANES), lambda p, i: (i * p, 0)),
                   pl.BlockSpec((1, LANES), lambda p, i: (0, 0)),
                   pl.BlockSpec((1, LANES), lambda p, i: (0, 0))],
        out_shape=[jax.ShapeDtypeStruct((T, LANES), I32), one, one],
        scratch_shapes=[pltpu.VMEM((1, LANES), F32), pltpu.VMEM((1, LANES), F32)],
        compiler_params=_cparams("arbitrary", "arbitrary"),
        name="route",
    )(route)


ZERO_ROWS = 64
DMA_UNROLL = 4


RING = 3


def _dispatch_kernel(padpos_ref, padcnt_ref, nu_ref, dest_ref, h_ref, xs_ref, zbuf, hbuf, sem, lsem, zsem,
                     *, td, nblk, nsteps):
    i = pl.program_id(0)

    @pl.when(i == 0)
    def _():
        zbuf[...] = jnp.zeros_like(zbuf)

        def per_expert(e, carry):
            off = padpos_ref[e]

            def zero_row(r):
                return pltpu.make_async_copy(zbuf.at[pl.ds(0, 1)], xs_ref.at[pl.ds(off + r, 1)], zsem)

            def z_issue(r, c):
                zero_row(r).start()
                return c

            def z_drain(r, c):
                zero_row(r).wait()
                return c

            lax.fori_loop(0, padcnt_ref[e], z_issue, 0)
            lax.fori_loop(0, padcnt_ref[e], z_drain, 0)
            return carry

        lax.fori_loop(0, N_EXPERTS, per_expert, 0)

        def zero_piece(j):
            row = pl.multiple_of(j * ZERO_ROWS, ZERO_ROWS)
            return pltpu.make_async_copy(zbuf, xs_ref.at[pl.ds(row, ZERO_ROWS)], zsem)

        per_blk = MOE_BLOCK // ZERO_ROWS

        def t_issue(j, c):
            zero_piece(j).start()
            return c

        def t_drain(j, c):
            zero_piece(j).wait()
            return c

        lax.fori_loop(nu_ref[0] * per_blk, nblk * per_blk, t_issue, 0)
        lax.fori_loop(nu_ref[0] * per_blk, nblk * per_blk, t_drain, 0)

    def load(step):
        s = lax.rem(step, RING)
        return pltpu.make_async_copy(h_ref.at[pl.ds(step * td, td)], hbuf.at[s], lsem.at[s])

    def wait_rows(step):
        s = lax.rem(step, RING)
        for _ in range(TOP_K):
            pltpu.make_async_copy(hbuf.at[s], xs_ref.at[pl.ds(0, td)], sem.at[s]).wait()

    @pl.when(i == 0)
    def _():
        load(0).start()
        if nsteps > 1:
            load(1).start()

    load(i).wait()
    slot = lax.rem(i, RING)

    def issue(r, carry):
        for k in range(TOP_K):
            pltpu.make_async_copy(hbuf.at[slot, pl.ds(r, 1)],
                                  xs_ref.at[pl.ds(dest_ref[r * TOP_K + k], 1)],
                                  sem.at[slot]).start(priority=k % 2)
        return carry

    lax.fori_loop(0, td, issue, 0, unroll=DMA_UNROLL)

    @pl.when(i > 0)
    def _():
        wait_rows(i - 1)

    @pl.when(i + 2 < nsteps)
    def _():
        load(i + 2).start()

    @pl.when(i == nsteps - 1)
    def _():
        wait_rows(i)


def _dispatch(hf, dest_flat, padpos, padcnt, n_used, n_slots, td=128):
    T = hf.shape[0]
    nsteps = T // td
    grid_spec = pltpu.PrefetchScalarGridSpec(
        num_scalar_prefetch=3,
        grid=(nsteps,),
        in_specs=[pl.BlockSpec((td * TOP_K,), lambda i, *_: (i,), memory_space=pltpu.SMEM),
                  pl.BlockSpec(memory_space=pl.ANY)],
        out_specs=pl.BlockSpec(memory_space=pl.ANY),
        scratch_shapes=[pltpu.VMEM((ZERO_ROWS, D_MODEL), F32), pltpu.VMEM((RING, td, D_MODEL), F32),
                        pltpu.SemaphoreType.DMA((RING,)), pltpu.SemaphoreType.DMA((RING,)),
                        pltpu.SemaphoreType.DMA(())],
    )
    return pl.pallas_call(
        functools.partial(_dispatch_kernel, td=td, nblk=n_slots // MOE_BLOCK, nsteps=nsteps),
        grid_spec=grid_spec,
        out_shape=jax.ShapeDtypeStruct((n_slots, D_MODEL), F32),
        compiler_params=_cparams("arbitrary"),
        name="dispatch",
    )(padpos, padcnt, n_used, dest_flat, hf)


def _moe_kernel(be_ref, nu_ref, x_ref, wg_ref, bg_ref, wu_ref, bu_ref, wd_ref, bd_ref, o_ref):
    used = pl.program_id(0) < nu_ref[0]

    @pl.when(jnp.logical_not(used))
    def _():
        o_ref[...] = jnp.zeros_like(o_ref)

    @pl.when(used)
    def _():
        x = x_ref[...].astype(BF16)
        gt = jnp.minimum(_dot(x, wg_ref[...]) + bg_ref[...], SWIGLU_LIMIT)
        up = jnp.clip(_dot(x, wu_ref[...]) + bu_ref[...], -SWIGLU_LIMIT, SWIGLU_LIMIT)
        act = (up + 1.0) * (gt * _sigmoid(SWIGLU_ALPHA * gt))
        o_ref[...] = _dot(act.astype(BF16), wd_ref[...]) + bd_ref[...]


def _moe(xs, block_expert, n_used, wg, bg, wu, bu, wd, bd):
    n_slots = xs.shape[0]
    nblk = n_slots // MOE_BLOCK
    blk = lambda i, be, nu: (jnp.minimum(i, nu[0] - 1), 0)
    exp = lambda i, be, nu: (be[jnp.minimum(i, nu[0] - 1)], 0, 0)
    wspec = pl.BlockSpec((None, D_MODEL, D_MODEL), exp)
    bspec = pl.BlockSpec((None, 1, D_MODEL), exp)
    grid_spec = pltpu.PrefetchScalarGridSpec(
        num_scalar_prefetch=2,
        grid=(nblk,),
        in_specs=[pl.BlockSpec((MOE_BLOCK, D_MODEL), blk), wspec, bspec, wspec, bspec, wspec, bspec],
        out_specs=pl.BlockSpec((MOE_BLOCK, D_MODEL), lambda i, be, nu: (i, 0)),
    )
    return pl.pallas_call(
        _moe_kernel,
        grid_spec=grid_spec,
        out_shape=jax.ShapeDtypeStruct((n_slots, D_MODEL), F32),
        compiler_params=_cparams("arbitrary"),
        name="moe",
    )(block_expert, n_used, xs, wg, bg, wu, bu, wd, bd)


def _combine_kernel(dcur_ref, dnext_ref, x1_ref, r_ref, g_ref, ys_ref, o_ref, buf, sem, *, tc, nsteps):
    i = pl.program_id(0)
    slot = lax.rem(i, 2)

    def gather(d_ref, s):
        def body(r, carry):
            for k in range(TOP_K):
                pltpu.make_async_copy(ys_ref.at[pl.ds(d_ref[r * TOP_K + k], 1)],
                                      buf.at[s, k, pl.ds(r, 1)], sem.at[s]).start(priority=k % 2)
            return carry

        lax.fori_loop(0, tc, body, 0, unroll=DMA_UNROLL)

    @pl.when(i == 0)
    def _():
        gather(dcur_ref, slot)

    @pl.when(i + 1 < nsteps)
    def _():
        gather(dnext_ref, 1 - slot)

    for k in range(TOP_K):
        pltpu.make_async_copy(ys_ref.at[pl.ds(0, tc)], buf.at[slot, k], sem.at[slot]).wait()
    gates = r_ref[...]
    acc = x1_ref[...]
    for k in range(TOP_K):
        acc += gates[:, k:k + 1] * buf[slot, k]
    o_ref[...] = _rms(acc, g_ref[...])


def _combine(x1, route, dest_flat, ys, g_final, tc=128):
    T = x1.shape[0]
    nsteps = T // tc
    return pl.pallas_call(
        functools.partial(_combine_kernel, tc=tc, nsteps=nsteps),
        grid=(nsteps,),
        in_specs=[pl.BlockSpec((tc * TOP_K,), lambda i: (i,), memory_space=pltpu.SMEM),
                  pl.BlockSpec((tc * TOP_K,), lambda i: (jnp.minimum(i + 1, nsteps - 1),),
                               memory_space=pltpu.SMEM),
                  pl.BlockSpec((tc, D_MODEL), lambda i: (i, 0)),
                  pl.BlockSpec((tc, LANES), lambda i: (i, 0)),
                  pl.BlockSpec((1, D_MODEL), lambda i: (0, 0)),
                  pl.BlockSpec(memory_space=pl.ANY)],
        out_specs=pl.BlockSpec((tc, D_MODEL), lambda i: (i, 0)),
        out_shape=jax.ShapeDtypeStruct((T, D_MODEL), F32),
        scratch_shapes=[pltpu.VMEM((2, TOP_K, tc, D_MODEL), F32), pltpu.SemaphoreType.DMA((2,))],
        compiler_params=_cparams("arbitrary"),
        name="combine",
    )(dest_flat, dest_flat, x1, route, g_final, ys)


def _moe_layer(x1, hf, route, ew, g_final):
    T = x1.shape[0]
    n_assign = T * TOP_K
    nblk = n_assign // MOE_BLOCK + N_EXPERTS
    n_slots = nblk * MOE_BLOCK
    dest, cnt, pst = _route(route)
    cnt_i = cnt[0, :N_EXPERTS].astype(I32)
    pstart = pst[0, :N_EXPERTS].astype(I32)
    padded = (cnt_i + MOE_BLOCK - 1) // MOE_BLOCK * MOE_BLOCK
    pend = pstart + padded
    block_expert = jnp.minimum(
        jnp.sum(jnp.arange(nblk, dtype=I32)[:, None] * MOE_BLOCK >= pend[None, :], axis=1),
        N_EXPERTS - 1).astype(I32)
    n_used = (pend[-1] // MOE_BLOCK).reshape(1).astype(I32)
    dest_flat = dest[:, :TOP_K].reshape(n_assign)
    xs = _dispatch(hf, dest_flat, pstart + cnt_i, padded - cnt_i, n_used, n_slots)
    ys = _moe(xs, block_expert, n_used, *ew)
    return _combine(x1, route, dest_flat, ys, g_final)


def _trunk(x, p, bb):
    B, L, D = x.shape
    T = B * L
    x2d = x.reshape(T, D)
    v, x1g, x2g, u = _inproj(x2d, L, p["g_mix"], p["w_in"], p["b_in"], p["short_w"], p["short_b"])
    ya = _hyena(v, x1g, x2g, B, L, p["filt"], p["filt_bias"], bb)
    ybp = _s5(u, B, L, p["s5"])
    x1, hf, route = _mix(x2d, ya, ybp, p["w_glu"], p["b_glu"], p["g_hyena"], p["g_s5"], p["w_out"],
                         p["g_ffn"], p["w_router"], p["b_router"])
    y = _moe_layer(x1, hf, route, p["experts"], p["g_final"])
    return y.reshape(B, L, D)


def kernel(x_prompt, x_sample, g_mix, w_in, b_in, short_w, short_b, filt_w1, filt_b1, filt_w2, filt_b2, filt_w3, filt_b3, filt_freq, filt_bias, s5_lam_re, s5_lam_im, s5_log_dt, s5_b_re, s5_b_im, s5_c_re, s5_c_im, s5_d, w_glu, b_glu, g_hyena, g_s5, w_out, g_ffn, w_router, b_router, w_gate, b_gate, w_up, b_up, w_down, b_down, g_final):
    assert g_mix.shape[0] == 1, "one encoder layer"
    row = lambda a: a[0][None, :].astype(F32)
    p = dict(
        g_mix=row(g_mix), w_in=w_in[0].astype(BF16), b_in=row(b_in),
        short_w=short_w[0].astype(F32), short_b=row(short_b),
        filt=(filt_w1[0], filt_b1[0], filt_w2[0], filt_b2[0], filt_w3[0], filt_b3[0], filt_freq[0]),
        filt_bias=filt_bias[0].astype(F32),
        s5=_s5_operators(s5_lam_re[0], s5_lam_im[0], s5_log_dt[0], s5_b_re[0], s5_b_im[0],
                         s5_c_re[0], s5_c_im[0], s5_d[0]),
        w_glu=w_glu[0].astype(BF16), b_glu=row(b_glu), g_hyena=row(g_hyena), g_s5=row(g_s5),
        w_out=w_out[0].astype(BF16), g_ffn=row(g_ffn),
        w_router=_split_bf16(jnp.pad(w_router[0].astype(F32), ((0, 0), (0, LANES - N_EXPERTS)))),
        b_router=jnp.pad(b_router[0].astype(F32), (0, LANES - N_EXPERTS))[None, :],
        experts=(w_gate[0].astype(BF16), b_gate[0][:, None, :], w_up[0].astype(BF16),
                 b_up[0][:, None, :], w_down[0].astype(BF16), b_down[0][:, None, :]),
        g_final=g_final[None, :].astype(F32),
    )
    y_prompt = _trunk(x_prompt, p, bb=1)
    y_sample = _trunk(x_sample, p, bb=4)
    return (y_prompt, y_sample)
```

```python
import functools
import math

import numpy as np
import jax
import jax.numpy as jnp
from jax import lax
from jax.experimental import pallas as pl
from jax.experimental.pallas import tpu as pltpu

F32 = jnp.float32
BF16 = jnp.bfloat16
I32 = jnp.int32
HIGHEST = lax.Precision.HIGHEST

LANES = 128
SUBLANES = 8
D_MODEL = 1024
C_HY = 512
C_S5 = 512
D_IN = 3 * C_HY + C_S5
Q_HY = C_HY // LANES
FILTER_BANDS = 16
FILTER_WIDTH = 64
Z_PAD = 128
DECAY_MAX = math.log(1e-2) / 0.3
DECAY_MIN = math.log(1e-2) / 1.5
S5_G, S5_H, S5_P = 32, 16, 64
S5_CHUNK = 16
S5_UW = S5_CHUNK * S5_H
S5_SW = 2 * S5_P
S5_GB = LANES // S5_H
S5_NB = S5_G // S5_GB
S5_KW = S5_CHUNK * LANES
S5_XW = S5_GB * S5_SW
S5_STEP_ROWS = 512
LAMBDA_RE_MAX = -1e-4
N_EXPERTS = 32
TOP_K = 4
SWIGLU_LIMIT = 7.0
SWIGLU_ALPHA = 1.702
MOE_BLOCK = 512
RMS_EPS = 1e-6
VMEM_LIMIT_BYTES = 56 * 1024 * 1024


def _cparams(*sem):
    return pltpu.CompilerParams(dimension_semantics=sem, vmem_limit_bytes=VMEM_LIMIT_BYTES)


def _rms(x, g):
    return x * lax.rsqrt(jnp.mean(x * x, axis=-1, keepdims=True) + RMS_EPS) * g


def _sigmoid(x):
    return 1.0 / (1.0 + jnp.exp(-x))


def _dot(a, b):
    return jnp.dot(a, b, preferred_element_type=F32)


def _split_bf16(w):
    hi = w.astype(BF16)
    return hi, (w - hi.astype(F32)).astype(BF16)


def _lane_block(x, q):
    return x[:, q * LANES:(q + 1) * LANES]


def _as_rows(ref):
    return ref.reshape(math.prod(ref.shape[:-1]), LANES)


TOKEN_TILE = (D_MODEL // LANES, LANES)


def _load_token_tiles(ref, n, row0=0):
    rows = _as_rows(ref)
    nq = TOKEN_TILE[0]
    return jnp.concatenate([rows[pl.ds(row0 * nq + q, n, stride=nq), :] for q in range(nq)], axis=1)


def _store_token_tiles(ref, x):
    rows = _as_rows(ref)
    nq = TOKEN_TILE[0]
    for q in range(nq):
        rows[pl.ds(q, x.shape[0], stride=nq), :] = _lane_block(x, q)


def _inproj_kernel(x_ref, xp_ref, xn_ref, g_ref, w_ref, b_ref, sw_ref, sb_ref,
                   v_ref, x1_ref, x2_ref, u_ref, *, tm, seq):
    i = pl.program_id(0)
    g = g_ref[...]
    h = _rms(x_ref[...], g).astype(BF16)
    proj = _dot(h, w_ref[...]) + b_ref[...]
    u_ref[...] = proj[:, 3 * C_HY:]
    z = proj[:, :3 * C_HY]
    halo = jnp.concatenate([xp_ref[...], xn_ref[...]], axis=0)
    zh = _dot(_rms(halo, g).astype(BF16), w_ref[:, :3 * C_HY]) + b_ref[:, :3 * C_HY]
    row0 = i * tm
    has_prev = lax.rem(row0, seq) != 0
    has_next = lax.rem(row0 + tm, seq) != 0
    zp = jnp.where(has_prev, zh[SUBLANES - 1:SUBLANES], 0.0)
    zn = jnp.where(has_next, zh[SUBLANES:SUBLANES + 1], 0.0)
    rid = lax.broadcasted_iota(I32, (tm, 1), 0)
    zm1 = jnp.where(rid == 0, zp, pltpu.roll(z, 1, 0))
    zp1 = jnp.where(rid == tm - 1, zn, pltpu.roll(z, tm - 1, 0))
    sw = sw_ref[...]
    o = zm1 * sw[0:1] + z * sw[1:2] + zp1 * sw[2:3] + sb_ref[...]
    for q in range(Q_HY):
        v_ref[q] = _lane_block(o, q)
        x1_ref[q] = _lane_block(o, Q_HY + q)
        x2_ref[q] = _lane_block(o, 2 * Q_HY + q)


def _inproj(x2d, seq, g_mix, w_in_bf, b_in, short_w, short_b, tm=512):
    T = x2d.shape[0]
    nb8 = T // SUBLANES
    tb = tm // SUBLANES
    hy = jax.ShapeDtypeStruct((Q_HY, T, LANES), F32)
    hy_spec = pl.BlockSpec((Q_HY, tm, LANES), lambda i: (0, i, 0))
    const = lambda i: (0, 0)
    return pl.pallas_call(
        functools.partial(_inproj_kernel, tm=tm, seq=seq),
        grid=(T // tm,),
        in_specs=[
            pl.BlockSpec((tm, D_MODEL), lambda i: (i, 0)),
            pl.BlockSpec((SUBLANES, D_MODEL), lambda i: (jnp.maximum(i * tb - 1, 0), 0)),
            pl.BlockSpec((SUBLANES, D_MODEL), lambda i: (jnp.minimum((i + 1) * tb, nb8 - 1), 0)),
            pl.BlockSpec((1, D_MODEL), const),
            pl.BlockSpec((D_MODEL, D_IN), const),
            pl.BlockSpec((1, D_IN), const),
            pl.BlockSpec((3, 3 * C_HY), const),
            pl.BlockSpec((1, 3 * C_HY), const),
        ],
        out_specs=[hy_spec, hy_spec, hy_spec, pl.BlockSpec((tm, C_S5), lambda i: (i, 0))],
        out_shape=[hy, hy, hy, jax.ShapeDtypeStruct((T, C_S5), F32)],
        compiler_params=_cparams("arbitrary"),
        name="inproj",
    )(x2d, x2d, x2d, g_mix, w_in_bf, b_in, short_w, short_b)


def _filter_kernel(z_ref, w1_ref, b1_ref, w2_ref, b2_ref, w3_ref, b3_ref, fr_ref, dl_ref,
                   k_ref, s_ref, *, tr, seq):
    i = pl.program_id(0)
    z = z_ref[...]
    fr = fr_ref[...]
    h = jnp.sin(fr[0:1] * (jnp.dot(z, w1_ref[...], precision=HIGHEST,
                                   preferred_element_type=F32) + b1_ref[...]))
    h = jnp.sin(fr[1:2] * (jnp.dot(h, w2_ref[...], precision=HIGHEST,
                                   preferred_element_type=F32) + b2_ref[...]))
    h = jnp.dot(h, w3_ref[...], precision=HIGHEST, preferred_element_type=F32) + b3_ref[...]
    h = h * jnp.exp(-z[:, 0:1] * dl_ref[...])
    rid = i * tr + lax.broadcasted_iota(I32, (tr, 1), 0)
    h = jnp.where(rid == seq, 0.0, h)
    for q in range(2 * Q_HY):
        k_ref[q] = _lane_block(h, q)

    @pl.when(i == 0)
    def _():
        s_ref[...] = jnp.zeros_like(s_ref)

    s_ref[...] += jnp.sum(jnp.abs(h), axis=0, keepdims=True)


def _filter_taps(seq, w1, b1, w2, b2, w3, b3, freq, tr=512):
    L = seq
    N = 2 * L
    n = jnp.arange(N, dtype=I32)
    pos = jnp.where(n < L, n, jnp.where(n == L, 0, N - n)).astype(F32)
    t = (pos * (1.0 / (L - 1)))[:, None]
    w = 2.0 * math.pi * pos / L
    bands = jnp.linspace(1e-4, FILTER_BANDS - 1, FILTER_BANDS, dtype=F32)
    ang = w[:, None] * bands[None, :]
    zc = jnp.concatenate([t, jnp.cos(ang), -jnp.sin(ang),
                          jnp.zeros((N, Z_PAD - 1 - 2 * FILTER_BANDS), F32)], axis=-1)
    w1p = jnp.pad(w1, ((0, Z_PAD - w1.shape[0]), (0, 0)))
    w3d = w3.reshape(FILTER_WIDTH, 2, 2 * C_HY).transpose(1, 0, 2)
    b3d = b3.reshape(2, 1, 2 * C_HY)
    deltas = jnp.abs(jnp.linspace(DECAY_MIN, DECAY_MAX, C_HY, dtype=F32))
    dl = jnp.tile(deltas, 2)[None, :]
    half = (N // tr) // 2
    const = lambda i: (0, 0)
    taps, sums = pl.pallas_call(
        functools.partial(_filter_kernel, tr=tr, seq=L),
        grid=(N // tr,),
        in_specs=[
            pl.BlockSpec((tr, Z_PAD), lambda i: (i, 0)),
            pl.BlockSpec((Z_PAD, FILTER_WIDTH), const),
            pl.BlockSpec((1, FILTER_WIDTH), const),
            pl.BlockSpec((FILTER_WIDTH, FILTER_WIDTH), const),
            pl.BlockSpec((1, FILTER_WIDTH), const),
            pl.BlockSpec((None, FILTER_WIDTH, 2 * C_HY), lambda i: (i // half, 0, 0)),
            pl.BlockSpec((None, 1, 2 * C_HY), lambda i: (i // half, 0, 0)),
            pl.BlockSpec((2, FILTER_WIDTH), const),
            pl.BlockSpec((1, 2 * C_HY), const),
        ],
        out_specs=[pl.BlockSpec((2 * Q_HY, tr, LANES), lambda i: (0, i, 0)),
                   pl.BlockSpec((1, 2 * C_HY), const)],
        out_shape=[jax.ShapeDtypeStruct((2 * Q_HY, N, LANES), F32),
                   jax.ShapeDtypeStruct((1, 2 * C_HY), F32)],
        compiler_params=_cparams("arbitrary"),
        name="hyena_filter",
    )(zc, w1p, b1[None, :], w2, b2[None, :], w3d, b3d, freq, dl)
    return taps, sums


def _split_n(N):
    n1 = {32768: 128, 4096: 64}.get(N)
    if n1 is None:
        n1 = 1 << (int(math.log2(N)) // 2)
    return n1, N // n1


def _dft_constants(N1, N2):
    N = N1 * N2
    k1 = np.arange(N1)[:, None]
    n1 = np.arange(N1)[None, :]
    ang = 2.0 * np.pi * ((k1 * n1) % N1) / N1
    fa_full = np.concatenate([np.cos(ang), -np.sin(ang)], axis=0)
    fa_half = fa_full[:, :N1 // 2]
    fa_inv = np.concatenate([np.cos(ang), -np.sin(ang)], axis=1)[:N1 // 2] / N
    k2 = np.arange(N2)[:, None]
    n2 = np.arange(N2)[None, :]
    a2 = 2.0 * np.pi * ((k2 * n2) % N2) / N2
    cr, ci = np.cos(a2), -np.sin(a2)
    f2 = np.block([[cr, -ci], [ci, cr]])
    g2 = np.block([[cr, ci], [-ci, cr]])
    to = lambda a: jnp.asarray(a.astype(np.float32)).astype(BF16)
    kk = lax.broadcasted_iota(I32, (N1, N2, LANES), 0)
    nn = lax.broadcasted_iota(I32, (N1, N2, LANES), 1)
    ta = (2.0 * math.pi / N) * lax.rem(kk * nn, N).astype(F32)
    tw = jnp.stack([jnp.cos(ta), -jnp.sin(ta)], axis=0)
    return dict(fa_full=to(fa_full), fa_half=to(fa_half), fa_inv=to(fa_inv), f2=to(f2), g2=to(g2),
                tw=tw)


def _dft_a_kernel(f_ref, x_ref, o_ref, *, rows):
    nq = x_ref.shape[0]
    x2d = _as_rows(x_ref)
    f = f_ref[...]
    for j in range(SUBLANES):
        xs = jnp.concatenate(
            [x2d[pl.ds(q * rows * SUBLANES + j, rows, stride=SUBLANES), :] for q in range(nq)], axis=1)
        r = _dot(f, xs.astype(BF16))
        for q in range(nq):
            o_ref[q, j] = _lane_block(r, q)


def _dft_a(f, x):
    Q, Bt, K, N2, _ = x.shape
    R = f.shape[0]
    return pl.pallas_call(
        functools.partial(_dft_a_kernel, rows=K),
        grid=(Bt, N2 // SUBLANES, Q // Q_HY),
        in_specs=[pl.BlockSpec((R, K), lambda b, j, w: (0, 0)),
                  pl.BlockSpec((Q_HY, None, K, SUBLANES, LANES), lambda b, j, w: (w, b, 0, j, 0))],
        out_specs=pl.BlockSpec((Q_HY, None, SUBLANES, R, LANES), lambda b, j, w: (w, b, j, 0, 0)),
        out_shape=jax.ShapeDtypeStruct((Q, Bt, N2, R, LANES), F32),
        compiler_params=_cparams("arbitrary", "arbitrary", "arbitrary"),
        name="dft_a",
    )(f, x)


def _dft_a_inv_kernel(f_ref, z_ref, gate_ref, v_ref, bias_ref, o_ref, *, rows):
    g2d = _as_rows(gate_ref)
    v2d = _as_rows(v_ref)
    o2d = _as_rows(o_ref)
    f = f_ref[...]
    bias = bias_ref[...]
    for j in range(SUBLANES):
        zj = jnp.concatenate([z_ref[q, j] for q in range(Q_HY)], axis=1)
        y = _dot(f, zj.astype(BF16))
        for q in range(Q_HY):
            sl = pl.ds(q * rows * SUBLANES + j, rows, stride=SUBLANES)
            o2d[sl, :] = g2d[sl, :] * (_lane_block(y, q) + v2d[sl, :] * _lane_block(bias, q))


def _dft_a_inv(f, z, gate, v, bias_row):
    Q, Bt, N2, R2, _ = z.shape
    K = f.shape[0]
    nat = pl.BlockSpec((Q, None, K, SUBLANES, LANES), lambda b, j: (0, b, 0, j, 0))
    return pl.pallas_call(
        functools.partial(_dft_a_inv_kernel, rows=K),
        grid=(Bt, N2 // SUBLANES),
        in_specs=[pl.BlockSpec((K, R2), lambda b, j: (0, 0)),
                  pl.BlockSpec((Q, None, SUBLANES, R2, LANES), lambda b, j: (0, b, j, 0, 0)),
                  nat, nat,
                  pl.BlockSpec((1, C_HY), lambda b, j: (0, 0))],
        out_specs=nat,
        out_shape=jax.ShapeDtypeStruct((Q, Bt, K, N2, LANES), F32),
        compiler_params=_cparams("arbitrary", "arbitrary"),
        name="dft_a_inv",
    )(f, z, gate, v, bias_row)


C_STEP_Q = 2
K1_STEP = SUBLANES


def _dft_c_kernel(a_ref, k_ref, tw_ref, f_ref, g_ref, z_ref, *, bb, n2):
    per = n2 * 2 * K1_STEP
    a2d = _as_rows(a_ref)
    z2d = _as_rows(z_ref)
    cols = [(b, q) for b in range(bb) for q in range(C_STEP_Q)]

    def rows(b, q, kk, im):
        return pl.ds((q * bb + b) * per + im * K1_STEP + kk, n2, stride=2 * K1_STEP)

    for kk in range(K1_STEP):
        twr = jnp.tile(tw_ref[0, kk], (1, len(cols)))
        twi = jnp.tile(tw_ref[1, kk], (1, len(cols)))
        kr = jnp.tile(k_ref[0, kk], (1, bb))
        ki = jnp.tile(k_ref[1, kk], (1, bb))
        ar = jnp.concatenate([a2d[rows(b, q, kk, 0), :] for b, q in cols], axis=1)
        ai = jnp.concatenate([a2d[rows(b, q, kk, 1), :] for b, q in cols], axis=1)
        pr = ar * twr - ai * twi
        pi = ar * twi + ai * twr
        x = _dot(f_ref[...], jnp.concatenate([pr, pi], axis=0).astype(BF16))
        xr = x[:n2]
        xi = x[n2:]
        yr = xr * kr - xi * ki
        yi = xr * ki + xi * kr
        zz = _dot(g_ref[...], jnp.concatenate([yr, yi], axis=0).astype(BF16))
        zr = zz[:n2]
        zi = zz[n2:]
        outr = zr * twr + zi * twi
        outi = zi * twr - zr * twi
        for c, (b, q) in enumerate(cols):
            z2d[rows(b, q, kk, 0), :] = _lane_block(outr, c)
            z2d[rows(b, q, kk, 1), :] = _lane_block(outi, c)


def _dft_c(a6, khat, order, tw, f2, g2, bb):
    Q, Bt, N2, _, N1, _ = a6.shape
    cw = C_STEP_Q * LANES
    ablk = pl.BlockSpec((C_STEP_Q, bb, N2, 2, K1_STEP, LANES), lambda g, c, b: (c, b, 0, 0, g, 0))
    return pl.pallas_call(
        functools.partial(_dft_c_kernel, bb=bb, n2=N2),
        grid=(N1 // K1_STEP, Q // C_STEP_Q, Bt // bb),
        in_specs=[ablk,
                  pl.BlockSpec((2, K1_STEP, N2, cw), lambda g, c, b: (0, g, 0, order * (C_HY // cw) + c)),
                  pl.BlockSpec((2, K1_STEP, N2, LANES), lambda g, c, b: (0, g, 0, 0)),
                  pl.BlockSpec((2 * N2, 2 * N2), lambda g, c, b: (0, 0)),
                  pl.BlockSpec((2 * N2, 2 * N2), lambda g, c, b: (0, 0))],
        out_specs=ablk,
        out_shape=jax.ShapeDtypeStruct(a6.shape, F32),
        compiler_params=_cparams("arbitrary", "arbitrary", "arbitrary"),
        name="dft_c",
    )(a6, khat, tw, f2, g2)


def _dft_c_filter_kernel(a_ref, tw_ref, f_ref, s_ref, o_ref, *, n2):
    per = n2 * 2 * K1_STEP
    a2d = _as_rows(a_ref)
    sc = 1.0 / (s_ref[...] + 1e-6)
    for kk in range(K1_STEP):
        twr = jnp.tile(tw_ref[0, kk], (1, C_STEP_Q))
        twi = jnp.tile(tw_ref[1, kk], (1, C_STEP_Q))
        ar = jnp.concatenate([a2d[pl.ds(q * per + kk, n2, stride=2 * K1_STEP), :]
                              for q in range(C_STEP_Q)], axis=1)
        ai = jnp.concatenate([a2d[pl.ds(q * per + K1_STEP + kk, n2, stride=2 * K1_STEP), :]
                              for q in range(C_STEP_Q)], axis=1)
        pr = ar * twr - ai * twi
        pi = ar * twi + ai * twr
        x = _dot(f_ref[...], jnp.concatenate([pr, pi], axis=0).astype(BF16))
        o_ref[0, kk] = x[:n2] * sc
        o_ref[1, kk] = x[n2:] * sc


def _dft_c_filter(a5, tw, f2, sums):
    Q, N2, _, N1, _ = a5.shape
    cw = C_STEP_Q * LANES
    return pl.pallas_call(
        functools.partial(_dft_c_filter_kernel, n2=N2),
        grid=(N1 // K1_STEP, Q // C_STEP_Q),
        in_specs=[pl.BlockSpec((C_STEP_Q, N2, 2, K1_STEP, LANES), lambda g, c: (c, 0, 0, g, 0)),
                  pl.BlockSpec((2, K1_STEP, N2, LANES), lambda g, c: (0, g, 0, 0)),
                  pl.BlockSpec((2 * N2, 2 * N2), lambda g, c: (0, 0)),
                  pl.BlockSpec((1, cw), lambda g, c: (0, c))],
        out_specs=pl.BlockSpec((2, K1_STEP, N2, cw), lambda g, c: (0, g, 0, c)),
        out_shape=jax.ShapeDtypeStruct((2, N1, N2, Q * LANES), F32),
        compiler_params=_cparams("arbitrary", "arbitrary"),
        name="dft_c_filter",
    )(a5, tw, f2, sums)


def _hyena(v, x1, x2, B, seq, filt, filt_bias, bb):
    L = seq
    N = 2 * L
    N1, N2 = _split_n(N)
    cst = _dft_constants(N1, N2)
    taps, sums = _filter_taps(L, *filt)
    ka = _dft_a(cst["fa_full"], taps.reshape(2 * Q_HY, 1, N1, N2, LANES))
    khat = _dft_c_filter(ka.reshape(2 * Q_HY, N2, 2, N1, LANES), cst["tw"], cst["f2"], sums)
    nat = lambda a: a.reshape(Q_HY, B, N1 // 2, N2, LANES)
    cur = nat(v)
    for o, gate in enumerate((x1, x2)):
        a = _dft_a(cst["fa_half"], cur)
        z = _dft_c(a.reshape(Q_HY, B, N2, 2, N1, LANES), khat, o, cst["tw"], cst["f2"], cst["g2"], bb)
        cur = _dft_a_inv(cst["fa_inv"], z.reshape(Q_HY, B, N2, 2 * N1, LANES), nat(gate), cur,
                         filt_bias[o][None, :])
    return cur.reshape(Q_HY, B * L, LANES)


def _s5_operators(lam_re, lam_im, log_dt, b_re, b_im, c_re, c_im, d_skip):
    Tc, G, H, P = S5_CHUNK, S5_G, S5_H, S5_P
    lam = jnp.minimum(lam_re.astype(F32), LAMBDA_RE_MAX) + 1j * lam_im.astype(F32)
    dt = jnp.exp(log_dt.astype(F32))[..., None]
    lam_dt = lam * dt
    lam_bar = jnp.exp(lam_dt)
    b_bar = ((lam_bar - 1.0) / lam)[..., None] * (b_re.astype(F32) + 1j * b_im.astype(F32))
    c = c_re.astype(F32) + 1j * c_im.astype(F32)
    tau = jnp.arange(Tc + 1, dtype=F32)
    pw = jnp.exp(lam_dt[None] * tau[:, None, None, None])
    kk = jnp.einsum('dghp,tdgp,dgpk->dtghk', c, pw[:Tc], b_bar).real
    i = jnp.arange(Tc)
    lag = i[:, None] - i[None, :]
    kf = jnp.where((lag >= 0)[:, :, None, None, None], kk[0][jnp.clip(lag, 0, Tc - 1)], 0.0)
    kb = jnp.where((lag <= 0)[:, :, None, None, None], kk[1][jnp.clip(-lag, 0, Tc - 1)], 0.0)
    m = kf + kb
    eye = (lag == 0)[:, :, None, None, None] * jnp.eye(H, dtype=F32)[None, None, None]
    m = m + eye * d_skip.astype(F32).reshape(G, H)[None, None, :, :, None]
    clf = c[0][None] * pw[1:Tc + 1, 0][:, :, None, :]
    clb = c[1][None] * pw[Tc - i, 1][:, :, None, :]
    sf = pw[Tc - 1 - i, 0][:, :, :, None] * b_bar[0][None]
    sb = pw[i, 1][:, :, :, None] * b_bar[1][None]
    w_intra = m.transpose(2, 1, 4, 0, 3).reshape(G, Tc * H, Tc * H)
    st2y = lambda cl: jnp.concatenate([cl.real, -cl.imag], axis=-1).transpose(1, 3, 0, 2) \
        .reshape(G, 2 * P, Tc * H)
    u2s = lambda s: jnp.concatenate([s.real, s.imag], axis=2).transpose(1, 0, 3, 2) \
        .reshape(G, Tc * H, 2 * P)
    w_state = jnp.concatenate([u2s(sf), u2s(sb)], axis=-1)
    pair = lambda a: a.reshape(2, G // 2, 2 * P)
    lam16 = jnp.concatenate([pair(pw[Tc].real), pair(pw[Tc].imag)], axis=-1).reshape(2, G * 2 * P)
    col = np.arange(S5_KW)
    spread = (np.arange(S5_UW)[:, None] == (col // LANES * S5_H + col % S5_H)[None, :])
    return dict(w_intra=w_intra.astype(BF16), w_xf=st2y(clf).astype(BF16), w_xb=st2y(clb).astype(BF16),
                w_state=w_state.astype(BF16), lam16=lam16,
                spread=jnp.asarray(spread.astype(np.float32)).astype(BF16))


def _s5_rows(i, cl, B, ncc):
    return pl.ds(S5_CHUNK * cl + i, B, stride=S5_CHUNK * ncc)


def _s5_chunk_inputs(u2d, B, ncc):
    cols = []
    for i in range(S5_CHUNK):
        if B == 1:
            cols.append(u2d[pl.ds(i, ncc, stride=S5_CHUNK), :])
        else:
            cols.append(jnp.concatenate([u2d[_s5_rows(i, cl, B, ncc), :] for cl in range(ncc)], axis=0))
    return jnp.concatenate(cols, axis=1).astype(BF16)


def _s5_state_lane(a, part):
    return (a // 2) * 2 * S5_SW + part * S5_SW + (a % 2) * S5_P


def _s5_state_kernel(u_ref, w_ref, sf_ref, sb_ref, wblk, *, B, ncc):
    @pl.when(pl.program_id(1) == 0)
    def _():
        wblk[...] = jnp.zeros_like(wblk)
        for a in range(S5_GB):
            for j in range(S5_CHUNK):
                r0 = j * LANES + a * S5_H
                for d in range(2):
                    for part in range(2):
                        c0 = d * S5_XW + _s5_state_lane(a, part)
                        s0 = d * S5_SW + part * S5_P
                        wblk[r0:r0 + S5_H, c0:c0 + S5_P] = w_ref[a, j * S5_H:(j + 1) * S5_H, s0:s0 + S5_P]

    lhs = _s5_chunk_inputs(_as_rows(u_ref), B, ncc)
    r = _dot(lhs, wblk[...])
    sf_ref[...] = r[:, :S5_XW]
    sb_ref[...] = r[:, S5_XW:]


def _s5_state(u3, w_state, ncc):
    B, L, _ = u3.shape
    nch = L // S5_CHUNK
    out = jax.ShapeDtypeStruct((nch * B, S5_NB * S5_XW), F32)
    return pl.pallas_call(
        functools.partial(_s5_state_kernel, B=B, ncc=ncc),
        grid=(S5_NB, nch // ncc),
        in_specs=[pl.BlockSpec((B, S5_CHUNK * ncc, LANES), lambda q, t: (0, t, q)),
                  pl.BlockSpec((S5_GB, S5_UW, 2 * S5_SW), lambda q, t: (q, 0, 0))],
        out_specs=[pl.BlockSpec((ncc * B, S5_XW), lambda q, t: (t, q))] * 2,
        out_shape=[out, out],
        scratch_shapes=[pltpu.VMEM((S5_KW, 2 * S5_XW), BF16)],
        compiler_params=_cparams("arbitrary", "arbitrary"),
        name="s5_state",
    )(u3, w_state)


def _s5_scan_kernel(sf_ref, sb_ref, lam_ref, xf_ref, xb_ref, *, nch, rows):
    lb = sf_ref.shape[1]
    vr = max(rows, SUBLANES)
    lam_f, lam_b = lam_ref[0], lam_ref[1]

    def advance(x, lam, inc):
        out = []
        for u in range(0, lb, 2 * S5_SW):
            re, im = x[:, u:u + S5_SW], x[:, u + S5_SW:u + 2 * S5_SW]
            lr, li = lam[:, u:u + S5_SW], lam[:, u + S5_SW:u + 2 * S5_SW]
            out += [lr * re - li * im, lr * im + li * re]
        return jnp.concatenate(out, axis=1) + inc

    def body(c, carry):
        xf, xb = carry
        rf = pl.ds(pl.multiple_of(c * rows, rows), rows)
        xf_ref[rf, :] = xf[:rows]
        xf = advance(xf, lam_f, sf_ref[rf, :])
        rb = pl.ds(pl.multiple_of((nch - 1 - c) * rows, rows), rows)
        xb_ref[rb, :] = xb[:rows]
        xb = advance(xb, lam_b, sb_ref[rb, :])
        return xf, xb

    zero = jnp.zeros((vr, lb), F32)
    lax.fori_loop(0, nch, body, (zero, zero))


def _s5_scan(sf, sb, lam16, nch, rows, lb):
    R, lanes = sf.shape
    blk = pl.BlockSpec((R, lb), lambda j: (0, j))
    out = jax.ShapeDtypeStruct((R, lanes), F32)
    return pl.pallas_call(
        functools.partial(_s5_scan_kernel, nch=nch, rows=rows),
        grid=(lanes // lb,),
        in_specs=[blk, blk, pl.BlockSpec((2, 1, lb), lambda j: (0, 0, j))],
        out_specs=[blk, blk],
        out_shape=[out, out],
        compiler_params=_cparams("arbitrary"),
        name="s5_scan",
    )(sf, sb, lam16[:, None, :])


def _s5_out_kernel(u_ref, xf_ref, xb_ref, wm_ref, wf_ref, wb_ref, e_ref, y_ref, wm_blk, wf_blk, wb_blk,
                   *, B, ncc):
    @pl.when(pl.program_id(1) == 0)
    def _():
        lane = lax.broadcasted_iota(I32, (1, S5_KW), 1)
        slot = (lane // S5_H) % S5_GB
        e = e_ref[...]
        for a in range(S5_GB):
            own = slot == a
            ex = jnp.where(own, _dot(wm_ref[a], e), 0.0).astype(BF16)
            for j in range(S5_CHUNK):
                r0 = j * LANES + a * S5_H
                wm_blk[r0:r0 + S5_H, :] = ex[j * S5_H:(j + 1) * S5_H, :]
            for src, dst in ((wf_ref, wf_blk), (wb_ref, wb_blk)):
                ex = jnp.where(own, _dot(src[a], e), 0.0).astype(BF16)
                for part in range(2):
                    r0 = _s5_state_lane(a, part)
                    dst[r0:r0 + S5_P, :] = ex[part * S5_P:(part + 1) * S5_P, :]

    lhs = _s5_chunk_inputs(_as_rows(u_ref), B, ncc)
    acc = _dot(lhs, wm_blk[...])
    acc += _dot(xf_ref[...].astype(BF16), wf_blk[...])
    acc += _dot(xb_ref[...].astype(BF16), wb_blk[...])
    y2d = _as_rows(y_ref)
    for i in range(S5_CHUNK):
        piece = _lane_block(acc, i)
        if B == 1:
            y2d[pl.ds(i, ncc, stride=S5_CHUNK), :] = piece
        else:
            for cl in range(ncc):
                y2d[_s5_rows(i, cl, B, ncc), :] = piece[cl * B:(cl + 1) * B]


def _s5_out(u3, xf, xb, ops, ncc):
    B, L, _ = u3.shape
    nch = L // S5_CHUNK
    tok = pl.BlockSpec((B, S5_CHUNK * ncc, LANES), lambda q, t: (0, t, q))
    st = pl.BlockSpec((ncc * B, S5_XW), lambda q, t: (t, q))
    return pl.pallas_call(
        functools.partial(_s5_out_kernel, B=B, ncc=ncc),
        grid=(S5_NB, nch // ncc),
        in_specs=[tok, st, st,
                  pl.BlockSpec((S5_GB, S5_UW, S5_UW), lambda q, t: (q, 0, 0)),
                  pl.BlockSpec((S5_GB, S5_SW, S5_UW), lambda q, t: (q, 0, 0)),
                  pl.BlockSpec((S5_GB, S5_SW, S5_UW), lambda q, t: (q, 0, 0)),
                  pl.BlockSpec((S5_UW, S5_KW), lambda q, t: (0, 0))],
        out_specs=tok,
        out_shape=jax.ShapeDtypeStruct(u3.shape, F32),
        scratch_shapes=[pltpu.VMEM((S5_KW, S5_KW), BF16), pltpu.VMEM((S5_XW, S5_KW), BF16),
                        pltpu.VMEM((S5_XW, S5_KW), BF16)],
        compiler_params=_cparams("arbitrary", "arbitrary"),
        name="s5_out",
    )(u3, xf, xb, ops["w_intra"], ops["w_xf"], ops["w_xb"], ops["spread"])


def _s5(u, B, seq, ops):
    nch = seq // S5_CHUNK
    u3 = u.reshape(B, seq, C_S5)
    ncc = min(nch, S5_STEP_ROWS // B)
    sf, sb = _s5_state(u3, ops["w_state"], ncc)
    lb = 1024 if B == 1 else 256
    xf, xb = _s5_scan(sf, sb, ops["lam16"], nch, B, lb)
    return _s5_out(u3, xf, xb, ops, ncc).reshape(B * seq, C_S5)


def _mix_kernel(x_ref, ya_ref, yb_ref, wglu_ref, bglu_ref, gh_ref, gs_ref, wout_ref, gffn_ref,
                wrh_ref, wrl_ref, br_ref, x1_ref, hf_ref, route_ref, *, tm):
    g = jax.nn.gelu(yb_ref[...])
    yb = g * _sigmoid(_dot(g.astype(BF16), wglu_ref[...]) + bglu_ref[...])
    ya = jnp.concatenate([ya_ref[q] for q in range(Q_HY)], axis=1)
    na = _rms(ya, gh_ref[...]).astype(BF16)
    nb = _rms(yb, gs_ref[...]).astype(BF16)
    mixed = _dot(na, wout_ref[:C_HY, :]) + _dot(nb, wout_ref[C_HY:, :])
    x1 = x_ref[...] + mixed
    x1_ref[...] = x1
    hf = _rms(x1, gffn_ref[...])
    _store_token_tiles(hf_ref, hf)
    hf_hi = hf.astype(BF16)
    hf_lo = (hf - hf_hi.astype(F32)).astype(BF16)
    logits = (_dot(hf_hi, wrh_ref[...]) + _dot(hf_lo, wrh_ref[...]) + _dot(hf_hi, wrl_ref[...])
              + br_ref[...])
    lane = lax.broadcasted_iota(I32, (tm, LANES), 1)
    neg = jnp.float32(-jnp.inf)
    l = jnp.where(lane < N_EXPERTS, logits, neg)
    vals, idxs = [], []
    for _ in range(TOP_K):
        m = jnp.max(l, axis=-1, keepdims=True)
        idx = jnp.min(jnp.where(l == m, lane, LANES), axis=-1, keepdims=True)
        vals.append(m)
        idxs.append(idx)
        l = jnp.where(lane == idx, neg, l)
    es = [jnp.exp(v - vals[0]) for v in vals]
    den = es[0] + es[1] + es[2] + es[3]
    route = jnp.zeros((tm, LANES), F32)
    for k in range(TOP_K):
        route = jnp.where(lane == k, es[k] / den, route)
        route = jnp.where(lane == TOP_K + k, idxs[k].astype(F32), route)
    route_ref[...] = route


def _mix(x2d, ya4, ybp, w_glu_bf, b_glu, g_hyena, g_s5, w_out_bf, g_ffn, w_router_p, b_router_p, tm=512):
    T = x2d.shape[0]
    const = lambda i: (0, 0)
    row = lambda w: pl.BlockSpec((tm, w), lambda i: (i, 0))
    return pl.pallas_call(
        functools.partial(_mix_kernel, tm=tm),
        grid=(T // tm,),
        in_specs=[row(D_MODEL), pl.BlockSpec((Q_HY, tm, LANES), lambda i: (0, i, 0)), row(C_S5),
                  pl.BlockSpec((C_S5, C_S5), const), pl.BlockSpec((1, C_S5), const),
                  pl.BlockSpec((1, C_HY), const), pl.BlockSpec((1, C_S5), const),
                  pl.BlockSpec((D_MODEL, D_MODEL), const), pl.BlockSpec((1, D_MODEL), const),
                  pl.BlockSpec((D_MODEL, LANES), const), pl.BlockSpec((D_MODEL, LANES), const),
                  pl.BlockSpec((1, LANES), const)],
        out_specs=[row(D_MODEL), pl.BlockSpec((tm,) + TOKEN_TILE, lambda i: (i, 0, 0)), row(LANES)],
        out_shape=[jax.ShapeDtypeStruct((T, D_MODEL), F32), jax.ShapeDtypeStruct((T,) + TOKEN_TILE, F32),
                   jax.ShapeDtypeStruct((T, LANES), F32)],
        compiler_params=_cparams("arbitrary"),
        name="mix",
    )(x2d, ya4, ybp, w_glu_bf, b_glu, g_hyena, g_s5, w_out_bf, g_ffn, w_router_p[0], w_router_p[1],
      b_router_p)


def _route_kernel(r_ref, dest_ref, cnt_ref, pst_ref, carry, pstart, *, tm):
    p = pl.program_id(0)
    i = pl.program_id(1)
    lane = lax.broadcasted_iota(I32, (tm, LANES), 1)
    r = r_ref[...]
    eids = [r[:, TOP_K + k:TOP_K + k + 1].astype(I32) for k in range(TOP_K)]
    oh = jnp.zeros((tm, LANES), F32)
    for e in eids:
        oh += (lane == e).astype(F32)
    tot = jnp.sum(oh, axis=0, keepdims=True)

    @pl.when((p == 0) & (i == 0))
    def _():
        carry[...] = jnp.zeros_like(carry)

    @pl.when((p == 1) & (i == 0))
    def _():
        cnt = carry[...]
        cnt_ref[...] = cnt
        padded = jnp.floor((cnt + (MOE_BLOCK - 1)) * (1.0 / MOE_BLOCK)) * MOE_BLOCK
        a = lax.broadcasted_iota(I32, (LANES, LANES), 0)
        b = lax.broadcasted_iota(I32, (LANES, LANES), 1)
        excl = jnp.dot(jnp.broadcast_to(padded, (SUBLANES, LANES)), (a < b).astype(F32),
                       precision=HIGHEST, preferred_element_type=F32)
        pstart[...] = excl[0:1]
        pst_ref[...] = excl[0:1]
        carry[...] = jnp.zeros_like(carry)

    @pl.when(p == 1)
    def _():
        a = lax.broadcasted_iota(I32, (tm, tm), 0)
        b = lax.broadcasted_iota(I32, (tm, tm), 1)
        before = _dot((b < a).astype(BF16), oh.astype(BF16))
        base = before + carry[...] + pstart[...]
        out = jnp.zeros((tm, LANES), F32)
        for k, e in enumerate(eids):
            d = jnp.sum(jnp.where(lane == e, base, 0.0), axis=-1, keepdims=True)
            out = jnp.where(lane == k, d, out)
        dest_ref[...] = out.astype(I32)

    carry[...] += tot


def _route(route, tm=512):
    T = route.shape[0]
    one = jax.ShapeDtypeStruct((1, LANES), F32)
    return pl.pallas_call(
        functools.partial(_route_kernel, tm=tm),
        grid=(2, T // tm),
        in_specs=[pl.BlockSpec((tm, LANES), lambda p, i: (i, 0))],
        out_specs=[pl.BlockSpec((tm, LANES), lambda p, i: (i * p, 0)),
                   pl.BlockSpec((1, LANES), lambda p, i: (0, 0)),
                   pl.BlockSpec((1, LANES), lambda p, i: (0, 0))],
        out_shape=[jax.ShapeDtypeStruct((T, LANES), I32), one, one],
        scratch_shapes=[pltpu.VMEM((1, LANES), F32), pltpu.VMEM((1, LANES), F32)],
        compiler_params=_cparams("arbitrary", "arbitrary"),
        name="route",
    )(route)


ZERO_ROWS = 64
DMA_UNROLL = 4


RING = 3


def _dispatch_kernel(padpos_ref, padcnt_ref, nu_ref, dest_ref, h_ref, xs_ref, zbuf, hbuf, sem, lsem, zsem,
                     *, td, nblk, nsteps):
    i = pl.program_id(0)

    @pl.when(i == 0)
    def _():
        zbuf[...] = jnp.zeros_like(zbuf)

        def per_expert(e, carry):
            off = padpos_ref[e]

            def zero_row(r):
                return pltpu.make_async_copy(zbuf.at[0], xs_ref.at[off + r], zsem)

            def z_issue(r, c):
                zero_row(r).start()
                return c

            def z_drain(r, c):
                zero_row(r).wait()
                return c

            lax.fori_loop(0, padcnt_ref[e], z_issue, 0)
            lax.fori_loop(0, padcnt_ref[e], z_drain, 0)
            return carry

        lax.fori_loop(0, N_EXPERTS, per_expert, 0)

        def zero_piece(j):
            row = pl.multiple_of(j * ZERO_ROWS, ZERO_ROWS)
            return pltpu.make_async_copy(zbuf, xs_ref.at[pl.ds(row, ZERO_ROWS)], zsem)

        per_blk = MOE_BLOCK // ZERO_ROWS

        def t_issue(j, c):
            zero_piece(j).start()
            return c

        def t_drain(j, c):
            zero_piece(j).wait()
            return c

        lax.fori_loop(nu_ref[0] * per_blk, nblk * per_blk, t_issue, 0)
        lax.fori_loop(nu_ref[0] * per_blk, nblk * per_blk, t_drain, 0)

    def load(step):
        s = lax.rem(step, RING)
        return pltpu.make_async_copy(h_ref.at[pl.ds(step * td, td)], hbuf.at[s], lsem.at[s])

    def wait_rows(step):
        s = lax.rem(step, RING)
        for _ in range(TOP_K):
            pltpu.make_async_copy(hbuf.at[s], xs_ref.at[pl.ds(0, td)], sem.at[s]).wait()

    @pl.when(i == 0)
    def _():
        load(0).start()
        if nsteps > 1:
            load(1).start()

    load(i).wait()
    slot = lax.rem(i, RING)

    def issue(r, carry):
        for k in range(TOP_K):
            pltpu.make_async_copy(hbuf.at[slot, r], xs_ref.at[dest_ref[r * TOP_K + k]],
                                  sem.at[slot]).start(priority=k % 2)
        return carry

    lax.fori_loop(0, td, issue, 0, unroll=DMA_UNROLL)

    @pl.when(i > 0)
    def _():
        wait_rows(i - 1)

    @pl.when(i + 2 < nsteps)
    def _():
        load(i + 2).start()

    @pl.when(i == nsteps - 1)
    def _():
        wait_rows(i)


def _dispatch(hf, dest_flat, padpos, padcnt, n_used, n_slots, td=128):
    T = hf.shape[0]
    nsteps = T // td
    grid_spec = pltpu.PrefetchScalarGridSpec(
        num_scalar_prefetch=3,
        grid=(nsteps,),
        in_specs=[pl.BlockSpec((td * TOP_K,), lambda i, *_: (i,), memory_space=pltpu.SMEM),
                  pl.BlockSpec(memory_space=pl.ANY)],
        out_specs=pl.BlockSpec(memory_space=pl.ANY),
        scratch_shapes=[pltpu.VMEM((ZERO_ROWS,) + TOKEN_TILE, F32), pltpu.VMEM((RING, td) + TOKEN_TILE, F32),
                        pltpu.SemaphoreType.DMA((RING,)), pltpu.SemaphoreType.DMA((RING,)),
                        pltpu.SemaphoreType.DMA(())],
    )
    return pl.pallas_call(
        functools.partial(_dispatch_kernel, td=td, nblk=n_slots // MOE_BLOCK, nsteps=nsteps),
        grid_spec=grid_spec,
        out_shape=jax.ShapeDtypeStruct((n_slots,) + TOKEN_TILE, F32),
        compiler_params=_cparams("arbitrary"),
        name="dispatch",
    )(padpos, padcnt, n_used, dest_flat, hf)


def _moe_kernel(be_ref, nu_ref, x_ref, wg_ref, bg_ref, wu_ref, bu_ref, wd_ref, bd_ref, o_ref):
    used = pl.program_id(0) < nu_ref[0]

    @pl.when(jnp.logical_not(used))
    def _():
        o_ref[...] = jnp.zeros_like(o_ref)

    @pl.when(used)
    def _():
        x = _load_token_tiles(x_ref, MOE_BLOCK).astype(BF16)
        gt = jnp.minimum(_dot(x, wg_ref[...]) + bg_ref[...], SWIGLU_LIMIT)
        up = jnp.clip(_dot(x, wu_ref[...]) + bu_ref[...], -SWIGLU_LIMIT, SWIGLU_LIMIT)
        act = (up + 1.0) * (gt * _sigmoid(SWIGLU_ALPHA * gt))
        _store_token_tiles(o_ref, _dot(act.astype(BF16), wd_ref[...]) + bd_ref[...])


def _moe(xs, block_expert, n_used, wg, bg, wu, bu, wd, bd):
    n_slots = xs.shape[0]
    nblk = n_slots // MOE_BLOCK
    blk = lambda i, be, nu: (jnp.minimum(i, nu[0] - 1), 0, 0)
    exp = lambda i, be, nu: (be[jnp.minimum(i, nu[0] - 1)], 0, 0)
    wspec = pl.BlockSpec((None, D_MODEL, D_MODEL), exp)
    bspec = pl.BlockSpec((None, 1, D_MODEL), exp)
    grid_spec = pltpu.PrefetchScalarGridSpec(
        num_scalar_prefetch=2,
        grid=(nblk,),
        in_specs=[pl.BlockSpec((MOE_BLOCK,) + TOKEN_TILE, blk), wspec, bspec, wspec, bspec, wspec, bspec],
        out_specs=pl.BlockSpec((MOE_BLOCK,) + TOKEN_TILE, lambda i, be, nu: (i, 0, 0)),
    )
    return pl.pallas_call(
        _moe_kernel,
        grid_spec=grid_spec,
        out_shape=jax.ShapeDtypeStruct((n_slots,) + TOKEN_TILE, F32),
        compiler_params=_cparams("arbitrary"),
        name="moe",
    )(block_expert, n_used, xs, wg, bg, wu, bu, wd, bd)


def _combine_kernel(dcur_ref, dnext_ref, x1_ref, r_ref, g_ref, ys_ref, o_ref, buf, sem, *, tc, nsteps):
    i = pl.program_id(0)
    slot = lax.rem(i, 2)

    def gather(d_ref, s):
        def body(r, carry):
            for k in range(TOP_K):
                pltpu.make_async_copy(ys_ref.at[d_ref[r * TOP_K + k]], buf.at[s, k, r],
                                      sem.at[s]).start(priority=k % 2)
            return carry

        lax.fori_loop(0, tc, body, 0, unroll=DMA_UNROLL)

    @pl.when(i == 0)
    def _():
        gather(dcur_ref, slot)

    @pl.when(i + 1 < nsteps)
    def _():
        gather(dnext_ref, 1 - slot)

    for k in range(TOP_K):
        pltpu.make_async_copy(ys_ref.at[pl.ds(0, tc)], buf.at[slot, k], sem.at[slot]).wait()
    gates = r_ref[...]
    acc = x1_ref[...]
    for k in range(TOP_K):
        acc += gates[:, k:k + 1] * _load_token_tiles(buf, tc, (slot * TOP_K + k) * tc)
    o_ref[...] = _rms(acc, g_ref[...])


def _combine(x1, route, dest_flat, ys, g_final, tc=128):
    T = x1.shape[0]
    nsteps = T // tc
    return pl.pallas_call(
        functools.partial(_combine_kernel, tc=tc, nsteps=nsteps),
        grid=(nsteps,),
        in_specs=[pl.BlockSpec((tc * TOP_K,), lambda i: (i,), memory_space=pltpu.SMEM),
                  pl.BlockSpec((tc * TOP_K,), lambda i: (jnp.minimum(i + 1, nsteps - 1),),
                               memory_space=pltpu.SMEM),
                  pl.BlockSpec((tc, D_MODEL), lambda i: (i, 0)),
                  pl.BlockSpec((tc, LANES), lambda i: (i, 0)),
                  pl.BlockSpec((1, D_MODEL), lambda i: (0, 0)),
                  pl.BlockSpec(memory_space=pl.ANY)],
        out_specs=pl.BlockSpec((tc, D_MODEL), lambda i: (i, 0)),
        out_shape=jax.ShapeDtypeStruct((T, D_MODEL), F32),
        scratch_shapes=[pltpu.VMEM((2, TOP_K, tc) + TOKEN_TILE, F32), pltpu.SemaphoreType.DMA((2,))],
        compiler_params=_cparams("arbitrary"),
        name="combine",
    )(dest_flat, dest_flat, x1, route, g_final, ys)


def _moe_layer(x1, hf, route, ew, g_final):
    T = x1.shape[0]
    n_assign = T * TOP_K
    nblk = n_assign // MOE_BLOCK + N_EXPERTS
    n_slots = nblk * MOE_BLOCK
    dest, cnt, pst = _route(route)
    cnt_i = cnt[0, :N_EXPERTS].astype(I32)
    pstart = pst[0, :N_EXPERTS].astype(I32)
    padded = (cnt_i + MOE_BLOCK - 1) // MOE_BLOCK * MOE_BLOCK
    pend = pstart + padded
    block_expert = jnp.minimum(
        jnp.sum(jnp.arange(nblk, dtype=I32)[:, None] * MOE_BLOCK >= pend[None, :], axis=1),
        N_EXPERTS - 1).astype(I32)
    n_used = (pend[-1] // MOE_BLOCK).reshape(1).astype(I32)
    dest_flat = dest[:, :TOP_K].reshape(n_assign)
    xs = _dispatch(hf, dest_flat, pstart + cnt_i, padded - cnt_i, n_used, n_slots)
    ys = _moe(xs, block_expert, n_used, *ew)
    return _combine(x1, route, dest_flat, ys, g_final)


def _trunk(x, p, bb):
    B, L, D = x.shape
    T = B * L
    x2d = x.reshape(T, D)
    v, x1g, x2g, u = _inproj(x2d, L, p["g_mix"], p["w_in"], p["b_in"], p["short_w"], p["short_b"])
    ya = _hyena(v, x1g, x2g, B, L, p["filt"], p["filt_bias"], bb)
    ybp = _s5(u, B, L, p["s5"])
    x1, hf, route = _mix(x2d, ya, ybp, p["w_glu"], p["b_glu"], p["g_hyena"], p["g_s5"], p["w_out"],
                         p["g_ffn"], p["w_router"], p["b_router"])
    y = _moe_layer(x1, hf, route, p["experts"], p["g_final"])
    return y.reshape(B, L, D)


def kernel(x_prompt, x_sample, g_mix, w_in, b_in, short_w, short_b, filt_w1, filt_b1, filt_w2, filt_b2, filt_w3, filt_b3, filt_freq, filt_bias, s5_lam_re, s5_lam_im, s5_log_dt, s5_b_re, s5_b_im, s5_c_re, s5_c_im, s5_d, w_glu, b_glu, g_hyena, g_s5, w_out, g_ffn, w_router, b_router, w_gate, b_gate, w_up, b_up, w_down, b_down, g_final):
    assert g_mix.shape[0] == 1, "one encoder layer"
    row = lambda a: a[0][None, :].astype(F32)
    p = dict(
        g_mix=row(g_mix), w_in=w_in[0].astype(BF16), b_in=row(b_in),
        short_w=short_w[0].astype(F32), short_b=row(short_b),
        filt=(filt_w1[0], filt_b1[0], filt_w2[0], filt_b2[0], filt_w3[0], filt_b3[0], filt_freq[0]),
        filt_bias=filt_bias[0].astype(F32),
        s5=_s5_operators(s5_lam_re[0], s5_lam_im[0], s5_log_dt[0], s5_b_re[0], s5_b_im[0],
                         s5_c_re[0], s5_c_im[0], s5_d[0]),
        w_glu=w_glu[0].astype(BF16), b_glu=row(b_glu), g_hyena=row(g_hyena), g_s5=row(g_s5),
        w_out=w_out[0].astype(BF16), g_ffn=row(g_ffn),
        w_router=_split_bf16(jnp.pad(w_router[0].astype(F32), ((0, 0), (0, LANES - N_EXPERTS)))),
        b_router=jnp.pad(b_router[0].astype(F32), (0, LANES - N_EXPERTS))[None, :],
        experts=(w_gate[0].astype(BF16), b_gate[0][:, None, :], w_up[0].astype(BF16),
                 b_up[0][:, None, :], w_down[0].astype(BF16), b_down[0][:, None, :]),
        g_final=g_final[None, :].astype(F32),
    )
    y_prompt = _trunk(x_prompt, p, bb=1)
    y_sample = _trunk(x_sample, p, bb=4)
    return (y_prompt, y_sample)
```

```python
import functools
import math

import numpy as np
import jax
import jax.numpy as jnp
from jax import lax
from jax.experimental import pallas as pl
from jax.experimental.pallas import tpu as pltpu

F32 = jnp.float32
BF16 = jnp.bfloat16
I32 = jnp.int32
HIGHEST = lax.Precision.HIGHEST

LANES = 128
SUBLANES = 8
D_MODEL = 1024
C_HY = 512
C_S5 = 512
D_IN = 3 * C_HY + C_S5
Q_HY = C_HY // LANES
FILTER_BANDS = 16
FILTER_WIDTH = 64
Z_PAD = 128
DECAY_MAX = math.log(1e-2) / 0.3
DECAY_MIN = math.log(1e-2) / 1.5
S5_G, S5_H, S5_P = 32, 16, 64
S5_CHUNK = 16
S5_UW = S5_CHUNK * S5_H
S5_SW = 2 * S5_P
S5_GB = LANES // S5_H
S5_NB = S5_G // S5_GB
S5_KW = S5_CHUNK * LANES
S5_XW = S5_GB * S5_SW
S5_STEP_ROWS = 512
LAMBDA_RE_MAX = -1e-4
N_EXPERTS = 32
TOP_K = 4
SWIGLU_LIMIT = 7.0
SWIGLU_ALPHA = 1.702
MOE_BLOCK = 512
RMS_EPS = 1e-6
VMEM_LIMIT_BYTES = 56 * 1024 * 1024


def _cparams(*sem):
    return pltpu.CompilerParams(dimension_semantics=sem, vmem_limit_bytes=VMEM_LIMIT_BYTES)


def _rms(x, g):
    return x * lax.rsqrt(jnp.mean(x * x, axis=-1, keepdims=True) + RMS_EPS) * g


def _sigmoid(x):
    return 1.0 / (1.0 + jnp.exp(-x))


def _dot(a, b):
    return jnp.dot(a, b, preferred_element_type=F32)


def _split_bf16(w):
    hi = w.astype(BF16)
    return hi, (w - hi.astype(F32)).astype(BF16)


def _lane_block(x, q):
    return x[:, q * LANES:(q + 1) * LANES]


def _as_rows(ref):
    return ref.reshape(math.prod(ref.shape[:-1]), LANES)


TOKEN_TILE = (D_MODEL // LANES, LANES)


def _load_token_tiles(ref, n, row0=0):
    rows = _as_rows(ref)
    nq = TOKEN_TILE[0]
    return jnp.concatenate([rows[pl.ds(row0 * nq + q, n, stride=nq), :] for q in range(nq)], axis=1)


def _store_token_tiles(ref, x):
    rows = _as_rows(ref)
    nq = TOKEN_TILE[0]
    for q in range(nq):
        rows[pl.ds(q, x.shape[0], stride=nq), :] = _lane_block(x, q)


def _inproj_kernel(x_ref, xp_ref, xn_ref, g_ref, w_ref, b_ref, sw_ref, sb_ref,
                   v_ref, x1_ref, x2_ref, u_ref, *, tm, seq):
    i = pl.program_id(0)
    g = g_ref[...]
    h = _rms(x_ref[...], g).astype(BF16)
    proj = _dot(h, w_ref[...]) + b_ref[...]
    u_ref[...] = proj[:, 3 * C_HY:]
    z = proj[:, :3 * C_HY]
    halo = jnp.concatenate([xp_ref[...], xn_ref[...]], axis=0)
    zh = _dot(_rms(halo, g).astype(BF16), w_ref[:, :3 * C_HY]) + b_ref[:, :3 * C_HY]
    row0 = i * tm
    has_prev = lax.rem(row0, seq) != 0
    has_next = lax.rem(row0 + tm, seq) != 0
    zp = jnp.where(has_prev, zh[SUBLANES - 1:SUBLANES], 0.0)
    zn = jnp.where(has_next, zh[SUBLANES:SUBLANES + 1], 0.0)
    rid = lax.broadcasted_iota(I32, (tm, 1), 0)
    zm1 = jnp.where(rid == 0, zp, pltpu.roll(z, 1, 0))
    zp1 = jnp.where(rid == tm - 1, zn, pltpu.roll(z, tm - 1, 0))
    sw = sw_ref[...]
    o = zm1 * sw[0:1] + z * sw[1:2] + zp1 * sw[2:3] + sb_ref[...]
    for q in range(Q_HY):
        v_ref[q] = _lane_block(o, q)
        x1_ref[q] = _lane_block(o, Q_HY + q)
        x2_ref[q] = _lane_block(o, 2 * Q_HY + q)


def _inproj(x2d, seq, g_mix, w_in_bf, b_in, short_w, short_b, tm=512):
    T = x2d.shape[0]
    nb8 = T // SUBLANES
    tb = tm // SUBLANES
    hy = jax.ShapeDtypeStruct((Q_HY, T, LANES), F32)
    hy_spec = pl.BlockSpec((Q_HY, tm, LANES), lambda i: (0, i, 0))
    const = lambda i: (0, 0)
    return pl.pallas_call(
        functools.partial(_inproj_kernel, tm=tm, seq=seq),
        grid=(T // tm,),
        in_specs=[
            pl.BlockSpec((tm, D_MODEL), lambda i: (i, 0)),
            pl.BlockSpec((SUBLANES, D_MODEL), lambda i: (jnp.maximum(i * tb - 1, 0), 0)),
            pl.BlockSpec((SUBLANES, D_MODEL), lambda i: (jnp.minimum((i + 1) * tb, nb8 - 1), 0)),
            pl.BlockSpec((1, D_MODEL), const),
            pl.BlockSpec((D_MODEL, D_IN), const),
            pl.BlockSpec((1, D_IN), const),
            pl.BlockSpec((3, 3 * C_HY), const),
            pl.BlockSpec((1, 3 * C_HY), const),
        ],
        out_specs=[hy_spec, hy_spec, hy_spec, pl.BlockSpec((tm, C_S5), lambda i: (i, 0))],
        out_shape=[hy, hy, hy, jax.ShapeDtypeStruct((T, C_S5), F32)],
        compiler_params=_cparams("arbitrary"),
        name="inproj",
    )(x2d, x2d, x2d, g_mix, w_in_bf, b_in, short_w, short_b)


def _filter_kernel(z_ref, w1_ref, b1_ref, w2_ref, b2_ref, w3h_ref, w3l_ref, b3_ref, fr_ref, dl_ref,
                   k_ref, s_ref, *, tr, seq):
    i = pl.program_id(0)
    z = z_ref[...]
    fr = fr_ref[...]
    h = jnp.sin(fr[0:1] * (jnp.dot(z, w1_ref[...], precision=HIGHEST,
                                   preferred_element_type=F32) + b1_ref[...]))
    h = jnp.sin(fr[1:2] * (jnp.dot(h, w2_ref[...], precision=HIGHEST,
                                   preferred_element_type=F32) + b2_ref[...]))
    h_hi = h.astype(BF16)
    h_lo = (h - h_hi.astype(F32)).astype(BF16)
    h = _dot(h_hi, w3h_ref[...]) + _dot(h_lo, w3h_ref[...]) + _dot(h_hi, w3l_ref[...]) + b3_ref[...]
    h = h * jnp.exp(-z[:, 0:1] * dl_ref[...])
    rid = i * tr + lax.broadcasted_iota(I32, (tr, 1), 0)
    h = jnp.where(rid == seq, 0.0, h)
    for q in range(2 * Q_HY):
        k_ref[q] = _lane_block(h, q)

    @pl.when(i == 0)
    def _():
        s_ref[...] = jnp.zeros_like(s_ref)

    s_ref[...] += jnp.sum(jnp.abs(h), axis=0, keepdims=True)


def _filter_taps(seq, w1, b1, w2, b2, w3, b3, freq, tr=512):
    L = seq
    N = 2 * L
    n = jnp.arange(N, dtype=I32)
    pos = jnp.where(n < L, n, jnp.where(n == L, 0, N - n)).astype(F32)
    t = (pos * (1.0 / (L - 1)))[:, None]
    w = 2.0 * math.pi * pos / L
    bands = jnp.linspace(1e-4, FILTER_BANDS - 1, FILTER_BANDS, dtype=F32)
    ang = w[:, None] * bands[None, :]
    zc = jnp.concatenate([t, jnp.cos(ang), -jnp.sin(ang),
                          jnp.zeros((N, Z_PAD - 1 - 2 * FILTER_BANDS), F32)], axis=-1)
    w1p = jnp.pad(w1, ((0, Z_PAD - w1.shape[0]), (0, 0)))
    w3d = w3.reshape(FILTER_WIDTH, 2, 2 * C_HY).transpose(1, 0, 2)
    b3d = b3.reshape(2, 1, 2 * C_HY)
    deltas = jnp.abs(jnp.linspace(DECAY_MIN, DECAY_MAX, C_HY, dtype=F32))
    dl = jnp.tile(deltas, 2)[None, :]
    half = (N // tr) // 2
    const = lambda i: (0, 0)
    taps, sums = pl.pallas_call(
        functools.partial(_filter_kernel, tr=tr, seq=L),
        grid=(N // tr,),
        in_specs=[
            pl.BlockSpec((tr, Z_PAD), lambda i: (i, 0)),
            pl.BlockSpec((Z_PAD, FILTER_WIDTH), const),
            pl.BlockSpec((1, FILTER_WIDTH), const),
            pl.BlockSpec((FILTER_WIDTH, FILTER_WIDTH), const),
            pl.BlockSpec((1, FILTER_WIDTH), const),
            pl.BlockSpec((None, FILTER_WIDTH, 2 * C_HY), lambda i: (i // half, 0, 0)),
            pl.BlockSpec((None, FILTER_WIDTH, 2 * C_HY), lambda i: (i // half, 0, 0)),
            pl.BlockSpec((None, 1, 2 * C_HY), lambda i: (i // half, 0, 0)),
            pl.BlockSpec((2, FILTER_WIDTH), const),
            pl.BlockSpec((1, 2 * C_HY), const),
        ],
        out_specs=[pl.BlockSpec((2 * Q_HY, tr, LANES), lambda i: (0, i, 0)),
                   pl.BlockSpec((1, 2 * C_HY), const)],
        out_shape=[jax.ShapeDtypeStruct((2 * Q_HY, N, LANES), F32),
                   jax.ShapeDtypeStruct((1, 2 * C_HY), F32)],
        compiler_params=_cparams("arbitrary"),
        name="hyena_filter",
    )(zc, w1p, b1[None, :], w2, b2[None, :], *_split_bf16(w3d.astype(F32)), b3d, freq, dl)
    return taps, sums


def _split_n(N):
    n1 = {32768: 128, 4096: 64}.get(N)
    if n1 is None:
        n1 = 1 << (int(math.log2(N)) // 2)
    return n1, N // n1


def _dft_constants(N1, N2):
    N = N1 * N2
    k1 = np.arange(N1)[:, None]
    n1 = np.arange(N1)[None, :]
    ang = 2.0 * np.pi * ((k1 * n1) % N1) / N1
    fa_full = np.concatenate([np.cos(ang), -np.sin(ang)], axis=0)
    fa_half = fa_full[:, :N1 // 2]
    fa_inv = np.concatenate([np.cos(ang), -np.sin(ang)], axis=1)[:N1 // 2] / N
    k2 = np.arange(N2)[:, None]
    n2 = np.arange(N2)[None, :]
    a2 = 2.0 * np.pi * ((k2 * n2) % N2) / N2
    cr = jnp.asarray(np.cos(a2).astype(np.float32))[None]
    ci = jnp.asarray((-np.sin(a2)).astype(np.float32))[None]
    to = lambda a: jnp.asarray(a.astype(np.float32)).astype(BF16)
    kk = lax.broadcasted_iota(I32, (N1, N2), 0)
    nn = lax.broadcasted_iota(I32, (N1, N2), 1)
    ta = (2.0 * math.pi / N) * lax.rem(kk * nn, N).astype(F32)
    tr, ti = jnp.cos(ta), -jnp.sin(ta)
    stack = lambda re, im: jnp.concatenate(
        [jnp.concatenate([re, -im], axis=-1), jnp.concatenate([im, re], axis=-1)], axis=-2).astype(BF16)
    tc_r, tc_i = tr[:, None, :], ti[:, None, :]
    fk = stack(cr * tc_r - ci * tc_i, cr * tc_i + ci * tc_r)
    tr_r, tr_i = tr[:, :, None], ti[:, :, None]
    gk = stack(cr * tr_r - ci * tr_i, -(cr * tr_i + ci * tr_r))
    return dict(fa_full=to(fa_full), fa_half=to(fa_half), fa_inv=to(fa_inv), fk=fk, gk=gk)


U32 = jnp.uint32
_HI16 = 0xFFFF0000
_HALF16 = 0x8000


def _pack_spectrum(re, im):
    rb = lax.bitcast_convert_type(re, U32) + U32(_HALF16)
    ib = lax.bitcast_convert_type(im, U32) + U32(_HALF16)
    return (rb & U32(_HI16)) | (ib >> 16)


def _unpack_spectrum(w):
    return (lax.bitcast_convert_type(w & U32(_HI16), F32), lax.bitcast_convert_type(w << 16, F32))


def _dft_a_kernel(f_ref, x_ref, o_ref, *, rows):
    nq = x_ref.shape[0]
    x2d = _as_rows(x_ref)
    f = f_ref[...]
    half = f.shape[0] // 2
    for j in range(SUBLANES):
        xs = jnp.concatenate(
            [x2d[pl.ds(q * rows * SUBLANES + j, rows, stride=SUBLANES), :] for q in range(nq)], axis=1)
        r = _dot(f, xs.astype(BF16))
        w = _pack_spectrum(r[:half], r[half:])
        for q in range(nq):
            o_ref[q, j] = _lane_block(w, q)


def _dft_a(f, x):
    Q, Bt, K, N2, _ = x.shape
    R = f.shape[0] // 2
    return pl.pallas_call(
        functools.partial(_dft_a_kernel, rows=K),
        grid=(Bt, N2 // SUBLANES, Q // Q_HY),
        in_specs=[pl.BlockSpec((2 * R, K), lambda b, j, w: (0, 0)),
                  pl.BlockSpec((Q_HY, None, K, SUBLANES, LANES), lambda b, j, w: (w, b, 0, j, 0))],
        out_specs=pl.BlockSpec((Q_HY, None, SUBLANES, R, LANES), lambda b, j, w: (w, b, j, 0, 0)),
        out_shape=jax.ShapeDtypeStruct((Q, Bt, N2, R, LANES), U32),
        compiler_params=_cparams("arbitrary", "arbitrary", "arbitrary"),
        name="dft_a",
    )(f, x)


def _dft_a_inv_kernel(f_ref, z_ref, gate_ref, v_ref, bias_ref, o_ref, *, rows):
    g2d = _as_rows(gate_ref)
    v2d = _as_rows(v_ref)
    o2d = _as_rows(o_ref)
    f = f_ref[...]
    bias = bias_ref[...]
    for j in range(SUBLANES):
        zr, zi = _unpack_spectrum(jnp.concatenate([z_ref[q, j] for q in range(Q_HY)], axis=1))
        y = _dot(f, jnp.concatenate([zr, zi], axis=0).astype(BF16))
        for q in range(Q_HY):
            sl = pl.ds(q * rows * SUBLANES + j, rows, stride=SUBLANES)
            o2d[sl, :] = g2d[sl, :] * (_lane_block(y, q) + v2d[sl, :] * _lane_block(bias, q))


def _dft_a_inv(f, z, gate, v, bias_row):
    Q, Bt, N2, N1, _ = z.shape
    K = f.shape[0]
    nat = pl.BlockSpec((Q, None, K, SUBLANES, LANES), lambda b, j: (0, b, 0, j, 0))
    return pl.pallas_call(
        functools.partial(_dft_a_inv_kernel, rows=K),
        grid=(Bt, N2 // SUBLANES),
        in_specs=[pl.BlockSpec((K, 2 * N1), lambda b, j: (0, 0)),
                  pl.BlockSpec((Q, None, SUBLANES, N1, LANES), lambda b, j: (0, b, j, 0, 0)),
                  nat, nat,
                  pl.BlockSpec((1, C_HY), lambda b, j: (0, 0))],
        out_specs=nat,
        out_shape=jax.ShapeDtypeStruct((Q, Bt, K, N2, LANES), F32),
        compiler_params=_cparams("arbitrary", "arbitrary"),
        name="dft_a_inv",
    )(f, z, gate, v, bias_row)


C_STEP_Q = 2
K1_STEP = SUBLANES


def _dft_c_kernel(a_ref, k_ref, f_ref, g_ref, z_ref, *, bb, n2):
    per = n2 * K1_STEP
    a2d = _as_rows(a_ref)
    z2d = _as_rows(z_ref)
    cols = [(b, q) for b in range(bb) for q in range(C_STEP_Q)]

    def rows(b, q, kk):
        return pl.ds((q * bb + b) * per + kk, n2, stride=K1_STEP)

    for kk in range(K1_STEP):
        kr = jnp.tile(k_ref[0, kk], (1, bb))
        ki = jnp.tile(k_ref[1, kk], (1, bb))
        ar, ai = _unpack_spectrum(jnp.concatenate([a2d[rows(b, q, kk), :] for b, q in cols], axis=1))
        x = _dot(f_ref[kk], jnp.concatenate([ar, ai], axis=0).astype(BF16))
        xr = x[:n2]
        xi = x[n2:]
        yr = xr * kr - xi * ki
        yi = xr * ki + xi * kr
        zz = _dot(g_ref[kk], jnp.concatenate([yr, yi], axis=0).astype(BF16))
        w = _pack_spectrum(zz[:n2], zz[n2:])
        for c, (b, q) in enumerate(cols):
            z2d[rows(b, q, kk), :] = _lane_block(w, c)


def _dft_c(a5, khat, order, fk, gk, bb):
    Q, Bt, N2, N1, _ = a5.shape
    cw = C_STEP_Q * LANES
    ablk = pl.BlockSpec((C_STEP_Q, bb, N2, K1_STEP, LANES), lambda g, c, b: (c, b, 0, g, 0))
    mats = pl.BlockSpec((K1_STEP, 2 * N2, 2 * N2), lambda g, c, b: (g, 0, 0))
    return pl.pallas_call(
        functools.partial(_dft_c_kernel, bb=bb, n2=N2),
        grid=(N1 // K1_STEP, Q // C_STEP_Q, Bt // bb),
        in_specs=[ablk,
                  pl.BlockSpec((2, K1_STEP, N2, cw), lambda g, c, b: (0, g, 0, order * (C_HY // cw) + c)),
                  mats, mats],
        out_specs=ablk,
        out_shape=jax.ShapeDtypeStruct(a5.shape, U32),
        compiler_params=_cparams("arbitrary", "arbitrary", "arbitrary"),
        name="dft_c",
    )(a5, khat, fk, gk)


def _dft_c_filter_kernel(a_ref, f_ref, s_ref, o_ref, *, n2):
    per = n2 * K1_STEP
    a2d = _as_rows(a_ref)
    sc = 1.0 / (s_ref[...] + 1e-6)
    for kk in range(K1_STEP):
        ar, ai = _unpack_spectrum(jnp.concatenate(
            [a2d[pl.ds(q * per + kk, n2, stride=K1_STEP), :] for q in range(C_STEP_Q)], axis=1))
        x = _dot(f_ref[kk], jnp.concatenate([ar, ai], axis=0).astype(BF16))
        o_ref[0, kk] = x[:n2] * sc
        o_ref[1, kk] = x[n2:] * sc


def _dft_c_filter(a4, fk, sums):
    Q, N2, N1, _ = a4.shape
    cw = C_STEP_Q * LANES
    return pl.pallas_call(
        functools.partial(_dft_c_filter_kernel, n2=N2),
        grid=(N1 // K1_STEP, Q // C_STEP_Q),
        in_specs=[pl.BlockSpec((C_STEP_Q, N2, K1_STEP, LANES), lambda g, c: (c, 0, g, 0)),
                  pl.BlockSpec((K1_STEP, 2 * N2, 2 * N2), lambda g, c: (g, 0, 0)),
                  pl.BlockSpec((1, cw), lambda g, c: (0, c))],
        out_specs=pl.BlockSpec((2, K1_STEP, N2, cw), lambda g, c: (0, g, 0, c)),
        out_shape=jax.ShapeDtypeStruct((2, N1, N2, Q * LANES), F32),
        compiler_params=_cparams("arbitrary", "arbitrary"),
        name="dft_c_filter",
    )(a4, fk, sums)


def _hyena(v, x1, x2, B, seq, filt, filt_bias, bb):
    L = seq
    N = 2 * L
    N1, N2 = _split_n(N)
    cst = _dft_constants(N1, N2)
    taps, sums = _filter_taps(L, *filt)
    ka = _dft_a(cst["fa_full"], taps.reshape(2 * Q_HY, 1, N1, N2, LANES))
    khat = _dft_c_filter(ka.reshape(2 * Q_HY, N2, N1, LANES), cst["fk"], sums)
    nat = lambda a: a.reshape(Q_HY, B, N1 // 2, N2, LANES)
    cur = nat(v)
    for o, gate in enumerate((x1, x2)):
        a = _dft_a(cst["fa_half"], cur)
        z = _dft_c(a, khat, o, cst["fk"], cst["gk"], bb)
        cur = _dft_a_inv(cst["fa_inv"], z, nat(gate), cur, filt_bias[o][None, :])
    return cur.reshape(Q_HY, B * L, LANES)


def _s5_operators(lam_re, lam_im, log_dt, b_re, b_im, c_re, c_im, d_skip):
    Tc, G, H, P = S5_CHUNK, S5_G, S5_H, S5_P
    lam = jnp.minimum(lam_re.astype(F32), LAMBDA_RE_MAX) + 1j * lam_im.astype(F32)
    dt = jnp.exp(log_dt.astype(F32))[..., None]
    lam_dt = lam * dt
    lam_bar = jnp.exp(lam_dt)
    b_bar = ((lam_bar - 1.0) / lam)[..., None] * (b_re.astype(F32) + 1j * b_im.astype(F32))
    c = c_re.astype(F32) + 1j * c_im.astype(F32)
    tau = jnp.arange(Tc + 1, dtype=F32)
    pw = jnp.exp(lam_dt[None] * tau[:, None, None, None])
    kk = jnp.einsum('dghp,tdgp,dgpk->dtghk', c, pw[:Tc], b_bar).real
    i = jnp.arange(Tc)
    lag = i[:, None] - i[None, :]
    kf = jnp.where((lag >= 0)[:, :, None, None, None], kk[0][jnp.clip(lag, 0, Tc - 1)], 0.0)
    kb = jnp.where((lag <= 0)[:, :, None, None, None], kk[1][jnp.clip(-lag, 0, Tc - 1)], 0.0)
    m = kf + kb
    eye = (lag == 0)[:, :, None, None, None] * jnp.eye(H, dtype=F32)[None, None, None]
    m = m + eye * d_skip.astype(F32).reshape(G, H)[None, None, :, :, None]
    clf = c[0][None] * pw[1:Tc + 1, 0][:, :, None, :]
    clb = c[1][None] * pw[Tc - i, 1][:, :, None, :]
    sf = pw[Tc - 1 - i, 0][:, :, :, None] * b_bar[0][None]
    sb = pw[i, 1][:, :, :, None] * b_bar[1][None]
    w_intra = m.transpose(2, 1, 4, 0, 3).reshape(G, Tc * H, Tc * H)
    st2y = lambda cl: jnp.concatenate([cl.real, -cl.imag], axis=-1).transpose(1, 3, 0, 2) \
        .reshape(G, 2 * P, Tc * H)
    u2s = lambda s: jnp.concatenate([s.real, s.imag], axis=2).transpose(1, 0, 3, 2) \
        .reshape(G, Tc * H, 2 * P)
    w_state = jnp.concatenate([u2s(sf), u2s(sb)], axis=-1)
    pair = lambda a: a.reshape(2, G // 2, 2 * P)
    lam16 = jnp.concatenate([pair(pw[Tc].real), pair(pw[Tc].imag)], axis=-1).reshape(2, G * 2 * P)
    col = np.arange(S5_KW)
    spread = (np.arange(S5_UW)[:, None] == (col // LANES * S5_H + col % S5_H)[None, :])
    return dict(w_intra=w_intra.astype(BF16), w_xf=st2y(clf).astype(BF16), w_xb=st2y(clb).astype(BF16),
                w_state=w_state.astype(BF16), lam16=lam16,
                spread=jnp.asarray(spread.astype(np.float32)).astype(BF16))


def _s5_rows(i, cl, B, ncc):
    return pl.ds(S5_CHUNK * cl + i, B, stride=S5_CHUNK * ncc)


def _s5_chunk_inputs(u2d, B, ncc):
    cols = []
    for i in range(S5_CHUNK):
        if B == 1:
            cols.append(u2d[pl.ds(i, ncc, stride=S5_CHUNK), :])
        else:
            cols.append(jnp.concatenate([u2d[_s5_rows(i, cl, B, ncc), :] for cl in range(ncc)], axis=0))
    return jnp.concatenate(cols, axis=1).astype(BF16)


def _s5_state_lane(a, part):
    return (a // 2) * 2 * S5_SW + part * S5_SW + (a % 2) * S5_P


def _s5_state_kernel(u_ref, w_ref, sf_ref, sb_ref, wblk, *, B, ncc):
    @pl.when(pl.program_id(1) == 0)
    def _():
        wblk[...] = jnp.zeros_like(wblk)
        for a in range(S5_GB):
            for j in range(S5_CHUNK):
                r0 = j * LANES + a * S5_H
                for d in range(2):
                    for part in range(2):
                        c0 = d * S5_XW + _s5_state_lane(a, part)
                        s0 = d * S5_SW + part * S5_P
                        wblk[r0:r0 + S5_H, c0:c0 + S5_P] = w_ref[a, j * S5_H:(j + 1) * S5_H, s0:s0 + S5_P]

    lhs = _s5_chunk_inputs(_as_rows(u_ref), B, ncc)
    r = _dot(lhs, wblk[...])
    sf_ref[...] = r[:, :S5_XW]
    sb_ref[...] = r[:, S5_XW:]


def _s5_state(u3, w_state, ncc):
    B, L, _ = u3.shape
    nch = L // S5_CHUNK
    out = jax.ShapeDtypeStruct((nch * B, S5_NB * S5_XW), F32)
    return pl.pallas_call(
        functools.partial(_s5_state_kernel, B=B, ncc=ncc),
        grid=(S5_NB, nch // ncc),
        in_specs=[pl.BlockSpec((B, S5_CHUNK * ncc, LANES), lambda q, t: (0, t, q)),
                  pl.BlockSpec((S5_GB, S5_UW, 2 * S5_SW), lambda q, t: (q, 0, 0))],
        out_specs=[pl.BlockSpec((ncc * B, S5_XW), lambda q, t: (t, q))] * 2,
        out_shape=[out, out],
        scratch_shapes=[pltpu.VMEM((S5_KW, 2 * S5_XW), BF16)],
        compiler_params=_cparams("arbitrary", "arbitrary"),
        name="s5_state",
    )(u3, w_state)


def _s5_scan_kernel(sf_ref, sb_ref, lam_ref, xf_ref, xb_ref, *, nch, rows):
    lb = sf_ref.shape[1]
    vr = max(rows, SUBLANES)
    lam_f, lam_b = lam_ref[0], lam_ref[1]

    def advance(x, lam, inc):
        out = []
        for u in range(0, lb, 2 * S5_SW):
            re, im = x[:, u:u + S5_SW], x[:, u + S5_SW:u + 2 * S5_SW]
            lr, li = lam[:, u:u + S5_SW], lam[:, u + S5_SW:u + 2 * S5_SW]
            out += [lr * re - li * im, lr * im + li * re]
        return jnp.concatenate(out, axis=1) + inc

    def body(c, carry):
        xf, xb = carry
        rf = pl.ds(pl.multiple_of(c * rows, rows), rows)
        xf_ref[rf, :] = xf[:rows]
        xf = advance(xf, lam_f, sf_ref[rf, :])
        rb = pl.ds(pl.multiple_of((nch - 1 - c) * rows, rows), rows)
        xb_ref[rb, :] = xb[:rows]
        xb = advance(xb, lam_b, sb_ref[rb, :])
        return xf, xb

    zero = jnp.zeros((vr, lb), F32)
    lax.fori_loop(0, nch, body, (zero, zero))


def _s5_scan(sf, sb, lam16, nch, rows, lb):
    R, lanes = sf.shape
    blk = pl.BlockSpec((R, lb), lambda j: (0, j))
    out = jax.ShapeDtypeStruct((R, lanes), F32)
    return pl.pallas_call(
        functools.partial(_s5_scan_kernel, nch=nch, rows=rows),
        grid=(lanes // lb,),
        in_specs=[blk, blk, pl.BlockSpec((2, 1, lb), lambda j: (0, 0, j))],
        out_specs=[blk, blk],
        out_shape=[out, out],
        compiler_params=_cparams("arbitrary"),
        name="s5_scan",
    )(sf, sb, lam16[:, None, :])


def _s5_out_kernel(u_ref, xf_ref, xb_ref, wm_ref, wf_ref, wb_ref, e_ref, y_ref, wm_blk, wf_blk, wb_blk,
                   *, B, ncc):
    @pl.when(pl.program_id(1) == 0)
    def _():
        lane = lax.broadcasted_iota(I32, (1, S5_KW), 1)
        slot = (lane // S5_H) % S5_GB
        e = e_ref[...]
        for a in range(S5_GB):
            own = slot == a
            ex = jnp.where(own, _dot(wm_ref[a], e), 0.0).astype(BF16)
            for j in range(S5_CHUNK):
                r0 = j * LANES + a * S5_H
                wm_blk[r0:r0 + S5_H, :] = ex[j * S5_H:(j + 1) * S5_H, :]
            for src, dst in ((wf_ref, wf_blk), (wb_ref, wb_blk)):
                ex = jnp.where(own, _dot(src[a], e), 0.0).astype(BF16)
                for part in range(2):
                    r0 = _s5_state_lane(a, part)
                    dst[r0:r0 + S5_P, :] = ex[part * S5_P:(part + 1) * S5_P, :]

    lhs = _s5_chunk_inputs(_as_rows(u_ref), B, ncc)
    acc = _dot(lhs, wm_blk[...])
    acc += _dot(xf_ref[...].astype(BF16), wf_blk[...])
    acc += _dot(xb_ref[...].astype(BF16), wb_blk[...])
    y2d = _as_rows(y_ref)
    for i in range(S5_CHUNK):
        piece = _lane_block(acc, i)
        if B == 1:
            y2d[pl.ds(i, ncc, stride=S5_CHUNK), :] = piece
        else:
            for cl in range(ncc):
                y2d[_s5_rows(i, cl, B, ncc), :] = piece[cl * B:(cl + 1) * B]


def _s5_out(u3, xf, xb, ops, ncc):
    B, L, _ = u3.shape
    nch = L // S5_CHUNK
    tok = pl.BlockSpec((B, S5_CHUNK * ncc, LANES), lambda q, t: (0, t, q))
    st = pl.BlockSpec((ncc * B, S5_XW), lambda q, t: (t, q))
    return pl.pallas_call(
        functools.partial(_s5_out_kernel, B=B, ncc=ncc),
        grid=(S5_NB, nch // ncc),
        in_specs=[tok, st, st,
                  pl.BlockSpec((S5_GB, S5_UW, S5_UW), lambda q, t: (q, 0, 0)),
                  pl.BlockSpec((S5_GB, S5_SW, S5_UW), lambda q, t: (q, 0, 0)),
                  pl.BlockSpec((S5_GB, S5_SW, S5_UW), lambda q, t: (q, 0, 0)),
                  pl.BlockSpec((S5_UW, S5_KW), lambda q, t: (0, 0))],
        out_specs=tok,
        out_shape=jax.ShapeDtypeStruct(u3.shape, F32),
        scratch_shapes=[pltpu.VMEM((S5_KW, S5_KW), BF16), pltpu.VMEM((S5_XW, S5_KW), BF16),
                        pltpu.VMEM((S5_XW, S5_KW), BF16)],
        compiler_params=_cparams("arbitrary", "arbitrary"),
        name="s5_out",
    )(u3, xf, xb, ops["w_intra"], ops["w_xf"], ops["w_xb"], ops["spread"])


def _s5(u, B, seq, ops):
    nch = seq // S5_CHUNK
    u3 = u.reshape(B, seq, C_S5)
    ncc = min(nch, S5_STEP_ROWS // B)
    sf, sb = _s5_state(u3, ops["w_state"], ncc)
    lb = 1024 if B == 1 else 256
    xf, xb = _s5_scan(sf, sb, ops["lam16"], nch, B, lb)
    return _s5_out(u3, xf, xb, ops, ncc).reshape(B * seq, C_S5)


def _mix_kernel(x_ref, ya_ref, yb_ref, wglu_ref, bglu_ref, gh_ref, gs_ref, wout_ref, gffn_ref,
                wrh_ref, wrl_ref, br_ref, x1_ref, hf_ref, route_ref, *, tm):
    g = jax.nn.gelu(yb_ref[...])
    yb = g * _sigmoid(_dot(g.astype(BF16), wglu_ref[...]) + bglu_ref[...])
    ya = jnp.concatenate([ya_ref[q] for q in range(Q_HY)], axis=1)
    na = _rms(ya, gh_ref[...]).astype(BF16)
    nb = _rms(yb, gs_ref[...]).astype(BF16)
    mixed = _dot(na, wout_ref[:C_HY, :]) + _dot(nb, wout_ref[C_HY:, :])
    x1 = x_ref[...] + mixed
    x1_ref[...] = x1
    hf = _rms(x1, gffn_ref[...])
    _store_token_tiles(hf_ref, hf)
    hf_hi = hf.astype(BF16)
    hf_lo = (hf - hf_hi.astype(F32)).astype(BF16)
    logits = (_dot(hf_hi, wrh_ref[...]) + _dot(hf_lo, wrh_ref[...]) + _dot(hf_hi, wrl_ref[...])
              + br_ref[...])
    lane = lax.broadcasted_iota(I32, (tm, LANES), 1)
    neg = jnp.float32(-jnp.inf)
    l = jnp.where(lane < N_EXPERTS, logits, neg)
    vals, idxs = [], []
    for _ in range(TOP_K):
        m = jnp.max(l, axis=-1, keepdims=True)
        idx = jnp.min(jnp.where(l == m, lane, LANES), axis=-1, keepdims=True)
        vals.append(m)
        idxs.append(idx)
        l = jnp.where(lane == idx, neg, l)
    es = [jnp.exp(v - vals[0]) for v in vals]
    den = es[0] + es[1] + es[2] + es[3]
    route = jnp.zeros((tm, LANES), F32)
    for k in range(TOP_K):
        route = jnp.where(lane == k, es[k] / den, route)
        route = jnp.where(lane == TOP_K + k, idxs[k].astype(F32), route)
    route_ref[...] = route


def _mix(x2d, ya4, ybp, w_glu_bf, b_glu, g_hyena, g_s5, w_out_bf, g_ffn, w_router_p, b_router_p, tm=512):
    T = x2d.shape[0]
    const = lambda i: (0, 0)
    row = lambda w: pl.BlockSpec((tm, w), lambda i: (i, 0))
    return pl.pallas_call(
        functools.partial(_mix_kernel, tm=tm),
        grid=(T // tm,),
        in_specs=[row(D_MODEL), pl.BlockSpec((Q_HY, tm, LANES), lambda i: (0, i, 0)), row(C_S5),
                  pl.BlockSpec((C_S5, C_S5), const), pl.BlockSpec((1, C_S5), const),
                  pl.BlockSpec((1, C_HY), const), pl.BlockSpec((1, C_S5), const),
                  pl.BlockSpec((D_MODEL, D_MODEL), const), pl.BlockSpec((1, D_MODEL), const),
                  pl.BlockSpec((D_MODEL, LANES), const), pl.BlockSpec((D_MODEL, LANES), const),
                  pl.BlockSpec((1, LANES), const)],
        out_specs=[row(D_MODEL), pl.BlockSpec((tm,) + TOKEN_TILE, lambda i: (i, 0, 0)), row(LANES)],
        out_shape=[jax.ShapeDtypeStruct((T, D_MODEL), F32), jax.ShapeDtypeStruct((T,) + TOKEN_TILE, F32),
                   jax.ShapeDtypeStruct((T, LANES), F32)],
        compiler_params=_cparams("arbitrary"),
        name="mix",
    )(x2d, ya4, ybp, w_glu_bf, b_glu, g_hyena, g_s5, w_out_bf, g_ffn, w_router_p[0], w_router_p[1],
      b_router_p)


def _route_kernel(r_ref, dest_ref, cnt_ref, pst_ref, carry, pstart, *, tm):
    p = pl.program_id(0)
    i = pl.program_id(1)
    lane = lax.broadcasted_iota(I32, (tm, LANES), 1)
    r = r_ref[...]
    eids = [r[:, TOP_K + k:TOP_K + k + 1].astype(I32) for k in range(TOP_K)]
    oh = jnp.zeros((tm, LANES), F32)
    for e in eids:
        oh += (lane == e).astype(F32)
    tot = jnp.sum(oh, axis=0, keepdims=True)

    @pl.when((p == 0) & (i == 0))
    def _():
        carry[...] = jnp.zeros_like(carry)

    @pl.when((p == 1) & (i == 0))
    def _():
        cnt = carry[...]
        cnt_ref[...] = cnt
        padded = jnp.floor((cnt + (MOE_BLOCK - 1)) * (1.0 / MOE_BLOCK)) * MOE_BLOCK
        a = lax.broadcasted_iota(I32, (LANES, LANES), 0)
        b = lax.broadcasted_iota(I32, (LANES, LANES), 1)
        excl = jnp.dot(jnp.broadcast_to(padded, (SUBLANES, LANES)), (a < b).astype(F32),
                       precision=HIGHEST, preferred_element_type=F32)
        pstart[...] = excl[0:1]
        pst_ref[...] = excl[0:1]
        carry[...] = jnp.zeros_like(carry)

    @pl.when(p == 1)
    def _():
        a = lax.broadcasted_iota(I32, (tm, tm), 0)
        b = lax.broadcasted_iota(I32, (tm, tm), 1)
        before = _dot((b < a).astype(BF16), oh.astype(BF16))
        base = before + carry[...] + pstart[...]
        out = jnp.zeros((tm, LANES), F32)
        for k, e in enumerate(eids):
            d = jnp.sum(jnp.where(lane == e, base, 0.0), axis=-1, keepdims=True)
            out = jnp.where(lane == k, d, out)
        dest_ref[...] = out.astype(I32)

    carry[...] += tot


def _route(route, tm=512):
    T = route.shape[0]
    one = jax.ShapeDtypeStruct((1, LANES), F32)
    return pl.pallas_call(
        functools.partial(_route_kernel, tm=tm),
        grid=(2, T // tm),
        in_specs=[pl.BlockSpec((tm, LANES), lambda p, i: (i, 0))],
        out_specs=[pl.BlockSpec((tm, LANES), lambda p, i: (i * p, 0)),
                   pl.BlockSpec((1, LANES), lambda p, i: (0, 0)),
                   pl.BlockSpec((1, LANES), lambda p, i: (0, 0))],
        out_shape=[jax.ShapeDtypeStruct((T, LANES), I32), one, one],
        scratch_shapes=[pltpu.VMEM((1, LANES), F32), pltpu.VMEM((1, LANES), F32)],
        compiler_params=_cparams("arbitrary", "arbitrary"),
        name="route",
    )(route)


ZERO_ROWS = 64
DMA_UNROLL = 4


RING = 3


def _dispatch_kernel(padpos_ref, padcnt_ref, nu_ref, dest_ref, h_ref, xs_ref, zbuf, hbuf, sem, lsem, zsem,
                     *, td, nblk, nsteps):
    i = pl.program_id(0)

    @pl.when(i == 0)
    def _():
        zbuf[...] = jnp.zeros_like(zbuf)

        def per_expert(e, carry):
            off = padpos_ref[e]

            def zero_row(r):
                return pltpu.make_async_copy(zbuf.at[0], xs_ref.at[off + r], zsem)

            def z_issue(r, c):
                zero_row(r).start()
                return c

            def z_drain(r, c):
                zero_row(r).wait()
                return c

            lax.fori_loop(0, padcnt_ref[e], z_issue, 0)
            lax.fori_loop(0, padcnt_ref[e], z_drain, 0)
            return carry

        lax.fori_loop(0, N_EXPERTS, per_expert, 0)

        def zero_piece(j):
            row = pl.multiple_of(j * ZERO_ROWS, ZERO_ROWS)
            return pltpu.make_async_copy(zbuf, xs_ref.at[pl.ds(row, ZERO_ROWS)], zsem)

        per_blk = MOE_BLOCK // ZERO_ROWS

        def t_issue(j, c):
            zero_piece(j).start()
            return c

        def t_drain(j, c):
            zero_piece(j).wait()
            return c

        lax.fori_loop(nu_ref[0] * per_blk, nblk * per_blk, t_issue, 0)
        lax.fori_loop(nu_ref[0] * per_blk, nblk * per_blk, t_drain, 0)

    def load(step):
        s = lax.rem(step, RING)
        return pltpu.make_async_copy(h_ref.at[pl.ds(step * td, td)], hbuf.at[s], lsem.at[s])

    def wait_rows(step):
        s = lax.rem(step, RING)
        for _ in range(TOP_K):
            pltpu.make_async_copy(hbuf.at[s], xs_ref.at[pl.ds(0, td)], sem.at[s]).wait()

    @pl.when(i == 0)
    def _():
        load(0).start()
        if nsteps > 1:
            load(1).start()

    load(i).wait()
    slot = lax.rem(i, RING)

    def issue(r, carry):
        for k in range(TOP_K):
            pltpu.make_async_copy(hbuf.at[slot, r], xs_ref.at[dest_ref[r * TOP_K + k]],
                                  sem.at[slot]).start(priority=k % 2)
        return carry

    lax.fori_loop(0, td, issue, 0, unroll=DMA_UNROLL)

    @pl.when(i > 0)
    def _():
        wait_rows(i - 1)

    @pl.when(i + 2 < nsteps)
    def _():
        load(i + 2).start()

    @pl.when(i == nsteps - 1)
    def _():
        wait_rows(i)


def _dispatch(hf, dest_flat, padpos, padcnt, n_used, n_slots, td=128):
    T = hf.shape[0]
    nsteps = T // td
    grid_spec = pltpu.PrefetchScalarGridSpec(
        num_scalar_prefetch=3,
        grid=(nsteps,),
        in_specs=[pl.BlockSpec((td * TOP_K,), lambda i, *_: (i,), memory_space=pltpu.SMEM),
                  pl.BlockSpec(memory_space=pl.ANY)],
        out_specs=pl.BlockSpec(memory_space=pl.ANY),
        scratch_shapes=[pltpu.VMEM((ZERO_ROWS,) + TOKEN_TILE, F32), pltpu.VMEM((RING, td) + TOKEN_TILE, F32),
                        pltpu.SemaphoreType.DMA((RING,)), pltpu.SemaphoreType.DMA((RING,)),
                        pltpu.SemaphoreType.DMA(())],
    )
    return pl.pallas_call(
        functools.partial(_dispatch_kernel, td=td, nblk=n_slots // MOE_BLOCK, nsteps=nsteps),
        grid_spec=grid_spec,
        out_shape=jax.ShapeDtypeStruct((n_slots,) + TOKEN_TILE, F32),
        compiler_params=_cparams("arbitrary"),
        name="dispatch",
    )(padpos, padcnt, n_used, dest_flat, hf)


def _moe_kernel(be_ref, nu_ref, x_ref, wg_ref, bg_ref, wu_ref, bu_ref, wd_ref, bd_ref, o_ref):
    used = pl.program_id(0) < nu_ref[0]

    @pl.when(jnp.logical_not(used))
    def _():
        o_ref[...] = jnp.zeros_like(o_ref)

    @pl.when(used)
    def _():
        x = _load_token_tiles(x_ref, MOE_BLOCK).astype(BF16)
        gt = jnp.minimum(_dot(x, wg_ref[...]) + bg_ref[...], SWIGLU_LIMIT)
        up = jnp.clip(_dot(x, wu_ref[...]) + bu_ref[...], -SWIGLU_LIMIT, SWIGLU_LIMIT)
        act = (up + 1.0) * (gt * _sigmoid(SWIGLU_ALPHA * gt))
        _store_token_tiles(o_ref, _dot(act.astype(BF16), wd_ref[...]) + bd_ref[...])


def _moe(xs, block_expert, n_used, wg, bg, wu, bu, wd, bd):
    n_slots = xs.shape[0]
    nblk = n_slots // MOE_BLOCK
    blk = lambda i, be, nu: (jnp.minimum(i, nu[0] - 1), 0, 0)
    exp = lambda i, be, nu: (be[jnp.minimum(i, nu[0] - 1)], 0, 0)
    wspec = pl.BlockSpec((None, D_MODEL, D_MODEL), exp)
    bspec = pl.BlockSpec((None, 1, D_MODEL), exp)
    grid_spec = pltpu.PrefetchScalarGridSpec(
        num_scalar_prefetch=2,
        grid=(nblk,),
        in_specs=[pl.BlockSpec((MOE_BLOCK,) + TOKEN_TILE, blk), wspec, bspec, wspec, bspec, wspec, bspec],
        out_specs=pl.BlockSpec((MOE_BLOCK,) + TOKEN_TILE, lambda i, be, nu: (i, 0, 0)),
    )
    return pl.pallas_call(
        _moe_kernel,
        grid_spec=grid_spec,
        out_shape=jax.ShapeDtypeStruct((n_slots,) + TOKEN_TILE, F32),
        compiler_params=_cparams("arbitrary"),
        name="moe",
    )(block_expert, n_used, xs, wg, bg, wu, bu, wd, bd)


def _combine_kernel(dcur_ref, dnext_ref, x1_ref, r_ref, g_ref, ys_ref, o_ref, buf, sem, *, tc, nsteps):
    i = pl.program_id(0)
    slot = lax.rem(i, 2)

    def gather(d_ref, s):
        def body(r, carry):
            for k in range(TOP_K):
                pltpu.make_async_copy(ys_ref.at[d_ref[r * TOP_K + k]], buf.at[s, k, r],
                                      sem.at[s]).start(priority=k % 2)
            return carry

        lax.fori_loop(0, tc, body, 0, unroll=DMA_UNROLL)

    @pl.when(i == 0)
    def _():
        gather(dcur_ref, slot)

    @pl.when(i + 1 < nsteps)
    def _():
        gather(dnext_ref, 1 - slot)

    for k in range(TOP_K):
        pltpu.make_async_copy(ys_ref.at[pl.ds(0, tc)], buf.at[slot, k], sem.at[slot]).wait()
    gates = r_ref[...]
    acc = x1_ref[...]
    for k in range(TOP_K):
        acc += gates[:, k:k + 1] * _load_token_tiles(buf, tc, (slot * TOP_K + k) * tc)
    o_ref[...] = _rms(acc, g_ref[...])


def _combine(x1, route, dest_flat, ys, g_final, tc=128):
    T = x1.shape[0]
    nsteps = T // tc
    return pl.pallas_call(
        functools.partial(_combine_kernel, tc=tc, nsteps=nsteps),
        grid=(nsteps,),
        in_specs=[pl.BlockSpec((tc * TOP_K,), lambda i: (i,), memory_space=pltpu.SMEM),
                  pl.BlockSpec((tc * TOP_K,), lambda i: (jnp.minimum(i + 1, nsteps - 1),),
                               memory_space=pltpu.SMEM),
                  pl.BlockSpec((tc, D_MODEL), lambda i: (i, 0)),
                  pl.BlockSpec((tc, LANES), lambda i: (i, 0)),
                  pl.BlockSpec((1, D_MODEL), lambda i: (0, 0)),
                  pl.BlockSpec(memory_space=pl.ANY)],
        out_specs=pl.BlockSpec((tc, D_MODEL), lambda i: (i, 0)),
        out_shape=jax.ShapeDtypeStruct((T, D_MODEL), F32),
        scratch_shapes=[pltpu.VMEM((2, TOP_K, tc) + TOKEN_TILE, F32), pltpu.SemaphoreType.DMA((2,))],
        compiler_params=_cparams("arbitrary"),
        name="combine",
    )(dest_flat, dest_flat, x1, route, g_final, ys)


def _moe_layer(x1, hf, route, ew, g_final):
    T = x1.shape[0]
    n_assign = T * TOP_K
    nblk = n_assign // MOE_BLOCK + N_EXPERTS
    n_slots = nblk * MOE_BLOCK
    dest, cnt, pst = _route(route)
    cnt_i = cnt[0, :N_EXPERTS].astype(I32)
    pstart = pst[0, :N_EXPERTS].astype(I32)
    padded = (cnt_i + MOE_BLOCK - 1) // MOE_BLOCK * MOE_BLOCK
    pend = pstart + padded
    block_expert = jnp.minimum(
        jnp.sum(jnp.arange(nblk, dtype=I32)[:, None] * MOE_BLOCK >= pend[None, :], axis=1),
        N_EXPERTS - 1).astype(I32)
    n_used = (pend[-1] // MOE_BLOCK).reshape(1).astype(I32)
    dest_flat = dest[:, :TOP_K].reshape(n_assign)
    xs = _dispatch(hf, dest_flat, pstart + cnt_i, padded - cnt_i, n_used, n_slots)
    ys = _moe(xs, block_expert, n_used, *ew)
    return _combine(x1, route, dest_flat, ys, g_final)


def _trunk(x, p, bb):
    B, L, D = x.shape
    T = B * L
    x2d = x.reshape(T, D)
    v, x1g, x2g, u = _inproj(x2d, L, p["g_mix"], p["w_in"], p["b_in"], p["short_w"], p["short_b"])
    ya = _hyena(v, x1g, x2g, B, L, p["filt"], p["filt_bias"], bb)
    ybp = _s5(u, B, L, p["s5"])
    x1, hf, route = _mix(x2d, ya, ybp, p["w_glu"], p["b_glu"], p["g_hyena"], p["g_s5"], p["w_out"],
                         p["g_ffn"], p["w_router"], p["b_router"])
    y = _moe_layer(x1, hf, route, p["experts"], p["g_final"])
    return y.reshape(B, L, D)


def kernel(x_prompt, x_sample, g_mix, w_in, b_in, short_w, short_b, filt_w1, filt_b1, filt_w2, filt_b2, filt_w3, filt_b3, filt_freq, filt_bias, s5_lam_re, s5_lam_im, s5_log_dt, s5_b_re, s5_b_im, s5_c_re, s5_c_im, s5_d, w_glu, b_glu, g_hyena, g_s5, w_out, g_ffn, w_router, b_router, w_gate, b_gate, w_up, b_up, w_down, b_down, g_final):
    assert g_mix.shape[0] == 1, "one encoder layer"
    row = lambda a: a[0][None, :].astype(F32)
    p = dict(
        g_mix=row(g_mix), w_in=w_in[0].astype(BF16), b_in=row(b_in),
        short_w=short_w[0].astype(F32), short_b=row(short_b),
        filt=(filt_w1[0], filt_b1[0], filt_w2[0], filt_b2[0], filt_w3[0], filt_b3[0], filt_freq[0]),
        filt_bias=filt_bias[0].astype(F32),
        s5=_s5_operators(s5_lam_re[0], s5_lam_im[0], s5_log_dt[0], s5_b_re[0], s5_b_im[0],
                         s5_c_re[0], s5_c_im[0], s5_d[0]),
        w_glu=w_glu[0].astype(BF16), b_glu=row(b_glu), g_hyena=row(g_hyena), g_s5=row(g_s5),
        w_out=w_out[0].astype(BF16), g_ffn=row(g_ffn),
        w_router=_split_bf16(jnp.pad(w_router[0].astype(F32), ((0, 0), (0, LANES - N_EXPERTS)))),
        b_router=jnp.pad(b_router[0].astype(F32), (0, LANES - N_EXPERTS))[None, :],
        experts=(w_gate[0].astype(BF16), b_gate[0][:, None, :], w_up[0].astype(BF16),
                 b_up[0][:, None, :], w_down[0].astype(BF16), b_down[0][:, None, :]),
        g_final=g_final[None, :].astype(F32),
    )
    y_prompt = _trunk(x_prompt, p, bb=1)
    y_sample = _trunk(x_sample, p, bb=4)
    return (y_prompt, y_sample)
```

```python
import functools
import math

import numpy as np
import jax
import jax.numpy as jnp
from jax import lax
from jax.experimental import pallas as pl
from jax.experimental.pallas import tpu as pltpu

F32 = jnp.float32
BF16 = jnp.bfloat16
I32 = jnp.int32
HIGHEST = lax.Precision.HIGHEST

LANES = 128
SUBLANES = 8
D_MODEL = 1024
C_HY = 512
C_S5 = 512
D_IN = 3 * C_HY + C_S5
Q_HY = C_HY // LANES
FILTER_BANDS = 16
FILTER_WIDTH = 64
Z_PAD = 128
DECAY_MAX = math.log(1e-2) / 0.3
DECAY_MIN = math.log(1e-2) / 1.5
S5_G, S5_H, S5_P = 32, 16, 64
S5_CHUNK = 16
S5_UW = S5_CHUNK * S5_H
S5_SW = 2 * S5_P
S5_GB = LANES // S5_H
S5_NB = S5_G // S5_GB
S5_KW = S5_CHUNK * LANES
S5_XW = S5_GB * S5_SW
S5_STEP_ROWS = 512
LAMBDA_RE_MAX = -1e-4
N_EXPERTS = 32
TOP_K = 4
SWIGLU_LIMIT = 7.0
SWIGLU_ALPHA = 1.702
MOE_BLOCK = 512
RMS_EPS = 1e-6
VMEM_LIMIT_BYTES = 56 * 1024 * 1024


def _cparams(*sem):
    return pltpu.CompilerParams(dimension_semantics=sem, vmem_limit_bytes=VMEM_LIMIT_BYTES)


def _rms(x, g):
    return x * lax.rsqrt(jnp.mean(x * x, axis=-1, keepdims=True) + RMS_EPS) * g


def _sigmoid(x):
    return 1.0 / (1.0 + jnp.exp(-x))


def _dot(a, b):
    return jnp.dot(a, b, preferred_element_type=F32)


def _split_bf16(w):
    hi = w.astype(BF16)
    return hi, (w - hi.astype(F32)).astype(BF16)


def _lane_block(x, q):
    return x[:, q * LANES:(q + 1) * LANES]


def _as_rows(ref):
    return ref.reshape(math.prod(ref.shape[:-1]), LANES)


TOKEN_TILE = (D_MODEL // LANES, LANES)


def _load_token_tiles(ref, n, row0=0):
    rows = _as_rows(ref)
    nq = TOKEN_TILE[0]
    return jnp.concatenate([rows[pl.ds(row0 * nq + q, n, stride=nq), :] for q in range(nq)], axis=1)


def _store_token_tiles(ref, x):
    rows = _as_rows(ref)
    nq = TOKEN_TILE[0]
    for q in range(nq):
        rows[pl.ds(q, x.shape[0], stride=nq), :] = _lane_block(x, q)


def _inproj_kernel(x_ref, xp_ref, xn_ref, g_ref, w_ref, b_ref, sw_ref, sb_ref,
                   v_ref, x1_ref, x2_ref, u_ref, *, tm, seq):
    i = pl.program_id(0)
    g = g_ref[...]
    rows = jnp.concatenate([x_ref[...], xp_ref[...], xn_ref[...]], axis=0)
    proj = _dot(_rms(rows, g).astype(BF16), w_ref[...]) + b_ref[...]
    u_ref[...] = proj[:tm, 3 * C_HY:]
    z = proj[:tm, :3 * C_HY]
    zh = proj[tm:, :3 * C_HY]
    row0 = i * tm
    has_prev = lax.rem(row0, seq) != 0
    has_next = lax.rem(row0 + tm, seq) != 0
    zp = jnp.where(has_prev, zh[SUBLANES - 1:SUBLANES], 0.0)
    zn = jnp.where(has_next, zh[SUBLANES:SUBLANES + 1], 0.0)
    rid = lax.broadcasted_iota(I32, (tm, 1), 0)
    zm1 = jnp.where(rid == 0, zp, pltpu.roll(z, 1, 0))
    zp1 = jnp.where(rid == tm - 1, zn, pltpu.roll(z, tm - 1, 0))
    sw = sw_ref[...]
    o = zm1 * sw[0:1] + z * sw[1:2] + zp1 * sw[2:3] + sb_ref[...]
    for q in range(Q_HY):
        v_ref[q] = _lane_block(o, q)
        x1_ref[q] = _lane_block(o, Q_HY + q)
        x2_ref[q] = _lane_block(o, 2 * Q_HY + q)


def _inproj(x2d, seq, g_mix, w_in_bf, b_in, short_w, short_b, tm=512):
    T = x2d.shape[0]
    nb8 = T // SUBLANES
    tb = tm // SUBLANES
    hy = jax.ShapeDtypeStruct((Q_HY, T, LANES), F32)
    hy_spec = pl.BlockSpec((Q_HY, tm, LANES), lambda i: (0, i, 0))
    const = lambda i: (0, 0)
    return pl.pallas_call(
        functools.partial(_inproj_kernel, tm=tm, seq=seq),
        grid=(T // tm,),
        in_specs=[
            pl.BlockSpec((tm, D_MODEL), lambda i: (i, 0)),
            pl.BlockSpec((SUBLANES, D_MODEL), lambda i: (jnp.maximum(i * tb - 1, 0), 0)),
            pl.BlockSpec((SUBLANES, D_MODEL), lambda i: (jnp.minimum((i + 1) * tb, nb8 - 1), 0)),
            pl.BlockSpec((1, D_MODEL), const),
            pl.BlockSpec((D_MODEL, D_IN), const),
            pl.BlockSpec((1, D_IN), const),
            pl.BlockSpec((3, 3 * C_HY), const),
            pl.BlockSpec((1, 3 * C_HY), const),
        ],
        out_specs=[hy_spec, hy_spec, hy_spec, pl.BlockSpec((tm, C_S5), lambda i: (i, 0))],
        out_shape=[hy, hy, hy, jax.ShapeDtypeStruct((T, C_S5), F32)],
        compiler_params=_cparams("arbitrary"),
        name="inproj",
    )(x2d, x2d, x2d, g_mix, w_in_bf, b_in, short_w, short_b)


def _filter_kernel(z_ref, w1_ref, b1_ref, w2_ref, b2_ref, w3h_ref, w3l_ref, b3_ref, fr_ref, dl_ref,
                   k_ref, s_ref, *, tr, seq):
    i = pl.program_id(0)
    z = z_ref[...]
    fr = fr_ref[...]
    h = jnp.sin(fr[0:1] * (jnp.dot(z, w1_ref[...], precision=HIGHEST,
                                   preferred_element_type=F32) + b1_ref[...]))
    h = jnp.sin(fr[1:2] * (jnp.dot(h, w2_ref[...], precision=HIGHEST,
                                   preferred_element_type=F32) + b2_ref[...]))
    h_hi = h.astype(BF16)
    h_lo = (h - h_hi.astype(F32)).astype(BF16)
    h = _dot(h_hi, w3h_ref[...]) + _dot(h_lo, w3h_ref[...]) + _dot(h_hi, w3l_ref[...]) + b3_ref[...]
    h = h * jnp.exp(-z[:, 0:1] * dl_ref[...])
    rid = i * tr + lax.broadcasted_iota(I32, (tr, 1), 0)
    h = jnp.where(rid == seq, 0.0, h)
    for q in range(2 * Q_HY):
        k_ref[q] = _lane_block(h, q)

    @pl.when(i == 0)
    def _():
        s_ref[...] = jnp.zeros_like(s_ref)

    s_ref[...] += jnp.sum(jnp.abs(h), axis=0, keepdims=True)


def _filter_taps(seq, w1, b1, w2, b2, w3, b3, freq, tr=512):
    L = seq
    N = 2 * L
    n = jnp.arange(N, dtype=I32)
    pos = jnp.where(n < L, n, jnp.where(n == L, 0, N - n)).astype(F32)
    t = (pos * (1.0 / (L - 1)))[:, None]
    w = 2.0 * math.pi * pos / L
    bands = jnp.linspace(1e-4, FILTER_BANDS - 1, FILTER_BANDS, dtype=F32)
    ang = w[:, None] * bands[None, :]
    zc = jnp.concatenate([t, jnp.cos(ang), -jnp.sin(ang),
                          jnp.zeros((N, Z_PAD - 1 - 2 * FILTER_BANDS), F32)], axis=-1)
    w1p = jnp.pad(w1, ((0, Z_PAD - w1.shape[0]), (0, 0)))
    w3d = w3.reshape(FILTER_WIDTH, 2, 2 * C_HY).transpose(1, 0, 2)
    b3d = b3.reshape(2, 1, 2 * C_HY)
    deltas = jnp.abs(jnp.linspace(DECAY_MIN, DECAY_MAX, C_HY, dtype=F32))
    dl = jnp.tile(deltas, 2)[None, :]
    half = (N // tr) // 2
    const = lambda i: (0, 0)
    taps, sums = pl.pallas_call(
        functools.partial(_filter_kernel, tr=tr, seq=L),
        grid=(N // tr,),
        in_specs=[
            pl.BlockSpec((tr, Z_PAD), lambda i: (i, 0)),
            pl.BlockSpec((Z_PAD, FILTER_WIDTH), const),
            pl.BlockSpec((1, FILTER_WIDTH), const),
            pl.BlockSpec((FILTER_WIDTH, FILTER_WIDTH), const),
            pl.BlockSpec((1, FILTER_WIDTH), const),
            pl.BlockSpec((None, FILTER_WIDTH, 2 * C_HY), lambda i: (i // half, 0, 0)),
            pl.BlockSpec((None, FILTER_WIDTH, 2 * C_HY), lambda i: (i // half, 0, 0)),
            pl.BlockSpec((None, 1, 2 * C_HY), lambda i: (i // half, 0, 0)),
            pl.BlockSpec((2, FILTER_WIDTH), const),
            pl.BlockSpec((1, 2 * C_HY), const),
        ],
        out_specs=[pl.BlockSpec((2 * Q_HY, tr, LANES), lambda i: (0, i, 0)),
                   pl.BlockSpec((1, 2 * C_HY), const)],
        out_shape=[jax.ShapeDtypeStruct((2 * Q_HY, N, LANES), F32),
                   jax.ShapeDtypeStruct((1, 2 * C_HY), F32)],
        compiler_params=_cparams("arbitrary"),
        name="hyena_filter",
    )(zc, w1p, b1[None, :], w2, b2[None, :], *_split_bf16(w3d.astype(F32)), b3d, freq, dl)
    return taps, sums


def _split_n(N):
    n1 = {32768: 128, 4096: 64}.get(N)
    if n1 is None:
        n1 = 1 << (int(math.log2(N)) // 2)
    return n1, N // n1


def _dft_constants(N1, N2):
    N = N1 * N2
    k1 = np.arange(N1)[:, None]
    n1 = np.arange(N1)[None, :]
    ang = 2.0 * np.pi * ((k1 * n1) % N1) / N1
    fa_full = np.concatenate([np.cos(ang), -np.sin(ang)], axis=0)
    fa_half = fa_full[:, :N1 // 2]
    fa_inv = np.concatenate([np.cos(ang), -np.sin(ang)], axis=1)[:N1 // 2] / N
    k2 = np.arange(N2)[:, None]
    n2 = np.arange(N2)[None, :]
    a2 = 2.0 * np.pi * ((k2 * n2) % N2) / N2
    cr = jnp.asarray(np.cos(a2).astype(np.float32))[None]
    ci = jnp.asarray((-np.sin(a2)).astype(np.float32))[None]
    to = lambda a: jnp.asarray(a.astype(np.float32)).astype(BF16)
    kk = lax.broadcasted_iota(I32, (N1, N2), 0)
    nn = lax.broadcasted_iota(I32, (N1, N2), 1)
    ta = (2.0 * math.pi / N) * lax.rem(kk * nn, N).astype(F32)
    tr, ti = jnp.cos(ta), -jnp.sin(ta)
    stack = lambda re, im: jnp.concatenate(
        [jnp.concatenate([re, -im], axis=-1), jnp.concatenate([im, re], axis=-1)], axis=-2).astype(BF16)
    tc_r, tc_i = tr[:, None, :], ti[:, None, :]
    fk = stack(cr * tc_r - ci * tc_i, cr * tc_i + ci * tc_r)
    tr_r, tr_i = tr[:, :, None], ti[:, :, None]
    gk = stack(cr * tr_r - ci * tr_i, -(cr * tr_i + ci * tr_r))
    return dict(fa_full=to(fa_full), fa_half=to(fa_half), fa_inv=to(fa_inv), fk=fk, gk=gk)


U32 = jnp.uint32
_HI16 = 0xFFFF0000
_HALF16 = 0x8000


def _pack_spectrum(re, im):
    rb = lax.bitcast_convert_type(re, U32) + U32(_HALF16)
    ib = lax.bitcast_convert_type(im, U32) + U32(_HALF16)
    return (rb & U32(_HI16)) | (ib >> 16)


def _unpack_spectrum(w):
    return (lax.bitcast_convert_type(w & U32(_HI16), F32), lax.bitcast_convert_type(w << 16, F32))


def _dft_a_kernel(f_ref, x_ref, o_ref, *, rows):
    nq = x_ref.shape[0]
    x2d = _as_rows(x_ref)
    f = f_ref[...]
    half = f.shape[0] // 2
    for j in range(SUBLANES):
        xs = jnp.concatenate(
            [x2d[pl.ds(q * rows * SUBLANES + j, rows, stride=SUBLANES), :] for q in range(nq)], axis=1)
        r = _dot(f, xs.astype(BF16))
        w = _pack_spectrum(r[:half], r[half:])
        for q in range(nq):
            o_ref[q, j] = _lane_block(w, q)


def _dft_a(f, x):
    Q, Bt, K, N2, _ = x.shape
    R = f.shape[0] // 2
    return pl.pallas_call(
        functools.partial(_dft_a_kernel, rows=K),
        grid=(Bt, N2 // SUBLANES, Q // Q_HY),
        in_specs=[pl.BlockSpec((2 * R, K), lambda b, j, w: (0, 0)),
                  pl.BlockSpec((Q_HY, None, K, SUBLANES, LANES), lambda b, j, w: (w, b, 0, j, 0))],
        out_specs=pl.BlockSpec((Q_HY, None, SUBLANES, R, LANES), lambda b, j, w: (w, b, j, 0, 0)),
        out_shape=jax.ShapeDtypeStruct((Q, Bt, N2, R, LANES), U32),
        compiler_params=_cparams("arbitrary", "arbitrary", "arbitrary"),
        name="dft_a",
    )(f, x)


def _dft_a_inv_kernel(f_ref, z_ref, gate_ref, v_ref, bias_ref, o_ref, *, rows):
    g2d = _as_rows(gate_ref)
    v2d = _as_rows(v_ref)
    o2d = _as_rows(o_ref)
    f = f_ref[...]
    bias = bias_ref[...]
    for j in range(SUBLANES):
        zr, zi = _unpack_spectrum(jnp.concatenate([z_ref[q, j] for q in range(Q_HY)], axis=1))
        y = _dot(f, jnp.concatenate([zr, zi], axis=0).astype(BF16))
        for q in range(Q_HY):
            sl = pl.ds(q * rows * SUBLANES + j, rows, stride=SUBLANES)
            o2d[sl, :] = g2d[sl, :] * (_lane_block(y, q) + v2d[sl, :] * _lane_block(bias, q))


def _dft_a_inv(f, z, gate, v, bias_row):
    Q, Bt, N2, N1, _ = z.shape
    K = f.shape[0]
    nat = pl.BlockSpec((Q, None, K, SUBLANES, LANES), lambda b, j: (0, b, 0, j, 0))
    return pl.pallas_call(
        functools.partial(_dft_a_inv_kernel, rows=K),
        grid=(Bt, N2 // SUBLANES),
        in_specs=[pl.BlockSpec((K, 2 * N1), lambda b, j: (0, 0)),
                  pl.BlockSpec((Q, None, SUBLANES, N1, LANES), lambda b, j: (0, b, j, 0, 0)),
                  nat, nat,
                  pl.BlockSpec((1, C_HY), lambda b, j: (0, 0))],
        out_specs=nat,
        out_shape=jax.ShapeDtypeStruct((Q, Bt, K, N2, LANES), F32),
        compiler_params=_cparams("arbitrary", "arbitrary"),
        name="dft_a_inv",
    )(f, z, gate, v, bias_row)


C_STEP_Q = 2
K1_STEP = SUBLANES


def _dft_c_kernel(a_ref, k_ref, f_ref, g_ref, z_ref, *, bb, n2):
    per = n2 * K1_STEP
    a2d = _as_rows(a_ref)
    z2d = _as_rows(z_ref)
    cols = [(b, q) for b in range(bb) for q in range(C_STEP_Q)]

    def rows(b, q, kk):
        return pl.ds((q * bb + b) * per + kk, n2, stride=K1_STEP)

    for kk in range(K1_STEP):
        kr = jnp.tile(k_ref[0, kk], (1, bb))
        ki = jnp.tile(k_ref[1, kk], (1, bb))
        ar, ai = _unpack_spectrum(jnp.concatenate([a2d[rows(b, q, kk), :] for b, q in cols], axis=1))
        x = _dot(f_ref[kk], jnp.concatenate([ar, ai], axis=0).astype(BF16))
        xr = x[:n2]
        xi = x[n2:]
        yr = xr * kr - xi * ki
        yi = xr * ki + xi * kr
        zz = _dot(g_ref[kk], jnp.concatenate([yr, yi], axis=0).astype(BF16))
        w = _pack_spectrum(zz[:n2], zz[n2:])
        for c, (b, q) in enumerate(cols):
            z2d[rows(b, q, kk), :] = _lane_block(w, c)


def _dft_c(a5, khat, order, fk, gk, bb):
    Q, Bt, N2, N1, _ = a5.shape
    cw = C_STEP_Q * LANES
    ablk = pl.BlockSpec((C_STEP_Q, bb, N2, K1_STEP, LANES), lambda g, c, b: (c, b, 0, g, 0))
    mats = pl.BlockSpec((K1_STEP, 2 * N2, 2 * N2), lambda g, c, b: (g, 0, 0))
    return pl.pallas_call(
        functools.partial(_dft_c_kernel, bb=bb, n2=N2),
        grid=(N1 // K1_STEP, Q // C_STEP_Q, Bt // bb),
        in_specs=[ablk,
                  pl.BlockSpec((2, K1_STEP, N2, cw), lambda g, c, b: (0, g, 0, order * (C_HY // cw) + c)),
                  mats, mats],
        out_specs=ablk,
        out_shape=jax.ShapeDtypeStruct(a5.shape, U32),
        compiler_params=_cparams("arbitrary", "arbitrary", "arbitrary"),
        name="dft_c",
    )(a5, khat, fk, gk)


def _dft_c_filter_kernel(a_ref, f_ref, s_ref, o_ref, *, n2):
    per = n2 * K1_STEP
    a2d = _as_rows(a_ref)
    sc = 1.0 / (s_ref[...] + 1e-6)
    for kk in range(K1_STEP):
        ar, ai = _unpack_spectrum(jnp.concatenate(
            [a2d[pl.ds(q * per + kk, n2, stride=K1_STEP), :] for q in range(C_STEP_Q)], axis=1))
        x = _dot(f_ref[kk], jnp.concatenate([ar, ai], axis=0).astype(BF16))
        o_ref[0, kk] = x[:n2] * sc
        o_ref[1, kk] = x[n2:] * sc


def _dft_c_filter(a4, fk, sums):
    Q, N2, N1, _ = a4.shape
    cw = C_STEP_Q * LANES
    return pl.pallas_call(
        functools.partial(_dft_c_filter_kernel, n2=N2),
        grid=(N1 // K1_STEP, Q // C_STEP_Q),
        in_specs=[pl.BlockSpec((C_STEP_Q, N2, K1_STEP, LANES), lambda g, c: (c, 0, g, 0)),
                  pl.BlockSpec((K1_STEP, 2 * N2, 2 * N2), lambda g, c: (g, 0, 0)),
                  pl.BlockSpec((1, cw), lambda g, c: (0, c))],
        out_specs=pl.BlockSpec((2, K1_STEP, N2, cw), lambda g, c: (0, g, 0, c)),
        out_shape=jax.ShapeDtypeStruct((2, N1, N2, Q * LANES), F32),
        compiler_params=_cparams("arbitrary", "arbitrary"),
        name="dft_c_filter",
    )(a4, fk, sums)


def _hyena(v, x1, x2, B, seq, filt, filt_bias, bb):
    L = seq
    N = 2 * L
    N1, N2 = _split_n(N)
    cst = _dft_constants(N1, N2)
    taps, sums = _filter_taps(L, *filt)
    ka = _dft_a(cst["fa_full"], taps.reshape(2 * Q_HY, 1, N1, N2, LANES))
    khat = _dft_c_filter(ka.reshape(2 * Q_HY, N2, N1, LANES), cst["fk"], sums)
    nat = lambda a: a.reshape(Q_HY, B, N1 // 2, N2, LANES)
    cur = nat(v)
    for o, gate in enumerate((x1, x2)):
        a = _dft_a(cst["fa_half"], cur)
        z = _dft_c(a, khat, o, cst["fk"], cst["gk"], bb)
        cur = _dft_a_inv(cst["fa_inv"], z, nat(gate), cur, filt_bias[o][None, :])
    return cur.reshape(Q_HY, B * L, LANES)


def _s5_operators(lam_re, lam_im, log_dt, b_re, b_im, c_re, c_im, d_skip):
    Tc, G, H, P = S5_CHUNK, S5_G, S5_H, S5_P
    lam = jnp.minimum(lam_re.astype(F32), LAMBDA_RE_MAX) + 1j * lam_im.astype(F32)
    dt = jnp.exp(log_dt.astype(F32))[..., None]
    lam_dt = lam * dt
    lam_bar = jnp.exp(lam_dt)
    b_bar = ((lam_bar - 1.0) / lam)[..., None] * (b_re.astype(F32) + 1j * b_im.astype(F32))
    c = c_re.astype(F32) + 1j * c_im.astype(F32)
    tau = jnp.arange(Tc + 1, dtype=F32)
    pw = jnp.exp(lam_dt[None] * tau[:, None, None, None])
    kk = jnp.einsum('dghp,tdgp,dgpk->dtghk', c, pw[:Tc], b_bar).real
    i = jnp.arange(Tc)
    lag = i[:, None] - i[None, :]
    kf = jnp.where((lag >= 0)[:, :, None, None, None], kk[0][jnp.clip(lag, 0, Tc - 1)], 0.0)
    kb = jnp.where((lag <= 0)[:, :, None, None, None], kk[1][jnp.clip(-lag, 0, Tc - 1)], 0.0)
    m = kf + kb
    eye = (lag == 0)[:, :, None, None, None] * jnp.eye(H, dtype=F32)[None, None, None]
    m = m + eye * d_skip.astype(F32).reshape(G, H)[None, None, :, :, None]
    clf = c[0][None] * pw[1:Tc + 1, 0][:, :, None, :]
    clb = c[1][None] * pw[Tc - i, 1][:, :, None, :]
    sf = pw[Tc - 1 - i, 0][:, :, :, None] * b_bar[0][None]
    sb = pw[i, 1][:, :, :, None] * b_bar[1][None]
    w_intra = m.transpose(2, 1, 4, 0, 3).reshape(G, Tc * H, Tc * H)
    st2y = lambda cl: jnp.concatenate([cl.real, -cl.imag], axis=-1).transpose(1, 3, 0, 2) \
        .reshape(G, 2 * P, Tc * H)
    u2s = lambda s: jnp.concatenate([s.real, s.imag], axis=2).transpose(1, 0, 3, 2) \
        .reshape(G, Tc * H, 2 * P)
    w_state = jnp.concatenate([u2s(sf), u2s(sb)], axis=-1)
    pair = lambda a: a.reshape(2, G // 2, 2 * P)
    lam16 = jnp.concatenate([pair(pw[Tc].real), pair(pw[Tc].imag)], axis=-1).reshape(2, G * 2 * P)
    col = np.arange(S5_KW)
    spread = (np.arange(S5_UW)[:, None] == (col // LANES * S5_H + col % S5_H)[None, :])
    return dict(w_intra=w_intra.astype(BF16), w_xf=st2y(clf).astype(BF16), w_xb=st2y(clb).astype(BF16),
                w_state=w_state.astype(BF16), lam16=lam16,
                spread=jnp.asarray(spread.astype(np.float32)).astype(BF16))


def _s5_rows(i, cl, B, ncc):
    return pl.ds(S5_CHUNK * cl + i, B, stride=S5_CHUNK * ncc)


def _s5_chunk_inputs(u2d, B, ncc):
    cols = []
    for i in range(S5_CHUNK):
        if B == 1:
            cols.append(u2d[pl.ds(i, ncc, stride=S5_CHUNK), :])
        else:
            cols.append(jnp.concatenate([u2d[_s5_rows(i, cl, B, ncc), :] for cl in range(ncc)], axis=0))
    return jnp.concatenate(cols, axis=1).astype(BF16)


def _s5_state_lane(a, part):
    return (a // 2) * 2 * S5_SW + part * S5_SW + (a % 2) * S5_P


def _s5_state_kernel(u_ref, w_ref, sf_ref, sb_ref, wblk, *, B, ncc):
    @pl.when(pl.program_id(1) == 0)
    def _():
        wblk[...] = jnp.zeros_like(wblk)
        for a in range(S5_GB):
            for j in range(S5_CHUNK):
                r0 = j * LANES + a * S5_H
                for d in range(2):
                    for part in range(2):
                        c0 = d * S5_XW + _s5_state_lane(a, part)
                        s0 = d * S5_SW + part * S5_P
                        wblk[r0:r0 + S5_H, c0:c0 + S5_P] = w_ref[a, j * S5_H:(j + 1) * S5_H, s0:s0 + S5_P]

    lhs = _s5_chunk_inputs(_as_rows(u_ref), B, ncc)
    r = _dot(lhs, wblk[...])
    sf_ref[...] = r[:, :S5_XW]
    sb_ref[...] = r[:, S5_XW:]


def _s5_state(u3, w_state, ncc):
    B, L, _ = u3.shape
    nch = L // S5_CHUNK
    out = jax.ShapeDtypeStruct((nch * B, S5_NB * S5_XW), F32)
    return pl.pallas_call(
        functools.partial(_s5_state_kernel, B=B, ncc=ncc),
        grid=(S5_NB, nch // ncc),
        in_specs=[pl.BlockSpec((B, S5_CHUNK * ncc, LANES), lambda q, t: (0, t, q)),
                  pl.BlockSpec((S5_GB, S5_UW, 2 * S5_SW), lambda q, t: (q, 0, 0))],
        out_specs=[pl.BlockSpec((ncc * B, S5_XW), lambda q, t: (t, q))] * 2,
        out_shape=[out, out],
        scratch_shapes=[pltpu.VMEM((S5_KW, 2 * S5_XW), BF16)],
        compiler_params=_cparams("arbitrary", "arbitrary"),
        name="s5_state",
    )(u3, w_state)


def _s5_scan_kernel(sf_ref, sb_ref, lam_ref, xf_ref, xb_ref, *, nch, rows):
    lb = sf_ref.shape[1]
    vr = max(rows, SUBLANES)
    lam_f, lam_b = lam_ref[0], lam_ref[1]

    def advance(x, lam, inc):
        out = []
        for u in range(0, lb, 2 * S5_SW):
            re, im = x[:, u:u + S5_SW], x[:, u + S5_SW:u + 2 * S5_SW]
            lr, li = lam[:, u:u + S5_SW], lam[:, u + S5_SW:u + 2 * S5_SW]
            out += [lr * re - li * im, lr * im + li * re]
        return jnp.concatenate(out, axis=1) + inc

    def body(c, carry):
        xf, xb = carry
        rf = pl.ds(pl.multiple_of(c * rows, rows), rows)
        xf_ref[rf, :] = xf[:rows]
        xf = advance(xf, lam_f, sf_ref[rf, :])
        rb = pl.ds(pl.multiple_of((nch - 1 - c) * rows, rows), rows)
        xb_ref[rb, :] = xb[:rows]
        xb = advance(xb, lam_b, sb_ref[rb, :])
        return xf, xb

    zero = jnp.zeros((vr, lb), F32)
    lax.fori_loop(0, nch, body, (zero, zero))


def _s5_scan(sf, sb, lam16, nch, rows, lb):
    R, lanes = sf.shape
    blk = pl.BlockSpec((R, lb), lambda j: (0, j))
    out = jax.ShapeDtypeStruct((R, lanes), F32)
    return pl.pallas_call(
        functools.partial(_s5_scan_kernel, nch=nch, rows=rows),
        grid=(lanes // lb,),
        in_specs=[blk, blk, pl.BlockSpec((2, 1, lb), lambda j: (0, 0, j))],
        out_specs=[blk, blk],
        out_shape=[out, out],
        compiler_params=_cparams("arbitrary"),
        name="s5_scan",
    )(sf, sb, lam16[:, None, :])


def _s5_out_kernel(u_ref, xf_ref, xb_ref, wm_ref, wf_ref, wb_ref, e_ref, y_ref, wm_blk, wf_blk, wb_blk,
                   *, B, ncc):
    @pl.when(pl.program_id(1) == 0)
    def _():
        lane = lax.broadcasted_iota(I32, (1, S5_KW), 1)
        slot = (lane // S5_H) % S5_GB
        e = e_ref[...]
        for a in range(S5_GB):
            own = slot == a
            ex = jnp.where(own, _dot(wm_ref[a], e), 0.0).astype(BF16)
            for j in range(S5_CHUNK):
                r0 = j * LANES + a * S5_H
                wm_blk[r0:r0 + S5_H, :] = ex[j * S5_H:(j + 1) * S5_H, :]
            for src, dst in ((wf_ref, wf_blk), (wb_ref, wb_blk)):
                ex = jnp.where(own, _dot(src[a], e), 0.0).astype(BF16)
                for part in range(2):
                    r0 = _s5_state_lane(a, part)
                    dst[r0:r0 + S5_P, :] = ex[part * S5_P:(part + 1) * S5_P, :]

    lhs = _s5_chunk_inputs(_as_rows(u_ref), B, ncc)
    acc = _dot(lhs, wm_blk[...])
    acc += _dot(xf_ref[...].astype(BF16), wf_blk[...])
    acc += _dot(xb_ref[...].astype(BF16), wb_blk[...])
    y2d = _as_rows(y_ref)
    for i in range(S5_CHUNK):
        piece = _lane_block(acc, i)
        if B == 1:
            y2d[pl.ds(i, ncc, stride=S5_CHUNK), :] = piece
        else:
            for cl in range(ncc):
                y2d[_s5_rows(i, cl, B, ncc), :] = piece[cl * B:(cl + 1) * B]


def _s5_out(u3, xf, xb, ops, ncc):
    B, L, _ = u3.shape
    nch = L // S5_CHUNK
    tok = pl.BlockSpec((B, S5_CHUNK * ncc, LANES), lambda q, t: (0, t, q))
    st = pl.BlockSpec((ncc * B, S5_XW), lambda q, t: (t, q))
    return pl.pallas_call(
        functools.partial(_s5_out_kernel, B=B, ncc=ncc),
        grid=(S5_NB, nch // ncc),
        in_specs=[tok, st, st,
                  pl.BlockSpec((S5_GB, S5_UW, S5_UW), lambda q, t: (q, 0, 0)),
                  pl.BlockSpec((S5_GB, S5_SW, S5_UW), lambda q, t: (q, 0, 0)),
                  pl.BlockSpec((S5_GB, S5_SW, S5_UW), lambda q, t: (q, 0, 0)),
                  pl.BlockSpec((S5_UW, S5_KW), lambda q, t: (0, 0))],
        out_specs=tok,
        out_shape=jax.ShapeDtypeStruct(u3.shape, F32),
        scratch_shapes=[pltpu.VMEM((S5_KW, S5_KW), BF16), pltpu.VMEM((S5_XW, S5_KW), BF16),
                        pltpu.VMEM((S5_XW, S5_KW), BF16)],
        compiler_params=_cparams("arbitrary", "arbitrary"),
        name="s5_out",
    )(u3, xf, xb, ops["w_intra"], ops["w_xf"], ops["w_xb"], ops["spread"])


def _s5(u, B, seq, ops):
    nch = seq // S5_CHUNK
    u3 = u.reshape(B, seq, C_S5)
    ncc = min(nch, S5_STEP_ROWS // B)
    sf, sb = _s5_state(u3, ops["w_state"], ncc)
    lb = 1024 if B == 1 else 256
    xf, xb = _s5_scan(sf, sb, ops["lam16"], nch, B, lb)
    return _s5_out(u3, xf, xb, ops, ncc).reshape(B * seq, C_S5)


def _mix_kernel(x_ref, ya_ref, yb_ref, wglu_ref, bglu_ref, gh_ref, gs_ref, wout_ref, gffn_ref,
                wrh_ref, wrl_ref, br_ref, x1_ref, hf_ref, route_ref, cnt_ref, *, tm):
    g = jax.nn.gelu(yb_ref[...])
    yb = g * _sigmoid(_dot(g.astype(BF16), wglu_ref[...]) + bglu_ref[...])
    ya = jnp.concatenate([ya_ref[q] for q in range(Q_HY)], axis=1)
    na = _rms(ya, gh_ref[...]).astype(BF16)
    nb = _rms(yb, gs_ref[...]).astype(BF16)
    mixed = _dot(na, wout_ref[:C_HY, :]) + _dot(nb, wout_ref[C_HY:, :])
    x1 = x_ref[...] + mixed
    x1_ref[...] = x1
    hf = _rms(x1, gffn_ref[...])
    _store_token_tiles(hf_ref, hf)
    hf_hi = hf.astype(BF16)
    hf_lo = (hf - hf_hi.astype(F32)).astype(BF16)
    logits = (_dot(hf_hi, wrh_ref[...]) + _dot(hf_lo, wrh_ref[...]) + _dot(hf_hi, wrl_ref[...])
              + br_ref[...])
    lane = lax.broadcasted_iota(I32, (tm, LANES), 1)
    neg = jnp.float32(-jnp.inf)
    l = jnp.where(lane < N_EXPERTS, logits, neg)
    vals, idxs = [], []
    for _ in range(TOP_K):
        m = jnp.max(l, axis=-1, keepdims=True)
        idx = jnp.min(jnp.where(l == m, lane, LANES), axis=-1, keepdims=True)
        vals.append(m)
        idxs.append(idx)
        l = jnp.where(lane == idx, neg, l)
    es = [jnp.exp(v - vals[0]) for v in vals]
    den = es[0] + es[1] + es[2] + es[3]
    route = jnp.zeros((tm, LANES), F32)
    for k in range(TOP_K):
        route = jnp.where(lane == k, es[k] / den, route)
        route = jnp.where(lane == TOP_K + k, idxs[k].astype(F32), route)
    route_ref[...] = route

    @pl.when(pl.program_id(0) == 0)
    def _():
        cnt_ref[...] = jnp.zeros_like(cnt_ref)

    cnt_ref[...] += jnp.sum(_expert_one_hot(idxs, tm), axis=0, keepdims=True)


def _mix(x2d, ya4, ybp, w_glu_bf, b_glu, g_hyena, g_s5, w_out_bf, g_ffn, w_router_p, b_router_p, tm=512):
    T = x2d.shape[0]
    const = lambda i: (0, 0)
    row = lambda w: pl.BlockSpec((tm, w), lambda i: (i, 0))
    return pl.pallas_call(
        functools.partial(_mix_kernel, tm=tm),
        grid=(T // tm,),
        in_specs=[row(D_MODEL), pl.BlockSpec((Q_HY, tm, LANES), lambda i: (0, i, 0)), row(C_S5),
                  pl.BlockSpec((C_S5, C_S5), const), pl.BlockSpec((1, C_S5), const),
                  pl.BlockSpec((1, C_HY), const), pl.BlockSpec((1, C_S5), const),
                  pl.BlockSpec((D_MODEL, D_MODEL), const), pl.BlockSpec((1, D_MODEL), const),
                  pl.BlockSpec((D_MODEL, LANES), const), pl.BlockSpec((D_MODEL, LANES), const),
                  pl.BlockSpec((1, LANES), const)],
        out_specs=[row(D_MODEL), pl.BlockSpec((tm,) + TOKEN_TILE, lambda i: (i, 0, 0)), row(LANES),
                   pl.BlockSpec((1, LANES), const)],
        out_shape=[jax.ShapeDtypeStruct((T, D_MODEL), F32), jax.ShapeDtypeStruct((T,) + TOKEN_TILE, F32),
                   jax.ShapeDtypeStruct((T, LANES), F32), jax.ShapeDtypeStruct((1, LANES), F32)],
        compiler_params=_cparams("arbitrary"),
        name="mix",
    )(x2d, ya4, ybp, w_glu_bf, b_glu, g_hyena, g_s5, w_out_bf, g_ffn, w_router_p[0], w_router_p[1],
      b_router_p)


def _expert_one_hot(eids, tm):
    lane = lax.broadcasted_iota(I32, (tm, LANES), 1)
    oh = jnp.zeros((tm, LANES), F32)
    for e in eids:
        oh += (lane == e).astype(F32)
    return oh


def _route_kernel(r_ref, cnt_ref, dest_ref, pst_ref, carry, pstart, *, tm):
    i = pl.program_id(0)
    lane = lax.broadcasted_iota(I32, (tm, LANES), 1)
    r = r_ref[...]
    eids = [r[:, TOP_K + k:TOP_K + k + 1].astype(I32) for k in range(TOP_K)]
    oh = _expert_one_hot(eids, tm)

    @pl.when(i == 0)
    def _():
        padded = jnp.floor((cnt_ref[...] + (MOE_BLOCK - 1)) * (1.0 / MOE_BLOCK)) * MOE_BLOCK
        a = lax.broadcasted_iota(I32, (LANES, LANES), 0)
        b = lax.broadcasted_iota(I32, (LANES, LANES), 1)
        excl = jnp.dot(jnp.broadcast_to(padded, (SUBLANES, LANES)), (a < b).astype(F32),
                       precision=HIGHEST, preferred_element_type=F32)
        pstart[...] = excl[0:1]
        pst_ref[...] = excl[0:1]
        carry[...] = jnp.zeros_like(carry)

    a = lax.broadcasted_iota(I32, (tm, tm), 0)
    b = lax.broadcasted_iota(I32, (tm, tm), 1)
    before = _dot((b < a).astype(BF16), oh.astype(BF16))
    base = before + carry[...] + pstart[...]
    out = jnp.zeros((tm, LANES), F32)
    for k, e in enumerate(eids):
        d = jnp.sum(jnp.where(lane == e, base, 0.0), axis=-1, keepdims=True)
        out = jnp.where(lane == k, d, out)
    dest_ref[...] = out.astype(I32)
    carry[...] += jnp.sum(oh, axis=0, keepdims=True)


def _route(route, cnt, tm=512):
    T = route.shape[0]
    return pl.pallas_call(
        functools.partial(_route_kernel, tm=tm),
        grid=(T // tm,),
        in_specs=[pl.BlockSpec((tm, LANES), lambda i: (i, 0)),
                  pl.BlockSpec((1, LANES), lambda i: (0, 0))],
        out_specs=[pl.BlockSpec((tm, LANES), lambda i: (i, 0)),
                   pl.BlockSpec((1, LANES), lambda i: (0, 0))],
        out_shape=[jax.ShapeDtypeStruct((T, LANES), I32), jax.ShapeDtypeStruct((1, LANES), F32)],
        scratch_shapes=[pltpu.VMEM((1, LANES), F32), pltpu.VMEM((1, LANES), F32)],
        compiler_params=_cparams("arbitrary"),
        name="route",
    )(route, cnt)


ZERO_ROWS = 64
DMA_UNROLL = 4


RING = 3


def _dispatch_kernel(padpos_ref, padcnt_ref, nu_ref, dest_ref, h_ref, xs_ref, zbuf, hbuf, sem, lsem, zsem,
                     *, td, nblk, nsteps):
    i = pl.program_id(0)

    @pl.when(i == 0)
    def _():
        zbuf[...] = jnp.zeros_like(zbuf)

        def per_expert(e, carry):
            off = padpos_ref[e]

            def zero_row(r):
                return pltpu.make_async_copy(zbuf.at[0], xs_ref.at[off + r], zsem)

            def z_issue(r, c):
                zero_row(r).start()
                return c

            def z_drain(r, c):
                zero_row(r).wait()
                return c

            lax.fori_loop(0, padcnt_ref[e], z_issue, 0)
            lax.fori_loop(0, padcnt_ref[e], z_drain, 0)
            return carry

        lax.fori_loop(0, N_EXPERTS, per_expert, 0)

        def zero_piece(j):
            row = pl.multiple_of(j * ZERO_ROWS, ZERO_ROWS)
            return pltpu.make_async_copy(zbuf, xs_ref.at[pl.ds(row, ZERO_ROWS)], zsem)

        per_blk = MOE_BLOCK // ZERO_ROWS

        def t_issue(j, c):
            zero_piece(j).start()
            return c

        def t_drain(j, c):
            zero_piece(j).wait()
            return c

        lax.fori_loop(nu_ref[0] * per_blk, nblk * per_blk, t_issue, 0)
        lax.fori_loop(nu_ref[0] * per_blk, nblk * per_blk, t_drain, 0)

    def load(step):
        s = lax.rem(step, RING)
        return pltpu.make_async_copy(h_ref.at[pl.ds(step * td, td)], hbuf.at[s], lsem.at[s])

    def wait_rows(step):
        s = lax.rem(step, RING)
        for _ in range(TOP_K):
            pltpu.make_async_copy(hbuf.at[s], xs_ref.at[pl.ds(0, td)], sem.at[s]).wait()

    @pl.when(i == 0)
    def _():
        load(0).start()
        if nsteps > 1:
            load(1).start()

    load(i).wait()
    slot = lax.rem(i, RING)

    def issue(r, carry):
        for k in range(TOP_K):
            pltpu.make_async_copy(hbuf.at[slot, r], xs_ref.at[dest_ref[r * TOP_K + k]],
                                  sem.at[slot]).start(priority=k % 2)
        return carry

    lax.fori_loop(0, td, issue, 0, unroll=DMA_UNROLL)

    @pl.when(i > 0)
    def _():
        wait_rows(i - 1)

    @pl.when(i + 2 < nsteps)
    def _():
        load(i + 2).start()

    @pl.when(i == nsteps - 1)
    def _():
        wait_rows(i)


def _dispatch(hf, dest_flat, padpos, padcnt, n_used, n_slots, td=256):
    T = hf.shape[0]
    nsteps = T // td
    grid_spec = pltpu.PrefetchScalarGridSpec(
        num_scalar_prefetch=3,
        grid=(nsteps,),
        in_specs=[pl.BlockSpec((td * TOP_K,), lambda i, *_: (i,), memory_space=pltpu.SMEM),
                  pl.BlockSpec(memory_space=pl.ANY)],
        out_specs=pl.BlockSpec(memory_space=pl.ANY),
        scratch_shapes=[pltpu.VMEM((ZERO_ROWS,) + TOKEN_TILE, F32), pltpu.VMEM((RING, td) + TOKEN_TILE, F32),
                        pltpu.SemaphoreType.DMA((RING,)), pltpu.SemaphoreType.DMA((RING,)),
                        pltpu.SemaphoreType.DMA(())],
    )
    return pl.pallas_call(
        functools.partial(_dispatch_kernel, td=td, nblk=n_slots // MOE_BLOCK, nsteps=nsteps),
        grid_spec=grid_spec,
        out_shape=jax.ShapeDtypeStruct((n_slots,) + TOKEN_TILE, F32),
        compiler_params=_cparams("arbitrary"),
        name="dispatch",
    )(padpos, padcnt, n_used, dest_flat, hf)


def _moe_kernel(be_ref, nu_ref, x_ref, wg_ref, bg_ref, wu_ref, bu_ref, wd_ref, bd_ref, o_ref):
    used = pl.program_id(0) < nu_ref[0]

    @pl.when(jnp.logical_not(used))
    def _():
        o_ref[...] = jnp.zeros_like(o_ref)

    @pl.when(used)
    def _():
        x = _load_token_tiles(x_ref, MOE_BLOCK).astype(BF16)
        gt = jnp.minimum(_dot(x, wg_ref[...]) + bg_ref[...], SWIGLU_LIMIT)
        up = jnp.clip(_dot(x, wu_ref[...]) + bu_ref[...], -SWIGLU_LIMIT, SWIGLU_LIMIT)
        act = (up + 1.0) * (gt * _sigmoid(SWIGLU_ALPHA * gt))
        _store_token_tiles(o_ref, _dot(act.astype(BF16), wd_ref[...]) + bd_ref[...])


def _moe(xs, block_expert, n_used, wg, bg, wu, bu, wd, bd):
    n_slots = xs.shape[0]
    nblk = n_slots // MOE_BLOCK
    blk = lambda i, be, nu: (jnp.minimum(i, nu[0] - 1), 0, 0)
    exp = lambda i, be, nu: (be[jnp.minimum(i, nu[0] - 1)], 0, 0)
    wspec = pl.BlockSpec((None, D_MODEL, D_MODEL), exp)
    bspec = pl.BlockSpec((None, 1, D_MODEL), exp)
    grid_spec = pltpu.PrefetchScalarGridSpec(
        num_scalar_prefetch=2,
        grid=(nblk,),
        in_specs=[pl.BlockSpec((MOE_BLOCK,) + TOKEN_TILE, blk), wspec, bspec, wspec, bspec, wspec, bspec],
        out_specs=pl.BlockSpec((MOE_BLOCK,) + TOKEN_TILE, lambda i, be, nu: (i, 0, 0)),
    )
    return pl.pallas_call(
        _moe_kernel,
        grid_spec=grid_spec,
        out_shape=jax.ShapeDtypeStruct((n_slots,) + TOKEN_TILE, F32),
        compiler_params=_cparams("arbitrary"),
        name="moe",
    )(block_expert, n_used, xs, wg, bg, wu, bu, wd, bd)


def _combine_kernel(dcur_ref, dnext_ref, x1_ref, r_ref, g_ref, ys_ref, o_ref, buf, sem, *, tc, nsteps):
    i = pl.program_id(0)
    slot = lax.rem(i, 2)

    def gather(d_ref, s):
        def body(r, carry):
            for k in range(TOP_K):
                pltpu.make_async_copy(ys_ref.at[d_ref[r * TOP_K + k]], buf.at[s, k, r],
                                      sem.at[s]).start(priority=k % 2)
            return carry

        lax.fori_loop(0, tc, body, 0, unroll=DMA_UNROLL)

    @pl.when(i == 0)
    def _():
        gather(dcur_ref, slot)

    @pl.when(i + 1 < nsteps)
    def _():
        gather(dnext_ref, 1 - slot)

    for k in range(TOP_K):
        pltpu.make_async_copy(ys_ref.at[pl.ds(0, tc)], buf.at[slot, k], sem.at[slot]).wait()
    gates = r_ref[...]
    acc = x1_ref[...]
    for k in range(TOP_K):
        acc += gates[:, k:k + 1] * _load_token_tiles(buf, tc, (slot * TOP_K + k) * tc)
    o_ref[...] = _rms(acc, g_ref[...])


def _combine(x1, route, dest_flat, ys, g_final, tc=256):
    T = x1.shape[0]
    nsteps = T // tc
    return pl.pallas_call(
        functools.partial(_combine_kernel, tc=tc, nsteps=nsteps),
        grid=(nsteps,),
        in_specs=[pl.BlockSpec((tc * TOP_K,), lambda i: (i,), memory_space=pltpu.SMEM),
                  pl.BlockSpec((tc * TOP_K,), lambda i: (jnp.minimum(i + 1, nsteps - 1),),
                               memory_space=pltpu.SMEM),
                  pl.BlockSpec((tc, D_MODEL), lambda i: (i, 0)),
                  pl.BlockSpec((tc, LANES), lambda i: (i, 0)),
                  pl.BlockSpec((1, D_MODEL), lambda i: (0, 0)),
                  pl.BlockSpec(memory_space=pl.ANY)],
        out_specs=pl.BlockSpec((tc, D_MODEL), lambda i: (i, 0)),
        out_shape=jax.ShapeDtypeStruct((T, D_MODEL), F32),
        scratch_shapes=[pltpu.VMEM((2, TOP_K, tc) + TOKEN_TILE, F32), pltpu.SemaphoreType.DMA((2,))],
        compiler_params=_cparams("arbitrary"),
        name="combine",
    )(dest_flat, dest_flat, x1, route, g_final, ys)


def _moe_layer(x1, hf, route, cnt, ew, g_final):
    T = x1.shape[0]
    n_assign = T * TOP_K
    nblk = n_assign // MOE_BLOCK + N_EXPERTS
    n_slots = nblk * MOE_BLOCK
    dest, pst = _route(route, cnt)
    cnt_i = cnt[0, :N_EXPERTS].astype(I32)
    pstart = pst[0, :N_EXPERTS].astype(I32)
    padded = (cnt_i + MOE_BLOCK - 1) // MOE_BLOCK * MOE_BLOCK
    pend = pstart + padded
    block_expert = jnp.minimum(
        jnp.sum(jnp.arange(nblk, dtype=I32)[:, None] * MOE_BLOCK >= pend[None, :], axis=1),
        N_EXPERTS - 1).astype(I32)
    n_used = (pend[-1] // MOE_BLOCK).reshape(1).astype(I32)
    dest_flat = dest[:, :TOP_K].reshape(n_assign)
    xs = _dispatch(hf, dest_flat, pstart + cnt_i, padded - cnt_i, n_used, n_slots)
    ys = _moe(xs, block_expert, n_used, *ew)
    return _combine(x1, route, dest_flat, ys, g_final)


def _trunk(x, p, bb):
    B, L, D = x.shape
    T = B * L
    x2d = x.reshape(T, D)
    v, x1g, x2g, u = _inproj(x2d, L, p["g_mix"], p["w_in"], p["b_in"], p["short_w"], p["short_b"])
    ya = _hyena(v, x1g, x2g, B, L, p["filt"], p["filt_bias"], bb)
    ybp = _s5(u, B, L, p["s5"])
    x1, hf, route, cnt = _mix(x2d, ya, ybp, p["w_glu"], p["b_glu"], p["g_hyena"], p["g_s5"], p["w_out"],
                              p["g_ffn"], p["w_router"], p["b_router"])
    y = _moe_layer(x1, hf, route, cnt, p["experts"], p["g_final"])
    return y.reshape(B, L, D)


def kernel(x_prompt, x_sample, g_mix, w_in, b_in, short_w, short_b, filt_w1, filt_b1, filt_w2, filt_b2, filt_w3, filt_b3, filt_freq, filt_bias, s5_lam_re, s5_lam_im, s5_log_dt, s5_b_re, s5_b_im, s5_c_re, s5_c_im, s5_d, w_glu, b_glu, g_hyena, g_s5, w_out, g_ffn, w_router, b_router, w_gate, b_gate, w_up, b_up, w_down, b_down, g_final):
    assert g_mix.shape[0] == 1, "one encoder layer"
    row = lambda a: a[0][None, :].astype(F32)
    p = dict(
        g_mix=row(g_mix), w_in=w_in[0].astype(BF16), b_in=row(b_in),
        short_w=short_w[0].astype(F32), short_b=row(short_b),
        filt=(filt_w1[0], filt_b1[0], filt_w2[0], filt_b2[0], filt_w3[0], filt_b3[0], filt_freq[0]),
        filt_bias=filt_bias[0].astype(F32),
        s5=_s5_operators(s5_lam_re[0], s5_lam_im[0], s5_log_dt[0], s5_b_re[0], s5_b_im[0],
                         s5_c_re[0], s5_c_im[0], s5_d[0]),
        w_glu=w_glu[0].astype(BF16), b_glu=row(b_glu), g_hyena=row(g_hyena), g_s5=row(g_s5),
        w_out=w_out[0].astype(BF16), g_ffn=row(g_ffn),
        w_router=_split_bf16(jnp.pad(w_router[0].astype(F32), ((0, 0), (0, LANES - N_EXPERTS)))),
        b_router=jnp.pad(b_router[0].astype(F32), (0, LANES - N_EXPERTS))[None, :],
        experts=(w_gate[0].astype(BF16), b_gate[0][:, None, :], w_up[0].astype(BF16),
                 b_up[0][:, None, :], w_down[0].astype(BF16), b_down[0][:, None, :]),
        g_final=g_final[None, :].astype(F32),
    )
    y_prompt = _trunk(x_prompt, p, bb=1)
    y_sample = _trunk(x_sample, p, bb=4)
    return (y_prompt, y_sample)
```

```python
import functools
import math

import numpy as np
import jax
import jax.numpy as jnp
from jax import lax
from jax.experimental import pallas as pl
from jax.experimental.pallas import tpu as pltpu

F32 = jnp.float32
BF16 = jnp.bfloat16
I32 = jnp.int32
HIGHEST = lax.Precision.HIGHEST

LANES = 128
SUBLANES = 8
D_MODEL = 1024
C_HY = 512
C_S5 = 512
D_IN = 3 * C_HY + C_S5
Q_HY = C_HY // LANES
FILTER_BANDS = 16
FILTER_WIDTH = 64
Z_PAD = 128
DECAY_MAX = math.log(1e-2) / 0.3
DECAY_MIN = math.log(1e-2) / 1.5
S5_G, S5_H, S5_P = 32, 16, 64
S5_CHUNK = 16
S5_UW = S5_CHUNK * S5_H
S5_SW = 2 * S5_P
S5_GB = LANES // S5_H
S5_NB = S5_G // S5_GB
S5_KW = S5_CHUNK * LANES
S5_XW = S5_GB * S5_SW
S5_STEP_ROWS = 512
LAMBDA_RE_MAX = -1e-4
N_EXPERTS = 32
TOP_K = 4
SWIGLU_LIMIT = 7.0
SWIGLU_ALPHA = 1.702
MOE_BLOCK = 512
RMS_EPS = 1e-6
VMEM_LIMIT_BYTES = 56 * 1024 * 1024


def _cparams(*sem):
    return pltpu.CompilerParams(dimension_semantics=sem, vmem_limit_bytes=VMEM_LIMIT_BYTES)


def _rms(x, g):
    return x * lax.rsqrt(jnp.mean(x * x, axis=-1, keepdims=True) + RMS_EPS) * g


def _sigmoid(x):
    return 1.0 / (1.0 + jnp.exp(-x))


def _dot(a, b):
    return jnp.dot(a, b, preferred_element_type=F32)


def _split_bf16(w):
    hi = w.astype(BF16)
    return hi, (w - hi.astype(F32)).astype(BF16)


def _lane_block(x, q):
    return x[:, q * LANES:(q + 1) * LANES]


def _as_rows(ref):
    return ref.reshape(math.prod(ref.shape[:-1]), LANES)


TOKEN_TILE = (D_MODEL // LANES, LANES)


def _load_token_tiles(ref, n, row0=0):
    rows = _as_rows(ref)
    nq = TOKEN_TILE[0]
    return jnp.concatenate([rows[pl.ds(row0 * nq + q, n, stride=nq), :] for q in range(nq)], axis=1)


def _store_token_tiles(ref, x, row0=0):
    rows = _as_rows(ref)
    nq = TOKEN_TILE[0]
    for q in range(nq):
        rows[pl.ds(row0 * nq + q, x.shape[0], stride=nq), :] = _lane_block(x, q)


TAG_TOKEN_LO, TAG_TOKEN_HI, TAG_EXPERT0, TAG_VALID = 0, 1, 2, 2 + 4
_LO16 = 0xFFFF


def _tag_rows(h, token, experts):
    n = h.shape[0]
    w = lax.bitcast_convert_type(h.astype(BF16).astype(F32), U32)
    lane = lax.broadcasted_iota(I32, (n, LANES), 1)
    tag = jnp.where(lane == TAG_TOKEN_LO, token & _LO16, 0)
    tag = jnp.where(lane == TAG_TOKEN_HI, token >> 16, tag)
    for k, e in enumerate(experts):
        tag = jnp.where(lane == TAG_EXPERT0 + k, e, tag)
    tag = jnp.where(lane == TAG_VALID, 1, tag)
    return jnp.concatenate([_lane_block(w, 0) | tag.astype(U32), w[:, LANES:]], axis=1)


def _row_values(w):
    return lax.bitcast_convert_type(w & U32(_HI16), F32).astype(BF16)


def _row_destinations(w, expert, trash):
    tag = (_lane_block(w, 0) & U32(_LO16)).astype(I32).T
    token = tag[TAG_TOKEN_LO:TAG_TOKEN_LO + 1] | (tag[TAG_TOKEN_HI:TAG_TOKEN_HI + 1] << 16)
    pos = jnp.zeros_like(token)
    for k in range(1, TOP_K):
        pos = jnp.where(tag[TAG_EXPERT0 + k:TAG_EXPERT0 + k + 1] == expert, k, pos)
    spare = trash + lax.broadcasted_iota(I32, token.shape, 1)
    return jnp.where(tag[TAG_VALID:TAG_VALID + 1] == 1, token * TOP_K + pos, spare)


def _inproj_kernel(x_ref, xp_ref, xn_ref, g_ref, w_ref, b_ref, sw_ref, sb_ref,
                   v_ref, x1_ref, x2_ref, u_ref, *, tm, seq):
    i = pl.program_id(0)
    g = g_ref[...]
    rows = jnp.concatenate([x_ref[...], xp_ref[...], xn_ref[...]], axis=0)
    proj = _dot(_rms(rows, g).astype(BF16), w_ref[...]) + b_ref[...]
    u_ref[...] = proj[:tm, 3 * C_HY:]
    z = proj[:tm, :3 * C_HY]
    zh = proj[tm:, :3 * C_HY]
    row0 = i * tm
    has_prev = lax.rem(row0, seq) != 0
    has_next = lax.rem(row0 + tm, seq) != 0
    zp = jnp.where(has_prev, zh[SUBLANES - 1:SUBLANES], 0.0)
    zn = jnp.where(has_next, zh[SUBLANES:SUBLANES + 1], 0.0)
    rid = lax.broadcasted_iota(I32, (tm, 1), 0)
    zm1 = jnp.where(rid == 0, zp, pltpu.roll(z, 1, 0))
    zp1 = jnp.where(rid == tm - 1, zn, pltpu.roll(z, tm - 1, 0))
    sw = sw_ref[...]
    o = zm1 * sw[0:1] + z * sw[1:2] + zp1 * sw[2:3] + sb_ref[...]
    for q in range(Q_HY):
        v_ref[q] = _lane_block(o, q)
        x1_ref[q] = _lane_block(o, Q_HY + q)
        x2_ref[q] = _lane_block(o, 2 * Q_HY + q)


def _inproj(x2d, seq, g_mix, w_in_bf, b_in, short_w, short_b, tm=512):
    T = x2d.shape[0]
    nb8 = T // SUBLANES
    tb = tm // SUBLANES
    hy = jax.ShapeDtypeStruct((Q_HY, T, LANES), F32)
    hy_spec = pl.BlockSpec((Q_HY, tm, LANES), lambda i: (0, i, 0))
    const = lambda i: (0, 0)
    return pl.pallas_call(
        functools.partial(_inproj_kernel, tm=tm, seq=seq),
        grid=(T // tm,),
        in_specs=[
            pl.BlockSpec((tm, D_MODEL), lambda i: (i, 0)),
            pl.BlockSpec((SUBLANES, D_MODEL), lambda i: (jnp.maximum(i * tb - 1, 0), 0)),
            pl.BlockSpec((SUBLANES, D_MODEL), lambda i: (jnp.minimum((i + 1) * tb, nb8 - 1), 0)),
            pl.BlockSpec((1, D_MODEL), const),
            pl.BlockSpec((D_MODEL, D_IN), const),
            pl.BlockSpec((1, D_IN), const),
            pl.BlockSpec((3, 3 * C_HY), const),
            pl.BlockSpec((1, 3 * C_HY), const),
        ],
        out_specs=[hy_spec, hy_spec, hy_spec, pl.BlockSpec((tm, C_S5), lambda i: (i, 0))],
        out_shape=[hy, hy, hy, jax.ShapeDtypeStruct((T, C_S5), F32)],
        compiler_params=_cparams("arbitrary"),
        name="inproj",
    )(x2d, x2d, x2d, g_mix, w_in_bf, b_in, short_w, short_b)


def _filter_kernel(z_ref, w1_ref, b1_ref, w2_ref, b2_ref, w3h_ref, w3l_ref, b3_ref, fr_ref, dl_ref,
                   k_ref, s_ref, *, tr, seq):
    i = pl.program_id(0)
    z = z_ref[...]
    fr = fr_ref[...]
    h = jnp.sin(fr[0:1] * (jnp.dot(z, w1_ref[...], precision=HIGHEST,
                                   preferred_element_type=F32) + b1_ref[...]))
    h = jnp.sin(fr[1:2] * (jnp.dot(h, w2_ref[...], precision=HIGHEST,
                                   preferred_element_type=F32) + b2_ref[...]))
    h_hi = h.astype(BF16)
    h_lo = (h - h_hi.astype(F32)).astype(BF16)
    h = _dot(h_hi, w3h_ref[...]) + _dot(h_lo, w3h_ref[...]) + _dot(h_hi, w3l_ref[...]) + b3_ref[...]
    h = h * jnp.exp(-z[:, 0:1] * dl_ref[...])
    rid = i * tr + lax.broadcasted_iota(I32, (tr, 1), 0)
    h = jnp.where(rid == seq, 0.0, h)
    for q in range(2 * Q_HY):
        k_ref[q] = _lane_block(h, q)

    @pl.when(i == 0)
    def _():
        s_ref[...] = jnp.zeros_like(s_ref)

    s_ref[...] += jnp.sum(jnp.abs(h), axis=0, keepdims=True)


def _filter_taps(seq, w1, b1, w2, b2, w3, b3, freq, tr=512):
    L = seq
    N = 2 * L
    n = jnp.arange(N, dtype=I32)
    pos = jnp.where(n < L, n, jnp.where(n == L, 0, N - n)).astype(F32)
    t = (pos * (1.0 / (L - 1)))[:, None]
    w = 2.0 * math.pi * pos / L
    bands = jnp.linspace(1e-4, FILTER_BANDS - 1, FILTER_BANDS, dtype=F32)
    ang = w[:, None] * bands[None, :]
    zc = jnp.concatenate([t, jnp.cos(ang), -jnp.sin(ang),
                          jnp.zeros((N, Z_PAD - 1 - 2 * FILTER_BANDS), F32)], axis=-1)
    w1p = jnp.pad(w1, ((0, Z_PAD - w1.shape[0]), (0, 0)))
    w3d = w3.reshape(FILTER_WIDTH, 2, 2 * C_HY).transpose(1, 0, 2)
    b3d = b3.reshape(2, 1, 2 * C_HY)
    deltas = jnp.abs(jnp.linspace(DECAY_MIN, DECAY_MAX, C_HY, dtype=F32))
    dl = jnp.tile(deltas, 2)[None, :]
    half = (N // tr) // 2
    const = lambda i: (0, 0)
    taps, sums = pl.pallas_call(
        functools.partial(_filter_kernel, tr=tr, seq=L),
        grid=(N // tr,),
        in_specs=[
            pl.BlockSpec((tr, Z_PAD), lambda i: (i, 0)),
            pl.BlockSpec((Z_PAD, FILTER_WIDTH), const),
            pl.BlockSpec((1, FILTER_WIDTH), const),
            pl.BlockSpec((FILTER_WIDTH, FILTER_WIDTH), const),
            pl.BlockSpec((1, FILTER_WIDTH), const),
            pl.BlockSpec((None, FILTER_WIDTH, 2 * C_HY), lambda i: (i // half, 0, 0)),
            pl.BlockSpec((None, FILTER_WIDTH, 2 * C_HY), lambda i: (i // half, 0, 0)),
            pl.BlockSpec((None, 1, 2 * C_HY), lambda i: (i // half, 0, 0)),
            pl.BlockSpec((2, FILTER_WIDTH), const),
            pl.BlockSpec((1, 2 * C_HY), const),
        ],
        out_specs=[pl.BlockSpec((2 * Q_HY, tr, LANES), lambda i: (0, i, 0)),
                   pl.BlockSpec((1, 2 * C_HY), const)],
        out_shape=[jax.ShapeDtypeStruct((2 * Q_HY, N, LANES), F32),
                   jax.ShapeDtypeStruct((1, 2 * C_HY), F32)],
        compiler_params=_cparams("arbitrary"),
        name="hyena_filter",
    )(zc, w1p, b1[None, :], w2, b2[None, :], *_split_bf16(w3d.astype(F32)), b3d, freq, dl)
    return taps, sums


def _split_n(N):
    n1 = {32768: 128, 4096: 64}.get(N)
    if n1 is None:
        n1 = 1 << (int(math.log2(N)) // 2)
    return n1, N // n1


def _dft_constants(N1, N2):
    N = N1 * N2
    k1 = np.arange(N1)[:, None]
    n1 = np.arange(N1)[None, :]
    ang = 2.0 * np.pi * ((k1 * n1) % N1) / N1
    fa_full = np.concatenate([np.cos(ang), -np.sin(ang)], axis=0)
    fa_half = fa_full[:, :N1 // 2]
    fa_inv = np.concatenate([np.cos(ang), -np.sin(ang)], axis=1)[:N1 // 2] / N
    k2 = np.arange(N2)[:, None]
    n2 = np.arange(N2)[None, :]
    a2 = 2.0 * np.pi * ((k2 * n2) % N2) / N2
    cr = jnp.asarray(np.cos(a2).astype(np.float32))[None]
    ci = jnp.asarray((-np.sin(a2)).astype(np.float32))[None]
    to = lambda a: jnp.asarray(a.astype(np.float32)).astype(BF16)
    kk = lax.broadcasted_iota(I32, (N1, N2), 0)
    nn = lax.broadcasted_iota(I32, (N1, N2), 1)
    ta = (2.0 * math.pi / N) * lax.rem(kk * nn, N).astype(F32)
    tr, ti = jnp.cos(ta), -jnp.sin(ta)
    stack = lambda re, im: jnp.concatenate(
        [jnp.concatenate([re, -im], axis=-1), jnp.concatenate([im, re], axis=-1)], axis=-2).astype(BF16)
    tc_r, tc_i = tr[:, None, :], ti[:, None, :]
    fk = stack(cr * tc_r - ci * tc_i, cr * tc_i + ci * tc_r)
    tr_r, tr_i = tr[:, :, None], ti[:, :, None]
    gk = stack(cr * tr_r - ci * tr_i, -(cr * tr_i + ci * tr_r))
    return dict(fa_full=to(fa_full), fa_half=to(fa_half), fa_inv=to(fa_inv), fk=fk, gk=gk)


U32 = jnp.uint32
_HI16 = 0xFFFF0000
_HALF16 = 0x8000


def _pack_spectrum(re, im):
    rb = lax.bitcast_convert_type(re, U32) + U32(_HALF16)
    ib = lax.bitcast_convert_type(im, U32) + U32(_HALF16)
    return (rb & U32(_HI16)) | (ib >> 16)


def _unpack_spectrum(w):
    return (lax.bitcast_convert_type(w & U32(_HI16), F32), lax.bitcast_convert_type(w << 16, F32))


def _dft_a_kernel(f_ref, x_ref, o_ref, *, rows):
    nq = x_ref.shape[0]
    x2d = _as_rows(x_ref)
    f = f_ref[...]
    half = f.shape[0] // 2
    for j in range(SUBLANES):
        xs = jnp.concatenate(
            [x2d[pl.ds(q * rows * SUBLANES + j, rows, stride=SUBLANES), :] for q in range(nq)], axis=1)
        r = _dot(f, xs.astype(BF16))
        w = _pack_spectrum(r[:half], r[half:])
        for q in range(nq):
            o_ref[q, j] = _lane_block(w, q)


def _dft_a(f, x):
    Q, Bt, K, N2, _ = x.shape
    R = f.shape[0] // 2
    return pl.pallas_call(
        functools.partial(_dft_a_kernel, rows=K),
        grid=(Bt, N2 // SUBLANES, Q // Q_HY),
        in_specs=[pl.BlockSpec((2 * R, K), lambda b, j, w: (0, 0)),
                  pl.BlockSpec((Q_HY, None, K, SUBLANES, LANES), lambda b, j, w: (w, b, 0, j, 0))],
        out_specs=pl.BlockSpec((Q_HY, None, SUBLANES, R, LANES), lambda b, j, w: (w, b, j, 0, 0)),
        out_shape=jax.ShapeDtypeStruct((Q, Bt, N2, R, LANES), U32),
        compiler_params=_cparams("arbitrary", "arbitrary", "arbitrary"),
        name="dft_a",
    )(f, x)


def _dft_a_inv_kernel(f_ref, z_ref, gate_ref, v_ref, bias_ref, o_ref, *, rows):
    g2d = _as_rows(gate_ref)
    v2d = _as_rows(v_ref)
    o2d = _as_rows(o_ref)
    f = f_ref[...]
    bias = bias_ref[...]
    for j in range(SUBLANES):
        zr, zi = _unpack_spectrum(jnp.concatenate([z_ref[q, j] for q in range(Q_HY)], axis=1))
        y = _dot(f, jnp.concatenate([zr, zi], axis=0).astype(BF16))
        for q in range(Q_HY):
            sl = pl.ds(q * rows * SUBLANES + j, rows, stride=SUBLANES)
            o2d[sl, :] = g2d[sl, :] * (_lane_block(y, q) + v2d[sl, :] * _lane_block(bias, q))


def _dft_a_inv(f, z, gate, v, bias_row):
    Q, Bt, N2, N1, _ = z.shape
    K = f.shape[0]
    nat = pl.BlockSpec((Q, None, K, SUBLANES, LANES), lambda b, j: (0, b, 0, j, 0))
    return pl.pallas_call(
        functools.partial(_dft_a_inv_kernel, rows=K),
        grid=(Bt, N2 // SUBLANES),
        in_specs=[pl.BlockSpec((K, 2 * N1), lambda b, j: (0, 0)),
                  pl.BlockSpec((Q, None, SUBLANES, N1, LANES), lambda b, j: (0, b, j, 0, 0)),
                  nat, nat,
                  pl.BlockSpec((1, C_HY), lambda b, j: (0, 0))],
        out_specs=nat,
        out_shape=jax.ShapeDtypeStruct((Q, Bt, K, N2, LANES), F32),
        compiler_params=_cparams("arbitrary", "arbitrary"),
        name="dft_a_inv",
    )(f, z, gate, v, bias_row)


C_STEP_Q = 2
K1_STEP = SUBLANES


def _dft_c_kernel(a_ref, k_ref, f_ref, g_ref, z_ref, *, bb, n2):
    per = n2 * K1_STEP
    a2d = _as_rows(a_ref)
    z2d = _as_rows(z_ref)
    cols = [(b, q) for b in range(bb) for q in range(C_STEP_Q)]

    def rows(b, q, kk):
        return pl.ds((q * bb + b) * per + kk, n2, stride=K1_STEP)

    for kk in range(K1_STEP):
        kr = jnp.tile(k_ref[0, kk], (1, bb))
        ki = jnp.tile(k_ref[1, kk], (1, bb))
        ar, ai = _unpack_spectrum(jnp.concatenate([a2d[rows(b, q, kk), :] for b, q in cols], axis=1))
        x = _dot(f_ref[kk], jnp.concatenate([ar, ai], axis=0).astype(BF16))
        xr = x[:n2]
        xi = x[n2:]
        yr = xr * kr - xi * ki
        yi = xr * ki + xi * kr
        zz = _dot(g_ref[kk], jnp.concatenate([yr, yi], axis=0).astype(BF16))
        w = _pack_spectrum(zz[:n2], zz[n2:])
        for c, (b, q) in enumerate(cols):
            z2d[rows(b, q, kk), :] = _lane_block(w, c)


def _dft_c(a5, khat, order, fk, gk, bb):
    Q, Bt, N2, N1, _ = a5.shape
    cw = C_STEP_Q * LANES
    ablk = pl.BlockSpec((C_STEP_Q, bb, N2, K1_STEP, LANES), lambda g, c, b: (c, b, 0, g, 0))
    mats = pl.BlockSpec((K1_STEP, 2 * N2, 2 * N2), lambda g, c, b: (g, 0, 0))
    return pl.pallas_call(
        functools.partial(_dft_c_kernel, bb=bb, n2=N2),
        grid=(N1 // K1_STEP, Q // C_STEP_Q, Bt // bb),
        in_specs=[ablk,
                  pl.BlockSpec((2, K1_STEP, N2, cw), lambda g, c, b: (0, g, 0, order * (C_HY // cw) + c)),
                  mats, mats],
        out_specs=ablk,
        out_shape=jax.ShapeDtypeStruct(a5.shape, U32),
        compiler_params=_cparams("arbitrary", "arbitrary", "arbitrary"),
        name="dft_c",
    )(a5, khat, fk, gk)


def _dft_c_filter_kernel(a_ref, f_ref, s_ref, o_ref, *, n2):
    per = n2 * K1_STEP
    a2d = _as_rows(a_ref)
    sc = 1.0 / (s_ref[...] + 1e-6)
    for kk in range(K1_STEP):
        ar, ai = _unpack_spectrum(jnp.concatenate(
            [a2d[pl.ds(q * per + kk, n2, stride=K1_STEP), :] for q in range(C_STEP_Q)], axis=1))
        x = _dot(f_ref[kk], jnp.concatenate([ar, ai], axis=0).astype(BF16))
        o_ref[0, kk] = x[:n2] * sc
        o_ref[1, kk] = x[n2:] * sc


def _dft_c_filter(a4, fk, sums):
    Q, N2, N1, _ = a4.shape
    cw = C_STEP_Q * LANES
    return pl.pallas_call(
        functools.partial(_dft_c_filter_kernel, n2=N2),
        grid=(N1 // K1_STEP, Q // C_STEP_Q),
        in_specs=[pl.BlockSpec((C_STEP_Q, N2, K1_STEP, LANES), lambda g, c: (c, 0, g, 0)),
                  pl.BlockSpec((K1_STEP, 2 * N2, 2 * N2), lambda g, c: (g, 0, 0)),
                  pl.BlockSpec((1, cw), lambda g, c: (0, c))],
        out_specs=pl.BlockSpec((2, K1_STEP, N2, cw), lambda g, c: (0, g, 0, c)),
        out_shape=jax.ShapeDtypeStruct((2, N1, N2, Q * LANES), F32),
        compiler_params=_cparams("arbitrary", "arbitrary"),
        name="dft_c_filter",
    )(a4, fk, sums)


def _hyena(v, x1, x2, B, seq, filt, filt_bias, bb):
    L = seq
    N = 2 * L
    N1, N2 = _split_n(N)
    cst = _dft_constants(N1, N2)
    taps, sums = _filter_taps(L, *filt)
    ka = _dft_a(cst["fa_full"], taps.reshape(2 * Q_HY, 1, N1, N2, LANES))
    khat = _dft_c_filter(ka.reshape(2 * Q_HY, N2, N1, LANES), cst["fk"], sums)
    nat = lambda a: a.reshape(Q_HY, B, N1 // 2, N2, LANES)
    cur = nat(v)
    for o, gate in enumerate((x1, x2)):
        a = _dft_a(cst["fa_half"], cur)
        z = _dft_c(a, khat, o, cst["fk"], cst["gk"], bb)
        cur = _dft_a_inv(cst["fa_inv"], z, nat(gate), cur, filt_bias[o][None, :])
    return cur.reshape(Q_HY, B * L, LANES)


def _s5_operators(lam_re, lam_im, log_dt, b_re, b_im, c_re, c_im, d_skip):
    Tc, G, H, P = S5_CHUNK, S5_G, S5_H, S5_P
    lam = jnp.minimum(lam_re.astype(F32), LAMBDA_RE_MAX) + 1j * lam_im.astype(F32)
    dt = jnp.exp(log_dt.astype(F32))[..., None]
    lam_dt = lam * dt
    lam_bar = jnp.exp(lam_dt)
    b_bar = ((lam_bar - 1.0) / lam)[..., None] * (b_re.astype(F32) + 1j * b_im.astype(F32))
    c = c_re.astype(F32) + 1j * c_im.astype(F32)
    tau = jnp.arange(Tc + 1, dtype=F32)
    pw = jnp.exp(lam_dt[None] * tau[:, None, None, None])
    kk = jnp.einsum('dghp,tdgp,dgpk->dtghk', c, pw[:Tc], b_bar).real
    i = jnp.arange(Tc)
    lag = i[:, None] - i[None, :]
    kf = jnp.where((lag >= 0)[:, :, None, None, None], kk[0][jnp.clip(lag, 0, Tc - 1)], 0.0)
    kb = jnp.where((lag <= 0)[:, :, None, None, None], kk[1][jnp.clip(-lag, 0, Tc - 1)], 0.0)
    m = kf + kb
    eye = (lag == 0)[:, :, None, None, None] * jnp.eye(H, dtype=F32)[None, None, None]
    m = m + eye * d_skip.astype(F32).reshape(G, H)[None, None, :, :, None]
    clf = c[0][None] * pw[1:Tc + 1, 0][:, :, None, :]
    clb = c[1][None] * pw[Tc - i, 1][:, :, None, :]
    sf = pw[Tc - 1 - i, 0][:, :, :, None] * b_bar[0][None]
    sb = pw[i, 1][:, :, :, None] * b_bar[1][None]
    w_intra = m.transpose(2, 1, 4, 0, 3).reshape(G, Tc * H, Tc * H)
    st2y = lambda cl: jnp.concatenate([cl.real, -cl.imag], axis=-1).transpose(1, 3, 0, 2) \
        .reshape(G, 2 * P, Tc * H)
    u2s = lambda s: jnp.concatenate([s.real, s.imag], axis=2).transpose(1, 0, 3, 2) \
        .reshape(G, Tc * H, 2 * P)
    w_state = jnp.concatenate([u2s(sf), u2s(sb)], axis=-1)
    pair = lambda a: a.reshape(2, G // 2, 2 * P)
    lam16 = jnp.concatenate([pair(pw[Tc].real), pair(pw[Tc].imag)], axis=-1).reshape(2, G * 2 * P)
    col = np.arange(S5_KW)
    spread = (np.arange(S5_UW)[:, None] == (col // LANES * S5_H + col % S5_H)[None, :])
    return dict(w_intra=w_intra.astype(BF16), w_xf=st2y(clf).astype(BF16), w_xb=st2y(clb).astype(BF16),
                w_state=w_state.astype(BF16), lam16=lam16,
                spread=jnp.asarray(spread.astype(np.float32)).astype(BF16))


def _s5_rows(i, cl, B, ncc):
    return pl.ds(S5_CHUNK * cl + i, B, stride=S5_CHUNK * ncc)


def _s5_chunk_inputs(u2d, B, ncc):
    cols = []
    for i in range(S5_CHUNK):
        if B == 1:
            cols.append(u2d[pl.ds(i, ncc, stride=S5_CHUNK), :])
        else:
            cols.append(jnp.concatenate([u2d[_s5_rows(i, cl, B, ncc), :] for cl in range(ncc)], axis=0))
    return jnp.concatenate(cols, axis=1).astype(BF16)


def _s5_state_lane(a, part):
    return (a // 2) * 2 * S5_SW + part * S5_SW + (a % 2) * S5_P


def _s5_state_kernel(u_ref, w_ref, sf_ref, sb_ref, wblk, *, B, ncc):
    @pl.when(pl.program_id(1) == 0)
    def _():
        wblk[...] = jnp.zeros_like(wblk)
        for a in range(S5_GB):
            for j in range(S5_CHUNK):
                r0 = j * LANES + a * S5_H
                for d in range(2):
                    for part in range(2):
                        c0 = d * S5_XW + _s5_state_lane(a, part)
                        s0 = d * S5_SW + part * S5_P
                        wblk[r0:r0 + S5_H, c0:c0 + S5_P] = w_ref[a, j * S5_H:(j + 1) * S5_H, s0:s0 + S5_P]

    lhs = _s5_chunk_inputs(_as_rows(u_ref), B, ncc)
    r = _dot(lhs, wblk[...])
    sf_ref[...] = r[:, :S5_XW]
    sb_ref[...] = r[:, S5_XW:]


def _s5_state(u3, w_state, ncc):
    B, L, _ = u3.shape
    nch = L // S5_CHUNK
    out = jax.ShapeDtypeStruct((nch * B, S5_NB * S5_XW), F32)
    return pl.pallas_call(
        functools.partial(_s5_state_kernel, B=B, ncc=ncc),
        grid=(S5_NB, nch // ncc),
        in_specs=[pl.BlockSpec((B, S5_CHUNK * ncc, LANES), lambda q, t: (0, t, q)),
                  pl.BlockSpec((S5_GB, S5_UW, 2 * S5_SW), lambda q, t: (q, 0, 0))],
        out_specs=[pl.BlockSpec((ncc * B, S5_XW), lambda q, t: (t, q))] * 2,
        out_shape=[out, out],
        scratch_shapes=[pltpu.VMEM((S5_KW, 2 * S5_XW), BF16)],
        compiler_params=_cparams("arbitrary", "arbitrary"),
        name="s5_state",
    )(u3, w_state)


def _s5_scan_kernel(sf_ref, sb_ref, lam_ref, xf_ref, xb_ref, *, nch, rows):
    lb = sf_ref.shape[1]
    vr = max(rows, SUBLANES)
    lam_f, lam_b = lam_ref[0], lam_ref[1]

    def advance(x, lam, inc):
        out = []
        for u in range(0, lb, 2 * S5_SW):
            re, im = x[:, u:u + S5_SW], x[:, u + S5_SW:u + 2 * S5_SW]
            lr, li = lam[:, u:u + S5_SW], lam[:, u + S5_SW:u + 2 * S5_SW]
            out += [lr * re - li * im, lr * im + li * re]
        return jnp.concatenate(out, axis=1) + inc

    def body(c, carry):
        xf, xb = carry
        rf = pl.ds(pl.multiple_of(c * rows, rows), rows)
        xf_ref[rf, :] = xf[:rows]
        xf = advance(xf, lam_f, sf_ref[rf, :])
        rb = pl.ds(pl.multiple_of((nch - 1 - c) * rows, rows), rows)
        xb_ref[rb, :] = xb[:rows]
        xb = advance(xb, lam_b, sb_ref[rb, :])
        return xf, xb

    zero = jnp.zeros((vr, lb), F32)
    lax.fori_loop(0, nch, body, (zero, zero))


def _s5_scan(sf, sb, lam16, nch, rows, lb):
    R, lanes = sf.shape
    blk = pl.BlockSpec((R, lb), lambda j: (0, j))
    out = jax.ShapeDtypeStruct((R, lanes), F32)
    return pl.pallas_call(
        functools.partial(_s5_scan_kernel, nch=nch, rows=rows),
        grid=(lanes // lb,),
        in_specs=[blk, blk, pl.BlockSpec((2, 1, lb), lambda j: (0, 0, j))],
        out_specs=[blk, blk],
        out_shape=[out, out],
        compiler_params=_cparams("arbitrary"),
        name="s5_scan",
    )(sf, sb, lam16[:, None, :])


def _s5_out_kernel(u_ref, xf_ref, xb_ref, wm_ref, wf_ref, wb_ref, e_ref, y_ref, wm_blk, wf_blk, wb_blk,
                   *, B, ncc):
    @pl.when(pl.program_id(1) == 0)
    def _():
        lane = lax.broadcasted_iota(I32, (1, S5_KW), 1)
        slot = (lane // S5_H) % S5_GB
        e = e_ref[...]
        for a in range(S5_GB):
            own = slot == a
            ex = jnp.where(own, _dot(wm_ref[a], e), 0.0).astype(BF16)
            for j in range(S5_CHUNK):
                r0 = j * LANES + a * S5_H
                wm_blk[r0:r0 + S5_H, :] = ex[j * S5_H:(j + 1) * S5_H, :]
            for src, dst in ((wf_ref, wf_blk), (wb_ref, wb_blk)):
                ex = jnp.where(own, _dot(src[a], e), 0.0).astype(BF16)
                for part in range(2):
                    r0 = _s5_state_lane(a, part)
                    dst[r0:r0 + S5_P, :] = ex[part * S5_P:(part + 1) * S5_P, :]

    lhs = _s5_chunk_inputs(_as_rows(u_ref), B, ncc)
    acc = _dot(lhs, wm_blk[...])
    acc += _dot(xf_ref[...].astype(BF16), wf_blk[...])
    acc += _dot(xb_ref[...].astype(BF16), wb_blk[...])
    y2d = _as_rows(y_ref)
    for i in range(S5_CHUNK):
        piece = _lane_block(acc, i)
        if B == 1:
            y2d[pl.ds(i, ncc, stride=S5_CHUNK), :] = piece
        else:
            for cl in range(ncc):
                y2d[_s5_rows(i, cl, B, ncc), :] = piece[cl * B:(cl + 1) * B]


def _s5_out(u3, xf, xb, ops, ncc):
    B, L, _ = u3.shape
    nch = L // S5_CHUNK
    tok = pl.BlockSpec((B, S5_CHUNK * ncc, LANES), lambda q, t: (0, t, q))
    st = pl.BlockSpec((ncc * B, S5_XW), lambda q, t: (t, q))
    return pl.pallas_call(
        functools.partial(_s5_out_kernel, B=B, ncc=ncc),
        grid=(S5_NB, nch // ncc),
        in_specs=[tok, st, st,
                  pl.BlockSpec((S5_GB, S5_UW, S5_UW), lambda q, t: (q, 0, 0)),
                  pl.BlockSpec((S5_GB, S5_SW, S5_UW), lambda q, t: (q, 0, 0)),
                  pl.BlockSpec((S5_GB, S5_SW, S5_UW), lambda q, t: (q, 0, 0)),
                  pl.BlockSpec((S5_UW, S5_KW), lambda q, t: (0, 0))],
        out_specs=tok,
        out_shape=jax.ShapeDtypeStruct(u3.shape, F32),
        scratch_shapes=[pltpu.VMEM((S5_KW, S5_KW), BF16), pltpu.VMEM((S5_XW, S5_KW), BF16),
                        pltpu.VMEM((S5_XW, S5_KW), BF16)],
        compiler_params=_cparams("arbitrary", "arbitrary"),
        name="s5_out",
    )(u3, xf, xb, ops["w_intra"], ops["w_xf"], ops["w_xb"], ops["spread"])


def _s5(u, B, seq, ops):
    nch = seq // S5_CHUNK
    u3 = u.reshape(B, seq, C_S5)
    ncc = min(nch, S5_STEP_ROWS // B)
    sf, sb = _s5_state(u3, ops["w_state"], ncc)
    lb = 1024 if B == 1 else 256
    xf, xb = _s5_scan(sf, sb, ops["lam16"], nch, B, lb)
    return _s5_out(u3, xf, xb, ops, ncc).reshape(B * seq, C_S5)


def _mix_kernel(x_ref, ya_ref, yb_ref, wglu_ref, bglu_ref, gh_ref, gs_ref, wout_ref, gffn_ref,
                wrh_ref, wrl_ref, br_ref, x1_ref, hf_ref, route_ref, cnt_ref, *, tm):
    g = jax.nn.gelu(yb_ref[...])
    yb = g * _sigmoid(_dot(g.astype(BF16), wglu_ref[...]) + bglu_ref[...])
    ya = jnp.concatenate([ya_ref[q] for q in range(Q_HY)], axis=1)
    na = _rms(ya, gh_ref[...]).astype(BF16)
    nb = _rms(yb, gs_ref[...]).astype(BF16)
    mixed = _dot(na, wout_ref[:C_HY, :]) + _dot(nb, wout_ref[C_HY:, :])
    x1 = x_ref[...] + mixed
    x1_ref[...] = x1
    hf = _rms(x1, gffn_ref[...])
    hf_hi = hf.astype(BF16)
    hf_lo = (hf - hf_hi.astype(F32)).astype(BF16)
    logits = (_dot(hf_hi, wrh_ref[...]) + _dot(hf_lo, wrh_ref[...]) + _dot(hf_hi, wrl_ref[...])
              + br_ref[...])
    lane = lax.broadcasted_iota(I32, (tm, LANES), 1)
    neg = jnp.float32(-jnp.inf)
    l = jnp.where(lane < N_EXPERTS, logits, neg)
    vals, idxs = [], []
    for _ in range(TOP_K):
        m = jnp.max(l, axis=-1, keepdims=True)
        idx = jnp.min(jnp.where(l == m, lane, LANES), axis=-1, keepdims=True)
        vals.append(m)
        idxs.append(idx)
        l = jnp.where(lane == idx, neg, l)
    es = [jnp.exp(v - vals[0]) for v in vals]
    den = es[0] + es[1] + es[2] + es[3]
    route = jnp.zeros((tm, LANES), F32)
    for k in range(TOP_K):
        route = jnp.where(lane == k, es[k] / den, route)
        route = jnp.where(lane == TOP_K + k, idxs[k].astype(F32), route)
    route_ref[...] = route
    token = pl.program_id(0) * tm + lax.broadcasted_iota(I32, (tm, 1), 0)
    _store_token_tiles(hf_ref, _tag_rows(hf, token, idxs))

    @pl.when(pl.program_id(0) == 0)
    def _():
        cnt_ref[...] = jnp.zeros_like(cnt_ref)

    cnt_ref[...] += jnp.sum(_expert_one_hot(idxs, tm), axis=0, keepdims=True)


def _mix(x2d, ya4, ybp, w_glu_bf, b_glu, g_hyena, g_s5, w_out_bf, g_ffn, w_router_p, b_router_p, tm=512):
    T = x2d.shape[0]
    const = lambda i: (0, 0)
    row = lambda w: pl.BlockSpec((tm, w), lambda i: (i, 0))
    return pl.pallas_call(
        functools.partial(_mix_kernel, tm=tm),
        grid=(T // tm,),
        in_specs=[row(D_MODEL), pl.BlockSpec((Q_HY, tm, LANES), lambda i: (0, i, 0)), row(C_S5),
                  pl.BlockSpec((C_S5, C_S5), const), pl.BlockSpec((1, C_S5), const),
                  pl.BlockSpec((1, C_HY), const), pl.BlockSpec((1, C_S5), const),
                  pl.BlockSpec((D_MODEL, D_MODEL), const), pl.BlockSpec((1, D_MODEL), const),
                  pl.BlockSpec((D_MODEL, LANES), const), pl.BlockSpec((D_MODEL, LANES), const),
                  pl.BlockSpec((1, LANES), const)],
        out_specs=[row(D_MODEL), pl.BlockSpec((tm,) + TOKEN_TILE, lambda i: (i, 0, 0)), row(LANES),
                   pl.BlockSpec((1, LANES), const)],
        out_shape=[jax.ShapeDtypeStruct((T, D_MODEL), F32), jax.ShapeDtypeStruct((T,) + TOKEN_TILE, U32),
                   jax.ShapeDtypeStruct((T, LANES), F32), jax.ShapeDtypeStruct((1, LANES), F32)],
        compiler_params=_cparams("arbitrary"),
        name="mix",
    )(x2d, ya4, ybp, w_glu_bf, b_glu, g_hyena, g_s5, w_out_bf, g_ffn, w_router_p[0], w_router_p[1],
      b_router_p)


def _expert_one_hot(eids, tm):
    lane = lax.broadcasted_iota(I32, (tm, LANES), 1)
    oh = jnp.zeros((tm, LANES), F32)
    for e in eids:
        oh += (lane == e).astype(F32)
    return oh


def _route_kernel(r_ref, cnt_ref, dest_ref, pst_ref, carry, pstart, *, tm):
    i = pl.program_id(0)
    lane = lax.broadcasted_iota(I32, (tm, LANES), 1)
    r = r_ref[...]
    eids = [r[:, TOP_K + k:TOP_K + k + 1].astype(I32) for k in range(TOP_K)]
    oh = _expert_one_hot(eids, tm)

    @pl.when(i == 0)
    def _():
        padded = jnp.floor((cnt_ref[...] + (MOE_BLOCK - 1)) * (1.0 / MOE_BLOCK)) * MOE_BLOCK
        a = lax.broadcasted_iota(I32, (LANES, LANES), 0)
        b = lax.broadcasted_iota(I32, (LANES, LANES), 1)
        excl = jnp.dot(jnp.broadcast_to(padded, (SUBLANES, LANES)), (a < b).astype(F32),
                       precision=HIGHEST, preferred_element_type=F32)
        pstart[...] = excl[0:1]
        pst_ref[...] = excl[0:1]
        carry[...] = jnp.zeros_like(carry)

    a = lax.broadcasted_iota(I32, (tm, tm), 0)
    b = lax.broadcasted_iota(I32, (tm, tm), 1)
    before = _dot((b < a).astype(BF16), oh.astype(BF16))
    base = before + carry[...] + pstart[...]
    out = jnp.zeros((tm, LANES), F32)
    for k, e in enumerate(eids):
        d = jnp.sum(jnp.where(lane == e, base, 0.0), axis=-1, keepdims=True)
        out = jnp.where(lane == k, d, out)
    dest_ref[...] = out.astype(I32)
    carry[...] += jnp.sum(oh, axis=0, keepdims=True)


def _route(route, cnt, tm=512):
    T = route.shape[0]
    return pl.pallas_call(
        functools.partial(_route_kernel, tm=tm),
        grid=(T // tm,),
        in_specs=[pl.BlockSpec((tm, LANES), lambda i: (i, 0)),
                  pl.BlockSpec((1, LANES), lambda i: (0, 0))],
        out_specs=[pl.BlockSpec((tm, LANES), lambda i: (i, 0)),
                   pl.BlockSpec((1, LANES), lambda i: (0, 0))],
        out_shape=[jax.ShapeDtypeStruct((T, LANES), I32), jax.ShapeDtypeStruct((1, LANES), F32)],
        scratch_shapes=[pltpu.VMEM((1, LANES), F32), pltpu.VMEM((1, LANES), F32)],
        compiler_params=_cparams("arbitrary"),
        name="route",
    )(route, cnt)


ZERO_ROWS = 64
DMA_UNROLL = 4


RING = 3


def _dispatch_kernel(padpos_ref, padcnt_ref, nu_ref, dest_ref, h_ref, xs_ref, zbuf, hbuf, sem, lsem, zsem,
                     *, td, nblk, nsteps):
    i = pl.program_id(0)

    @pl.when(i == 0)
    def _():
        zbuf[...] = jnp.zeros_like(zbuf)

        def per_expert(e, carry):
            off = padpos_ref[e]

            def zero_row(r):
                return pltpu.make_async_copy(zbuf.at[0], xs_ref.at[off + r], zsem)

            def z_issue(r, c):
                zero_row(r).start()
                return c

            def z_drain(r, c):
                zero_row(r).wait()
                return c

            lax.fori_loop(0, padcnt_ref[e], z_issue, 0)
            lax.fori_loop(0, padcnt_ref[e], z_drain, 0)
            return carry

        lax.fori_loop(0, N_EXPERTS, per_expert, 0)

        def zero_piece(j):
            row = pl.multiple_of(j * ZERO_ROWS, ZERO_ROWS)
            return pltpu.make_async_copy(zbuf, xs_ref.at[pl.ds(row, ZERO_ROWS)], zsem)

        per_blk = MOE_BLOCK // ZERO_ROWS

        def t_issue(j, c):
            zero_piece(j).start()
            return c

        def t_drain(j, c):
            zero_piece(j).wait()
            return c

        lax.fori_loop(nu_ref[0] * per_blk, nblk * per_blk, t_issue, 0)
        lax.fori_loop(nu_ref[0] * per_blk, nblk * per_blk, t_drain, 0)

    def load(step):
        s = lax.rem(step, RING)
        return pltpu.make_async_copy(h_ref.at[pl.ds(step * td, td)], hbuf.at[s], lsem.at[s])

    def wait_rows(step):
        s = lax.rem(step, RING)
        for _ in range(TOP_K):
            pltpu.make_async_copy(hbuf.at[s], xs_ref.at[pl.ds(0, td)], sem.at[s]).wait()

    @pl.when(i == 0)
    def _():
        load(0).start()
        if nsteps > 1:
            load(1).start()

    load(i).wait()
    slot = lax.rem(i, RING)

    def issue(r, carry):
        for k in range(TOP_K):
            pltpu.make_async_copy(hbuf.at[slot, r], xs_ref.at[dest_ref[r * TOP_K + k]],
                                  sem.at[slot]).start(priority=k % 2)
        return carry

    lax.fori_loop(0, td, issue, 0, unroll=DMA_UNROLL)

    @pl.when(i > 0)
    def _():
        wait_rows(i - 1)

    @pl.when(i + 2 < nsteps)
    def _():
        load(i + 2).start()

    @pl.when(i == nsteps - 1)
    def _():
        wait_rows(i)


def _dispatch(hf, dest_flat, padpos, padcnt, n_used, n_slots, td=256):
    T = hf.shape[0]
    nsteps = T // td
    grid_spec = pltpu.PrefetchScalarGridSpec(
        num_scalar_prefetch=3,
        grid=(nsteps,),
        in_specs=[pl.BlockSpec((td * TOP_K,), lambda i, *_: (i,), memory_space=pltpu.SMEM),
                  pl.BlockSpec(memory_space=pl.ANY)],
        out_specs=pl.BlockSpec(memory_space=pl.ANY),
        scratch_shapes=[pltpu.VMEM((ZERO_ROWS,) + TOKEN_TILE, U32), pltpu.VMEM((RING, td) + TOKEN_TILE, U32),
                        pltpu.SemaphoreType.DMA((RING,)), pltpu.SemaphoreType.DMA((RING,)),
                        pltpu.SemaphoreType.DMA(())],
    )
    return pl.pallas_call(
        functools.partial(_dispatch_kernel, td=td, nblk=n_slots // MOE_BLOCK, nsteps=nsteps),
        grid_spec=grid_spec,
        out_shape=jax.ShapeDtypeStruct((n_slots,) + TOKEN_TILE, U32),
        compiler_params=_cparams("arbitrary"),
        name="dispatch",
    )(padpos, padcnt, n_used, dest_flat, hf)


DEST_TILE = (SUBLANES, LANES)
TRASH_ROWS = MOE_BLOCK


def _moe_kernel(be_ref, nu_ref, x_ref, wg_ref, bg_ref, wu_ref, bu_ref, wd_ref, bd_ref, ys_ref,
                ybuf, dvm, dsm, sem, msem, *, nblk, trash):
    i = pl.program_id(0)
    nu = nu_ref[0]
    slot = lax.rem(i, 2)

    def dest_copy(s):
        return pltpu.make_async_copy(dvm.at[s], dsm.at[s], msem.at[s])

    def send_rows(s):
        dest_copy(s).wait()
        for r in range(MOE_BLOCK):
            pltpu.make_async_copy(ybuf.at[s, r], ys_ref.at[dsm[s, r // LANES, r % LANES]],
                                  sem.at[s]).start(priority=r % 2)

    def wait_rows(s):
        pltpu.make_async_copy(ybuf.at[s], ys_ref.at[pl.ds(0, MOE_BLOCK)], sem.at[s]).wait()

    @pl.when(i == 0)
    def _():
        dvm[...] = jnp.zeros_like(dvm)
        ybuf[0] = jnp.zeros((MOE_BLOCK,) + TOKEN_TILE, F32)
        clear = pltpu.make_async_copy(ybuf.at[0], ys_ref.at[pl.ds(trash, TRASH_ROWS)], sem.at[0])
        clear.start()
        clear.wait()

    prev_used = (i > 0) & (i - 1 < nu)
    used = i < nu

    def compute():
        w = _load_token_tiles(x_ref, MOE_BLOCK)
        x = _row_values(w)
        gt = jnp.minimum(_dot(x, wg_ref[...]) + bg_ref[...], SWIGLU_LIMIT)
        up = jnp.clip(_dot(x, wu_ref[...]) + bu_ref[...], -SWIGLU_LIMIT, SWIGLU_LIMIT)
        act = (up + 1.0) * (gt * _sigmoid(SWIGLU_ALPHA * gt))
        _store_token_tiles(ybuf, _dot(act.astype(BF16), wd_ref[...]) + bd_ref[...], slot * MOE_BLOCK)
        for c in range(MOE_BLOCK // LANES):
            dvm[slot, c:c + 1, :] = _row_destinations(w[c * LANES:(c + 1) * LANES], be_ref[i],
                                                      trash + c * LANES)
        dest_copy(slot).start()

    @pl.when(prev_used & used)
    def _():
        send_rows(1 - slot)
        compute()

    @pl.when(prev_used & jnp.logical_not(used))
    def _():
        send_rows(1 - slot)

    @pl.when(jnp.logical_not(prev_used) & used)
    def _():
        compute()

    @pl.when(prev_used)
    def _():
        wait_rows(1 - slot)

    @pl.when((i == nblk - 1) & used)
    def _():
        send_rows(slot)
        wait_rows(slot)


def _moe(xs, block_expert, n_used, n_tokens, wg, bg, wu, bu, wd, bd):
    n_slots = xs.shape[0]
    nblk = n_slots // MOE_BLOCK
    trash = n_tokens * TOP_K
    blk = lambda i, be, nu: (jnp.minimum(i, nu[0] - 1), 0, 0)
    exp = lambda i, be, nu: (be[jnp.minimum(i, nu[0] - 1)], 0, 0)
    wspec = pl.BlockSpec((None, D_MODEL, D_MODEL), exp)
    bspec = pl.BlockSpec((None, 1, D_MODEL), exp)
    grid_spec = pltpu.PrefetchScalarGridSpec(
        num_scalar_prefetch=2,
        grid=(nblk,),
        in_specs=[pl.BlockSpec((MOE_BLOCK,) + TOKEN_TILE, blk), wspec, bspec, wspec, bspec, wspec, bspec],
        out_specs=pl.BlockSpec(memory_space=pl.ANY),
        scratch_shapes=[pltpu.VMEM((2, MOE_BLOCK) + TOKEN_TILE, F32), pltpu.VMEM((2,) + DEST_TILE, I32),
                        pltpu.SMEM((2,) + DEST_TILE, I32), pltpu.SemaphoreType.DMA((2,)),
                        pltpu.SemaphoreType.DMA((2,))],
    )
    return pl.pallas_call(
        functools.partial(_moe_kernel, nblk=nblk, trash=trash),
        grid_spec=grid_spec,
        out_shape=jax.ShapeDtypeStruct((trash + TRASH_ROWS,) + TOKEN_TILE, F32),
        compiler_params=_cparams("arbitrary"),
        name="moe",
    )(block_expert, n_used, xs, wg, bg, wu, bu, wd, bd)


def _combine_kernel(x1_ref, r_ref, g_ref, ys_ref, o_ref, *, tc):
    rows = _as_rows(ys_ref)
    nq = TOKEN_TILE[0]
    gates = r_ref[...]
    acc = x1_ref[...]
    for k in range(TOP_K):
        yk = jnp.concatenate([rows[pl.ds(k * nq + q, tc, stride=TOP_K * nq), :] for q in range(nq)], axis=1)
        acc += gates[:, k:k + 1] * yk
    o_ref[...] = _rms(acc, g_ref[...])


def _combine(x1, route, ys, g_final, tc=256):
    T = x1.shape[0]
    return pl.pallas_call(
        functools.partial(_combine_kernel, tc=tc),
        grid=(T // tc,),
        in_specs=[pl.BlockSpec((tc, D_MODEL), lambda i: (i, 0)),
                  pl.BlockSpec((tc, LANES), lambda i: (i, 0)),
                  pl.BlockSpec((1, D_MODEL), lambda i: (0, 0)),
                  pl.BlockSpec((tc * TOP_K,) + TOKEN_TILE, lambda i: (i, 0, 0))],
        out_specs=pl.BlockSpec((tc, D_MODEL), lambda i: (i, 0)),
        out_shape=jax.ShapeDtypeStruct((T, D_MODEL), F32),
        compiler_params=_cparams("arbitrary"),
        name="combine",
    )(x1, route, g_final, ys)


def _moe_layer(x1, hf, route, cnt, ew, g_final):
    T = x1.shape[0]
    n_assign = T * TOP_K
    nblk = n_assign // MOE_BLOCK + N_EXPERTS
    n_slots = nblk * MOE_BLOCK
    dest, pst = _route(route, cnt)
    cnt_i = cnt[0, :N_EXPERTS].astype(I32)
    pstart = pst[0, :N_EXPERTS].astype(I32)
    padded = (cnt_i + MOE_BLOCK - 1) // MOE_BLOCK * MOE_BLOCK
    pend = pstart + padded
    block_expert = jnp.minimum(
        jnp.sum(jnp.arange(nblk, dtype=I32)[:, None] * MOE_BLOCK >= pend[None, :], axis=1),
        N_EXPERTS - 1).astype(I32)
    n_used = (pend[-1] // MOE_BLOCK).reshape(1).astype(I32)
    dest_flat = dest[:, :TOP_K].reshape(n_assign)
    xs = _dispatch(hf, dest_flat, pstart + cnt_i, padded - cnt_i, n_used, n_slots)
    ys = _moe(xs, block_expert, n_used, T, *ew)
    return _combine(x1, route, ys, g_final)


def _trunk(x, p, bb):
    B, L, D = x.shape
    T = B * L
    x2d = x.reshape(T, D)
    v, x1g, x2g, u = _inproj(x2d, L, p["g_mix"], p["w_in"], p["b_in"], p["short_w"], p["short_b"])
    ya = _hyena(v, x1g, x2g, B, L, p["filt"], p["filt_bias"], bb)
    ybp = _s5(u, B, L, p["s5"])
    x1, hf, route, cnt = _mix(x2d, ya, ybp, p["w_glu"], p["b_glu"], p["g_hyena"], p["g_s5"], p["w_out"],
                              p["g_ffn"], p["w_router"], p["b_router"])
    y = _moe_layer(x1, hf, route, cnt, p["experts"], p["g_final"])
    return y.reshape(B, L, D)


def kernel(x_prompt, x_sample, g_mix, w_in, b_in, short_w, short_b, filt_w1, filt_b1, filt_w2, filt_b2, filt_w3, filt_b3, filt_freq, filt_bias, s5_lam_re, s5_lam_im, s5_log_dt, s5_b_re, s5_b_im, s5_c_re, s5_c_im, s5_d, w_glu, b_glu, g_hyena, g_s5, w_out, g_ffn, w_router, b_router, w_gate, b_gate, w_up, b_up, w_down, b_down, g_final):
    assert g_mix.shape[0] == 1, "one encoder layer"
    row = lambda a: a[0][None, :].astype(F32)
    p = dict(
        g_mix=row(g_mix), w_in=w_in[0].astype(BF16), b_in=row(b_in),
        short_w=short_w[0].astype(F32), short_b=row(short_b),
        filt=(filt_w1[0], filt_b1[0], filt_w2[0], filt_b2[0], filt_w3[0], filt_b3[0], filt_freq[0]),
        filt_bias=filt_bias[0].astype(F32),
        s5=_s5_operators(s5_lam_re[0], s5_lam_im[0], s5_log_dt[0], s5_b_re[0], s5_b_im[0],
                         s5_c_re[0], s5_c_im[0], s5_d[0]),
        w_glu=w_glu[0].astype(BF16), b_glu=row(b_glu), g_hyena=row(g_hyena), g_s5=row(g_s5),
        w_out=w_out[0].astype(BF16), g_ffn=row(g_ffn),
        w_router=_split_bf16(jnp.pad(w_router[0].astype(F32), ((0, 0), (0, LANES - N_EXPERTS)))),
        b_router=jnp.pad(b_router[0].astype(F32), (0, LANES - N_EXPERTS))[None, :],
        experts=(w_gate[0].astype(BF16), b_gate[0][:, None, :], w_up[0].astype(BF16),
                 b_up[0][:, None, :], w_down[0].astype(BF16), b_down[0][:, None, :]),
        g_final=g_final[None, :].astype(F32),
    )
    y_prompt = _trunk(x_prompt, p, bb=1)
    y_sample = _trunk(x_sample, p, bb=4)
    return (y_prompt, y_sample)
```

```python
import functools
import math

import numpy as np
import jax
import jax.numpy as jnp
from jax import lax
from jax.experimental import pallas as pl
from jax.experimental.pallas import tpu as pltpu

F32 = jnp.float32
BF16 = jnp.bfloat16
I32 = jnp.int32
HIGHEST = lax.Precision.HIGHEST

LANES = 128
SUBLANES = 8
D_MODEL = 1024
C_HY = 512
C_S5 = 512
D_IN = 3 * C_HY + C_S5
Q_HY = C_HY // LANES
FILTER_BANDS = 16
FILTER_WIDTH = 64
Z_PAD = 128
DECAY_MAX = math.log(1e-2) / 0.3
DECAY_MIN = math.log(1e-2) / 1.5
S5_G, S5_H, S5_P = 32, 16, 64
S5_CHUNK = 16
S5_UW = S5_CHUNK * S5_H
S5_SW = 2 * S5_P
S5_GB = LANES // S5_H
S5_NB = S5_G // S5_GB
S5_KW = S5_CHUNK * LANES
S5_XW = S5_GB * S5_SW
S5_STEP_ROWS = 512
LAMBDA_RE_MAX = -1e-4
N_EXPERTS = 32
TOP_K = 4
SWIGLU_LIMIT = 7.0
SWIGLU_ALPHA = 1.702
MOE_BLOCK = 512
RMS_EPS = 1e-6
VMEM_LIMIT_BYTES = 56 * 1024 * 1024


def _cparams(*sem):
    return pltpu.CompilerParams(dimension_semantics=sem, vmem_limit_bytes=VMEM_LIMIT_BYTES)


def _rms(x, g):
    return x * lax.rsqrt(jnp.mean(x * x, axis=-1, keepdims=True) + RMS_EPS) * g


def _sigmoid(x):
    return 1.0 / (1.0 + jnp.exp(-x))


def _dot(a, b):
    return jnp.dot(a, b, preferred_element_type=F32)


def _split_bf16(w):
    hi = w.astype(BF16)
    return hi, (w - hi.astype(F32)).astype(BF16)


def _lane_block(x, q):
    return x[:, q * LANES:(q + 1) * LANES]


def _as_rows(ref):
    return ref.reshape(math.prod(ref.shape[:-1]), LANES)


TOKEN_TILE = (D_MODEL // LANES, LANES)


def _load_token_tiles(ref, n, row0=0):
    rows = _as_rows(ref)
    nq = TOKEN_TILE[0]
    return jnp.concatenate([rows[pl.ds(row0 * nq + q, n, stride=nq), :] for q in range(nq)], axis=1)


def _store_token_tiles(ref, x):
    rows = _as_rows(ref)
    nq = TOKEN_TILE[0]
    for q in range(nq):
        rows[pl.ds(q, x.shape[0], stride=nq), :] = _lane_block(x, q)


def _inproj_kernel(x_ref, xp_ref, xn_ref, g_ref, w_ref, b_ref, sw_ref, sb_ref,
                   v_ref, x1_ref, x2_ref, u_ref, *, tm, seq):
    i = pl.program_id(0)
    g = g_ref[...]
    rows = jnp.concatenate([x_ref[...], xp_ref[...], xn_ref[...]], axis=0)
    proj = _dot(_rms(rows, g).astype(BF16), w_ref[...]) + b_ref[...]
    u_ref[...] = proj[:tm, 3 * C_HY:]
    z = proj[:tm, :3 * C_HY]
    zh = proj[tm:, :3 * C_HY]
    row0 = i * tm
    has_prev = lax.rem(row0, seq) != 0
    has_next = lax.rem(row0 + tm, seq) != 0
    zp = jnp.where(has_prev, zh[SUBLANES - 1:SUBLANES], 0.0)
    zn = jnp.where(has_next, zh[SUBLANES:SUBLANES + 1], 0.0)
    rid = lax.broadcasted_iota(I32, (tm, 1), 0)
    zm1 = jnp.where(rid == 0, zp, pltpu.roll(z, 1, 0))
    zp1 = jnp.where(rid == tm - 1, zn, pltpu.roll(z, tm - 1, 0))
    sw = sw_ref[...]
    o = zm1 * sw[0:1] + z * sw[1:2] + zp1 * sw[2:3] + sb_ref[...]
    for q in range(Q_HY):
        v_ref[q] = _lane_block(o, q)
        x1_ref[q] = _lane_block(o, Q_HY + q)
        x2_ref[q] = _lane_block(o, 2 * Q_HY + q)


def _inproj(x2d, seq, g_mix, w_in_bf, b_in, short_w, short_b, tm=512):
    T = x2d.shape[0]
    nb8 = T // SUBLANES
    tb = tm // SUBLANES
    hy = jax.ShapeDtypeStruct((Q_HY, T, LANES), F32)
    hy_spec = pl.BlockSpec((Q_HY, tm, LANES), lambda i: (0, i, 0))
    const = lambda i: (0, 0)
    return pl.pallas_call(
        functools.partial(_inproj_kernel, tm=tm, seq=seq),
        grid=(T // tm,),
        in_specs=[
            pl.BlockSpec((tm, D_MODEL), lambda i: (i, 0)),
            pl.BlockSpec((SUBLANES, D_MODEL), lambda i: (jnp.maximum(i * tb - 1, 0), 0)),
            pl.BlockSpec((SUBLANES, D_MODEL), lambda i: (jnp.minimum((i + 1) * tb, nb8 - 1), 0)),
            pl.BlockSpec((1, D_MODEL), const),
            pl.BlockSpec((D_MODEL, D_IN), const),
            pl.BlockSpec((1, D_IN), const),
            pl.BlockSpec((3, 3 * C_HY), const),
            pl.BlockSpec((1, 3 * C_HY), const),
        ],
        out_specs=[hy_spec, hy_spec, hy_spec, pl.BlockSpec((tm, C_S5), lambda i: (i, 0))],
        out_shape=[hy, hy, hy, jax.ShapeDtypeStruct((T, C_S5), F32)],
        compiler_params=_cparams("arbitrary"),
        name="inproj",
    )(x2d, x2d, x2d, g_mix, w_in_bf, b_in, short_w, short_b)


def _filter_kernel(z_ref, w1_ref, b1_ref, w2_ref, b2_ref, w3h_ref, w3l_ref, b3_ref, fr_ref, dl_ref,
                   k_ref, s_ref, *, tr, seq):
    i = pl.program_id(0)
    z = z_ref[...]
    fr = fr_ref[...]
    h = jnp.sin(fr[0:1] * (jnp.dot(z, w1_ref[...], precision=HIGHEST,
                                   preferred_element_type=F32) + b1_ref[...]))
    h = jnp.sin(fr[1:2] * (jnp.dot(h, w2_ref[...], precision=HIGHEST,
                                   preferred_element_type=F32) + b2_ref[...]))
    h_hi = h.astype(BF16)
    h_lo = (h - h_hi.astype(F32)).astype(BF16)
    h = _dot(h_hi, w3h_ref[...]) + _dot(h_lo, w3h_ref[...]) + _dot(h_hi, w3l_ref[...]) + b3_ref[...]
    h = h * jnp.exp(-z[:, 0:1] * dl_ref[...])
    rid = i * tr + lax.broadcasted_iota(I32, (tr, 1), 0)
    h = jnp.where(rid == seq, 0.0, h)
    for q in range(2 * Q_HY):
        k_ref[q] = _lane_block(h, q)

    @pl.when(i == 0)
    def _():
        s_ref[...] = jnp.zeros_like(s_ref)

    s_ref[...] += jnp.sum(jnp.abs(h), axis=0, keepdims=True)


def _filter_taps(seq, w1, b1, w2, b2, w3, b3, freq, tr=512):
    L = seq
    N = 2 * L
    n = jnp.arange(N, dtype=I32)
    pos = jnp.where(n < L, n, jnp.where(n == L, 0, N - n)).astype(F32)
    t = (pos * (1.0 / (L - 1)))[:, None]
    w = 2.0 * math.pi * pos / L
    bands = jnp.linspace(1e-4, FILTER_BANDS - 1, FILTER_BANDS, dtype=F32)
    ang = w[:, None] * bands[None, :]
    zc = jnp.concatenate([t, jnp.cos(ang), -jnp.sin(ang),
                          jnp.zeros((N, Z_PAD - 1 - 2 * FILTER_BANDS), F32)], axis=-1)
    w1p = jnp.pad(w1, ((0, Z_PAD - w1.shape[0]), (0, 0)))
    w3d = w3.reshape(FILTER_WIDTH, 2, 2 * C_HY).transpose(1, 0, 2)
    b3d = b3.reshape(2, 1, 2 * C_HY)
    deltas = jnp.abs(jnp.linspace(DECAY_MIN, DECAY_MAX, C_HY, dtype=F32))
    dl = jnp.tile(deltas, 2)[None, :]
    half = (N // tr) // 2
    const = lambda i: (0, 0)
    taps, sums = pl.pallas_call(
        functools.partial(_filter_kernel, tr=tr, seq=L),
        grid=(N // tr,),
        in_specs=[
            pl.BlockSpec((tr, Z_PAD), lambda i: (i, 0)),
            pl.BlockSpec((Z_PAD, FILTER_WIDTH), const),
            pl.BlockSpec((1, FILTER_WIDTH), const),
            pl.BlockSpec((FILTER_WIDTH, FILTER_WIDTH), const),
            pl.BlockSpec((1, FILTER_WIDTH), const),
            pl.BlockSpec((None, FILTER_WIDTH, 2 * C_HY), lambda i: (i // half, 0, 0)),
            pl.BlockSpec((None, FILTER_WIDTH, 2 * C_HY), lambda i: (i // half, 0, 0)),
            pl.BlockSpec((None, 1, 2 * C_HY), lambda i: (i // half, 0, 0)),
            pl.BlockSpec((2, FILTER_WIDTH), const),
            pl.BlockSpec((1, 2 * C_HY), const),
        ],
        out_specs=[pl.BlockSpec((2 * Q_HY, tr, LANES), lambda i: (0, i, 0)),
                   pl.BlockSpec((1, 2 * C_HY), const)],
        out_shape=[jax.ShapeDtypeStruct((2 * Q_HY, N, LANES), F32),
                   jax.ShapeDtypeStruct((1, 2 * C_HY), F32)],
        compiler_params=_cparams("arbitrary"),
        name="hyena_filter",
    )(zc, w1p, b1[None, :], w2, b2[None, :], *_split_bf16(w3d.astype(F32)), b3d, freq, dl)
    return taps, sums


def _split_n(N):
    n1 = {32768: 128, 4096: 64}.get(N)
    if n1 is None:
        n1 = 1 << (int(math.log2(N)) // 2)
    return n1, N // n1


def _dft_constants(N1, N2):
    N = N1 * N2
    k1 = np.arange(N1)[:, None]
    n1 = np.arange(N1)[None, :]
    ang = 2.0 * np.pi * ((k1 * n1) % N1) / N1
    fa_full = np.concatenate([np.cos(ang), -np.sin(ang)], axis=0)
    fa_half = fa_full[:, :N1 // 2]
    fa_inv = np.concatenate([np.cos(ang), -np.sin(ang)], axis=1)[:N1 // 2] / N
    k2 = np.arange(N2)[:, None]
    n2 = np.arange(N2)[None, :]
    a2 = 2.0 * np.pi * ((k2 * n2) % N2) / N2
    cr = jnp.asarray(np.cos(a2).astype(np.float32))[None]
    ci = jnp.asarray((-np.sin(a2)).astype(np.float32))[None]
    to = lambda a: jnp.asarray(a.astype(np.float32)).astype(BF16)
    kk = lax.broadcasted_iota(I32, (N1, N2), 0)
    nn = lax.broadcasted_iota(I32, (N1, N2), 1)
    ta = (2.0 * math.pi / N) * lax.rem(kk * nn, N).astype(F32)
    tr, ti = jnp.cos(ta), -jnp.sin(ta)
    stack = lambda re, im: jnp.concatenate(
        [jnp.concatenate([re, -im], axis=-1), jnp.concatenate([im, re], axis=-1)], axis=-2).astype(BF16)
    tc_r, tc_i = tr[:, None, :], ti[:, None, :]
    fk = stack(cr * tc_r - ci * tc_i, cr * tc_i + ci * tc_r)
    tr_r, tr_i = tr[:, :, None], ti[:, :, None]
    gk = stack(cr * tr_r - ci * tr_i, -(cr * tr_i + ci * tr_r))
    return dict(fa_full=to(fa_full), fa_half=to(fa_half), fa_inv=to(fa_inv), fk=fk, gk=gk)


U32 = jnp.uint32
_HI16 = 0xFFFF0000
_HALF16 = 0x8000


def _pack_spectrum(re, im):
    rb = lax.bitcast_convert_type(re, U32) + U32(_HALF16)
    ib = lax.bitcast_convert_type(im, U32) + U32(_HALF16)
    return (rb & U32(_HI16)) | (ib >> 16)


def _unpack_spectrum(w):
    return (lax.bitcast_convert_type(w & U32(_HI16), F32), lax.bitcast_convert_type(w << 16, F32))


N2_STEP = 16


def _dft_a_kernel(f_ref, x_ref, o_ref, *, rows):
    nq = x_ref.shape[0]
    x2d = _as_rows(x_ref)
    f = f_ref[...]
    half = f.shape[0] // 2
    for j in range(N2_STEP):
        xs = jnp.concatenate(
            [x2d[pl.ds(q * rows * N2_STEP + j, rows, stride=N2_STEP), :] for q in range(nq)], axis=1)
        r = _dot(f, xs.astype(BF16))
        w = _pack_spectrum(r[:half], r[half:])
        for q in range(nq):
            o_ref[q, j] = _lane_block(w, q)


def _dft_a(f, x):
    Q, Bt, K, N2, _ = x.shape
    R = f.shape[0] // 2
    return pl.pallas_call(
        functools.partial(_dft_a_kernel, rows=K),
        grid=(Bt, N2 // N2_STEP, Q // Q_HY),
        in_specs=[pl.BlockSpec((2 * R, K), lambda b, j, w: (0, 0)),
                  pl.BlockSpec((Q_HY, None, K, N2_STEP, LANES), lambda b, j, w: (w, b, 0, j, 0))],
        out_specs=pl.BlockSpec((Q_HY, None, N2_STEP, R, LANES), lambda b, j, w: (w, b, j, 0, 0)),
        out_shape=jax.ShapeDtypeStruct((Q, Bt, N2, R, LANES), U32),
        compiler_params=_cparams("arbitrary", "arbitrary", "arbitrary"),
        name="dft_a",
    )(f, x)


def _dft_a_inv_kernel(f_ref, z_ref, gate_ref, v_ref, bias_ref, *rest, rows, with_next):
    if with_next:
        fn_ref, o_ref, an_ref = rest
    else:
        (o_ref,) = rest
    g2d = _as_rows(gate_ref)
    v2d = _as_rows(v_ref)
    o2d = _as_rows(o_ref)
    f = f_ref[...]
    bias = bias_ref[...]
    for j in range(N2_STEP):
        zr, zi = _unpack_spectrum(jnp.concatenate([z_ref[q, j] for q in range(Q_HY)], axis=1))
        y = _dot(f, jnp.concatenate([zr, zi], axis=0).astype(BF16))
        outs = []
        for q in range(Q_HY):
            sl = pl.ds(q * rows * N2_STEP + j, rows, stride=N2_STEP)
            outs.append(g2d[sl, :] * (_lane_block(y, q) + v2d[sl, :] * _lane_block(bias, q)))
            o2d[sl, :] = outs[-1]
        if with_next:
            r = _dot(fn_ref[...], jnp.concatenate(outs, axis=1).astype(BF16))
            half = r.shape[0] // 2
            w = _pack_spectrum(r[:half], r[half:])
            for q in range(Q_HY):
                an_ref[q, j] = _lane_block(w, q)


def _dft_a_inv(f, z, gate, v, bias_row, f_next=None):
    Q, Bt, N2, N1, _ = z.shape
    K = f.shape[0]
    nat = pl.BlockSpec((Q, None, K, N2_STEP, LANES), lambda b, j: (0, b, 0, j, 0))
    spec = pl.BlockSpec((Q, None, N2_STEP, N1, LANES), lambda b, j: (0, b, j, 0, 0))
    with_next = f_next is not None
    in_specs = [pl.BlockSpec((K, 2 * N1), lambda b, j: (0, 0)), spec, nat, nat,
                pl.BlockSpec((1, C_HY), lambda b, j: (0, 0))]
    out_specs, out_shape = [nat], [jax.ShapeDtypeStruct((Q, Bt, K, N2, LANES), F32)]
    args = [f, z, gate, v, bias_row]
    if with_next:
        in_specs.append(pl.BlockSpec((2 * N1, K), lambda b, j: (0, 0)))
        out_specs.append(spec)
        out_shape.append(jax.ShapeDtypeStruct(z.shape, U32))
        args.append(f_next)
    return pl.pallas_call(
        functools.partial(_dft_a_inv_kernel, rows=K, with_next=with_next),
        grid=(Bt, N2 // N2_STEP),
        in_specs=in_specs,
        out_specs=out_specs,
        out_shape=out_shape,
        compiler_params=_cparams("arbitrary", "arbitrary"),
        name="dft_a_inv",
    )(*args)


C_STEP_Q = 2
K1_STEP = SUBLANES


def _dft_c_kernel(a_ref, k_ref, f_ref, g_ref, z_ref, *, bb, n2):
    per = n2 * K1_STEP
    a2d = _as_rows(a_ref)
    z2d = _as_rows(z_ref)
    cols = [(b, q) for b in range(bb) for q in range(C_STEP_Q)]

    def rows(b, q, kk):
        return pl.ds((q * bb + b) * per + kk, n2, stride=K1_STEP)

    for kk in range(K1_STEP):
        kr = jnp.tile(k_ref[0, kk], (1, bb))
        ki = jnp.tile(k_ref[1, kk], (1, bb))
        ar, ai = _unpack_spectrum(jnp.concatenate([a2d[rows(b, q, kk), :] for b, q in cols], axis=1))
        x = _dot(f_ref[kk], jnp.concatenate([ar, ai], axis=0).astype(BF16))
        xr = x[:n2]
        xi = x[n2:]
        yr = xr * kr - xi * ki
        yi = xr * ki + xi * kr
        zz = _dot(g_ref[kk], jnp.concatenate([yr, yi], axis=0).astype(BF16))
        w = _pack_spectrum(zz[:n2], zz[n2:])
        for c, (b, q) in enumerate(cols):
            z2d[rows(b, q, kk), :] = _lane_block(w, c)


def _dft_c(a5, khat, order, fk, gk, bb):
    Q, Bt, N2, N1, _ = a5.shape
    cw = C_STEP_Q * LANES
    ablk = pl.BlockSpec((C_STEP_Q, bb, N2, K1_STEP, LANES), lambda g, c, b: (c, b, 0, g, 0))
    mats = pl.BlockSpec((K1_STEP, 2 * N2, 2 * N2), lambda g, c, b: (g, 0, 0))
    return pl.pallas_call(
        functools.partial(_dft_c_kernel, bb=bb, n2=N2),
        grid=(N1 // K1_STEP, Q // C_STEP_Q, Bt // bb),
        in_specs=[ablk,
                  pl.BlockSpec((2, K1_STEP, N2, cw), lambda g, c, b: (0, g, 0, order * (C_HY // cw) + c)),
                  mats, mats],
        out_specs=ablk,
        out_shape=jax.ShapeDtypeStruct(a5.shape, U32),
        compiler_params=_cparams("arbitrary", "arbitrary", "arbitrary"),
        name="dft_c",
    )(a5, khat, fk, gk)


def _dft_c_filter_kernel(a_ref, f_ref, s_ref, o_ref, *, n2):
    per = n2 * K1_STEP
    a2d = _as_rows(a_ref)
    sc = 1.0 / (s_ref[...] + 1e-6)
    for kk in range(K1_STEP):
        ar, ai = _unpack_spectrum(jnp.concatenate(
            [a2d[pl.ds(q * per + kk, n2, stride=K1_STEP), :] for q in range(C_STEP_Q)], axis=1))
        x = _dot(f_ref[kk], jnp.concatenate([ar, ai], axis=0).astype(BF16))
        o_ref[0, kk] = x[:n2] * sc
        o_ref[1, kk] = x[n2:] * sc


def _dft_c_filter(a4, fk, sums):
    Q, N2, N1, _ = a4.shape
    cw = C_STEP_Q * LANES
    return pl.pallas_call(
        functools.partial(_dft_c_filter_kernel, n2=N2),
        grid=(N1 // K1_STEP, Q // C_STEP_Q),
        in_specs=[pl.BlockSpec((C_STEP_Q, N2, K1_STEP, LANES), lambda g, c: (c, 0, g, 0)),
                  pl.BlockSpec((K1_STEP, 2 * N2, 2 * N2), lambda g, c: (g, 0, 0)),
                  pl.BlockSpec((1, cw), lambda g, c: (0, c))],
        out_specs=pl.BlockSpec((2, K1_STEP, N2, cw), lambda g, c: (0, g, 0, c)),
        out_shape=jax.ShapeDtypeStruct((2, N1, N2, Q * LANES), F32),
        compiler_params=_cparams("arbitrary", "arbitrary"),
        name="dft_c_filter",
    )(a4, fk, sums)


def _hyena(v, x1, x2, B, seq, filt, filt_bias, bb):
    L = seq
    N = 2 * L
    N1, N2 = _split_n(N)
    cst = _dft_constants(N1, N2)
    taps, sums = _filter_taps(L, *filt)
    ka = _dft_a(cst["fa_full"], taps.reshape(2 * Q_HY, 1, N1, N2, LANES))
    khat = _dft_c_filter(ka.reshape(2 * Q_HY, N2, N1, LANES), cst["fk"], sums)
    nat = lambda a: a.reshape(Q_HY, B, N1 // 2, N2, LANES)
    cur = nat(v)
    a = _dft_a(cst["fa_half"], cur)
    for o, gate in enumerate((x1, x2)):
        z = _dft_c(a, khat, o, cst["fk"], cst["gk"], bb)
        if o == 0:
            cur, a = _dft_a_inv(cst["fa_inv"], z, nat(gate), cur, filt_bias[o][None, :], cst["fa_half"])
        else:
            (cur,) = _dft_a_inv(cst["fa_inv"], z, nat(gate), cur, filt_bias[o][None, :])
    return cur.reshape(Q_HY, B * L, LANES)


def _s5_operators(lam_re, lam_im, log_dt, b_re, b_im, c_re, c_im, d_skip):
    Tc, G, H, P = S5_CHUNK, S5_G, S5_H, S5_P
    lam = jnp.minimum(lam_re.astype(F32), LAMBDA_RE_MAX) + 1j * lam_im.astype(F32)
    dt = jnp.exp(log_dt.astype(F32))[..., None]
    lam_dt = lam * dt
    lam_bar = jnp.exp(lam_dt)
    b_bar = ((lam_bar - 1.0) / lam)[..., None] * (b_re.astype(F32) + 1j * b_im.astype(F32))
    c = c_re.astype(F32) + 1j * c_im.astype(F32)
    tau = jnp.arange(Tc + 1, dtype=F32)
    pw = jnp.exp(lam_dt[None] * tau[:, None, None, None])
    kk = jnp.einsum('dghp,tdgp,dgpk->dtghk', c, pw[:Tc], b_bar).real
    i = jnp.arange(Tc)
    lag = i[:, None] - i[None, :]
    kf = jnp.where((lag >= 0)[:, :, None, None, None], kk[0][jnp.clip(lag, 0, Tc - 1)], 0.0)
    kb = jnp.where((lag <= 0)[:, :, None, None, None], kk[1][jnp.clip(-lag, 0, Tc - 1)], 0.0)
    m = kf + kb
    eye = (lag == 0)[:, :, None, None, None] * jnp.eye(H, dtype=F32)[None, None, None]
    m = m + eye * d_skip.astype(F32).reshape(G, H)[None, None, :, :, None]
    clf = c[0][None] * pw[1:Tc + 1, 0][:, :, None, :]
    clb = c[1][None] * pw[Tc - i, 1][:, :, None, :]
    sf = pw[Tc - 1 - i, 0][:, :, :, None] * b_bar[0][None]
    sb = pw[i, 1][:, :, :, None] * b_bar[1][None]
    w_intra = m.transpose(2, 1, 4, 0, 3).reshape(G, Tc * H, Tc * H)
    st2y = lambda cl: jnp.concatenate([cl.real, -cl.imag], axis=-1).transpose(1, 3, 0, 2) \
        .reshape(G, 2 * P, Tc * H)
    u2s = lambda s: jnp.concatenate([s.real, s.imag], axis=2).transpose(1, 0, 3, 2) \
        .reshape(G, Tc * H, 2 * P)
    w_state = jnp.concatenate([u2s(sf), u2s(sb)], axis=-1)
    pair = lambda a: a.reshape(2, G // 2, 2 * P)
    lam16 = jnp.concatenate([pair(pw[Tc].real), pair(pw[Tc].imag)], axis=-1).reshape(2, G * 2 * P)
    col = np.arange(S5_KW)
    spread = (np.arange(S5_UW)[:, None] == (col // LANES * S5_H + col % S5_H)[None, :])
    return dict(w_intra=w_intra.astype(BF16), w_xf=st2y(clf).astype(BF16), w_xb=st2y(clb).astype(BF16),
                w_state=w_state.astype(BF16), lam16=lam16,
                spread=jnp.asarray(spread.astype(np.float32)).astype(BF16))


def _s5_rows(i, cl, B, ncc):
    return pl.ds(S5_CHUNK * cl + i, B, stride=S5_CHUNK * ncc)


def _s5_chunk_inputs(u2d, B, ncc):
    cols = []
    for i in range(S5_CHUNK):
        if B == 1:
            cols.append(u2d[pl.ds(i, ncc, stride=S5_CHUNK), :])
        else:
            cols.append(jnp.concatenate([u2d[_s5_rows(i, cl, B, ncc), :] for cl in range(ncc)], axis=0))
    return jnp.concatenate(cols, axis=1).astype(BF16)


def _s5_state_lane(a, part):
    return (a // 2) * 2 * S5_SW + part * S5_SW + (a % 2) * S5_P


def _s5_state_kernel(u_ref, w_ref, sf_ref, sb_ref, wblk, *, B, ncc):
    @pl.when(pl.program_id(1) == 0)
    def _():
        wblk[...] = jnp.zeros_like(wblk)
        for a in range(S5_GB):
            for j in range(S5_CHUNK):
                r0 = j * LANES + a * S5_H
                for d in range(2):
                    for part in range(2):
                        c0 = d * S5_XW + _s5_state_lane(a, part)
                        s0 = d * S5_SW + part * S5_P
                        wblk[r0:r0 + S5_H, c0:c0 + S5_P] = w_ref[a, j * S5_H:(j + 1) * S5_H, s0:s0 + S5_P]

    lhs = _s5_chunk_inputs(_as_rows(u_ref), B, ncc)
    r = _dot(lhs, wblk[...])
    sf_ref[...] = r[:, :S5_XW]
    sb_ref[...] = r[:, S5_XW:]


def _s5_state(u3, w_state, ncc):
    B, L, _ = u3.shape
    nch = L // S5_CHUNK
    out = jax.ShapeDtypeStruct((nch * B, S5_NB * S5_XW), F32)
    return pl.pallas_call(
        functools.partial(_s5_state_kernel, B=B, ncc=ncc),
        grid=(S5_NB, nch // ncc),
        in_specs=[pl.BlockSpec((B, S5_CHUNK * ncc, LANES), lambda q, t: (0, t, q)),
                  pl.BlockSpec((S5_GB, S5_UW, 2 * S5_SW), lambda q, t: (q, 0, 0))],
        out_specs=[pl.BlockSpec((ncc * B, S5_XW), lambda q, t: (t, q))] * 2,
        out_shape=[out, out],
        scratch_shapes=[pltpu.VMEM((S5_KW, 2 * S5_XW), BF16)],
        compiler_params=_cparams("arbitrary", "arbitrary"),
        name="s5_state",
    )(u3, w_state)


def _s5_scan_kernel(sf_ref, sb_ref, lam_ref, xf_ref, xb_ref, *, nch, rows):
    lb = sf_ref.shape[1]
    vr = max(rows, SUBLANES)
    lam_f, lam_b = lam_ref[0], lam_ref[1]

    def advance(x, lam, inc):
        out = []
        for u in range(0, lb, 2 * S5_SW):
            re, im = x[:, u:u + S5_SW], x[:, u + S5_SW:u + 2 * S5_SW]
            lr, li = lam[:, u:u + S5_SW], lam[:, u + S5_SW:u + 2 * S5_SW]
            out += [lr * re - li * im, lr * im + li * re]
        return jnp.concatenate(out, axis=1) + inc

    def body(c, carry):
        xf, xb = carry
        rf = pl.ds(pl.multiple_of(c * rows, rows), rows)
        xf_ref[rf, :] = xf[:rows]
        xf = advance(xf, lam_f, sf_ref[rf, :])
        rb = pl.ds(pl.multiple_of((nch - 1 - c) * rows, rows), rows)
        xb_ref[rb, :] = xb[:rows]
        xb = advance(xb, lam_b, sb_ref[rb, :])
        return xf, xb

    zero = jnp.zeros((vr, lb), F32)
    lax.fori_loop(0, nch, body, (zero, zero))


def _s5_scan(sf, sb, lam16, nch, rows, lb):
    R, lanes = sf.shape
    blk = pl.BlockSpec((R, lb), lambda j: (0, j))
    out = jax.ShapeDtypeStruct((R, lanes), F32)
    return pl.pallas_call(
        functools.partial(_s5_scan_kernel, nch=nch, rows=rows),
        grid=(lanes // lb,),
        in_specs=[blk, blk, pl.BlockSpec((2, 1, lb), lambda j: (0, 0, j))],
        out_specs=[blk, blk],
        out_shape=[out, out],
        compiler_params=_cparams("arbitrary"),
        name="s5_scan",
    )(sf, sb, lam16[:, None, :])


def _s5_out_kernel(u_ref, xf_ref, xb_ref, wm_ref, wf_ref, wb_ref, e_ref, y_ref, wm_blk, wf_blk, wb_blk,
                   *, B, ncc):
    @pl.when(pl.program_id(1) == 0)
    def _():
        lane = lax.broadcasted_iota(I32, (1, S5_KW), 1)
        slot = (lane // S5_H) % S5_GB
        e = e_ref[...]
        for a in range(S5_GB):
            own = slot == a
            ex = jnp.where(own, _dot(wm_ref[a], e), 0.0).astype(BF16)
            for j in range(S5_CHUNK):
                r0 = j * LANES + a * S5_H
                wm_blk[r0:r0 + S5_H, :] = ex[j * S5_H:(j + 1) * S5_H, :]
            for src, dst in ((wf_ref, wf_blk), (wb_ref, wb_blk)):
                ex = jnp.where(own, _dot(src[a], e), 0.0).astype(BF16)
                for part in range(2):
                    r0 = _s5_state_lane(a, part)
                    dst[r0:r0 + S5_P, :] = ex[part * S5_P:(part + 1) * S5_P, :]

    lhs = _s5_chunk_inputs(_as_rows(u_ref), B, ncc)
    acc = _dot(lhs, wm_blk[...])
    acc += _dot(xf_ref[...].astype(BF16), wf_blk[...])
    acc += _dot(xb_ref[...].astype(BF16), wb_blk[...])
    y2d = _as_rows(y_ref)
    for i in range(S5_CHUNK):
        piece = _lane_block(acc, i)
        if B == 1:
            y2d[pl.ds(i, ncc, stride=S5_CHUNK), :] = piece
        else:
            for cl in range(ncc):
                y2d[_s5_rows(i, cl, B, ncc), :] = piece[cl * B:(cl + 1) * B]


def _s5_out(u3, xf, xb, ops, ncc):
    B, L, _ = u3.shape
    nch = L // S5_CHUNK
    tok = pl.BlockSpec((B, S5_CHUNK * ncc, LANES), lambda q, t: (0, t, q))
    st = pl.BlockSpec((ncc * B, S5_XW), lambda q, t: (t, q))
    return pl.pallas_call(
        functools.partial(_s5_out_kernel, B=B, ncc=ncc),
        grid=(S5_NB, nch // ncc),
        in_specs=[tok, st, st,
                  pl.BlockSpec((S5_GB, S5_UW, S5_UW), lambda q, t: (q, 0, 0)),
                  pl.BlockSpec((S5_GB, S5_SW, S5_UW), lambda q, t: (q, 0, 0)),
                  pl.BlockSpec((S5_GB, S5_SW, S5_UW), lambda q, t: (q, 0, 0)),
                  pl.BlockSpec((S5_UW, S5_KW), lambda q, t: (0, 0))],
        out_specs=tok,
        out_shape=jax.ShapeDtypeStruct(u3.shape, F32),
        scratch_shapes=[pltpu.VMEM((S5_KW, S5_KW), BF16), pltpu.VMEM((S5_XW, S5_KW), BF16),
                        pltpu.VMEM((S5_XW, S5_KW), BF16)],
        compiler_params=_cparams("arbitrary", "arbitrary"),
        name="s5_out",
    )(u3, xf, xb, ops["w_intra"], ops["w_xf"], ops["w_xb"], ops["spread"])


def _s5(u, B, seq, ops):
    nch = seq // S5_CHUNK
    u3 = u.reshape(B, seq, C_S5)
    ncc = min(nch, S5_STEP_ROWS // B)
    sf, sb = _s5_state(u3, ops["w_state"], ncc)
    lb = 1024 if B == 1 else 256
    xf, xb = _s5_scan(sf, sb, ops["lam16"], nch, B, lb)
    return _s5_out(u3, xf, xb, ops, ncc).reshape(B * seq, C_S5)


def _mix_kernel(x_ref, ya_ref, yb_ref, wglu_ref, bglu_ref, gh_ref, gs_ref, wout_ref, gffn_ref,
                wrh_ref, wrl_ref, br_ref, x1_ref, hf_ref, route_ref, cnt_ref, *, tm):
    g = jax.nn.gelu(yb_ref[...])
    yb = g * _sigmoid(_dot(g.astype(BF16), wglu_ref[...]) + bglu_ref[...])
    ya = jnp.concatenate([ya_ref[q] for q in range(Q_HY)], axis=1)
    na = _rms(ya, gh_ref[...]).astype(BF16)
    nb = _rms(yb, gs_ref[...]).astype(BF16)
    mixed = _dot(na, wout_ref[:C_HY, :]) + _dot(nb, wout_ref[C_HY:, :])
    x1 = x_ref[...] + mixed
    x1_ref[...] = x1
    hf = _rms(x1, gffn_ref[...])
    _store_token_tiles(hf_ref, hf)
    hf_hi = hf.astype(BF16)
    hf_lo = (hf - hf_hi.astype(F32)).astype(BF16)
    logits = (_dot(hf_hi, wrh_ref[...]) + _dot(hf_lo, wrh_ref[...]) + _dot(hf_hi, wrl_ref[...])
              + br_ref[...])
    lane = lax.broadcasted_iota(I32, (tm, LANES), 1)
    neg = jnp.float32(-jnp.inf)
    l = jnp.where(lane < N_EXPERTS, logits, neg)
    vals, idxs = [], []
    for _ in range(TOP_K):
        m = jnp.max(l, axis=-1, keepdims=True)
        idx = jnp.min(jnp.where(l == m, lane, LANES), axis=-1, keepdims=True)
        vals.append(m)
        idxs.append(idx)
        l = jnp.where(lane == idx, neg, l)
    es = [jnp.exp(v - vals[0]) for v in vals]
    den = es[0] + es[1] + es[2] + es[3]
    route = jnp.zeros((tm, LANES), F32)
    for k in range(TOP_K):
        route = jnp.where(lane == k, es[k] / den, route)
        route = jnp.where(lane == TOP_K + k, idxs[k].astype(F32), route)
    route_ref[...] = route

    @pl.when(pl.program_id(0) == 0)
    def _():
        cnt_ref[...] = jnp.zeros_like(cnt_ref)

    cnt_ref[...] += jnp.sum(_expert_one_hot(idxs, tm), axis=0, keepdims=True)


def _mix(x2d, ya4, ybp, w_glu_bf, b_glu, g_hyena, g_s5, w_out_bf, g_ffn, w_router_p, b_router_p, tm=512):
    T = x2d.shape[0]
    const = lambda i: (0, 0)
    row = lambda w: pl.BlockSpec((tm, w), lambda i: (i, 0))
    return pl.pallas_call(
        functools.partial(_mix_kernel, tm=tm),
        grid=(T // tm,),
        in_specs=[row(D_MODEL), pl.BlockSpec((Q_HY, tm, LANES), lambda i: (0, i, 0)), row(C_S5),
                  pl.BlockSpec((C_S5, C_S5), const), pl.BlockSpec((1, C_S5), const),
                  pl.BlockSpec((1, C_HY), const), pl.BlockSpec((1, C_S5), const),
                  pl.BlockSpec((D_MODEL, D_MODEL), const), pl.BlockSpec((1, D_MODEL), const),
                  pl.BlockSpec((D_MODEL, LANES), const), pl.BlockSpec((D_MODEL, LANES), const),
                  pl.BlockSpec((1, LANES), const)],
        out_specs=[row(D_MODEL), pl.BlockSpec((tm,) + TOKEN_TILE, lambda i: (i, 0, 0)), row(LANES),
                   pl.BlockSpec((1, LANES), const)],
        out_shape=[jax.ShapeDtypeStruct((T, D_MODEL), F32), jax.ShapeDtypeStruct((T,) + TOKEN_TILE, F32),
                   jax.ShapeDtypeStruct((T, LANES), F32), jax.ShapeDtypeStruct((1, LANES), F32)],
        compiler_params=_cparams("arbitrary"),
        name="mix",
    )(x2d, ya4, ybp, w_glu_bf, b_glu, g_hyena, g_s5, w_out_bf, g_ffn, w_router_p[0], w_router_p[1],
      b_router_p)


def _expert_one_hot(eids, tm):
    lane = lax.broadcasted_iota(I32, (tm, LANES), 1)
    oh = jnp.zeros((tm, LANES), F32)
    for e in eids:
        oh += (lane == e).astype(F32)
    return oh


def _route_kernel(r_ref, cnt_ref, dest_ref, pst_ref, carry, pstart, *, tm):
    i = pl.program_id(0)
    lane = lax.broadcasted_iota(I32, (tm, LANES), 1)
    r = r_ref[...]
    eids = [r[:, TOP_K + k:TOP_K + k + 1].astype(I32) for k in range(TOP_K)]
    oh = _expert_one_hot(eids, tm)

    @pl.when(i == 0)
    def _():
        padded = jnp.floor((cnt_ref[...] + (MOE_BLOCK - 1)) * (1.0 / MOE_BLOCK)) * MOE_BLOCK
        a = lax.broadcasted_iota(I32, (LANES, LANES), 0)
        b = lax.broadcasted_iota(I32, (LANES, LANES), 1)
        excl = jnp.dot(jnp.broadcast_to(padded, (SUBLANES, LANES)), (a < b).astype(F32),
                       precision=HIGHEST, preferred_element_type=F32)
        pstart[...] = excl[0:1]
        pst_ref[...] = excl[0:1]
        carry[...] = jnp.zeros_like(carry)

    a = lax.broadcasted_iota(I32, (tm, tm), 0)
    b = lax.broadcasted_iota(I32, (tm, tm), 1)
    before = _dot((b < a).astype(BF16), oh.astype(BF16))
    base = before + carry[...] + pstart[...]
    out = jnp.zeros((tm, LANES), F32)
    for k, e in enumerate(eids):
        d = jnp.sum(jnp.where(lane == e, base, 0.0), axis=-1, keepdims=True)
        out = jnp.where(lane == k, d, out)
    dest_ref[...] = out.astype(I32)
    carry[...] += jnp.sum(oh, axis=0, keepdims=True)


def _route(route, cnt, tm=512):
    T = route.shape[0]
    return pl.pallas_call(
        functools.partial(_route_kernel, tm=tm),
        grid=(T // tm,),
        in_specs=[pl.BlockSpec((tm, LANES), lambda i: (i, 0)),
                  pl.BlockSpec((1, LANES), lambda i: (0, 0))],
        out_specs=[pl.BlockSpec((tm, LANES), lambda i: (i, 0)),
                   pl.BlockSpec((1, LANES), lambda i: (0, 0))],
        out_shape=[jax.ShapeDtypeStruct((T, LANES), I32), jax.ShapeDtypeStruct((1, LANES), F32)],
        scratch_shapes=[pltpu.VMEM((1, LANES), F32), pltpu.VMEM((1, LANES), F32)],
        compiler_params=_cparams("arbitrary"),
        name="route",
    )(route, cnt)


ZERO_ROWS = 64
DMA_UNROLL = 4


RING = 3


def _dispatch_kernel(padpos_ref, padcnt_ref, nu_ref, dest_ref, h_ref, xs_ref, zbuf, hbuf, sem, lsem, zsem,
                     *, td, nblk, nsteps):
    i = pl.program_id(0)

    @pl.when(i == 0)
    def _():
        zbuf[...] = jnp.zeros_like(zbuf)

        def per_expert(e, carry):
            off = padpos_ref[e]

            def zero_row(r):
                return pltpu.make_async_copy(zbuf.at[0], xs_ref.at[off + r], zsem)

            def z_issue(r, c):
                zero_row(r).start()
                return c

            def z_drain(r, c):
                zero_row(r).wait()
                return c

            lax.fori_loop(0, padcnt_ref[e], z_issue, 0)
            lax.fori_loop(0, padcnt_ref[e], z_drain, 0)
            return carry

        lax.fori_loop(0, N_EXPERTS, per_expert, 0)

        def zero_piece(j):
            row = pl.multiple_of(j * ZERO_ROWS, ZERO_ROWS)
            return pltpu.make_async_copy(zbuf, xs_ref.at[pl.ds(row, ZERO_ROWS)], zsem)

        per_blk = MOE_BLOCK // ZERO_ROWS

        def t_issue(j, c):
            zero_piece(j).start()
            return c

        def t_drain(j, c):
            zero_piece(j).wait()
            return c

        lax.fori_loop(nu_ref[0] * per_blk, nblk * per_blk, t_issue, 0)
        lax.fori_loop(nu_ref[0] * per_blk, nblk * per_blk, t_drain, 0)

    def load(step):
        s = lax.rem(step, RING)
        return pltpu.make_async_copy(h_ref.at[pl.ds(step * td, td)], hbuf.at[s], lsem.at[s])

    def wait_rows(step):
        s = lax.rem(step, RING)
        for _ in range(TOP_K):
            pltpu.make_async_copy(hbuf.at[s], xs_ref.at[pl.ds(0, td)], sem.at[s]).wait()

    @pl.when(i == 0)
    def _():
        load(0).start()
        if nsteps > 1:
            load(1).start()

    load(i).wait()
    slot = lax.rem(i, RING)

    def issue(r, carry):
        for k in range(TOP_K):
            pltpu.make_async_copy(hbuf.at[slot, r], xs_ref.at[dest_ref[r * TOP_K + k]],
                                  sem.at[slot]).start(priority=k % 2)
        return carry

    lax.fori_loop(0, td, issue, 0, unroll=DMA_UNROLL)

    @pl.when(i > 0)
    def _():
        wait_rows(i - 1)

    @pl.when(i + 2 < nsteps)
    def _():
        load(i + 2).start()

    @pl.when(i == nsteps - 1)
    def _():
        wait_rows(i)


def _dispatch(hf, dest_flat, padpos, padcnt, n_used, n_slots, td=256):
    T = hf.shape[0]
    nsteps = T // td
    grid_spec = pltpu.PrefetchScalarGridSpec(
        num_scalar_prefetch=3,
        grid=(nsteps,),
        in_specs=[pl.BlockSpec((td * TOP_K,), lambda i, *_: (i,), memory_space=pltpu.SMEM),
                  pl.BlockSpec(memory_space=pl.ANY)],
        out_specs=pl.BlockSpec(memory_space=pl.ANY),
        scratch_shapes=[pltpu.VMEM((ZERO_ROWS,) + TOKEN_TILE, F32), pltpu.VMEM((RING, td) + TOKEN_TILE, F32),
                        pltpu.SemaphoreType.DMA((RING,)), pltpu.SemaphoreType.DMA((RING,)),
                        pltpu.SemaphoreType.DMA(())],
    )
    return pl.pallas_call(
        functools.partial(_dispatch_kernel, td=td, nblk=n_slots // MOE_BLOCK, nsteps=nsteps),
        grid_spec=grid_spec,
        out_shape=jax.ShapeDtypeStruct((n_slots,) + TOKEN_TILE, F32),
        compiler_params=_cparams("arbitrary"),
        name="dispatch",
    )(padpos, padcnt, n_used, dest_flat, hf)


def _moe_kernel(be_ref, nu_ref, x_ref, wg_ref, bg_ref, wu_ref, bu_ref, wd_ref, bd_ref, o_ref):
    used = pl.program_id(0) < nu_ref[0]

    @pl.when(jnp.logical_not(used))
    def _():
        o_ref[...] = jnp.zeros_like(o_ref)

    @pl.when(used)
    def _():
        x = _load_token_tiles(x_ref, MOE_BLOCK).astype(BF16)
        gt = jnp.minimum(_dot(x, wg_ref[...]) + bg_ref[...], SWIGLU_LIMIT)
        up = jnp.clip(_dot(x, wu_ref[...]) + bu_ref[...], -SWIGLU_LIMIT, SWIGLU_LIMIT)
        act = (up + 1.0) * (gt * _sigmoid(SWIGLU_ALPHA * gt))
        _store_token_tiles(o_ref, _dot(act.astype(BF16), wd_ref[...]) + bd_ref[...])


def _moe(xs, block_expert, n_used, wg, bg, wu, bu, wd, bd):
    n_slots = xs.shape[0]
    nblk = n_slots // MOE_BLOCK
    blk = lambda i, be, nu: (jnp.minimum(i, nu[0] - 1), 0, 0)
    exp = lambda i, be, nu: (be[jnp.minimum(i, nu[0] - 1)], 0, 0)
    wspec = pl.BlockSpec((None, D_MODEL, D_MODEL), exp)
    bspec = pl.BlockSpec((None, 1, D_MODEL), exp)
    grid_spec = pltpu.PrefetchScalarGridSpec(
        num_scalar_prefetch=2,
        grid=(nblk,),
        in_specs=[pl.BlockSpec((MOE_BLOCK,) + TOKEN_TILE, blk), wspec, bspec, wspec, bspec, wspec, bspec],
        out_specs=pl.BlockSpec((MOE_BLOCK,) + TOKEN_TILE, lambda i, be, nu: (i, 0, 0)),
    )
    return pl.pallas_call(
        _moe_kernel,
        grid_spec=grid_spec,
        out_shape=jax.ShapeDtypeStruct((n_slots,) + TOKEN_TILE, F32),
        compiler_params=_cparams("arbitrary"),
        name="moe",
    )(block_expert, n_used, xs, wg, bg, wu, bu, wd, bd)


def _combine_kernel(dcur_ref, dnext_ref, x1_ref, r_ref, g_ref, ys_ref, o_ref, buf, sem, *, tc, nsteps):
    i = pl.program_id(0)
    slot = lax.rem(i, 2)

    def gather(d_ref, s):
        def body(r, carry):
            for k in range(TOP_K):
                pltpu.make_async_copy(ys_ref.at[d_ref[r * TOP_K + k]], buf.at[s, k, r],
                                      sem.at[s]).start(priority=k % 2)
            return carry

        lax.fori_loop(0, tc, body, 0, unroll=DMA_UNROLL)

    @pl.when(i == 0)
    def _():
        gather(dcur_ref, slot)

    @pl.when(i + 1 < nsteps)
    def _():
        gather(dnext_ref, 1 - slot)

    for k in range(TOP_K):
        pltpu.make_async_copy(ys_ref.at[pl.ds(0, tc)], buf.at[slot, k], sem.at[slot]).wait()
    gates = r_ref[...]
    acc = x1_ref[...]
    for k in range(TOP_K):
        acc += gates[:, k:k + 1] * _load_token_tiles(buf, tc, (slot * TOP_K + k) * tc)
    o_ref[...] = _rms(acc, g_ref[...])


def _combine(x1, route, dest_flat, ys, g_final, tc=256):
    T = x1.shape[0]
    nsteps = T // tc
    return pl.pallas_call(
        functools.partial(_combine_kernel, tc=tc, nsteps=nsteps),
        grid=(nsteps,),
        in_specs=[pl.BlockSpec((tc * TOP_K,), lambda i: (i,), memory_space=pltpu.SMEM),
                  pl.BlockSpec((tc * TOP_K,), lambda i: (jnp.minimum(i + 1, nsteps - 1),),
                               memory_space=pltpu.SMEM),
                  pl.BlockSpec((tc, D_MODEL), lambda i: (i, 0)),
                  pl.BlockSpec((tc, LANES), lambda i: (i, 0)),
                  pl.BlockSpec((1, D_MODEL), lambda i: (0, 0)),
                  pl.BlockSpec(memory_space=pl.ANY)],
        out_specs=pl.BlockSpec((tc, D_MODEL), lambda i: (i, 0)),
        out_shape=jax.ShapeDtypeStruct((T, D_MODEL), F32),
        scratch_shapes=[pltpu.VMEM((2, TOP_K, tc) + TOKEN_TILE, F32), pltpu.SemaphoreType.DMA((2,))],
        compiler_params=_cparams("arbitrary"),
        name="combine",
    )(dest_flat, dest_flat, x1, route, g_final, ys)


def _moe_layer(x1, hf, route, cnt, ew, g_final):
    T = x1.shape[0]
    n_assign = T * TOP_K
    nblk = n_assign // MOE_BLOCK + N_EXPERTS
    n_slots = nblk * MOE_BLOCK
    dest, pst = _route(route, cnt)
    cnt_i = cnt[0, :N_EXPERTS].astype(I32)
    pstart = pst[0, :N_EXPERTS].astype(I32)
    padded = (cnt_i + MOE_BLOCK - 1) // MOE_BLOCK * MOE_BLOCK
    pend = pstart + padded
    block_expert = jnp.minimum(
        jnp.sum(jnp.arange(nblk, dtype=I32)[:, None] * MOE_BLOCK >= pend[None, :], axis=1),
        N_EXPERTS - 1).astype(I32)
    n_used = (pend[-1] // MOE_BLOCK).reshape(1).astype(I32)
    dest_flat = dest[:, :TOP_K].reshape(n_assign)
    xs = _dispatch(hf, dest_flat, pstart + cnt_i, padded - cnt_i, n_used, n_slots)
    ys = _moe(xs, block_expert, n_used, *ew)
    return _combine(x1, route, dest_flat, ys, g_final)


def _trunk(x, p, bb):
    B, L, D = x.shape
    T = B * L
    x2d = x.reshape(T, D)
    v, x1g, x2g, u = _inproj(x2d, L, p["g_mix"], p["w_in"], p["b_in"], p["short_w"], p["short_b"])
    ya = _hyena(v, x1g, x2g, B, L, p["filt"], p["filt_bias"], bb)
    ybp = _s5(u, B, L, p["s5"])
    x1, hf, route, cnt = _mix(x2d, ya, ybp, p["w_glu"], p["b_glu"], p["g_hyena"], p["g_s5"], p["w_out"],
                              p["g_ffn"], p["w_router"], p["b_router"])
    y = _moe_layer(x1, hf, route, cnt, p["experts"], p["g_final"])
    return y.reshape(B, L, D)


def kernel(x_prompt, x_sample, g_mix, w_in, b_in, short_w, short_b, filt_w1, filt_b1, filt_w2, filt_b2, filt_w3, filt_b3, filt_freq, filt_bias, s5_lam_re, s5_lam_im, s5_log_dt, s5_b_re, s5_b_im, s5_c_re, s5_c_im, s5_d, w_glu, b_glu, g_hyena, g_s5, w_out, g_ffn, w_router, b_router, w_gate, b_gate, w_up, b_up, w_down, b_down, g_final):
    assert g_mix.shape[0] == 1, "one encoder layer"
    row = lambda a: a[0][None, :].astype(F32)
    p = dict(
        g_mix=row(g_mix), w_in=w_in[0].astype(BF16), b_in=row(b_in),
        short_w=short_w[0].astype(F32), short_b=row(short_b),
        filt=(filt_w1[0], filt_b1[0], filt_w2[0], filt_b2[0], filt_w3[0], filt_b3[0], filt_freq[0]),
        filt_bias=filt_bias[0].astype(F32),
        s5=_s5_operators(s5_lam_re[0], s5_lam_im[0], s5_log_dt[0], s5_b_re[0], s5_b_im[0],
                         s5_c_re[0], s5_c_im[0], s5_d[0]),
        w_glu=w_glu[0].astype(BF16), b_glu=row(b_glu), g_hyena=row(g_hyena), g_s5=row(g_s5),
        w_out=w_out[0].astype(BF16), g_ffn=row(g_ffn),
        w_router=_split_bf16(jnp.pad(w_router[0].astype(F32), ((0, 0), (0, LANES - N_EXPERTS)))),
        b_router=jnp.pad(b_router[0].astype(F32), (0, LANES - N_EXPERTS))[None, :],
        experts=(w_gate[0].astype(BF16), b_gate[0][:, None, :], w_up[0].astype(BF16),
                 b_up[0][:, None, :], w_down[0].astype(BF16), b_down[0][:, None, :]),
        g_final=g_final[None, :].astype(F32),
    )
    y_prompt = _trunk(x_prompt, p, bb=1)
    y_sample = _trunk(x_sample, p, bb=8)
    return (y_prompt, y_sample)
```

```python
import functools
import math

import numpy as np
import jax
import jax.numpy as jnp
from jax import lax
from jax.experimental import pallas as pl
from jax.experimental.pallas import tpu as pltpu

F32 = jnp.float32
BF16 = jnp.bfloat16
I32 = jnp.int32
HIGHEST = lax.Precision.HIGHEST

LANES = 128
SUBLANES = 8
D_MODEL = 1024
C_HY = 512
C_S5 = 512
D_IN = 3 * C_HY + C_S5
Q_HY = C_HY // LANES
FILTER_BANDS = 16
FILTER_WIDTH = 64
Z_PAD = 128
DECAY_MAX = math.log(1e-2) / 0.3
DECAY_MIN = math.log(1e-2) / 1.5
S5_G, S5_H, S5_P = 32, 16, 64
S5_CHUNK = 16
S5_UW = S5_CHUNK * S5_H
S5_SW = 2 * S5_P
S5_GB = LANES // S5_H
S5_NB = S5_G // S5_GB
S5_KW = S5_CHUNK * LANES
S5_XW = S5_GB * S5_SW
S5_STEP_ROWS = 512
LAMBDA_RE_MAX = -1e-4
N_EXPERTS = 32
TOP_K = 4
SWIGLU_LIMIT = 7.0
SWIGLU_ALPHA = 1.702
MOE_BLOCK = 512
RMS_EPS = 1e-6
VMEM_LIMIT_BYTES = 56 * 1024 * 1024


def _cparams(*sem):
    return pltpu.CompilerParams(dimension_semantics=sem, vmem_limit_bytes=VMEM_LIMIT_BYTES)


def _rms(x, g):
    return x * lax.rsqrt(jnp.mean(x * x, axis=-1, keepdims=True) + RMS_EPS) * g


def _sigmoid(x):
    return 1.0 / (1.0 + jnp.exp(-x))


def _dot(a, b):
    return jnp.dot(a, b, preferred_element_type=F32)


def _split_bf16(w):
    hi = w.astype(BF16)
    return hi, (w - hi.astype(F32)).astype(BF16)


def _lane_block(x, q):
    return x[:, q * LANES:(q + 1) * LANES]


def _as_rows(ref):
    return ref.reshape(math.prod(ref.shape[:-1]), LANES)


TOKEN_TILE = (D_MODEL // LANES, LANES)


def _load_token_tiles(ref, n, row0=0):
    rows = _as_rows(ref)
    nq = TOKEN_TILE[0]
    return jnp.concatenate([rows[pl.ds(row0 * nq + q, n, stride=nq), :] for q in range(nq)], axis=1)


def _store_token_tiles(ref, x):
    rows = _as_rows(ref)
    nq = TOKEN_TILE[0]
    for q in range(nq):
        rows[pl.ds(q, x.shape[0], stride=nq), :] = _lane_block(x, q)


def _inproj_kernel(x_ref, xp_ref, xn_ref, g_ref, w_ref, b_ref, sw_ref, sb_ref,
                   v_ref, x1_ref, x2_ref, u_ref, *, tm, seq):
    i = pl.program_id(0)
    g = g_ref[...]
    rows = jnp.concatenate([x_ref[...], xp_ref[...], xn_ref[...]], axis=0)
    proj = _dot(_rms(rows, g).astype(BF16), w_ref[...]) + b_ref[...]
    u_ref[...] = proj[:tm, 3 * C_HY:]
    z = proj[:tm, :3 * C_HY]
    zh = proj[tm:, :3 * C_HY]
    row0 = i * tm
    has_prev = lax.rem(row0, seq) != 0
    has_next = lax.rem(row0 + tm, seq) != 0
    zp = jnp.where(has_prev, zh[SUBLANES - 1:SUBLANES], 0.0)
    zn = jnp.where(has_next, zh[SUBLANES:SUBLANES + 1], 0.0)
    rid = lax.broadcasted_iota(I32, (tm, 1), 0)
    zm1 = jnp.where(rid == 0, zp, pltpu.roll(z, 1, 0))
    zp1 = jnp.where(rid == tm - 1, zn, pltpu.roll(z, tm - 1, 0))
    sw = sw_ref[...]
    o = zm1 * sw[0:1] + z * sw[1:2] + zp1 * sw[2:3] + sb_ref[...]
    for q in range(Q_HY):
        v_ref[q] = _lane_block(o, q)
        x1_ref[q] = _lane_block(o, Q_HY + q)
        x2_ref[q] = _lane_block(o, 2 * Q_HY + q)


def _inproj(x2d, seq, g_mix, w_in_bf, b_in, short_w, short_b, tm=512):
    T = x2d.shape[0]
    nb8 = T // SUBLANES
    tb = tm // SUBLANES
    hy = jax.ShapeDtypeStruct((Q_HY, T, LANES), F32)
    hy_spec = pl.BlockSpec((Q_HY, tm, LANES), lambda i: (0, i, 0))
    const = lambda i: (0, 0)
    return pl.pallas_call(
        functools.partial(_inproj_kernel, tm=tm, seq=seq),
        grid=(T // tm,),
        in_specs=[
            pl.BlockSpec((tm, D_MODEL), lambda i: (i, 0)),
            pl.BlockSpec((SUBLANES, D_MODEL), lambda i: (jnp.maximum(i * tb - 1, 0), 0)),
            pl.BlockSpec((SUBLANES, D_MODEL), lambda i: (jnp.minimum((i + 1) * tb, nb8 - 1), 0)),
            pl.BlockSpec((1, D_MODEL), const),
            pl.BlockSpec((D_MODEL, D_IN), const),
            pl.BlockSpec((1, D_IN), const),
            pl.BlockSpec((3, 3 * C_HY), const),
            pl.BlockSpec((1, 3 * C_HY), const),
        ],
        out_specs=[hy_spec, hy_spec, hy_spec, pl.BlockSpec((tm, C_S5), lambda i: (i, 0))],
        out_shape=[hy, hy, hy, jax.ShapeDtypeStruct((T, C_S5), F32)],
        compiler_params=_cparams("arbitrary"),
        name="inproj",
    )(x2d, x2d, x2d, g_mix, w_in_bf, b_in, short_w, short_b)


def _filter_kernel(z_ref, w1_ref, b1_ref, w2_ref, b2_ref, w3h_ref, w3l_ref, b3_ref, fr_ref, dl_ref,
                   k_ref, s_ref, *, tr, seq):
    i = pl.program_id(0)
    z = z_ref[...]
    fr = fr_ref[...]
    h = jnp.sin(fr[0:1] * (jnp.dot(z, w1_ref[...], precision=HIGHEST,
                                   preferred_element_type=F32) + b1_ref[...]))
    h = jnp.sin(fr[1:2] * (jnp.dot(h, w2_ref[...], precision=HIGHEST,
                                   preferred_element_type=F32) + b2_ref[...]))
    h_hi = h.astype(BF16)
    h_lo = (h - h_hi.astype(F32)).astype(BF16)
    h = _dot(h_hi, w3h_ref[...]) + _dot(h_lo, w3h_ref[...]) + _dot(h_hi, w3l_ref[...]) + b3_ref[...]
    h = h * jnp.exp(-z[:, 0:1] * dl_ref[...])
    rid = i * tr + lax.broadcasted_iota(I32, (tr, 1), 0)
    h = jnp.where(rid == seq, 0.0, h)
    for q in range(2 * Q_HY):
        k_ref[q] = _lane_block(h, q)

    @pl.when(i == 0)
    def _():
        s_ref[...] = jnp.zeros_like(s_ref)

    s_ref[...] += jnp.sum(jnp.abs(h), axis=0, keepdims=True)


def _filter_taps(seq, w1, b1, w2, b2, w3, b3, freq, tr=512):
    L = seq
    N = 2 * L
    n = jnp.arange(N, dtype=I32)
    pos = jnp.where(n < L, n, jnp.where(n == L, 0, N - n)).astype(F32)
    t = (pos * (1.0 / (L - 1)))[:, None]
    w = 2.0 * math.pi * pos / L
    bands = jnp.linspace(1e-4, FILTER_BANDS - 1, FILTER_BANDS, dtype=F32)
    ang = w[:, None] * bands[None, :]
    zc = jnp.concatenate([t, jnp.cos(ang), -jnp.sin(ang),
                          jnp.zeros((N, Z_PAD - 1 - 2 * FILTER_BANDS), F32)], axis=-1)
    w1p = jnp.pad(w1, ((0, Z_PAD - w1.shape[0]), (0, 0)))
    w3d = w3.reshape(FILTER_WIDTH, 2, 2 * C_HY).transpose(1, 0, 2)
    b3d = b3.reshape(2, 1, 2 * C_HY)
    deltas = jnp.abs(jnp.linspace(DECAY_MIN, DECAY_MAX, C_HY, dtype=F32))
    dl = jnp.tile(deltas, 2)[None, :]
    half = (N // tr) // 2
    const = lambda i: (0, 0)
    taps, sums = pl.pallas_call(
        functools.partial(_filter_kernel, tr=tr, seq=L),
        grid=(N // tr,),
        in_specs=[
            pl.BlockSpec((tr, Z_PAD), lambda i: (i, 0)),
            pl.BlockSpec((Z_PAD, FILTER_WIDTH), const),
            pl.BlockSpec((1, FILTER_WIDTH), const),
            pl.BlockSpec((FILTER_WIDTH, FILTER_WIDTH), const),
            pl.BlockSpec((1, FILTER_WIDTH), const),
            pl.BlockSpec((None, FILTER_WIDTH, 2 * C_HY), lambda i: (i // half, 0, 0)),
            pl.BlockSpec((None, FILTER_WIDTH, 2 * C_HY), lambda i: (i // half, 0, 0)),
            pl.BlockSpec((None, 1, 2 * C_HY), lambda i: (i // half, 0, 0)),
            pl.BlockSpec((2, FILTER_WIDTH), const),
            pl.BlockSpec((1, 2 * C_HY), const),
        ],
        out_specs=[pl.BlockSpec((2 * Q_HY, tr, LANES), lambda i: (0, i, 0)),
                   pl.BlockSpec((1, 2 * C_HY), const)],
        out_shape=[jax.ShapeDtypeStruct((2 * Q_HY, N, LANES), F32),
                   jax.ShapeDtypeStruct((1, 2 * C_HY), F32)],
        compiler_params=_cparams("arbitrary"),
        name="hyena_filter",
    )(zc, w1p, b1[None, :], w2, b2[None, :], *_split_bf16(w3d.astype(F32)), b3d, freq, dl)
    return taps, sums


def _split_n(N):
    n1 = {32768: 128, 4096: 64}.get(N)
    if n1 is None:
        n1 = 1 << (int(math.log2(N)) // 2)
    return n1, N // n1


def _dft_constants(N1, N2):
    N = N1 * N2
    k1 = np.arange(N1)[:, None]
    n1 = np.arange(N1)[None, :]
    ang = 2.0 * np.pi * ((k1 * n1) % N1) / N1
    fa_full = np.concatenate([np.cos(ang), -np.sin(ang)], axis=0)
    fa_half = fa_full[:, :N1 // 2]
    fa_inv = np.concatenate([np.cos(ang), -np.sin(ang)], axis=1)[:N1 // 2] / N
    k2 = np.arange(N2)[:, None]
    n2 = np.arange(N2)[None, :]
    a2 = 2.0 * np.pi * ((k2 * n2) % N2) / N2
    cr = jnp.asarray(np.cos(a2).astype(np.float32))[None]
    ci = jnp.asarray((-np.sin(a2)).astype(np.float32))[None]
    to = lambda a: jnp.asarray(a.astype(np.float32)).astype(BF16)
    kk = lax.broadcasted_iota(I32, (N1, N2), 0)
    nn = lax.broadcasted_iota(I32, (N1, N2), 1)
    ta = (2.0 * math.pi / N) * lax.rem(kk * nn, N).astype(F32)
    tr, ti = jnp.cos(ta), -jnp.sin(ta)
    stack = lambda re, im: jnp.concatenate(
        [jnp.concatenate([re, -im], axis=-1), jnp.concatenate([im, re], axis=-1)], axis=-2).astype(BF16)
    tc_r, tc_i = tr[:, None, :], ti[:, None, :]
    fk = stack(cr * tc_r - ci * tc_i, cr * tc_i + ci * tc_r)
    tr_r, tr_i = tr[:, :, None], ti[:, :, None]
    gk = stack(cr * tr_r - ci * tr_i, -(cr * tr_i + ci * tr_r))
    return dict(fa_full=to(fa_full), fa_half=to(fa_half), fa_inv=to(fa_inv), fk=fk, gk=gk)


U32 = jnp.uint32
_HI16 = 0xFFFF0000
_HALF16 = 0x8000


def _pack_spectrum(re, im):
    rb = lax.bitcast_convert_type(re, U32) + U32(_HALF16)
    ib = lax.bitcast_convert_type(im, U32) + U32(_HALF16)
    return (rb & U32(_HI16)) | (ib >> 16)


def _unpack_spectrum(w):
    return (lax.bitcast_convert_type(w & U32(_HI16), F32), lax.bitcast_convert_type(w << 16, F32))


N2_STEP = 16


def _dft_a_kernel(f_ref, x_ref, o_ref, *, rows):
    nq = x_ref.shape[0]
    x2d = _as_rows(x_ref)
    f = f_ref[...]
    half = f.shape[0] // 2
    for j in range(N2_STEP):
        xs = jnp.concatenate(
            [x2d[pl.ds(q * rows * N2_STEP + j, rows, stride=N2_STEP), :] for q in range(nq)], axis=1)
        r = _dot(f, xs.astype(BF16))
        w = _pack_spectrum(r[:half], r[half:])
        for q in range(nq):
            o_ref[q, j] = _lane_block(w, q)


def _dft_a(f, x):
    Q, Bt, K, N2, _ = x.shape
    R = f.shape[0] // 2
    return pl.pallas_call(
        functools.partial(_dft_a_kernel, rows=K),
        grid=(Bt, N2 // N2_STEP, Q // Q_HY),
        in_specs=[pl.BlockSpec((2 * R, K), lambda b, j, w: (0, 0)),
                  pl.BlockSpec((Q_HY, None, K, N2_STEP, LANES), lambda b, j, w: (w, b, 0, j, 0))],
        out_specs=pl.BlockSpec((Q_HY, None, N2_STEP, R, LANES), lambda b, j, w: (w, b, j, 0, 0)),
        out_shape=jax.ShapeDtypeStruct((Q, Bt, N2, R, LANES), U32),
        compiler_params=_cparams("arbitrary", "arbitrary", "arbitrary"),
        name="dft_a",
    )(f, x)


def _dft_a_inv_kernel(f_ref, z_ref, gate_ref, v_ref, bias_ref, *rest, rows, with_next):
    if with_next:
        fn_ref, o_ref, an_ref = rest
    else:
        (o_ref,) = rest
    g2d = _as_rows(gate_ref)
    v2d = _as_rows(v_ref)
    o2d = _as_rows(o_ref)
    f = f_ref[...]
    bias = bias_ref[...]
    for j in range(N2_STEP):
        zr, zi = _unpack_spectrum(jnp.concatenate([z_ref[q, j] for q in range(Q_HY)], axis=1))
        y = _dot(f, jnp.concatenate([zr, zi], axis=0).astype(BF16))
        outs = []
        for q in range(Q_HY):
            sl = pl.ds(q * rows * N2_STEP + j, rows, stride=N2_STEP)
            outs.append(g2d[sl, :] * (_lane_block(y, q) + v2d[sl, :] * _lane_block(bias, q)))
            o2d[sl, :] = outs[-1]
        if with_next:
            r = _dot(fn_ref[...], jnp.concatenate(outs, axis=1).astype(BF16))
            half = r.shape[0] // 2
            w = _pack_spectrum(r[:half], r[half:])
            for q in range(Q_HY):
                an_ref[q, j] = _lane_block(w, q)


def _dft_a_inv(f, z, gate, v, bias_row, f_next=None):
    Q, Bt, N2, N1, _ = z.shape
    K = f.shape[0]
    nat = pl.BlockSpec((Q, None, K, N2_STEP, LANES), lambda b, j: (0, b, 0, j, 0))
    spec = pl.BlockSpec((Q, None, N2_STEP, N1, LANES), lambda b, j: (0, b, j, 0, 0))
    with_next = f_next is not None
    in_specs = [pl.BlockSpec((K, 2 * N1), lambda b, j: (0, 0)), spec, nat, nat,
                pl.BlockSpec((1, C_HY), lambda b, j: (0, 0))]
    out_specs, out_shape = [nat], [jax.ShapeDtypeStruct((Q, Bt, K, N2, LANES), F32)]
    args = [f, z, gate, v, bias_row]
    if with_next:
        in_specs.append(pl.BlockSpec((2 * N1, K), lambda b, j: (0, 0)))
        out_specs.append(spec)
        out_shape.append(jax.ShapeDtypeStruct(z.shape, U32))
        args.append(f_next)
    return pl.pallas_call(
        functools.partial(_dft_a_inv_kernel, rows=K, with_next=with_next),
        grid=(Bt, N2 // N2_STEP),
        in_specs=in_specs,
        out_specs=out_specs,
        out_shape=out_shape,
        compiler_params=_cparams("arbitrary", "arbitrary"),
        name="dft_a_inv",
    )(*args)


C_STEP_Q = 2
K1_STEP = SUBLANES


def _dft_c_kernel(a_ref, k_ref, f_ref, g_ref, z_ref, *, bb, n2):
    per = n2 * K1_STEP
    a2d = _as_rows(a_ref)
    z2d = _as_rows(z_ref)
    cols = [(b, q) for b in range(bb) for q in range(C_STEP_Q)]

    def rows(b, q, kk):
        return pl.ds((q * bb + b) * per + kk, n2, stride=K1_STEP)

    for kk in range(K1_STEP):
        kr, ki = _unpack_spectrum(jnp.tile(k_ref[kk], (1, bb)))
        ar, ai = _unpack_spectrum(jnp.concatenate([a2d[rows(b, q, kk), :] for b, q in cols], axis=1))
        x = _dot(f_ref[kk], jnp.concatenate([ar, ai], axis=0).astype(BF16))
        xr = x[:n2]
        xi = x[n2:]
        yr = xr * kr - xi * ki
        yi = xr * ki + xi * kr
        zz = _dot(g_ref[kk], jnp.concatenate([yr, yi], axis=0).astype(BF16))
        w = _pack_spectrum(zz[:n2], zz[n2:])
        for c, (b, q) in enumerate(cols):
            z2d[rows(b, q, kk), :] = _lane_block(w, c)


def _dft_c(a5, khat, order, fk, gk, bb):
    Q, Bt, N2, N1, _ = a5.shape
    cw = C_STEP_Q * LANES
    ablk = pl.BlockSpec((C_STEP_Q, bb, N2, K1_STEP, LANES), lambda g, c, b: (c, b, 0, g, 0))
    mats = pl.BlockSpec((K1_STEP, 2 * N2, 2 * N2), lambda g, c, b: (g, 0, 0))
    return pl.pallas_call(
        functools.partial(_dft_c_kernel, bb=bb, n2=N2),
        grid=(N1 // K1_STEP, Q // C_STEP_Q, Bt // bb),
        in_specs=[ablk,
                  pl.BlockSpec((K1_STEP, N2, cw), lambda g, c, b: (g, 0, order * (C_HY // cw) + c)),
                  mats, mats],
        out_specs=ablk,
        out_shape=jax.ShapeDtypeStruct(a5.shape, U32),
        compiler_params=_cparams("arbitrary", "arbitrary", "arbitrary"),
        name="dft_c",
    )(a5, khat, fk, gk)


def _dft_c_filter_kernel(a_ref, f_ref, s_ref, o_ref, *, n2):
    per = n2 * K1_STEP
    a2d = _as_rows(a_ref)
    sc = 1.0 / (s_ref[...] + 1e-6)
    for kk in range(K1_STEP):
        ar, ai = _unpack_spectrum(jnp.concatenate(
            [a2d[pl.ds(q * per + kk, n2, stride=K1_STEP), :] for q in range(C_STEP_Q)], axis=1))
        x = _dot(f_ref[kk], jnp.concatenate([ar, ai], axis=0).astype(BF16))
        o_ref[kk] = _pack_spectrum(x[:n2] * sc, x[n2:] * sc)


def _dft_c_filter(a4, fk, sums):
    Q, N2, N1, _ = a4.shape
    cw = C_STEP_Q * LANES
    return pl.pallas_call(
        functools.partial(_dft_c_filter_kernel, n2=N2),
        grid=(N1 // K1_STEP, Q // C_STEP_Q),
        in_specs=[pl.BlockSpec((C_STEP_Q, N2, K1_STEP, LANES), lambda g, c: (c, 0, g, 0)),
                  pl.BlockSpec((K1_STEP, 2 * N2, 2 * N2), lambda g, c: (g, 0, 0)),
                  pl.BlockSpec((1, cw), lambda g, c: (0, c))],
        out_specs=pl.BlockSpec((K1_STEP, N2, cw), lambda g, c: (g, 0, c)),
        out_shape=jax.ShapeDtypeStruct((N1, N2, Q * LANES), U32),
        compiler_params=_cparams("arbitrary", "arbitrary"),
        name="dft_c_filter",
    )(a4, fk, sums)


def _hyena(v, x1, x2, B, seq, filt, filt_bias, bb):
    L = seq
    N = 2 * L
    N1, N2 = _split_n(N)
    cst = _dft_constants(N1, N2)
    taps, sums = _filter_taps(L, *filt)
    ka = _dft_a(cst["fa_full"], taps.reshape(2 * Q_HY, 1, N1, N2, LANES))
    khat = _dft_c_filter(ka.reshape(2 * Q_HY, N2, N1, LANES), cst["fk"], sums)
    nat = lambda a: a.reshape(Q_HY, B, N1 // 2, N2, LANES)
    cur = nat(v)
    a = _dft_a(cst["fa_half"], cur)
    for o, gate in enumerate((x1, x2)):
        z = _dft_c(a, khat, o, cst["fk"], cst["gk"], bb)
        if o == 0:
            cur, a = _dft_a_inv(cst["fa_inv"], z, nat(gate), cur, filt_bias[o][None, :], cst["fa_half"])
        else:
            (cur,) = _dft_a_inv(cst["fa_inv"], z, nat(gate), cur, filt_bias[o][None, :])
    return cur.reshape(Q_HY, B * L, LANES)


def _s5_operators(lam_re, lam_im, log_dt, b_re, b_im, c_re, c_im, d_skip):
    Tc, G, H, P = S5_CHUNK, S5_G, S5_H, S5_P
    lam = jnp.minimum(lam_re.astype(F32), LAMBDA_RE_MAX) + 1j * lam_im.astype(F32)
    dt = jnp.exp(log_dt.astype(F32))[..., None]
    lam_dt = lam * dt
    lam_bar = jnp.exp(lam_dt)
    b_bar = ((lam_bar - 1.0) / lam)[..., None] * (b_re.astype(F32) + 1j * b_im.astype(F32))
    c = c_re.astype(F32) + 1j * c_im.astype(F32)
    tau = jnp.arange(Tc + 1, dtype=F32)
    pw = jnp.exp(lam_dt[None] * tau[:, None, None, None])
    kk = jnp.einsum('dghp,tdgp,dgpk->dtghk', c, pw[:Tc], b_bar).real
    i = jnp.arange(Tc)
    lag = i[:, None] - i[None, :]
    kf = jnp.where((lag >= 0)[:, :, None, None, None], kk[0][jnp.clip(lag, 0, Tc - 1)], 0.0)
    kb = jnp.where((lag <= 0)[:, :, None, None, None], kk[1][jnp.clip(-lag, 0, Tc - 1)], 0.0)
    m = kf + kb
    eye = (lag == 0)[:, :, None, None, None] * jnp.eye(H, dtype=F32)[None, None, None]
    m = m + eye * d_skip.astype(F32).reshape(G, H)[None, None, :, :, None]
    clf = c[0][None] * pw[1:Tc + 1, 0][:, :, None, :]
    clb = c[1][None] * pw[Tc - i, 1][:, :, None, :]
    sf = pw[Tc - 1 - i, 0][:, :, :, None] * b_bar[0][None]
    sb = pw[i, 1][:, :, :, None] * b_bar[1][None]
    w_intra = m.transpose(2, 1, 4, 0, 3).reshape(G, Tc * H, Tc * H)
    st2y = lambda cl: jnp.concatenate([cl.real, -cl.imag], axis=-1).transpose(1, 3, 0, 2) \
        .reshape(G, 2 * P, Tc * H)
    u2s = lambda s: jnp.concatenate([s.real, s.imag], axis=2).transpose(1, 0, 3, 2) \
        .reshape(G, Tc * H, 2 * P)
    w_state = jnp.concatenate([u2s(sf), u2s(sb)], axis=-1)
    pair = lambda a: a.reshape(2, G // 2, 2 * P)
    lam16 = jnp.concatenate([pair(pw[Tc].real), pair(pw[Tc].imag)], axis=-1).reshape(2, G * 2 * P)
    col = np.arange(S5_KW)
    spread = (np.arange(S5_UW)[:, None] == (col // LANES * S5_H + col % S5_H)[None, :])
    return dict(w_intra=w_intra.astype(BF16), w_xf=st2y(clf).astype(BF16), w_xb=st2y(clb).astype(BF16),
                w_state=w_state.astype(BF16), lam16=lam16,
                spread=jnp.asarray(spread.astype(np.float32)).astype(BF16))


def _s5_rows(i, cl, B, ncc):
    return pl.ds(S5_CHUNK * cl + i, B, stride=S5_CHUNK * ncc)


def _s5_chunk_inputs(u2d, B, ncc):
    cols = []
    for i in range(S5_CHUNK):
        if B == 1:
            cols.append(u2d[pl.ds(i, ncc, stride=S5_CHUNK), :])
        else:
            cols.append(jnp.concatenate([u2d[_s5_rows(i, cl, B, ncc), :] for cl in range(ncc)], axis=0))
    return jnp.concatenate(cols, axis=1).astype(BF16)


def _s5_state_lane(a, part):
    return (a // 2) * 2 * S5_SW + part * S5_SW + (a % 2) * S5_P


def _s5_state_kernel(u_ref, w_ref, sf_ref, sb_ref, wblk, *, B, ncc):
    @pl.when(pl.program_id(1) == 0)
    def _():
        wblk[...] = jnp.zeros_like(wblk)
        for a in range(S5_GB):
            for j in range(S5_CHUNK):
                r0 = j * LANES + a * S5_H
                for d in range(2):
                    for part in range(2):
                        c0 = d * S5_XW + _s5_state_lane(a, part)
                        s0 = d * S5_SW + part * S5_P
                        wblk[r0:r0 + S5_H, c0:c0 + S5_P] = w_ref[a, j * S5_H:(j + 1) * S5_H, s0:s0 + S5_P]

    lhs = _s5_chunk_inputs(_as_rows(u_ref), B, ncc)
    r = _dot(lhs, wblk[...])
    sf_ref[...] = r[:, :S5_XW]
    sb_ref[...] = r[:, S5_XW:]


def _s5_state(u3, w_state, ncc):
    B, L, _ = u3.shape
    nch = L // S5_CHUNK
    out = jax.ShapeDtypeStruct((nch * B, S5_NB * S5_XW), F32)
    return pl.pallas_call(
        functools.partial(_s5_state_kernel, B=B, ncc=ncc),
        grid=(S5_NB, nch // ncc),
        in_specs=[pl.BlockSpec((B, S5_CHUNK * ncc, LANES), lambda q, t: (0, t, q)),
                  pl.BlockSpec((S5_GB, S5_UW, 2 * S5_SW), lambda q, t: (q, 0, 0))],
        out_specs=[pl.BlockSpec((ncc * B, S5_XW), lambda q, t: (t, q))] * 2,
        out_shape=[out, out],
        scratch_shapes=[pltpu.VMEM((S5_KW, 2 * S5_XW), BF16)],
        compiler_params=_cparams("arbitrary", "arbitrary"),
        name="s5_state",
    )(u3, w_state)


def _s5_scan_kernel(sf_ref, sb_ref, lam_ref, xf_ref, xb_ref, *, nch, rows):
    lb = sf_ref.shape[1]
    vr = max(rows, SUBLANES)
    lam_f, lam_b = lam_ref[0], lam_ref[1]

    def advance(x, lam, inc):
        out = []
        for u in range(0, lb, 2 * S5_SW):
            re, im = x[:, u:u + S5_SW], x[:, u + S5_SW:u + 2 * S5_SW]
            lr, li = lam[:, u:u + S5_SW], lam[:, u + S5_SW:u + 2 * S5_SW]
            out += [lr * re - li * im, lr * im + li * re]
        return jnp.concatenate(out, axis=1) + inc

    def body(c, carry):
        xf, xb = carry
        rf = pl.ds(pl.multiple_of(c * rows, rows), rows)
        xf_ref[rf, :] = xf[:rows]
        xf = advance(xf, lam_f, sf_ref[rf, :])
        rb = pl.ds(pl.multiple_of((nch - 1 - c) * rows, rows), rows)
        xb_ref[rb, :] = xb[:rows]
        xb = advance(xb, lam_b, sb_ref[rb, :])
        return xf, xb

    zero = jnp.zeros((vr, lb), F32)
    lax.fori_loop(0, nch, body, (zero, zero))


def _s5_scan(sf, sb, lam16, nch, rows, lb):
    R, lanes = sf.shape
    blk = pl.BlockSpec((R, lb), lambda j: (0, j))
    out = jax.ShapeDtypeStruct((R, lanes), F32)
    return pl.pallas_call(
        functools.partial(_s5_scan_kernel, nch=nch, rows=rows),
        grid=(lanes // lb,),
        in_specs=[blk, blk, pl.BlockSpec((2, 1, lb), lambda j: (0, 0, j))],
        out_specs=[blk, blk],
        out_shape=[out, out],
        compiler_params=_cparams("arbitrary"),
        name="s5_scan",
    )(sf, sb, lam16[:, None, :])


def _s5_out_kernel(u_ref, xf_ref, xb_ref, wm_ref, wf_ref, wb_ref, e_ref, y_ref, wm_blk, wf_blk, wb_blk,
                   *, B, ncc):
    @pl.when(pl.program_id(1) == 0)
    def _():
        lane = lax.broadcasted_iota(I32, (1, S5_KW), 1)
        slot = (lane // S5_H) % S5_GB
        e = e_ref[...]
        for a in range(S5_GB):
            own = slot == a
            ex = jnp.where(own, _dot(wm_ref[a], e), 0.0).astype(BF16)
            for j in range(S5_CHUNK):
                r0 = j * LANES + a * S5_H
                wm_blk[r0:r0 + S5_H, :] = ex[j * S5_H:(j + 1) * S5_H, :]
            for src, dst in ((wf_ref, wf_blk), (wb_ref, wb_blk)):
                ex = jnp.where(own, _dot(src[a], e), 0.0).astype(BF16)
                for part in range(2):
                    r0 = _s5_state_lane(a, part)
                    dst[r0:r0 + S5_P, :] = ex[part * S5_P:(part + 1) * S5_P, :]

    lhs = _s5_chunk_inputs(_as_rows(u_ref), B, ncc)
    acc = _dot(lhs, wm_blk[...])
    acc += _dot(xf_ref[...].astype(BF16), wf_blk[...])
    acc += _dot(xb_ref[...].astype(BF16), wb_blk[...])
    y2d = _as_rows(y_ref)
    for i in range(S5_CHUNK):
        piece = _lane_block(acc, i)
        if B == 1:
            y2d[pl.ds(i, ncc, stride=S5_CHUNK), :] = piece
        else:
            for cl in range(ncc):
                y2d[_s5_rows(i, cl, B, ncc), :] = piece[cl * B:(cl + 1) * B]


def _s5_out(u3, xf, xb, ops, ncc):
    B, L, _ = u3.shape
    nch = L // S5_CHUNK
    tok = pl.BlockSpec((B, S5_CHUNK * ncc, LANES), lambda q, t: (0, t, q))
    st = pl.BlockSpec((ncc * B, S5_XW), lambda q, t: (t, q))
    return pl.pallas_call(
        functools.partial(_s5_out_kernel, B=B, ncc=ncc),
        grid=(S5_NB, nch // ncc),
        in_specs=[tok, st, st,
                  pl.BlockSpec((S5_GB, S5_UW, S5_UW), lambda q, t: (q, 0, 0)),
                  pl.BlockSpec((S5_GB, S5_SW, S5_UW), lambda q, t: (q, 0, 0)),
                  pl.BlockSpec((S5_GB, S5_SW, S5_UW), lambda q, t: (q, 0, 0)),
                  pl.BlockSpec((S5_UW, S5_KW), lambda q, t: (0, 0))],
        out_specs=tok,
        out_shape=jax.ShapeDtypeStruct(u3.shape, F32),
        scratch_shapes=[pltpu.VMEM((S5_KW, S5_KW), BF16), pltpu.VMEM((S5_XW, S5_KW), BF16),
                        pltpu.VMEM((S5_XW, S5_KW), BF16)],
        compiler_params=_cparams("arbitrary", "arbitrary"),
        name="s5_out",
    )(u3, xf, xb, ops["w_intra"], ops["w_xf"], ops["w_xb"], ops["spread"])


def _s5(u, B, seq, ops):
    nch = seq // S5_CHUNK
    u3 = u.reshape(B, seq, C_S5)
    ncc = min(nch, S5_STEP_ROWS // B)
    sf, sb = _s5_state(u3, ops["w_state"], ncc)
    lb = 1024 if B == 1 else 256
    xf, xb = _s5_scan(sf, sb, ops["lam16"], nch, B, lb)
    return _s5_out(u3, xf, xb, ops, ncc).reshape(B * seq, C_S5)


def _mix_kernel(x_ref, ya_ref, yb_ref, wglu_ref, bglu_ref, gh_ref, gs_ref, wout_ref, gffn_ref,
                wrh_ref, wrl_ref, br_ref, x1_ref, hf_ref, route_ref, cnt_ref, *, tm):
    g = jax.nn.gelu(yb_ref[...])
    yb = g * _sigmoid(_dot(g.astype(BF16), wglu_ref[...]) + bglu_ref[...])
    ya = jnp.concatenate([ya_ref[q] for q in range(Q_HY)], axis=1)
    na = _rms(ya, gh_ref[...]).astype(BF16)
    nb = _rms(yb, gs_ref[...]).astype(BF16)
    mixed = _dot(na, wout_ref[:C_HY, :]) + _dot(nb, wout_ref[C_HY:, :])
    x1 = x_ref[...] + mixed
    x1_ref[...] = x1
    hf = _rms(x1, gffn_ref[...])
    _store_token_tiles(hf_ref, hf)
    hf_hi = hf.astype(BF16)
    hf_lo = (hf - hf_hi.astype(F32)).astype(BF16)
    logits = (_dot(hf_hi, wrh_ref[...]) + _dot(hf_lo, wrh_ref[...]) + _dot(hf_hi, wrl_ref[...])
              + br_ref[...])
    lane = lax.broadcasted_iota(I32, (tm, LANES), 1)
    neg = jnp.float32(-jnp.inf)
    l = jnp.where(lane < N_EXPERTS, logits, neg)
    vals, idxs = [], []
    for _ in range(TOP_K):
        m = jnp.max(l, axis=-1, keepdims=True)
        idx = jnp.min(jnp.where(l == m, lane, LANES), axis=-1, keepdims=True)
        vals.append(m)
        idxs.append(idx)
        l = jnp.where(lane == idx, neg, l)
    es = [jnp.exp(v - vals[0]) for v in vals]
    den = es[0] + es[1] + es[2] + es[3]
    route = jnp.zeros((tm, LANES), F32)
    for k in range(TOP_K):
        route = jnp.where(lane == k, es[k] / den, route)
        route = jnp.where(lane == TOP_K + k, idxs[k].astype(F32), route)
    route_ref[...] = route

    @pl.when(pl.program_id(0) == 0)
    def _():
        cnt_ref[...] = jnp.zeros_like(cnt_ref)

    cnt_ref[...] += jnp.sum(_expert_one_hot(idxs, tm), axis=0, keepdims=True)


def _mix(x2d, ya4, ybp, w_glu_bf, b_glu, g_hyena, g_s5, w_out_bf, g_ffn, w_router_p, b_router_p, tm=512):
    T = x2d.shape[0]
    const = lambda i: (0, 0)
    row = lambda w: pl.BlockSpec((tm, w), lambda i: (i, 0))
    return pl.pallas_call(
        functools.partial(_mix_kernel, tm=tm),
        grid=(T // tm,),
        in_specs=[row(D_MODEL), pl.BlockSpec((Q_HY, tm, LANES), lambda i: (0, i, 0)), row(C_S5),
                  pl.BlockSpec((C_S5, C_S5), const), pl.BlockSpec((1, C_S5), const),
                  pl.BlockSpec((1, C_HY), const), pl.BlockSpec((1, C_S5), const),
                  pl.BlockSpec((D_MODEL, D_MODEL), const), pl.BlockSpec((1, D_MODEL), const),
                  pl.BlockSpec((D_MODEL, LANES), const), pl.BlockSpec((D_MODEL, LANES), const),
                  pl.BlockSpec((1, LANES), const)],
        out_specs=[row(D_MODEL), pl.BlockSpec((tm,) + TOKEN_TILE, lambda i: (i, 0, 0)), row(LANES),
                   pl.BlockSpec((1, LANES), const)],
        out_shape=[jax.ShapeDtypeStruct((T, D_MODEL), F32), jax.ShapeDtypeStruct((T,) + TOKEN_TILE, F32),
                   jax.ShapeDtypeStruct((T, LANES), F32), jax.ShapeDtypeStruct((1, LANES), F32)],
        compiler_params=_cparams("arbitrary"),
        name="mix",
    )(x2d, ya4, ybp, w_glu_bf, b_glu, g_hyena, g_s5, w_out_bf, g_ffn, w_router_p[0], w_router_p[1],
      b_router_p)


def _expert_one_hot(eids, tm):
    lane = lax.broadcasted_iota(I32, (tm, LANES), 1)
    oh = jnp.zeros((tm, LANES), F32)
    for e in eids:
        oh += (lane == e).astype(F32)
    return oh


def _route_kernel(r_ref, cnt_ref, dest_ref, pst_ref, carry, pstart, *, tm):
    i = pl.program_id(0)
    lane = lax.broadcasted_iota(I32, (tm, LANES), 1)
    r = r_ref[...]
    eids = [r[:, TOP_K + k:TOP_K + k + 1].astype(I32) for k in range(TOP_K)]
    oh = _expert_one_hot(eids, tm)

    @pl.when(i == 0)
    def _():
        padded = jnp.floor((cnt_ref[...] + (MOE_BLOCK - 1)) * (1.0 / MOE_BLOCK)) * MOE_BLOCK
        a = lax.broadcasted_iota(I32, (LANES, LANES), 0)
        b = lax.broadcasted_iota(I32, (LANES, LANES), 1)
        excl = jnp.dot(jnp.broadcast_to(padded, (SUBLANES, LANES)), (a < b).astype(F32),
                       precision=HIGHEST, preferred_element_type=F32)
        pstart[...] = excl[0:1]
        pst_ref[...] = excl[0:1]
        carry[...] = jnp.zeros_like(carry)

    a = lax.broadcasted_iota(I32, (tm, tm), 0)
    b = lax.broadcasted_iota(I32, (tm, tm), 1)
    before = _dot((b < a).astype(BF16), oh.astype(BF16))
    base = before + carry[...] + pstart[...]
    out = jnp.zeros((tm, LANES), F32)
    for k, e in enumerate(eids):
        d = jnp.sum(jnp.where(lane == e, base, 0.0), axis=-1, keepdims=True)
        out = jnp.where(lane == k, d, out)
    dest_ref[...] = out.astype(I32)
    carry[...] += jnp.sum(oh, axis=0, keepdims=True)


def _route(route, cnt, tm=512):
    T = route.shape[0]
    return pl.pallas_call(
        functools.partial(_route_kernel, tm=tm),
        grid=(T // tm,),
        in_specs=[pl.BlockSpec((tm, LANES), lambda i: (i, 0)),
                  pl.BlockSpec((1, LANES), lambda i: (0, 0))],
        out_specs=[pl.BlockSpec((tm, LANES), lambda i: (i, 0)),
                   pl.BlockSpec((1, LANES), lambda i: (0, 0))],
        out_shape=[jax.ShapeDtypeStruct((T, LANES), I32), jax.ShapeDtypeStruct((1, LANES), F32)],
        scratch_shapes=[pltpu.VMEM((1, LANES), F32), pltpu.VMEM((1, LANES), F32)],
        compiler_params=_cparams("arbitrary"),
        name="route",
    )(route, cnt)


ZERO_ROWS = 64
DMA_UNROLL = 4


RING = 3


def _dispatch_kernel(padpos_ref, padcnt_ref, nu_ref, dest_ref, h_ref, xs_ref, zbuf, hbuf, sem, lsem, zsem,
                     *, td, nblk, nsteps):
    i = pl.program_id(0)

    @pl.when(i == 0)
    def _():
        zbuf[...] = jnp.zeros_like(zbuf)

        def per_expert(e, carry):
            off = padpos_ref[e]

            def zero_row(r):
                return pltpu.make_async_copy(zbuf.at[0], xs_ref.at[off + r], zsem)

            def z_issue(r, c):
                zero_row(r).start()
                return c

            def z_drain(r, c):
                zero_row(r).wait()
                return c

            lax.fori_loop(0, padcnt_ref[e], z_issue, 0)
            lax.fori_loop(0, padcnt_ref[e], z_drain, 0)
            return carry

        lax.fori_loop(0, N_EXPERTS, per_expert, 0)

        def zero_piece(j):
            row = pl.multiple_of(j * ZERO_ROWS, ZERO_ROWS)
            return pltpu.make_async_copy(zbuf, xs_ref.at[pl.ds(row, ZERO_ROWS)], zsem)

        per_blk = MOE_BLOCK // ZERO_ROWS

        def t_issue(j, c):
            zero_piece(j).start()
            return c

        def t_drain(j, c):
            zero_piece(j).wait()
            return c

        lax.fori_loop(nu_ref[0] * per_blk, nblk * per_blk, t_issue, 0)
        lax.fori_loop(nu_ref[0] * per_blk, nblk * per_blk, t_drain, 0)

    def load(step):
        s = lax.rem(step, RING)
        return pltpu.make_async_copy(h_ref.at[pl.ds(step * td, td)], hbuf.at[s], lsem.at[s])

    def wait_rows(step):
        s = lax.rem(step, RING)
        for _ in range(TOP_K):
            pltpu.make_async_copy(hbuf.at[s], xs_ref.at[pl.ds(0, td)], sem.at[s]).wait()

    @pl.when(i == 0)
    def _():
        load(0).start()
        if nsteps > 1:
            load(1).start()

    load(i).wait()
    slot = lax.rem(i, RING)

    def issue(r, carry):
        for k in range(TOP_K):
            pltpu.make_async_copy(hbuf.at[slot, r], xs_ref.at[dest_ref[r * TOP_K + k]],
                                  sem.at[slot]).start(priority=k % 2)
        return carry

    lax.fori_loop(0, td, issue, 0, unroll=DMA_UNROLL)

    @pl.when(i > 0)
    def _():
        wait_rows(i - 1)

    @pl.when(i + 2 < nsteps)
    def _():
        load(i + 2).start()

    @pl.when(i == nsteps - 1)
    def _():
        wait_rows(i)


def _dispatch(hf, dest_flat, padpos, padcnt, n_used, n_slots, td=512):
    T = hf.shape[0]
    nsteps = T // td
    grid_spec = pltpu.PrefetchScalarGridSpec(
        num_scalar_prefetch=3,
        grid=(nsteps,),
        in_specs=[pl.BlockSpec((td * TOP_K,), lambda i, *_: (i,), memory_space=pltpu.SMEM),
                  pl.BlockSpec(memory_space=pl.ANY)],
        out_specs=pl.BlockSpec(memory_space=pl.ANY),
        scratch_shapes=[pltpu.VMEM((ZERO_ROWS,) + TOKEN_TILE, F32), pltpu.VMEM((RING, td) + TOKEN_TILE, F32),
                        pltpu.SemaphoreType.DMA((RING,)), pltpu.SemaphoreType.DMA((RING,)),
                        pltpu.SemaphoreType.DMA(())],
    )
    return pl.pallas_call(
        functools.partial(_dispatch_kernel, td=td, nblk=n_slots // MOE_BLOCK, nsteps=nsteps),
        grid_spec=grid_spec,
        out_shape=jax.ShapeDtypeStruct((n_slots,) + TOKEN_TILE, F32),
        compiler_params=_cparams("arbitrary"),
        name="dispatch",
    )(padpos, padcnt, n_used, dest_flat, hf)


def _moe_kernel(be_ref, nu_ref, x_ref, wg_ref, bg_ref, wu_ref, bu_ref, wd_ref, bd_ref, o_ref):
    used = pl.program_id(0) < nu_ref[0]

    @pl.when(jnp.logical_not(used))
    def _():
        o_ref[...] = jnp.zeros_like(o_ref)

    @pl.when(used)
    def _():
        x = _load_token_tiles(x_ref, MOE_BLOCK).astype(BF16)
        gt = jnp.minimum(_dot(x, wg_ref[...]) + bg_ref[...], SWIGLU_LIMIT)
        up = jnp.clip(_dot(x, wu_ref[...]) + bu_ref[...], -SWIGLU_LIMIT, SWIGLU_LIMIT)
        act = (up + 1.0) * (gt * _sigmoid(SWIGLU_ALPHA * gt))
        _store_token_tiles(o_ref, _dot(act.astype(BF16), wd_ref[...]) + bd_ref[...])


def _moe(xs, block_expert, n_used, wg, bg, wu, bu, wd, bd):
    n_slots = xs.shape[0]
    nblk = n_slots // MOE_BLOCK
    blk = lambda i, be, nu: (jnp.minimum(i, nu[0] - 1), 0, 0)
    exp = lambda i, be, nu: (be[jnp.minimum(i, nu[0] - 1)], 0, 0)
    wspec = pl.BlockSpec((None, D_MODEL, D_MODEL), exp)
    bspec = pl.BlockSpec((None, 1, D_MODEL), exp)
    grid_spec = pltpu.PrefetchScalarGridSpec(
        num_scalar_prefetch=2,
        grid=(nblk,),
        in_specs=[pl.BlockSpec((MOE_BLOCK,) + TOKEN_TILE, blk), wspec, bspec, wspec, bspec, wspec, bspec],
        out_specs=pl.BlockSpec((MOE_BLOCK,) + TOKEN_TILE, lambda i, be, nu: (i, 0, 0)),
    )
    return pl.pallas_call(
        _moe_kernel,
        grid_spec=grid_spec,
        out_shape=jax.ShapeDtypeStruct((n_slots,) + TOKEN_TILE, F32),
        compiler_params=_cparams("arbitrary"),
        name="moe",
    )(block_expert, n_used, xs, wg, bg, wu, bu, wd, bd)


def _combine_kernel(dcur_ref, dnext_ref, x1_ref, r_ref, g_ref, ys_ref, o_ref, buf, sem, *, tc, nsteps):
    i = pl.program_id(0)
    slot = lax.rem(i, 2)

    def gather(d_ref, s):
        def body(r, carry):
            for k in range(TOP_K):
                pltpu.make_async_copy(ys_ref.at[d_ref[r * TOP_K + k]], buf.at[s, k, r],
                                      sem.at[s]).start(priority=k % 2)
            return carry

        lax.fori_loop(0, tc, body, 0, unroll=DMA_UNROLL)

    @pl.when(i == 0)
    def _():
        gather(dcur_ref, slot)

    @pl.when(i + 1 < nsteps)
    def _():
        gather(dnext_ref, 1 - slot)

    for k in range(TOP_K):
        pltpu.make_async_copy(ys_ref.at[pl.ds(0, tc)], buf.at[slot, k], sem.at[slot]).wait()
    gates = r_ref[...]
    acc = x1_ref[...]
    for k in range(TOP_K):
        acc += gates[:, k:k + 1] * _load_token_tiles(buf, tc, (slot * TOP_K + k) * tc)
    o_ref[...] = _rms(acc, g_ref[...])


def _combine(x1, route, dest_flat, ys, g_final, tc=512):
    T = x1.shape[0]
    nsteps = T // tc
    return pl.pallas_call(
        functools.partial(_combine_kernel, tc=tc, nsteps=nsteps),
        grid=(nsteps,),
        in_specs=[pl.BlockSpec((tc * TOP_K,), lambda i: (i,), memory_space=pltpu.SMEM),
                  pl.BlockSpec((tc * TOP_K,), lambda i: (jnp.minimum(i + 1, nsteps - 1),),
                               memory_space=pltpu.SMEM),
                  pl.BlockSpec((tc, D_MODEL), lambda i: (i, 0)),
                  pl.BlockSpec((tc, LANES), lambda i: (i, 0)),
                  pl.BlockSpec((1, D_MODEL), lambda i: (0, 0)),
                  pl.BlockSpec(memory_space=pl.ANY)],
        out_specs=pl.BlockSpec((tc, D_MODEL), lambda i: (i, 0)),
        out_shape=jax.ShapeDtypeStruct((T, D_MODEL), F32),
        scratch_shapes=[pltpu.VMEM((2, TOP_K, tc) + TOKEN_TILE, F32), pltpu.SemaphoreType.DMA((2,))],
        compiler_params=_cparams("arbitrary"),
        name="combine",
    )(dest_flat, dest_flat, x1, route, g_final, ys)


def _moe_layer(x1, hf, route, cnt, ew, g_final):
    T = x1.shape[0]
    n_assign = T * TOP_K
    nblk = n_assign // MOE_BLOCK + N_EXPERTS
    n_slots = nblk * MOE_BLOCK
    dest, pst = _route(route, cnt)
    cnt_i = cnt[0, :N_EXPERTS].astype(I32)
    pstart = pst[0, :N_EXPERTS].astype(I32)
    padded = (cnt_i + MOE_BLOCK - 1) // MOE_BLOCK * MOE_BLOCK
    pend = pstart + padded
    block_expert = jnp.minimum(
        jnp.sum(jnp.arange(nblk, dtype=I32)[:, None] * MOE_BLOCK >= pend[None, :], axis=1),
        N_EXPERTS - 1).astype(I32)
    n_used = (pend[-1] // MOE_BLOCK).reshape(1).astype(I32)
    dest_flat = dest[:, :TOP_K].reshape(n_assign)
    xs = _dispatch(hf, dest_flat, pstart + cnt_i, padded - cnt_i, n_used, n_slots)
    ys = _moe(xs, block_expert, n_used, *ew)
    return _combine(x1, route, dest_flat, ys, g_final)


def _trunk(x, p, bb):
    B, L, D = x.shape
    T = B * L
    x2d = x.reshape(T, D)
    v, x1g, x2g, u = _inproj(x2d, L, p["g_mix"], p["w_in"], p["b_in"], p["short_w"], p["short_b"])
    ya = _hyena(v, x1g, x2g, B, L, p["filt"], p["filt_bias"], bb)
    ybp = _s5(u, B, L, p["s5"])
    x1, hf, route, cnt = _mix(x2d, ya, ybp, p["w_glu"], p["b_glu"], p["g_hyena"], p["g_s5"], p["w_out"],
                              p["g_ffn"], p["w_router"], p["b_router"])
    y = _moe_layer(x1, hf, route, cnt, p["experts"], p["g_final"])
    return y.reshape(B, L, D)


def kernel(x_prompt, x_sample, g_mix, w_in, b_in, short_w, short_b, filt_w1, filt_b1, filt_w2, filt_b2, filt_w3, filt_b3, filt_freq, filt_bias, s5_lam_re, s5_lam_im, s5_log_dt, s5_b_re, s5_b_im, s5_c_re, s5_c_im, s5_d, w_glu, b_glu, g_hyena, g_s5, w_out, g_ffn, w_router, b_router, w_gate, b_gate, w_up, b_up, w_down, b_down, g_final):
    assert g_mix.shape[0] == 1, "one encoder layer"
    row = lambda a: a[0][None, :].astype(F32)
    p = dict(
        g_mix=row(g_mix), w_in=w_in[0].astype(BF16), b_in=row(b_in),
        short_w=short_w[0].astype(F32), short_b=row(short_b),
        filt=(filt_w1[0], filt_b1[0], filt_w2[0], filt_b2[0], filt_w3[0], filt_b3[0], filt_freq[0]),
        filt_bias=filt_bias[0].astype(F32),
        s5=_s5_operators(s5_lam_re[0], s5_lam_im[0], s5_log_dt[0], s5_b_re[0], s5_b_im[0],
                         s5_c_re[0], s5_c_im[0], s5_d[0]),
        w_glu=w_glu[0].astype(BF16), b_glu=row(b_glu), g_hyena=row(g_hyena), g_s5=row(g_s5),
        w_out=w_out[0].astype(BF16), g_ffn=row(g_ffn),
        w_router=_split_bf16(jnp.pad(w_router[0].astype(F32), ((0, 0), (0, LANES - N_EXPERTS)))),
        b_router=jnp.pad(b_router[0].astype(F32), (0, LANES - N_EXPERTS))[None, :],
        experts=(w_gate[0].astype(BF16), b_gate[0][:, None, :], w_up[0].astype(BF16),
                 b_up[0][:, None, :], w_down[0].astype(BF16), b_down[0][:, None, :]),
        g_final=g_final[None, :].astype(F32),
    )
    y_prompt = _trunk(x_prompt, p, bb=1)
    y_sample = _trunk(x_sample, p, bb=8)
    return (y_prompt, y_sample)
```

```python
import functools
import math

import numpy as np
import jax
import jax.numpy as jnp
from jax import lax
from jax.experimental import pallas as pl
from jax.experimental.pallas import tpu as pltpu

F32 = jnp.float32
BF16 = jnp.bfloat16
I32 = jnp.int32
HIGHEST = lax.Precision.HIGHEST

LANES = 128
SUBLANES = 8
D_MODEL = 1024
C_HY = 512
C_S5 = 512
D_IN = 3 * C_HY + C_S5
Q_HY = C_HY // LANES
FILTER_BANDS = 16
FILTER_WIDTH = 64
Z_PAD = 128
DECAY_MAX = math.log(1e-2) / 0.3
DECAY_MIN = math.log(1e-2) / 1.5
S5_G, S5_H, S5_P = 32, 16, 64
S5_CHUNK = 16
S5_UW = S5_CHUNK * S5_H
S5_SW = 2 * S5_P
S5_GB = LANES // S5_H
S5_NB = S5_G // S5_GB
S5_KW = S5_CHUNK * LANES
S5_XW = S5_GB * S5_SW
S5_STEP_ROWS = 512
LAMBDA_RE_MAX = -1e-4
N_EXPERTS = 32
TOP_K = 4
SWIGLU_LIMIT = 7.0
SWIGLU_ALPHA = 1.702
MOE_BLOCK = 512
RMS_EPS = 1e-6
VMEM_LIMIT_BYTES = 56 * 1024 * 1024


def _cparams(*sem):
    return pltpu.CompilerParams(dimension_semantics=sem, vmem_limit_bytes=VMEM_LIMIT_BYTES)


def _rms(x, g):
    return x * lax.rsqrt(jnp.mean(x * x, axis=-1, keepdims=True) + RMS_EPS) * g


def _sigmoid(x):
    return 1.0 / (1.0 + jnp.exp(-x))


def _dot(a, b):
    return jnp.dot(a, b, preferred_element_type=F32)


def _split_bf16(w):
    hi = w.astype(BF16)
    return hi, (w - hi.astype(F32)).astype(BF16)


def _lane_block(x, q):
    return x[:, q * LANES:(q + 1) * LANES]


def _as_rows(ref):
    return ref.reshape(math.prod(ref.shape[:-1]), LANES)


TOKEN_TILE = (D_MODEL // LANES, LANES)


def _load_token_tiles(ref, n, row0=0):
    rows = _as_rows(ref)
    nq = TOKEN_TILE[0]
    return jnp.concatenate([rows[pl.ds(row0 * nq + q, n, stride=nq), :] for q in range(nq)], axis=1)


def _store_token_tiles(ref, x):
    rows = _as_rows(ref)
    nq = TOKEN_TILE[0]
    for q in range(nq):
        rows[pl.ds(q, x.shape[0], stride=nq), :] = _lane_block(x, q)


def _inproj_kernel(x_ref, xp_ref, xn_ref, g_ref, w_ref, b_ref, sw_ref, sb_ref,
                   v_ref, x1_ref, x2_ref, u_ref, *, tm, seq):
    i = pl.program_id(0)
    g = g_ref[...]
    rows = jnp.concatenate([x_ref[...], xp_ref[...], xn_ref[...]], axis=0)
    proj = _dot(_rms(rows, g).astype(BF16), w_ref[...]) + b_ref[...]
    u_ref[...] = proj[:tm, 3 * C_HY:]
    z = proj[:tm, :3 * C_HY]
    zh = proj[tm:, :3 * C_HY]
    row0 = i * tm
    has_prev = lax.rem(row0, seq) != 0
    has_next = lax.rem(row0 + tm, seq) != 0
    zp = jnp.where(has_prev, zh[SUBLANES - 1:SUBLANES], 0.0)
    zn = jnp.where(has_next, zh[SUBLANES:SUBLANES + 1], 0.0)
    rid = lax.broadcasted_iota(I32, (tm, 1), 0)
    zm1 = jnp.where(rid == 0, zp, pltpu.roll(z, 1, 0))
    zp1 = jnp.where(rid == tm - 1, zn, pltpu.roll(z, tm - 1, 0))
    sw = sw_ref[...]
    o = zm1 * sw[0:1] + z * sw[1:2] + zp1 * sw[2:3] + sb_ref[...]
    for q in range(Q_HY):
        v_ref[q] = _lane_block(o, q)
        x1_ref[q] = _lane_block(o, Q_HY + q)
        x2_ref[q] = _lane_block(o, 2 * Q_HY + q)


def _inproj(x2d, seq, g_mix, w_in_bf, b_in, short_w, short_b, tm=512):
    T = x2d.shape[0]
    nb8 = T // SUBLANES
    tb = tm // SUBLANES
    hy = jax.ShapeDtypeStruct((Q_HY, T, LANES), F32)
    hy_spec = pl.BlockSpec((Q_HY, tm, LANES), lambda i: (0, i, 0))
    const = lambda i: (0, 0)
    return pl.pallas_call(
        functools.partial(_inproj_kernel, tm=tm, seq=seq),
        grid=(T // tm,),
        in_specs=[
            pl.BlockSpec((tm, D_MODEL), lambda i: (i, 0)),
            pl.BlockSpec((SUBLANES, D_MODEL), lambda i: (jnp.maximum(i * tb - 1, 0), 0)),
            pl.BlockSpec((SUBLANES, D_MODEL), lambda i: (jnp.minimum((i + 1) * tb, nb8 - 1), 0)),
            pl.BlockSpec((1, D_MODEL), const),
            pl.BlockSpec((D_MODEL, D_IN), const),
            pl.BlockSpec((1, D_IN), const),
            pl.BlockSpec((3, 3 * C_HY), const),
            pl.BlockSpec((1, 3 * C_HY), const),
        ],
        out_specs=[hy_spec, hy_spec, hy_spec, pl.BlockSpec((tm, C_S5), lambda i: (i, 0))],
        out_shape=[hy, hy, hy, jax.ShapeDtypeStruct((T, C_S5), F32)],
        compiler_params=_cparams("arbitrary"),
        name="inproj",
    )(x2d, x2d, x2d, g_mix, w_in_bf, b_in, short_w, short_b)


def _filter_kernel(z_ref, w1_ref, b1_ref, w2_ref, b2_ref, w3h_ref, w3l_ref, b3_ref, fr_ref, dl_ref,
                   k_ref, s_ref, *, tr, seq):
    i = pl.program_id(0)
    z = z_ref[...]
    fr = fr_ref[...]
    h = jnp.sin(fr[0:1] * (jnp.dot(z, w1_ref[...], precision=HIGHEST,
                                   preferred_element_type=F32) + b1_ref[...]))
    h = jnp.sin(fr[1:2] * (jnp.dot(h, w2_ref[...], precision=HIGHEST,
                                   preferred_element_type=F32) + b2_ref[...]))
    h_hi = h.astype(BF16)
    h_lo = (h - h_hi.astype(F32)).astype(BF16)
    h = _dot(h_hi, w3h_ref[...]) + _dot(h_lo, w3h_ref[...]) + _dot(h_hi, w3l_ref[...]) + b3_ref[...]
    h = h * jnp.exp(-z[:, 0:1] * dl_ref[...])
    rid = i * tr + lax.broadcasted_iota(I32, (tr, 1), 0)
    h = jnp.where(rid == seq, 0.0, h)
    for q in range(2 * Q_HY):
        k_ref[q] = _lane_block(h, q)

    @pl.when(i == 0)
    def _():
        s_ref[...] = jnp.zeros_like(s_ref)

    s_ref[...] += jnp.sum(jnp.abs(h), axis=0, keepdims=True)


def _filter_taps(seq, w1, b1, w2, b2, w3, b3, freq, tr=512):
    L = seq
    N = 2 * L
    n = jnp.arange(N, dtype=I32)
    pos = jnp.where(n < L, n, jnp.where(n == L, 0, N - n)).astype(F32)
    t = (pos * (1.0 / (L - 1)))[:, None]
    w = 2.0 * math.pi * pos / L
    bands = jnp.linspace(1e-4, FILTER_BANDS - 1, FILTER_BANDS, dtype=F32)
    ang = w[:, None] * bands[None, :]
    zc = jnp.concatenate([t, jnp.cos(ang), -jnp.sin(ang),
                          jnp.zeros((N, Z_PAD - 1 - 2 * FILTER_BANDS), F32)], axis=-1)
    w1p = jnp.pad(w1, ((0, Z_PAD - w1.shape[0]), (0, 0)))
    w3d = w3.reshape(FILTER_WIDTH, 2, 2 * C_HY).transpose(1, 0, 2)
    b3d = b3.reshape(2, 1, 2 * C_HY)
    deltas = jnp.abs(jnp.linspace(DECAY_MIN, DECAY_MAX, C_HY, dtype=F32))
    dl = jnp.tile(deltas, 2)[None, :]
    half = (N // tr) // 2
    const = lambda i: (0, 0)
    taps, sums = pl.pallas_call(
        functools.partial(_filter_kernel, tr=tr, seq=L),
        grid=(N // tr,),
        in_specs=[
            pl.BlockSpec((tr, Z_PAD), lambda i: (i, 0)),
            pl.BlockSpec((Z_PAD, FILTER_WIDTH), const),
            pl.BlockSpec((1, FILTER_WIDTH), const),
            pl.BlockSpec((FILTER_WIDTH, FILTER_WIDTH), const),
            pl.BlockSpec((1, FILTER_WIDTH), const),
            pl.BlockSpec((None, FILTER_WIDTH, 2 * C_HY), lambda i: (i // half, 0, 0)),
            pl.BlockSpec((None, FILTER_WIDTH, 2 * C_HY), lambda i: (i // half, 0, 0)),
            pl.BlockSpec((None, 1, 2 * C_HY), lambda i: (i // half, 0, 0)),
            pl.BlockSpec((2, FILTER_WIDTH), const),
            pl.BlockSpec((1, 2 * C_HY), const),
        ],
        out_specs=[pl.BlockSpec((2 * Q_HY, tr, LANES), lambda i: (0, i, 0)),
                   pl.BlockSpec((1, 2 * C_HY), const)],
        out_shape=[jax.ShapeDtypeStruct((2 * Q_HY, N, LANES), F32),
                   jax.ShapeDtypeStruct((1, 2 * C_HY), F32)],
        compiler_params=_cparams("arbitrary"),
        name="hyena_filter",
    )(zc, w1p, b1[None, :], w2, b2[None, :], *_split_bf16(w3d.astype(F32)), b3d, freq, dl)
    return taps, sums


def _split_n(N):
    n1 = {32768: 128, 4096: 64}.get(N)
    if n1 is None:
        n1 = 1 << (int(math.log2(N)) // 2)
    return n1, N // n1


def _dft_constants(N1, N2):
    N = N1 * N2
    k1 = np.arange(N1)[:, None]
    n1 = np.arange(N1)[None, :]
    ang = 2.0 * np.pi * ((k1 * n1) % N1) / N1
    fa_full = np.concatenate([np.cos(ang), -np.sin(ang)], axis=0)
    fa_half = fa_full[:, :N1 // 2]
    fa_inv = np.concatenate([np.cos(ang), -np.sin(ang)], axis=1)[:N1 // 2] / N
    k2 = np.arange(N2)[:, None]
    n2 = np.arange(N2)[None, :]
    a2 = 2.0 * np.pi * ((k2 * n2) % N2) / N2
    cr = jnp.asarray(np.cos(a2).astype(np.float32))[None]
    ci = jnp.asarray((-np.sin(a2)).astype(np.float32))[None]
    to = lambda a: jnp.asarray(a.astype(np.float32)).astype(BF16)
    kk = lax.broadcasted_iota(I32, (N1, N2), 0)
    nn = lax.broadcasted_iota(I32, (N1, N2), 1)
    ta = (2.0 * math.pi / N) * lax.rem(kk * nn, N).astype(F32)
    tr, ti = jnp.cos(ta), -jnp.sin(ta)
    stack = lambda re, im: jnp.concatenate(
        [jnp.concatenate([re, -im], axis=-1), jnp.concatenate([im, re], axis=-1)], axis=-2).astype(BF16)
    tc_r, tc_i = tr[:, None, :], ti[:, None, :]
    fk = stack(cr * tc_r - ci * tc_i, cr * tc_i + ci * tc_r)
    tr_r, tr_i = tr[:, :, None], ti[:, :, None]
    gk = stack(cr * tr_r - ci * tr_i, -(cr * tr_i + ci * tr_r))
    return dict(fa_full=to(fa_full), fa_half=to(fa_half), fa_inv=to(fa_inv), fk=fk, gk=gk)


U32 = jnp.uint32
_HI16 = 0xFFFF0000
_HALF16 = 0x8000


def _pack_spectrum(re, im):
    rb = lax.bitcast_convert_type(re, U32) + U32(_HALF16)
    ib = lax.bitcast_convert_type(im, U32) + U32(_HALF16)
    return (rb & U32(_HI16)) | (ib >> 16)


def _unpack_spectrum(w):
    return (lax.bitcast_convert_type(w & U32(_HI16), F32), lax.bitcast_convert_type(w << 16, F32))


N2_STEP = 16


def _dft_a_kernel(f_ref, x_ref, o_ref, *, rows):
    nq = x_ref.shape[0]
    x2d = _as_rows(x_ref)
    f = f_ref[...]
    half = f.shape[0] // 2
    for j in range(N2_STEP):
        xs = jnp.concatenate(
            [x2d[pl.ds(q * rows * N2_STEP + j, rows, stride=N2_STEP), :] for q in range(nq)], axis=1)
        r = _dot(f, xs.astype(BF16))
        w = _pack_spectrum(r[:half], r[half:])
        for q in range(nq):
            o_ref[q, j] = _lane_block(w, q)


def _dft_a(f, x):
    Q, Bt, K, N2, _ = x.shape
    R = f.shape[0] // 2
    return pl.pallas_call(
        functools.partial(_dft_a_kernel, rows=K),
        grid=(Bt, N2 // N2_STEP, Q // Q_HY),
        in_specs=[pl.BlockSpec((2 * R, K), lambda b, j, w: (0, 0)),
                  pl.BlockSpec((Q_HY, None, K, N2_STEP, LANES), lambda b, j, w: (w, b, 0, j, 0))],
        out_specs=pl.BlockSpec((Q_HY, None, N2_STEP, R, LANES), lambda b, j, w: (w, b, j, 0, 0)),
        out_shape=jax.ShapeDtypeStruct((Q, Bt, N2, R, LANES), U32),
        compiler_params=_cparams("arbitrary", "arbitrary", "arbitrary"),
        name="dft_a",
    )(f, x)


def _dft_a_inv_kernel(f_ref, z_ref, gate_ref, v_ref, bias_ref, *rest, rows, with_next):
    if with_next:
        fn_ref, o_ref, an_ref = rest
    else:
        (o_ref,) = rest
    g2d = _as_rows(gate_ref)
    v2d = _as_rows(v_ref)
    o2d = _as_rows(o_ref)
    f = f_ref[...]
    bias = bias_ref[...]
    for j in range(N2_STEP):
        zr, zi = _unpack_spectrum(jnp.concatenate([z_ref[q, j] for q in range(Q_HY)], axis=1))
        y = _dot(f, jnp.concatenate([zr, zi], axis=0).astype(BF16))
        outs = []
        for q in range(Q_HY):
            sl = pl.ds(q * rows * N2_STEP + j, rows, stride=N2_STEP)
            outs.append(g2d[sl, :] * (_lane_block(y, q) + v2d[sl, :] * _lane_block(bias, q)))
            o2d[sl, :] = outs[-1]
        if with_next:
            r = _dot(fn_ref[...], jnp.concatenate(outs, axis=1).astype(BF16))
            half = r.shape[0] // 2
            w = _pack_spectrum(r[:half], r[half:])
            for q in range(Q_HY):
                an_ref[q, j] = _lane_block(w, q)


def _dft_a_inv(f, z, gate, v, bias_row, f_next=None):
    Q, Bt, N2, N1, _ = z.shape
    K = f.shape[0]
    nat = pl.BlockSpec((Q, None, K, N2_STEP, LANES), lambda b, j: (0, b, 0, j, 0))
    spec = pl.BlockSpec((Q, None, N2_STEP, N1, LANES), lambda b, j: (0, b, j, 0, 0))
    with_next = f_next is not None
    in_specs = [pl.BlockSpec((K, 2 * N1), lambda b, j: (0, 0)), spec, nat, nat,
                pl.BlockSpec((1, C_HY), lambda b, j: (0, 0))]
    out_specs, out_shape = [nat], [jax.ShapeDtypeStruct((Q, Bt, K, N2, LANES), F32)]
    args = [f, z, gate, v, bias_row]
    if with_next:
        in_specs.append(pl.BlockSpec((2 * N1, K), lambda b, j: (0, 0)))
        out_specs.append(spec)
        out_shape.append(jax.ShapeDtypeStruct(z.shape, U32))
        args.append(f_next)
    return pl.pallas_call(
        functools.partial(_dft_a_inv_kernel, rows=K, with_next=with_next),
        grid=(Bt, N2 // N2_STEP),
        in_specs=in_specs,
        out_specs=out_specs,
        out_shape=out_shape,
        compiler_params=_cparams("arbitrary", "arbitrary"),
        name="dft_a_inv",
    )(*args)


C_STEP_Q = 2
K1_STEP = SUBLANES


def _dft_c_kernel(a_ref, k_ref, f_ref, g_ref, z_ref, *, bb, n2):
    per = n2 * K1_STEP
    a2d = _as_rows(a_ref)
    z2d = _as_rows(z_ref)
    cols = [(b, q) for b in range(bb) for q in range(C_STEP_Q)]

    def rows(b, q, kk):
        return pl.ds((q * bb + b) * per + kk, n2, stride=K1_STEP)

    for kk in range(K1_STEP):
        kr, ki = _unpack_spectrum(jnp.tile(k_ref[kk], (1, bb)))
        ar, ai = _unpack_spectrum(jnp.concatenate([a2d[rows(b, q, kk), :] for b, q in cols], axis=1))
        x = _dot(f_ref[kk], jnp.concatenate([ar, ai], axis=0).astype(BF16))
        xr = x[:n2]
        xi = x[n2:]
        yr = xr * kr - xi * ki
        yi = xr * ki + xi * kr
        zz = _dot(g_ref[kk], jnp.concatenate([yr, yi], axis=0).astype(BF16))
        w = _pack_spectrum(zz[:n2], zz[n2:])
        for c, (b, q) in enumerate(cols):
            z2d[rows(b, q, kk), :] = _lane_block(w, c)


def _dft_c(a5, khat, order, fk, gk, bb):
    Q, Bt, N2, N1, _ = a5.shape
    cw = C_STEP_Q * LANES
    ablk = pl.BlockSpec((C_STEP_Q, bb, N2, K1_STEP, LANES), lambda g, c, b: (c, b, 0, g, 0))
    mats = pl.BlockSpec((K1_STEP, 2 * N2, 2 * N2), lambda g, c, b: (g, 0, 0))
    return pl.pallas_call(
        functools.partial(_dft_c_kernel, bb=bb, n2=N2),
        grid=(N1 // K1_STEP, Q // C_STEP_Q, Bt // bb),
        in_specs=[ablk,
                  pl.BlockSpec((K1_STEP, N2, cw), lambda g, c, b: (g, 0, order * (C_HY // cw) + c)),
                  mats, mats],
        out_specs=ablk,
        out_shape=jax.ShapeDtypeStruct(a5.shape, U32),
        compiler_params=_cparams("arbitrary", "arbitrary", "arbitrary"),
        name="dft_c",
    )(a5, khat, fk, gk)


def _dft_c_filter_kernel(a_ref, f_ref, s_ref, o_ref, *, n2):
    per = n2 * K1_STEP
    a2d = _as_rows(a_ref)
    sc = 1.0 / (s_ref[...] + 1e-6)
    for kk in range(K1_STEP):
        ar, ai = _unpack_spectrum(jnp.concatenate(
            [a2d[pl.ds(q * per + kk, n2, stride=K1_STEP), :] for q in range(C_STEP_Q)], axis=1))
        x = _dot(f_ref[kk], jnp.concatenate([ar, ai], axis=0).astype(BF16))
        o_ref[kk] = _pack_spectrum(x[:n2] * sc, x[n2:] * sc)


def _dft_c_filter(a4, fk, sums):
    Q, N2, N1, _ = a4.shape
    cw = C_STEP_Q * LANES
    return pl.pallas_call(
        functools.partial(_dft_c_filter_kernel, n2=N2),
        grid=(N1 // K1_STEP, Q // C_STEP_Q),
        in_specs=[pl.BlockSpec((C_STEP_Q, N2, K1_STEP, LANES), lambda g, c: (c, 0, g, 0)),
                  pl.BlockSpec((K1_STEP, 2 * N2, 2 * N2), lambda g, c: (g, 0, 0)),
                  pl.BlockSpec((1, cw), lambda g, c: (0, c))],
        out_specs=pl.BlockSpec((K1_STEP, N2, cw), lambda g, c: (g, 0, c)),
        out_shape=jax.ShapeDtypeStruct((N1, N2, Q * LANES), U32),
        compiler_params=_cparams("arbitrary", "arbitrary"),
        name="dft_c_filter",
    )(a4, fk, sums)


def _hyena(v, x1, x2, B, seq, filt, filt_bias, bb):
    L = seq
    N = 2 * L
    N1, N2 = _split_n(N)
    cst = _dft_constants(N1, N2)
    taps, sums = _filter_taps(L, *filt)
    ka = _dft_a(cst["fa_full"], taps.reshape(2 * Q_HY, 1, N1, N2, LANES))
    khat = _dft_c_filter(ka.reshape(2 * Q_HY, N2, N1, LANES), cst["fk"], sums)
    nat = lambda a: a.reshape(Q_HY, B, N1 // 2, N2, LANES)
    cur = nat(v)
    a = _dft_a(cst["fa_half"], cur)
    for o, gate in enumerate((x1, x2)):
        z = _dft_c(a, khat, o, cst["fk"], cst["gk"], bb)
        if o == 0:
            cur, a = _dft_a_inv(cst["fa_inv"], z, nat(gate), cur, filt_bias[o][None, :], cst["fa_half"])
        else:
            (cur,) = _dft_a_inv(cst["fa_inv"], z, nat(gate), cur, filt_bias[o][None, :])
    return cur.reshape(Q_HY, B * L, LANES)


def _s5_operators(lam_re, lam_im, log_dt, b_re, b_im, c_re, c_im, d_skip):
    Tc, G, H, P = S5_CHUNK, S5_G, S5_H, S5_P
    lam = jnp.minimum(lam_re.astype(F32), LAMBDA_RE_MAX) + 1j * lam_im.astype(F32)
    dt = jnp.exp(log_dt.astype(F32))[..., None]
    lam_dt = lam * dt
    lam_bar = jnp.exp(lam_dt)
    b_bar = ((lam_bar - 1.0) / lam)[..., None] * (b_re.astype(F32) + 1j * b_im.astype(F32))
    c = c_re.astype(F32) + 1j * c_im.astype(F32)
    tau = jnp.arange(Tc + 1, dtype=F32)
    pw = jnp.exp(lam_dt[None] * tau[:, None, None, None])
    kk = jnp.einsum('dghp,tdgp,dgpk->dtghk', c, pw[:Tc], b_bar).real
    i = jnp.arange(Tc)
    lag = i[:, None] - i[None, :]
    kf = jnp.where((lag >= 0)[:, :, None, None, None], kk[0][jnp.clip(lag, 0, Tc - 1)], 0.0)
    kb = jnp.where((lag <= 0)[:, :, None, None, None], kk[1][jnp.clip(-lag, 0, Tc - 1)], 0.0)
    m = kf + kb
    eye = (lag == 0)[:, :, None, None, None] * jnp.eye(H, dtype=F32)[None, None, None]
    m = m + eye * d_skip.astype(F32).reshape(G, H)[None, None, :, :, None]
    clf = c[0][None] * pw[1:Tc + 1, 0][:, :, None, :]
    clb = c[1][None] * pw[Tc - i, 1][:, :, None, :]
    sf = pw[Tc - 1 - i, 0][:, :, :, None] * b_bar[0][None]
    sb = pw[i, 1][:, :, :, None] * b_bar[1][None]
    w_intra = m.transpose(2, 1, 4, 0, 3).reshape(G, Tc * H, Tc * H)
    st2y = lambda cl: jnp.concatenate([cl.real, -cl.imag], axis=-1).transpose(1, 3, 0, 2) \
        .reshape(G, 2 * P, Tc * H)
    u2s = lambda s: jnp.concatenate([s.real, s.imag], axis=2).transpose(1, 0, 3, 2) \
        .reshape(G, Tc * H, 2 * P)
    w_state = jnp.concatenate([u2s(sf), u2s(sb)], axis=-1)
    pair = lambda a: a.reshape(2, G // 2, 2 * P)
    lam16 = jnp.concatenate([pair(pw[Tc].real), pair(pw[Tc].imag)], axis=-1).reshape(2, G * 2 * P)
    col = np.arange(S5_KW)
    spread = (np.arange(S5_UW)[:, None] == (col // LANES * S5_H + col % S5_H)[None, :])
    return dict(w_intra=w_intra.astype(BF16), w_xf=st2y(clf).astype(BF16), w_xb=st2y(clb).astype(BF16),
                w_state=w_state.astype(BF16), lam16=lam16,
                spread=jnp.asarray(spread.astype(np.float32)).astype(BF16))


def _s5_rows(i, cl, B, ncc):
    return pl.ds(S5_CHUNK * cl + i, B, stride=S5_CHUNK * ncc)


def _s5_chunk_inputs(u2d, B, ncc):
    cols = []
    for i in range(S5_CHUNK):
        if B == 1:
            cols.append(u2d[pl.ds(i, ncc, stride=S5_CHUNK), :])
        else:
            cols.append(jnp.concatenate([u2d[_s5_rows(i, cl, B, ncc), :] for cl in range(ncc)], axis=0))
    return jnp.concatenate(cols, axis=1).astype(BF16)


def _s5_state_lane(a, part):
    return (a // 2) * 2 * S5_SW + part * S5_SW + (a % 2) * S5_P


def _s5_state_kernel(u_ref, w_ref, sf_ref, sb_ref, wblk, *, B, ncc):
    @pl.when(pl.program_id(1) == 0)
    def _():
        wblk[...] = jnp.zeros_like(wblk)
        for a in range(S5_GB):
            for j in range(S5_CHUNK):
                r0 = j * LANES + a * S5_H
                for d in range(2):
                    for part in range(2):
                        c0 = d * S5_XW + _s5_state_lane(a, part)
                        s0 = d * S5_SW + part * S5_P
                        wblk[r0:r0 + S5_H, c0:c0 + S5_P] = w_ref[a, j * S5_H:(j + 1) * S5_H, s0:s0 + S5_P]

    lhs = _s5_chunk_inputs(_as_rows(u_ref), B, ncc)
    r = _dot(lhs, wblk[...])
    sf_ref[...] = r[:, :S5_XW]
    sb_ref[...] = r[:, S5_XW:]


def _s5_state(u3, w_state, ncc):
    B, L, _ = u3.shape
    nch = L // S5_CHUNK
    out = jax.ShapeDtypeStruct((nch * B, S5_NB * S5_XW), F32)
    return pl.pallas_call(
        functools.partial(_s5_state_kernel, B=B, ncc=ncc),
        grid=(S5_NB, nch // ncc),
        in_specs=[pl.BlockSpec((B, S5_CHUNK * ncc, LANES), lambda q, t: (0, t, q)),
                  pl.BlockSpec((S5_GB, S5_UW, 2 * S5_SW), lambda q, t: (q, 0, 0))],
        out_specs=[pl.BlockSpec((ncc * B, S5_XW), lambda q, t: (t, q))] * 2,
        out_shape=[out, out],
        scratch_shapes=[pltpu.VMEM((S5_KW, 2 * S5_XW), BF16)],
        compiler_params=_cparams("arbitrary", "arbitrary"),
        name="s5_state",
    )(u3, w_state)


def _s5_scan_kernel(sf_ref, sb_ref, lam_ref, xf_ref, xb_ref, *, nch, rows):
    lb = sf_ref.shape[1]
    vr = max(rows, SUBLANES)
    lam_f, lam_b = lam_ref[0], lam_ref[1]

    def advance(x, lam, inc):
        out = []
        for u in range(0, lb, 2 * S5_SW):
            re, im = x[:, u:u + S5_SW], x[:, u + S5_SW:u + 2 * S5_SW]
            lr, li = lam[:, u:u + S5_SW], lam[:, u + S5_SW:u + 2 * S5_SW]
            out += [lr * re - li * im, lr * im + li * re]
        return jnp.concatenate(out, axis=1) + inc

    def body(c, carry):
        xf, xb = carry
        rf = pl.ds(pl.multiple_of(c * rows, rows), rows)
        xf_ref[rf, :] = xf[:rows]
        xf = advance(xf, lam_f, sf_ref[rf, :])
        rb = pl.ds(pl.multiple_of((nch - 1 - c) * rows, rows), rows)
        xb_ref[rb, :] = xb[:rows]
        xb = advance(xb, lam_b, sb_ref[rb, :])
        return xf, xb

    zero = jnp.zeros((vr, lb), F32)
    lax.fori_loop(0, nch, body, (zero, zero))


def _s5_scan(sf, sb, lam16, nch, rows, lb):
    R, lanes = sf.shape
    blk = pl.BlockSpec((R, lb), lambda j: (0, j))
    out = jax.ShapeDtypeStruct((R, lanes), F32)
    return pl.pallas_call(
        functools.partial(_s5_scan_kernel, nch=nch, rows=rows),
        grid=(lanes // lb,),
        in_specs=[blk, blk, pl.BlockSpec((2, 1, lb), lambda j: (0, 0, j))],
        out_specs=[blk, blk],
        out_shape=[out, out],
        compiler_params=_cparams("arbitrary"),
        name="s5_scan",
    )(sf, sb, lam16[:, None, :])


def _s5_out_kernel(u_ref, xf_ref, xb_ref, wm_ref, wf_ref, wb_ref, e_ref, y_ref, wm_blk, wf_blk, wb_blk,
                   *, B, ncc):
    @pl.when(pl.program_id(1) == 0)
    def _():
        lane = lax.broadcasted_iota(I32, (1, S5_KW), 1)
        slot = (lane // S5_H) % S5_GB
        e = e_ref[...]
        for a in range(S5_GB):
            own = slot == a
            ex = jnp.where(own, _dot(wm_ref[a], e), 0.0).astype(BF16)
            for j in range(S5_CHUNK):
                r0 = j * LANES + a * S5_H
                wm_blk[r0:r0 + S5_H, :] = ex[j * S5_H:(j + 1) * S5_H, :]
            for src, dst in ((wf_ref, wf_blk), (wb_ref, wb_blk)):
                ex = jnp.where(own, _dot(src[a], e), 0.0).astype(BF16)
                for part in range(2):
                    r0 = _s5_state_lane(a, part)
                    dst[r0:r0 + S5_P, :] = ex[part * S5_P:(part + 1) * S5_P, :]

    lhs = _s5_chunk_inputs(_as_rows(u_ref), B, ncc)
    acc = _dot(lhs, wm_blk[...])
    acc += _dot(xf_ref[...].astype(BF16), wf_blk[...])
    acc += _dot(xb_ref[...].astype(BF16), wb_blk[...])
    y2d = _as_rows(y_ref)
    for i in range(S5_CHUNK):
        piece = _lane_block(acc, i)
        if B == 1:
            y2d[pl.ds(i, ncc, stride=S5_CHUNK), :] = piece
        else:
            for cl in range(ncc):
                y2d[_s5_rows(i, cl, B, ncc), :] = piece[cl * B:(cl + 1) * B]


def _s5_out(u3, xf, xb, ops, ncc):
    B, L, _ = u3.shape
    nch = L // S5_CHUNK
    tok = pl.BlockSpec((B, S5_CHUNK * ncc, LANES), lambda q, t: (0, t, q))
    st = pl.BlockSpec((ncc * B, S5_XW), lambda q, t: (t, q))
    return pl.pallas_call(
        functools.partial(_s5_out_kernel, B=B, ncc=ncc),
        grid=(S5_NB, nch // ncc),
        in_specs=[tok, st, st,
                  pl.BlockSpec((S5_GB, S5_UW, S5_UW), lambda q, t: (q, 0, 0)),
                  pl.BlockSpec((S5_GB, S5_SW, S5_UW), lambda q, t: (q, 0, 0)),
                  pl.BlockSpec((S5_GB, S5_SW, S5_UW), lambda q, t: (q, 0, 0)),
                  pl.BlockSpec((S5_UW, S5_KW), lambda q, t: (0, 0))],
        out_specs=tok,
        out_shape=jax.ShapeDtypeStruct(u3.shape, F32),
        scratch_shapes=[pltpu.VMEM((S5_KW, S5_KW), BF16), pltpu.VMEM((S5_XW, S5_KW), BF16),
                        pltpu.VMEM((S5_XW, S5_KW), BF16)],
        compiler_params=_cparams("arbitrary", "arbitrary"),
        name="s5_out",
    )(u3, xf, xb, ops["w_intra"], ops["w_xf"], ops["w_xb"], ops["spread"])


def _s5(u, B, seq, ops):
    nch = seq // S5_CHUNK
    u3 = u.reshape(B, seq, C_S5)
    ncc = min(nch, S5_STEP_ROWS // B)
    sf, sb = _s5_state(u3, ops["w_state"], ncc)
    lb = 1024 if B == 1 else 256
    xf, xb = _s5_scan(sf, sb, ops["lam16"], nch, B, lb)
    return _s5_out(u3, xf, xb, ops, ncc).reshape(B * seq, C_S5)


def _mix_kernel(x_ref, ya_ref, yb_ref, wglu_ref, bglu_ref, gh_ref, gs_ref, wout_ref, gffn_ref,
                wrh_ref, wrl_ref, br_ref, x1_ref, hf_ref, route_ref, cnt_ref, *, tm):
    g = jax.nn.gelu(yb_ref[...])
    yb = g * _sigmoid(_dot(g.astype(BF16), wglu_ref[...]) + bglu_ref[...])
    ya = jnp.concatenate([ya_ref[q] for q in range(Q_HY)], axis=1)
    na = _rms(ya, gh_ref[...]).astype(BF16)
    nb = _rms(yb, gs_ref[...]).astype(BF16)
    mixed = _dot(na, wout_ref[:C_HY, :]) + _dot(nb, wout_ref[C_HY:, :])
    x1 = x_ref[...] + mixed
    x1_ref[...] = x1
    hf = _rms(x1, gffn_ref[...])
    _store_token_tiles(hf_ref, hf)
    hf_hi = hf.astype(BF16)
    hf_lo = (hf - hf_hi.astype(F32)).astype(BF16)
    logits = (_dot(hf_hi, wrh_ref[...]) + _dot(hf_lo, wrh_ref[...]) + _dot(hf_hi, wrl_ref[...])
              + br_ref[...])
    l = logits.T[:N_EXPERTS]
    row = lax.broadcasted_iota(I32, (N_EXPERTS, tm), 0)
    neg = jnp.float32(-jnp.inf)
    vals, idxs = [], []
    for _ in range(TOP_K):
        m = jnp.max(l, axis=0, keepdims=True)
        idx = jnp.min(jnp.where(l == m, row, N_EXPERTS), axis=0, keepdims=True)
        vals.append(m)
        idxs.append(idx)
        l = jnp.where(row == idx, neg, l)
    es = [jnp.exp(v - vals[0]) for v in vals]
    den = es[0] + es[1] + es[2] + es[3]
    packed = jnp.concatenate([e / den for e in es] + [i.astype(F32) for i in idxs]
                             + [jnp.zeros((LANES - 2 * TOP_K, tm), F32)], axis=0)
    route_ref[...] = packed.T

    @pl.when(pl.program_id(0) == 0)
    def _():
        cnt_ref[...] = jnp.zeros_like(cnt_ref)

    chosen = jnp.zeros((N_EXPERTS, tm), F32)
    for idx in idxs:
        chosen += (row == idx).astype(F32)
    cnt_ref[...] += jnp.sum(chosen, axis=1, keepdims=True)


def _mix(x2d, ya4, ybp, w_glu_bf, b_glu, g_hyena, g_s5, w_out_bf, g_ffn, w_router_p, b_router_p, tm=512):
    T = x2d.shape[0]
    const = lambda i: (0, 0)
    row = lambda w: pl.BlockSpec((tm, w), lambda i: (i, 0))
    return pl.pallas_call(
        functools.partial(_mix_kernel, tm=tm),
        grid=(T // tm,),
        in_specs=[row(D_MODEL), pl.BlockSpec((Q_HY, tm, LANES), lambda i: (0, i, 0)), row(C_S5),
                  pl.BlockSpec((C_S5, C_S5), const), pl.BlockSpec((1, C_S5), const),
                  pl.BlockSpec((1, C_HY), const), pl.BlockSpec((1, C_S5), const),
                  pl.BlockSpec((D_MODEL, D_MODEL), const), pl.BlockSpec((1, D_MODEL), const),
                  pl.BlockSpec((D_MODEL, LANES), const), pl.BlockSpec((D_MODEL, LANES), const),
                  pl.BlockSpec((1, LANES), const)],
        out_specs=[row(D_MODEL), pl.BlockSpec((tm,) + TOKEN_TILE, lambda i: (i, 0, 0)), row(LANES),
                   pl.BlockSpec((N_EXPERTS, 1), const)],
        out_shape=[jax.ShapeDtypeStruct((T, D_MODEL), F32), jax.ShapeDtypeStruct((T,) + TOKEN_TILE, F32),
                   jax.ShapeDtypeStruct((T, LANES), F32), jax.ShapeDtypeStruct((N_EXPERTS, 1), F32)],
        compiler_params=_cparams("arbitrary"),
        name="mix",
    )(x2d, ya4, ybp, w_glu_bf, b_glu, g_hyena, g_s5, w_out_bf, g_ffn, w_router_p[0], w_router_p[1],
      b_router_p)


def _expert_one_hot(eids, tm):
    lane = lax.broadcasted_iota(I32, (tm, LANES), 1)
    oh = jnp.zeros((tm, LANES), F32)
    for e in eids:
        oh += (lane == e).astype(F32)
    return oh


def _route_kernel(r_ref, cnt_ref, dest_ref, pst_ref, carry, pstart, *, tm):
    i = pl.program_id(0)
    lane = lax.broadcasted_iota(I32, (tm, LANES), 1)
    r = r_ref[...]
    eids = [r[:, TOP_K + k:TOP_K + k + 1].astype(I32) for k in range(TOP_K)]
    oh = _expert_one_hot(eids, tm)

    @pl.when(i == 0)
    def _():
        padded = jnp.floor((cnt_ref[...] + (MOE_BLOCK - 1)) * (1.0 / MOE_BLOCK)) * MOE_BLOCK
        a = lax.broadcasted_iota(I32, (LANES, LANES), 0)
        b = lax.broadcasted_iota(I32, (LANES, LANES), 1)
        excl = jnp.dot(jnp.broadcast_to(padded, (SUBLANES, LANES)), (a < b).astype(F32),
                       precision=HIGHEST, preferred_element_type=F32)
        pstart[...] = excl[0:1]
        pst_ref[...] = excl[0:1]
        carry[...] = jnp.zeros_like(carry)

    a = lax.broadcasted_iota(I32, (tm, tm), 0)
    b = lax.broadcasted_iota(I32, (tm, tm), 1)
    before = _dot((b < a).astype(BF16), oh.astype(BF16))
    base = before + carry[...] + pstart[...]
    out = jnp.zeros((tm, LANES), F32)
    for k, e in enumerate(eids):
        d = jnp.sum(jnp.where(lane == e, base, 0.0), axis=-1, keepdims=True)
        out = jnp.where(lane == k, d, out)
    dest_ref[...] = out.astype(I32)
    carry[...] += jnp.sum(oh, axis=0, keepdims=True)


def _route(route, cnt, tm=512):
    T = route.shape[0]
    return pl.pallas_call(
        functools.partial(_route_kernel, tm=tm),
        grid=(T // tm,),
        in_specs=[pl.BlockSpec((tm, LANES), lambda i: (i, 0)),
                  pl.BlockSpec((1, LANES), lambda i: (0, 0))],
        out_specs=[pl.BlockSpec((tm, LANES), lambda i: (i, 0)),
                   pl.BlockSpec((1, LANES), lambda i: (0, 0))],
        out_shape=[jax.ShapeDtypeStruct((T, LANES), I32), jax.ShapeDtypeStruct((1, LANES), F32)],
        scratch_shapes=[pltpu.VMEM((1, LANES), F32), pltpu.VMEM((1, LANES), F32)],
        compiler_params=_cparams("arbitrary"),
        name="route",
    )(route, cnt)


ZERO_ROWS = 64
DMA_UNROLL = 4


RING = 3


def _dispatch_kernel(padpos_ref, padcnt_ref, nu_ref, dest_ref, h_ref, xs_ref, zbuf, hbuf, sem, lsem, zsem,
                     *, td, nblk, nsteps):
    i = pl.program_id(0)

    @pl.when(i == 0)
    def _():
        zbuf[...] = jnp.zeros_like(zbuf)

        def per_expert(e, carry):
            off = padpos_ref[e]

            def zero_row(r):
                return pltpu.make_async_copy(zbuf.at[0], xs_ref.at[off + r], zsem)

            def z_issue(r, c):
                zero_row(r).start()
                return c

            def z_drain(r, c):
                zero_row(r).wait()
                return c

            lax.fori_loop(0, padcnt_ref[e], z_issue, 0)
            lax.fori_loop(0, padcnt_ref[e], z_drain, 0)
            return carry

        lax.fori_loop(0, N_EXPERTS, per_expert, 0)

        def zero_piece(j):
            row = pl.multiple_of(j * ZERO_ROWS, ZERO_ROWS)
            return pltpu.make_async_copy(zbuf, xs_ref.at[pl.ds(row, ZERO_ROWS)], zsem)

        per_blk = MOE_BLOCK // ZERO_ROWS

        def t_issue(j, c):
            zero_piece(j).start()
            return c

        def t_drain(j, c):
            zero_piece(j).wait()
            return c

        lax.fori_loop(nu_ref[0] * per_blk, nblk * per_blk, t_issue, 0)
        lax.fori_loop(nu_ref[0] * per_blk, nblk * per_blk, t_drain, 0)

    def load(step):
        s = lax.rem(step, RING)
        return pltpu.make_async_copy(h_ref.at[pl.ds(step * td, td)], hbuf.at[s], lsem.at[s])

    def wait_rows(step):
        s = lax.rem(step, RING)
        for _ in range(TOP_K):
            pltpu.make_async_copy(hbuf.at[s], xs_ref.at[pl.ds(0, td)], sem.at[s]).wait()

    @pl.when(i == 0)
    def _():
        load(0).start()
        if nsteps > 1:
            load(1).start()

    load(i).wait()
    slot = lax.rem(i, RING)

    def issue(r, carry):
        for k in range(TOP_K):
            pltpu.make_async_copy(hbuf.at[slot, r], xs_ref.at[dest_ref[r * TOP_K + k]],
                                  sem.at[slot]).start(priority=k % 2)
        return carry

    lax.fori_loop(0, td, issue, 0, unroll=DMA_UNROLL)

    @pl.when(i > 0)
    def _():
        wait_rows(i - 1)

    @pl.when(i + 2 < nsteps)
    def _():
        load(i + 2).start()

    @pl.when(i == nsteps - 1)
    def _():
        wait_rows(i)


def _dispatch(hf, dest_flat, padpos, padcnt, n_used, n_slots, td=512):
    T = hf.shape[0]
    nsteps = T // td
    grid_spec = pltpu.PrefetchScalarGridSpec(
        num_scalar_prefetch=3,
        grid=(nsteps,),
        in_specs=[pl.BlockSpec((td * TOP_K,), lambda i, *_: (i,), memory_space=pltpu.SMEM),
                  pl.BlockSpec(memory_space=pl.ANY)],
        out_specs=pl.BlockSpec(memory_space=pl.ANY),
        scratch_shapes=[pltpu.VMEM((ZERO_ROWS,) + TOKEN_TILE, F32), pltpu.VMEM((RING, td) + TOKEN_TILE, F32),
                        pltpu.SemaphoreType.DMA((RING,)), pltpu.SemaphoreType.DMA((RING,)),
                        pltpu.SemaphoreType.DMA(())],
    )
    return pl.pallas_call(
        functools.partial(_dispatch_kernel, td=td, nblk=n_slots // MOE_BLOCK, nsteps=nsteps),
        grid_spec=grid_spec,
        out_shape=jax.ShapeDtypeStruct((n_slots,) + TOKEN_TILE, F32),
        compiler_params=_cparams("arbitrary"),
        name="dispatch",
    )(padpos, padcnt, n_used, dest_flat, hf)


def _moe_kernel(be_ref, nu_ref, x_ref, wg_ref, bg_ref, wu_ref, bu_ref, wd_ref, bd_ref, o_ref):
    used = pl.program_id(0) < nu_ref[0]

    @pl.when(jnp.logical_not(used))
    def _():
        o_ref[...] = jnp.zeros_like(o_ref)

    @pl.when(used)
    def _():
        x = _load_token_tiles(x_ref, MOE_BLOCK).astype(BF16)
        gt = jnp.minimum(_dot(x, wg_ref[...]) + bg_ref[...], SWIGLU_LIMIT)
        up = jnp.clip(_dot(x, wu_ref[...]) + bu_ref[...], -SWIGLU_LIMIT, SWIGLU_LIMIT)
        act = (up + 1.0) * (gt * _sigmoid(SWIGLU_ALPHA * gt))
        _store_token_tiles(o_ref, _dot(act.astype(BF16), wd_ref[...]) + bd_ref[...])


def _moe(xs, block_expert, n_used, wg, bg, wu, bu, wd, bd):
    n_slots = xs.shape[0]
    nblk = n_slots // MOE_BLOCK
    blk = lambda i, be, nu: (jnp.minimum(i, nu[0] - 1), 0, 0)
    exp = lambda i, be, nu: (be[jnp.minimum(i, nu[0] - 1)], 0, 0)
    wspec = pl.BlockSpec((None, D_MODEL, D_MODEL), exp)
    bspec = pl.BlockSpec((None, 1, D_MODEL), exp)
    grid_spec = pltpu.PrefetchScalarGridSpec(
        num_scalar_prefetch=2,
        grid=(nblk,),
        in_specs=[pl.BlockSpec((MOE_BLOCK,) + TOKEN_TILE, blk), wspec, bspec, wspec, bspec, wspec, bspec],
        out_specs=pl.BlockSpec((MOE_BLOCK,) + TOKEN_TILE, lambda i, be, nu: (i, 0, 0)),
    )
    return pl.pallas_call(
        _moe_kernel,
        grid_spec=grid_spec,
        out_shape=jax.ShapeDtypeStruct((n_slots,) + TOKEN_TILE, F32),
        compiler_params=_cparams("arbitrary"),
        name="moe",
    )(block_expert, n_used, xs, wg, bg, wu, bu, wd, bd)


def _combine_kernel(dcur_ref, dnext_ref, x1_ref, r_ref, g_ref, ys_ref, o_ref, buf, sem, *, tc, nsteps):
    i = pl.program_id(0)
    slot = lax.rem(i, 2)

    def gather(d_ref, s):
        def body(r, carry):
            for k in range(TOP_K):
                pltpu.make_async_copy(ys_ref.at[d_ref[r * TOP_K + k]], buf.at[s, k, r],
                                      sem.at[s]).start(priority=k % 2)
            return carry

        lax.fori_loop(0, tc, body, 0, unroll=DMA_UNROLL)

    @pl.when(i == 0)
    def _():
        gather(dcur_ref, slot)

    @pl.when(i + 1 < nsteps)
    def _():
        gather(dnext_ref, 1 - slot)

    for k in range(TOP_K):
        pltpu.make_async_copy(ys_ref.at[pl.ds(0, tc)], buf.at[slot, k], sem.at[slot]).wait()
    gates = r_ref[...]
    acc = x1_ref[...]
    for k in range(TOP_K):
        acc += gates[:, k:k + 1] * _load_token_tiles(buf, tc, (slot * TOP_K + k) * tc)
    o_ref[...] = _rms(acc, g_ref[...])


def _combine(x1, route, dest_flat, ys, g_final, tc=512):
    T = x1.shape[0]
    nsteps = T // tc
    return pl.pallas_call(
        functools.partial(_combine_kernel, tc=tc, nsteps=nsteps),
        grid=(nsteps,),
        in_specs=[pl.BlockSpec((tc * TOP_K,), lambda i: (i,), memory_space=pltpu.SMEM),
                  pl.BlockSpec((tc * TOP_K,), lambda i: (jnp.minimum(i + 1, nsteps - 1),),
                               memory_space=pltpu.SMEM),
                  pl.BlockSpec((tc, D_MODEL), lambda i: (i, 0)),
                  pl.BlockSpec((tc, LANES), lambda i: (i, 0)),
                  pl.BlockSpec((1, D_MODEL), lambda i: (0, 0)),
                  pl.BlockSpec(memory_space=pl.ANY)],
        out_specs=pl.BlockSpec((tc, D_MODEL), lambda i: (i, 0)),
        out_shape=jax.ShapeDtypeStruct((T, D_MODEL), F32),
        scratch_shapes=[pltpu.VMEM((2, TOP_K, tc) + TOKEN_TILE, F32), pltpu.SemaphoreType.DMA((2,))],
        compiler_params=_cparams("arbitrary"),
        name="combine",
    )(dest_flat, dest_flat, x1, route, g_final, ys)


def _moe_layer(x1, hf, route, cnt, ew, g_final):
    T = x1.shape[0]
    n_assign = T * TOP_K
    nblk = n_assign // MOE_BLOCK + N_EXPERTS
    n_slots = nblk * MOE_BLOCK
    dest, pst = _route(route, jnp.pad(cnt[:, 0], (0, LANES - N_EXPERTS))[None, :])
    cnt_i = cnt[:, 0].astype(I32)
    pstart = pst[0, :N_EXPERTS].astype(I32)
    padded = (cnt_i + MOE_BLOCK - 1) // MOE_BLOCK * MOE_BLOCK
    pend = pstart + padded
    block_expert = jnp.minimum(
        jnp.sum(jnp.arange(nblk, dtype=I32)[:, None] * MOE_BLOCK >= pend[None, :], axis=1),
        N_EXPERTS - 1).astype(I32)
    n_used = (pend[-1] // MOE_BLOCK).reshape(1).astype(I32)
    dest_flat = dest[:, :TOP_K].reshape(n_assign)
    xs = _dispatch(hf, dest_flat, pstart + cnt_i, padded - cnt_i, n_used, n_slots)
    ys = _moe(xs, block_expert, n_used, *ew)
    return _combine(x1, route, dest_flat, ys, g_final)


def _trunk(x, p, bb):
    B, L, D = x.shape
    T = B * L
    x2d = x.reshape(T, D)
    v, x1g, x2g, u = _inproj(x2d, L, p["g_mix"], p["w_in"], p["b_in"], p["short_w"], p["short_b"])
    ya = _hyena(v, x1g, x2g, B, L, p["filt"], p["filt_bias"], bb)
    ybp = _s5(u, B, L, p["s5"])
    x1, hf, route, cnt = _mix(x2d, ya, ybp, p["w_glu"], p["b_glu"], p["g_hyena"], p["g_s5"], p["w_out"],
                              p["g_ffn"], p["w_router"], p["b_router"])
    y = _moe_layer(x1, hf, route, cnt, p["experts"], p["g_final"])
    return y.reshape(B, L, D)


def kernel(x_prompt, x_sample, g_mix, w_in, b_in, short_w, short_b, filt_w1, filt_b1, filt_w2, filt_b2, filt_w3, filt_b3, filt_freq, filt_bias, s5_lam_re, s5_lam_im, s5_log_dt, s5_b_re, s5_b_im, s5_c_re, s5_c_im, s5_d, w_glu, b_glu, g_hyena, g_s5, w_out, g_ffn, w_router, b_router, w_gate, b_gate, w_up, b_up, w_down, b_down, g_final):
    assert g_mix.shape[0] == 1, "one encoder layer"
    row = lambda a: a[0][None, :].astype(F32)
    p = dict(
        g_mix=row(g_mix), w_in=w_in[0].astype(BF16), b_in=row(b_in),
        short_w=short_w[0].astype(F32), short_b=row(short_b),
        filt=(filt_w1[0], filt_b1[0], filt_w2[0], filt_b2[0], filt_w3[0], filt_b3[0], filt_freq[0]),
        filt_bias=filt_bias[0].astype(F32),
        s5=_s5_operators(s5_lam_re[0], s5_lam_im[0], s5_log_dt[0], s5_b_re[0], s5_b_im[0],
                         s5_c_re[0], s5_c_im[0], s5_d[0]),
        w_glu=w_glu[0].astype(BF16), b_glu=row(b_glu), g_hyena=row(g_hyena), g_s5=row(g_s5),
        w_out=w_out[0].astype(BF16), g_ffn=row(g_ffn),
        w_router=_split_bf16(jnp.pad(w_router[0].astype(F32), ((0, 0), (0, LANES - N_EXPERTS)))),
        b_router=jnp.pad(b_router[0].astype(F32), (0, LANES - N_EXPERTS))[None, :],
        experts=(w_gate[0].astype(BF16), b_gate[0][:, None, :], w_up[0].astype(BF16),
                 b_up[0][:, None, :], w_down[0].astype(BF16), b_down[0][:, None, :]),
        g_final=g_final[None, :].astype(F32),
    )
    y_prompt = _trunk(x_prompt, p, bb=1)
    y_sample = _trunk(x_sample, p, bb=8)
    return (y_prompt, y_sample)
```

```python
import functools
import math

import numpy as np
import jax
import jax.numpy as jnp
from jax import lax
from jax.experimental import pallas as pl
from jax.experimental.pallas import tpu as pltpu

F32 = jnp.float32
BF16 = jnp.bfloat16
I32 = jnp.int32
HIGHEST = lax.Precision.HIGHEST

LANES = 128
SUBLANES = 8
D_MODEL = 1024
C_HY = 512
C_S5 = 512
D_IN = 3 * C_HY + C_S5
Q_HY = C_HY // LANES
FILTER_BANDS = 16
FILTER_WIDTH = 64
Z_PAD = 128
DECAY_MAX = math.log(1e-2) / 0.3
DECAY_MIN = math.log(1e-2) / 1.5
S5_G, S5_H, S5_P = 32, 16, 64
S5_CHUNK = 16
S5_UW = S5_CHUNK * S5_H
S5_SW = 2 * S5_P
S5_GB = LANES // S5_H
S5_NB = S5_G // S5_GB
S5_KW = S5_CHUNK * LANES
S5_XW = S5_GB * S5_SW
S5_STEP_ROWS = 512
LAMBDA_RE_MAX = -1e-4
N_EXPERTS = 32
TOP_K = 4
SWIGLU_LIMIT = 7.0
SWIGLU_ALPHA = 1.702
MOE_BLOCK = 512
RMS_EPS = 1e-6
VMEM_LIMIT_BYTES = 56 * 1024 * 1024


def _cparams(*sem):
    return pltpu.CompilerParams(dimension_semantics=sem, vmem_limit_bytes=VMEM_LIMIT_BYTES)


def _rms(x, g):
    return x * lax.rsqrt(jnp.mean(x * x, axis=-1, keepdims=True) + RMS_EPS) * g


def _sigmoid(x):
    return 1.0 / (1.0 + jnp.exp(-x))


def _dot(a, b):
    return jnp.dot(a, b, preferred_element_type=F32)


def _split_bf16(w):
    hi = w.astype(BF16)
    return hi, (w - hi.astype(F32)).astype(BF16)


def _lane_block(x, q):
    return x[:, q * LANES:(q + 1) * LANES]


def _as_rows(ref):
    return ref.reshape(math.prod(ref.shape[:-1]), LANES)


TOKEN_TILE = (D_MODEL // LANES, LANES)


def _load_token_tiles(ref, n, row0=0):
    rows = _as_rows(ref)
    nq = TOKEN_TILE[0]
    return jnp.concatenate([rows[pl.ds(row0 * nq + q, n, stride=nq), :] for q in range(nq)], axis=1)


def _store_token_tiles(ref, x):
    rows = _as_rows(ref)
    nq = TOKEN_TILE[0]
    for q in range(nq):
        rows[pl.ds(q, x.shape[0], stride=nq), :] = _lane_block(x, q)


def _inproj_kernel(x_ref, xp_ref, xn_ref, g_ref, w_ref, b_ref, sw_ref, sb_ref,
                   v_ref, x1_ref, x2_ref, u_ref, *, tm, seq):
    i = pl.program_id(0)
    g = g_ref[...]
    rows = jnp.concatenate([x_ref[...], xp_ref[...], xn_ref[...]], axis=0)
    proj = _dot(_rms(rows, g).astype(BF16), w_ref[...]) + b_ref[...]
    u_ref[...] = proj[:tm, 3 * C_HY:]
    z = proj[:tm, :3 * C_HY]
    zh = proj[tm:, :3 * C_HY]
    row0 = i * tm
    has_prev = lax.rem(row0, seq) != 0
    has_next = lax.rem(row0 + tm, seq) != 0
    zp = jnp.where(has_prev, zh[SUBLANES - 1:SUBLANES], 0.0)
    zn = jnp.where(has_next, zh[SUBLANES:SUBLANES + 1], 0.0)
    rid = lax.broadcasted_iota(I32, (tm, 1), 0)
    zm1 = jnp.where(rid == 0, zp, pltpu.roll(z, 1, 0))
    zp1 = jnp.where(rid == tm - 1, zn, pltpu.roll(z, tm - 1, 0))
    sw = sw_ref[...]
    o = zm1 * sw[0:1] + z * sw[1:2] + zp1 * sw[2:3] + sb_ref[...]
    for q in range(Q_HY):
        v_ref[q] = _lane_block(o, q)
        x1_ref[q] = _lane_block(o, Q_HY + q)
        x2_ref[q] = _lane_block(o, 2 * Q_HY + q)


def _inproj(x2d, seq, g_mix, w_in_bf, b_in, short_w, short_b, tm=512):
    T = x2d.shape[0]
    nb8 = T // SUBLANES
    tb = tm // SUBLANES
    hy = jax.ShapeDtypeStruct((Q_HY, T, LANES), F32)
    hy_spec = pl.BlockSpec((Q_HY, tm, LANES), lambda i: (0, i, 0))
    const = lambda i: (0, 0)
    return pl.pallas_call(
        functools.partial(_inproj_kernel, tm=tm, seq=seq),
        grid=(T // tm,),
        in_specs=[
            pl.BlockSpec((tm, D_MODEL), lambda i: (i, 0)),
            pl.BlockSpec((SUBLANES, D_MODEL), lambda i: (jnp.maximum(i * tb - 1, 0), 0)),
            pl.BlockSpec((SUBLANES, D_MODEL), lambda i: (jnp.minimum((i + 1) * tb, nb8 - 1), 0)),
            pl.BlockSpec((1, D_MODEL), const),
            pl.BlockSpec((D_MODEL, D_IN), const),
            pl.BlockSpec((1, D_IN), const),
            pl.BlockSpec((3, 3 * C_HY), const),
            pl.BlockSpec((1, 3 * C_HY), const),
        ],
        out_specs=[hy_spec, hy_spec, hy_spec, pl.BlockSpec((tm, C_S5), lambda i: (i, 0))],
        out_shape=[hy, hy, hy, jax.ShapeDtypeStruct((T, C_S5), F32)],
        compiler_params=_cparams("arbitrary"),
        name="inproj",
    )(x2d, x2d, x2d, g_mix, w_in_bf, b_in, short_w, short_b)


def _filter_kernel(z_ref, w1_ref, b1_ref, w2_ref, b2_ref, w3h_ref, w3l_ref, b3_ref, fr_ref, dl_ref,
                   k_ref, s_ref, *, tr, seq):
    i = pl.program_id(0)
    z = z_ref[...]
    fr = fr_ref[...]
    h = jnp.sin(fr[0:1] * (jnp.dot(z, w1_ref[...], precision=HIGHEST,
                                   preferred_element_type=F32) + b1_ref[...]))
    h = jnp.sin(fr[1:2] * (jnp.dot(h, w2_ref[...], precision=HIGHEST,
                                   preferred_element_type=F32) + b2_ref[...]))
    h_hi = h.astype(BF16)
    h_lo = (h - h_hi.astype(F32)).astype(BF16)
    h = _dot(h_hi, w3h_ref[...]) + _dot(h_lo, w3h_ref[...]) + _dot(h_hi, w3l_ref[...]) + b3_ref[...]
    h = h * jnp.exp(-z[:, 0:1] * dl_ref[...])
    rid = i * tr + lax.broadcasted_iota(I32, (tr, 1), 0)
    h = jnp.where(rid == seq, 0.0, h)
    for q in range(2 * Q_HY):
        k_ref[q] = _lane_block(h, q)

    @pl.when(i == 0)
    def _():
        s_ref[...] = jnp.zeros_like(s_ref)

    s_ref[...] += jnp.sum(jnp.abs(h), axis=0, keepdims=True)


def _filter_taps(seq, w1, b1, w2, b2, w3, b3, freq, tr=512):
    L = seq
    N = 2 * L
    n = jnp.arange(N, dtype=I32)
    pos = jnp.where(n < L, n, jnp.where(n == L, 0, N - n)).astype(F32)
    t = (pos * (1.0 / (L - 1)))[:, None]
    w = 2.0 * math.pi * pos / L
    bands = jnp.linspace(1e-4, FILTER_BANDS - 1, FILTER_BANDS, dtype=F32)
    ang = w[:, None] * bands[None, :]
    zc = jnp.concatenate([t, jnp.cos(ang), -jnp.sin(ang),
                          jnp.zeros((N, Z_PAD - 1 - 2 * FILTER_BANDS), F32)], axis=-1)
    w1p = jnp.pad(w1, ((0, Z_PAD - w1.shape[0]), (0, 0)))
    w3d = w3.reshape(FILTER_WIDTH, 2, 2 * C_HY).transpose(1, 0, 2)
    b3d = b3.reshape(2, 1, 2 * C_HY)
    deltas = jnp.abs(jnp.linspace(DECAY_MIN, DECAY_MAX, C_HY, dtype=F32))
    dl = jnp.tile(deltas, 2)[None, :]
    half = (N // tr) // 2
    const = lambda i: (0, 0)
    taps, sums = pl.pallas_call(
        functools.partial(_filter_kernel, tr=tr, seq=L),
        grid=(N // tr,),
        in_specs=[
            pl.BlockSpec((tr, Z_PAD), lambda i: (i, 0)),
            pl.BlockSpec((Z_PAD, FILTER_WIDTH), const),
            pl.BlockSpec((1, FILTER_WIDTH), const),
            pl.BlockSpec((FILTER_WIDTH, FILTER_WIDTH), const),
            pl.BlockSpec((1, FILTER_WIDTH), const),
            pl.BlockSpec((None, FILTER_WIDTH, 2 * C_HY), lambda i: (i // half, 0, 0)),
            pl.BlockSpec((None, FILTER_WIDTH, 2 * C_HY), lambda i: (i // half, 0, 0)),
            pl.BlockSpec((None, 1, 2 * C_HY), lambda i: (i // half, 0, 0)),
            pl.BlockSpec((2, FILTER_WIDTH), const),
            pl.BlockSpec((1, 2 * C_HY), const),
        ],
        out_specs=[pl.BlockSpec((2 * Q_HY, tr, LANES), lambda i: (0, i, 0)),
                   pl.BlockSpec((1, 2 * C_HY), const)],
        out_shape=[jax.ShapeDtypeStruct((2 * Q_HY, N, LANES), F32),
                   jax.ShapeDtypeStruct((1, 2 * C_HY), F32)],
        compiler_params=_cparams("arbitrary"),
        name="hyena_filter",
    )(zc, w1p, b1[None, :], w2, b2[None, :], *_split_bf16(w3d.astype(F32)), b3d, freq, dl)
    return taps, sums


def _split_n(N):
    n1 = {32768: 128, 4096: 64}.get(N)
    if n1 is None:
        n1 = 1 << (int(math.log2(N)) // 2)
    return n1, N // n1


def _dft_constants(N1, N2):
    N = N1 * N2
    k1 = np.arange(N1)[:, None]
    n1 = np.arange(N1)[None, :]
    ang = 2.0 * np.pi * ((k1 * n1) % N1) / N1
    fa_full = np.concatenate([np.cos(ang), -np.sin(ang)], axis=0)
    fa_half = fa_full[:, :N1 // 2]
    fa_inv = np.concatenate([np.cos(ang), -np.sin(ang)], axis=1)[:N1 // 2] / N
    k2 = np.arange(N2)[:, None]
    n2 = np.arange(N2)[None, :]
    a2 = 2.0 * np.pi * ((k2 * n2) % N2) / N2
    cr = jnp.asarray(np.cos(a2).astype(np.float32))[None]
    ci = jnp.asarray((-np.sin(a2)).astype(np.float32))[None]
    to = lambda a: jnp.asarray(a.astype(np.float32)).astype(BF16)
    kk = lax.broadcasted_iota(I32, (N1, N2), 0)
    nn = lax.broadcasted_iota(I32, (N1, N2), 1)
    ta = (2.0 * math.pi / N) * lax.rem(kk * nn, N).astype(F32)
    tr, ti = jnp.cos(ta), -jnp.sin(ta)
    stack = lambda re, im: jnp.concatenate(
        [jnp.concatenate([re, -im], axis=-1), jnp.concatenate([im, re], axis=-1)], axis=-2).astype(BF16)
    tc_r, tc_i = tr[:, None, :], ti[:, None, :]
    fk = stack(cr * tc_r - ci * tc_i, cr * tc_i + ci * tc_r)
    tr_r, tr_i = tr[:, :, None], ti[:, :, None]
    gk = stack(cr * tr_r - ci * tr_i, -(cr * tr_i + ci * tr_r))
    return dict(fa_full=to(fa_full), fa_half=to(fa_half), fa_inv=to(fa_inv), fk=fk, gk=gk)


U32 = jnp.uint32
_HI16 = 0xFFFF0000
_HALF16 = 0x8000


def _pack_spectrum(re, im):
    rb = lax.bitcast_convert_type(re, U32) + U32(_HALF16)
    ib = lax.bitcast_convert_type(im, U32) + U32(_HALF16)
    return (rb & U32(_HI16)) | (ib >> 16)


def _unpack_spectrum(w):
    return (lax.bitcast_convert_type(w & U32(_HI16), F32), lax.bitcast_convert_type(w << 16, F32))


N2_STEP = 16


def _dft_a_kernel(f_ref, x_ref, o_ref, *, rows):
    nq = x_ref.shape[0]
    x2d = _as_rows(x_ref)
    f = f_ref[...]
    half = f.shape[0] // 2
    for j in range(N2_STEP):
        xs = jnp.concatenate(
            [x2d[pl.ds(q * rows * N2_STEP + j, rows, stride=N2_STEP), :] for q in range(nq)], axis=1)
        r = _dot(f, xs.astype(BF16))
        w = _pack_spectrum(r[:half], r[half:])
        for q in range(nq):
            o_ref[q, j] = _lane_block(w, q)


def _dft_a(f, x):
    Q, Bt, K, N2, _ = x.shape
    R = f.shape[0] // 2
    return pl.pallas_call(
        functools.partial(_dft_a_kernel, rows=K),
        grid=(Bt, N2 // N2_STEP, Q // Q_HY),
        in_specs=[pl.BlockSpec((2 * R, K), lambda b, j, w: (0, 0)),
                  pl.BlockSpec((Q_HY, None, K, N2_STEP, LANES), lambda b, j, w: (w, b, 0, j, 0))],
        out_specs=pl.BlockSpec((Q_HY, None, N2_STEP, R, LANES), lambda b, j, w: (w, b, j, 0, 0)),
        out_shape=jax.ShapeDtypeStruct((Q, Bt, N2, R, LANES), U32),
        compiler_params=_cparams("arbitrary", "arbitrary", "arbitrary"),
        name="dft_a",
    )(f, x)


def _dft_a_inv_kernel(f_ref, z_ref, gate_ref, v_ref, bias_ref, *rest, rows, with_next):
    if with_next:
        fn_ref, o_ref, an_ref = rest
    else:
        (o_ref,) = rest
    g2d = _as_rows(gate_ref)
    v2d = _as_rows(v_ref)
    o2d = _as_rows(o_ref)
    f = f_ref[...]
    bias = bias_ref[...]
    for j in range(N2_STEP):
        zr, zi = _unpack_spectrum(jnp.concatenate([z_ref[q, j] for q in range(Q_HY)], axis=1))
        y = _dot(f, jnp.concatenate([zr, zi], axis=0).astype(BF16))
        outs = []
        for q in range(Q_HY):
            sl = pl.ds(q * rows * N2_STEP + j, rows, stride=N2_STEP)
            outs.append(g2d[sl, :] * (_lane_block(y, q) + v2d[sl, :] * _lane_block(bias, q)))
            o2d[sl, :] = outs[-1]
        if with_next:
            r = _dot(fn_ref[...], jnp.concatenate(outs, axis=1).astype(BF16))
            half = r.shape[0] // 2
            w = _pack_spectrum(r[:half], r[half:])
            for q in range(Q_HY):
                an_ref[q, j] = _lane_block(w, q)


def _dft_a_inv(f, z, gate, v, bias_row, f_next=None):
    Q, Bt, N2, N1, _ = z.shape
    K = f.shape[0]
    nat = pl.BlockSpec((Q, None, K, N2_STEP, LANES), lambda b, j: (0, b, 0, j, 0))
    spec = pl.BlockSpec((Q, None, N2_STEP, N1, LANES), lambda b, j: (0, b, j, 0, 0))
    with_next = f_next is not None
    in_specs = [pl.BlockSpec((K, 2 * N1), lambda b, j: (0, 0)), spec, nat, nat,
                pl.BlockSpec((1, C_HY), lambda b, j: (0, 0))]
    out_specs, out_shape = [nat], [jax.ShapeDtypeStruct((Q, Bt, K, N2, LANES), F32)]
    args = [f, z, gate, v, bias_row]
    if with_next:
        in_specs.append(pl.BlockSpec((2 * N1, K), lambda b, j: (0, 0)))
        out_specs.append(spec)
        out_shape.append(jax.ShapeDtypeStruct(z.shape, U32))
        args.append(f_next)
    return pl.pallas_call(
        functools.partial(_dft_a_inv_kernel, rows=K, with_next=with_next),
        grid=(Bt, N2 // N2_STEP),
        in_specs=in_specs,
        out_specs=out_specs,
        out_shape=out_shape,
        compiler_params=_cparams("arbitrary", "arbitrary"),
        name="dft_a_inv",
    )(*args)


C_STEP_Q = 2
K1_STEP = SUBLANES


def _dft_c_kernel(a_ref, k_ref, f_ref, g_ref, z_ref, *, bb, n2):
    per = n2 * K1_STEP
    a2d = _as_rows(a_ref)
    z2d = _as_rows(z_ref)
    cols = [(b, q) for b in range(bb) for q in range(C_STEP_Q)]

    def rows(b, q, kk):
        return pl.ds((q * bb + b) * per + kk, n2, stride=K1_STEP)

    for kk in range(K1_STEP):
        kr, ki = _unpack_spectrum(jnp.tile(k_ref[kk], (1, bb)))
        ar, ai = _unpack_spectrum(jnp.concatenate([a2d[rows(b, q, kk), :] for b, q in cols], axis=1))
        x = _dot(f_ref[kk], jnp.concatenate([ar, ai], axis=0).astype(BF16))
        xr = x[:n2]
        xi = x[n2:]
        yr = xr * kr - xi * ki
        yi = xr * ki + xi * kr
        zz = _dot(g_ref[kk], jnp.concatenate([yr, yi], axis=0).astype(BF16))
        w = _pack_spectrum(zz[:n2], zz[n2:])
        for c, (b, q) in enumerate(cols):
            z2d[rows(b, q, kk), :] = _lane_block(w, c)


def _dft_c(a5, khat, order, fk, gk, bb):
    Q, Bt, N2, N1, _ = a5.shape
    cw = C_STEP_Q * LANES
    ablk = pl.BlockSpec((C_STEP_Q, bb, N2, K1_STEP, LANES), lambda g, c, b: (c, b, 0, g, 0))
    mats = pl.BlockSpec((K1_STEP, 2 * N2, 2 * N2), lambda g, c, b: (g, 0, 0))
    return pl.pallas_call(
        functools.partial(_dft_c_kernel, bb=bb, n2=N2),
        grid=(N1 // K1_STEP, Q // C_STEP_Q, Bt // bb),
        in_specs=[ablk,
                  pl.BlockSpec((K1_STEP, N2, cw), lambda g, c, b: (g, 0, order * (C_HY // cw) + c)),
                  mats, mats],
        out_specs=ablk,
        out_shape=jax.ShapeDtypeStruct(a5.shape, U32),
        compiler_params=_cparams("arbitrary", "arbitrary", "arbitrary"),
        name="dft_c",
    )(a5, khat, fk, gk)


def _dft_c_filter_kernel(a_ref, f_ref, s_ref, o_ref, *, n2):
    per = n2 * K1_STEP
    a2d = _as_rows(a_ref)
    sc = 1.0 / (s_ref[...] + 1e-6)
    for kk in range(K1_STEP):
        ar, ai = _unpack_spectrum(jnp.concatenate(
            [a2d[pl.ds(q * per + kk, n2, stride=K1_STEP), :] for q in range(C_STEP_Q)], axis=1))
        x = _dot(f_ref[kk], jnp.concatenate([ar, ai], axis=0).astype(BF16))
        o_ref[kk] = _pack_spectrum(x[:n2] * sc, x[n2:] * sc)


def _dft_c_filter(a4, fk, sums):
    Q, N2, N1, _ = a4.shape
    cw = C_STEP_Q * LANES
    return pl.pallas_call(
        functools.partial(_dft_c_filter_kernel, n2=N2),
        grid=(N1 // K1_STEP, Q // C_STEP_Q),
        in_specs=[pl.BlockSpec((C_STEP_Q, N2, K1_STEP, LANES), lambda g, c: (c, 0, g, 0)),
                  pl.BlockSpec((K1_STEP, 2 * N2, 2 * N2), lambda g, c: (g, 0, 0)),
                  pl.BlockSpec((1, cw), lambda g, c: (0, c))],
        out_specs=pl.BlockSpec((K1_STEP, N2, cw), lambda g, c: (g, 0, c)),
        out_shape=jax.ShapeDtypeStruct((N1, N2, Q * LANES), U32),
        compiler_params=_cparams("arbitrary", "arbitrary"),
        name="dft_c_filter",
    )(a4, fk, sums)


def _hyena(v, x1, x2, B, seq, filt, filt_bias, bb):
    L = seq
    N = 2 * L
    N1, N2 = _split_n(N)
    cst = _dft_constants(N1, N2)
    taps, sums = _filter_taps(L, *filt)
    ka = _dft_a(cst["fa_full"], taps.reshape(2 * Q_HY, 1, N1, N2, LANES))
    khat = _dft_c_filter(ka.reshape(2 * Q_HY, N2, N1, LANES), cst["fk"], sums)
    nat = lambda a: a.reshape(Q_HY, B, N1 // 2, N2, LANES)
    cur = nat(v)
    a = _dft_a(cst["fa_half"], cur)
    for o, gate in enumerate((x1, x2)):
        z = _dft_c(a, khat, o, cst["fk"], cst["gk"], bb)
        if o == 0:
            cur, a = _dft_a_inv(cst["fa_inv"], z, nat(gate), cur, filt_bias[o][None, :], cst["fa_half"])
        else:
            (cur,) = _dft_a_inv(cst["fa_inv"], z, nat(gate), cur, filt_bias[o][None, :])
    return cur.reshape(Q_HY, B * L, LANES)


def _s5_operators(lam_re, lam_im, log_dt, b_re, b_im, c_re, c_im, d_skip):
    Tc, G, H, P = S5_CHUNK, S5_G, S5_H, S5_P
    lam = jnp.minimum(lam_re.astype(F32), LAMBDA_RE_MAX) + 1j * lam_im.astype(F32)
    dt = jnp.exp(log_dt.astype(F32))[..., None]
    lam_dt = lam * dt
    lam_bar = jnp.exp(lam_dt)
    b_bar = ((lam_bar - 1.0) / lam)[..., None] * (b_re.astype(F32) + 1j * b_im.astype(F32))
    c = c_re.astype(F32) + 1j * c_im.astype(F32)
    tau = jnp.arange(Tc + 1, dtype=F32)
    pw = jnp.exp(lam_dt[None] * tau[:, None, None, None])
    kk = jnp.einsum('dghp,tdgp,dgpk->dtghk', c, pw[:Tc], b_bar).real
    i = jnp.arange(Tc)
    lag = i[:, None] - i[None, :]
    kf = jnp.where((lag >= 0)[:, :, None, None, None], kk[0][jnp.clip(lag, 0, Tc - 1)], 0.0)
    kb = jnp.where((lag <= 0)[:, :, None, None, None], kk[1][jnp.clip(-lag, 0, Tc - 1)], 0.0)
    m = kf + kb
    eye = (lag == 0)[:, :, None, None, None] * jnp.eye(H, dtype=F32)[None, None, None]
    m = m + eye * d_skip.astype(F32).reshape(G, H)[None, None, :, :, None]
    clf = c[0][None] * pw[1:Tc + 1, 0][:, :, None, :]
    clb = c[1][None] * pw[Tc - i, 1][:, :, None, :]
    sf = pw[Tc - 1 - i, 0][:, :, :, None] * b_bar[0][None]
    sb = pw[i, 1][:, :, :, None] * b_bar[1][None]
    w_intra = m.transpose(2, 1, 4, 0, 3).reshape(G, Tc * H, Tc * H)
    st2y = lambda cl: jnp.concatenate([cl.real, -cl.imag], axis=-1).transpose(1, 3, 0, 2) \
        .reshape(G, 2 * P, Tc * H)
    u2s = lambda s: jnp.concatenate([s.real, s.imag], axis=2).transpose(1, 0, 3, 2) \
        .reshape(G, Tc * H, 2 * P)
    w_state = jnp.concatenate([u2s(sf), u2s(sb)], axis=-1)
    pair = lambda a: a.reshape(2, G // 2, 2 * P)
    lam16 = jnp.concatenate([pair(pw[Tc].real), pair(pw[Tc].imag)], axis=-1).reshape(2, G * 2 * P)
    col = np.arange(S5_KW)
    spread = (np.arange(S5_UW)[:, None] == (col // LANES * S5_H + col % S5_H)[None, :])
    return dict(w_intra=w_intra.astype(BF16), w_xf=st2y(clf).astype(BF16), w_xb=st2y(clb).astype(BF16),
                w_state=w_state.astype(BF16), lam16=lam16,
                spread=jnp.asarray(spread.astype(np.float32)).astype(BF16))


def _s5_rows(i, cl, B, ncc):
    return pl.ds(S5_CHUNK * cl + i, B, stride=S5_CHUNK * ncc)


def _s5_chunk_inputs(u2d, B, ncc):
    cols = []
    for i in range(S5_CHUNK):
        if B == 1:
            cols.append(u2d[pl.ds(i, ncc, stride=S5_CHUNK), :])
        else:
            cols.append(jnp.concatenate([u2d[_s5_rows(i, cl, B, ncc), :] for cl in range(ncc)], axis=0))
    return jnp.concatenate(cols, axis=1).astype(BF16)


def _s5_state_lane(a, part):
    return (a // 2) * 2 * S5_SW + part * S5_SW + (a % 2) * S5_P


def _s5_state_kernel(u_ref, w_ref, sf_ref, sb_ref, wblk, *, B, ncc):
    @pl.when(pl.program_id(1) == 0)
    def _():
        wblk[...] = jnp.zeros_like(wblk)
        for a in range(S5_GB):
            for j in range(S5_CHUNK):
                r0 = j * LANES + a * S5_H
                for d in range(2):
                    for part in range(2):
                        c0 = d * S5_XW + _s5_state_lane(a, part)
                        s0 = d * S5_SW + part * S5_P
                        wblk[r0:r0 + S5_H, c0:c0 + S5_P] = w_ref[a, j * S5_H:(j + 1) * S5_H, s0:s0 + S5_P]

    lhs = _s5_chunk_inputs(_as_rows(u_ref), B, ncc)
    r = _dot(lhs, wblk[...])
    sf_ref[...] = r[:, :S5_XW]
    sb_ref[...] = r[:, S5_XW:]


def _s5_state(u3, w_state, ncc):
    B, L, _ = u3.shape
    nch = L // S5_CHUNK
    out = jax.ShapeDtypeStruct((nch * B, S5_NB * S5_XW), F32)
    return pl.pallas_call(
        functools.partial(_s5_state_kernel, B=B, ncc=ncc),
        grid=(S5_NB, nch // ncc),
        in_specs=[pl.BlockSpec((B, S5_CHUNK * ncc, LANES), lambda q, t: (0, t, q)),
                  pl.BlockSpec((S5_GB, S5_UW, 2 * S5_SW), lambda q, t: (q, 0, 0))],
        out_specs=[pl.BlockSpec((ncc * B, S5_XW), lambda q, t: (t, q))] * 2,
        out_shape=[out, out],
        scratch_shapes=[pltpu.VMEM((S5_KW, 2 * S5_XW), BF16)],
        compiler_params=_cparams("arbitrary", "arbitrary"),
        name="s5_state",
    )(u3, w_state)


def _s5_scan_kernel(sf_ref, sb_ref, lam_ref, xf_ref, xb_ref, *, nch, rows):
    lb = sf_ref.shape[1]
    vr = max(rows, SUBLANES)
    lam_f, lam_b = lam_ref[0], lam_ref[1]

    def advance(x, lam, inc):
        out = []
        for u in range(0, lb, 2 * S5_SW):
            re, im = x[:, u:u + S5_SW], x[:, u + S5_SW:u + 2 * S5_SW]
            lr, li = lam[:, u:u + S5_SW], lam[:, u + S5_SW:u + 2 * S5_SW]
            out += [lr * re - li * im, lr * im + li * re]
        return jnp.concatenate(out, axis=1) + inc

    def body(c, carry):
        xf, xb = carry
        rf = pl.ds(pl.multiple_of(c * rows, rows), rows)
        xf_ref[rf, :] = xf[:rows]
        xf = advance(xf, lam_f, sf_ref[rf, :])
        rb = pl.ds(pl.multiple_of((nch - 1 - c) * rows, rows), rows)
        xb_ref[rb, :] = xb[:rows]
        xb = advance(xb, lam_b, sb_ref[rb, :])
        return xf, xb

    zero = jnp.zeros((vr, lb), F32)
    lax.fori_loop(0, nch, body, (zero, zero))


def _s5_scan(sf, sb, lam16, nch, rows, lb):
    R, lanes = sf.shape
    blk = pl.BlockSpec((R, lb), lambda j: (0, j))
    out = jax.ShapeDtypeStruct((R, lanes), F32)
    return pl.pallas_call(
        functools.partial(_s5_scan_kernel, nch=nch, rows=rows),
        grid=(lanes // lb,),
        in_specs=[blk, blk, pl.BlockSpec((2, 1, lb), lambda j: (0, 0, j))],
        out_specs=[blk, blk],
        out_shape=[out, out],
        compiler_params=_cparams("arbitrary"),
        name="s5_scan",
    )(sf, sb, lam16[:, None, :])


def _s5_out_kernel(u_ref, xf_ref, xb_ref, wm_ref, wf_ref, wb_ref, e_ref, y_ref, wm_blk, wf_blk, wb_blk,
                   *, B, ncc):
    @pl.when(pl.program_id(1) == 0)
    def _():
        lane = lax.broadcasted_iota(I32, (1, S5_KW), 1)
        slot = (lane // S5_H) % S5_GB
        e = e_ref[...]
        for a in range(S5_GB):
            own = slot == a
            ex = jnp.where(own, _dot(wm_ref[a], e), 0.0).astype(BF16)
            for j in range(S5_CHUNK):
                r0 = j * LANES + a * S5_H
                wm_blk[r0:r0 + S5_H, :] = ex[j * S5_H:(j + 1) * S5_H, :]
            for src, dst in ((wf_ref, wf_blk), (wb_ref, wb_blk)):
                ex = jnp.where(own, _dot(src[a], e), 0.0).astype(BF16)
                for part in range(2):
                    r0 = _s5_state_lane(a, part)
                    dst[r0:r0 + S5_P, :] = ex[part * S5_P:(part + 1) * S5_P, :]

    lhs = _s5_chunk_inputs(_as_rows(u_ref), B, ncc)
    acc = _dot(lhs, wm_blk[...])
    acc += _dot(xf_ref[...].astype(BF16), wf_blk[...])
    acc += _dot(xb_ref[...].astype(BF16), wb_blk[...])
    y2d = _as_rows(y_ref)
    for i in range(S5_CHUNK):
        piece = _lane_block(acc, i)
        if B == 1:
            y2d[pl.ds(i, ncc, stride=S5_CHUNK), :] = piece
        else:
            for cl in range(ncc):
                y2d[_s5_rows(i, cl, B, ncc), :] = piece[cl * B:(cl + 1) * B]


def _s5_out(u3, xf, xb, ops, ncc):
    B, L, _ = u3.shape
    nch = L // S5_CHUNK
    tok = pl.BlockSpec((B, S5_CHUNK * ncc, LANES), lambda q, t: (0, t, q))
    st = pl.BlockSpec((ncc * B, S5_XW), lambda q, t: (t, q))
    return pl.pallas_call(
        functools.partial(_s5_out_kernel, B=B, ncc=ncc),
        grid=(S5_NB, nch // ncc),
        in_specs=[tok, st, st,
                  pl.BlockSpec((S5_GB, S5_UW, S5_UW), lambda q, t: (q, 0, 0)),
                  pl.BlockSpec((S5_GB, S5_SW, S5_UW), lambda q, t: (q, 0, 0)),
                  pl.BlockSpec((S5_GB, S5_SW, S5_UW), lambda q, t: (q, 0, 0)),
                  pl.BlockSpec((S5_UW, S5_KW), lambda q, t: (0, 0))],
        out_specs=tok,
        out_shape=jax.ShapeDtypeStruct(u3.shape, F32),
        scratch_shapes=[pltpu.VMEM((S5_KW, S5_KW), BF16), pltpu.VMEM((S5_XW, S5_KW), BF16),
                        pltpu.VMEM((S5_XW, S5_KW), BF16)],
        compiler_params=_cparams("arbitrary", "arbitrary"),
        name="s5_out",
    )(u3, xf, xb, ops["w_intra"], ops["w_xf"], ops["w_xb"], ops["spread"])


def _s5(u, B, seq, ops):
    nch = seq // S5_CHUNK
    u3 = u.reshape(B, seq, C_S5)
    ncc = min(nch, S5_STEP_ROWS // B)
    sf, sb = _s5_state(u3, ops["w_state"], ncc)
    lb = 1024 if B == 1 else 256
    xf, xb = _s5_scan(sf, sb, ops["lam16"], nch, B, lb)
    return _s5_out(u3, xf, xb, ops, ncc).reshape(B * seq, C_S5)


def _mix_kernel(x_ref, ya_ref, yb_ref, wglu_ref, bglu_ref, gh_ref, gs_ref, wout_ref, gffn_ref,
                wrh_ref, wrl_ref, br_ref, x1_ref, hf_ref, route_ref, cnt_ref, *, tm):
    g = jax.nn.gelu(yb_ref[...])
    yb = g * _sigmoid(_dot(g.astype(BF16), wglu_ref[...]) + bglu_ref[...])
    ya = jnp.concatenate([ya_ref[q] for q in range(Q_HY)], axis=1)
    na = _rms(ya, gh_ref[...]).astype(BF16)
    nb = _rms(yb, gs_ref[...]).astype(BF16)
    mixed = _dot(na, wout_ref[:C_HY, :]) + _dot(nb, wout_ref[C_HY:, :])
    x1 = x_ref[...] + mixed
    x1_ref[...] = x1
    hf = _rms(x1, gffn_ref[...])
    _store_token_tiles(hf_ref, hf)
    hf_hi = hf.astype(BF16)
    hf_lo = (hf - hf_hi.astype(F32)).astype(BF16)
    logits = (_dot(hf_hi, wrh_ref[...]) + _dot(hf_lo, wrh_ref[...]) + _dot(hf_hi, wrl_ref[...])
              + br_ref[...])
    l = logits.T[:N_EXPERTS]
    row = lax.broadcasted_iota(I32, (N_EXPERTS, tm), 0)
    neg = jnp.float32(-jnp.inf)
    vals, idxs = [], []
    for _ in range(TOP_K):
        m = jnp.max(l, axis=0, keepdims=True)
        idx = jnp.min(jnp.where(l == m, row, N_EXPERTS), axis=0, keepdims=True)
        vals.append(m)
        idxs.append(idx)
        l = jnp.where(row == idx, neg, l)
    es = [jnp.exp(v - vals[0]) for v in vals]
    den = es[0] + es[1] + es[2] + es[3]
    packed = jnp.concatenate([e / den for e in es] + [i.astype(F32) for i in idxs]
                             + [jnp.zeros((LANES - 2 * TOP_K, tm), F32)], axis=0)
    route_ref[...] = packed.T

    @pl.when(pl.program_id(0) == 0)
    def _():
        cnt_ref[...] = jnp.zeros_like(cnt_ref)

    chosen = jnp.zeros((N_EXPERTS, tm), F32)
    for idx in idxs:
        chosen += (row == idx).astype(F32)
    cnt_ref[...] += jnp.sum(chosen, axis=1, keepdims=True)


def _mix(x2d, ya4, ybp, w_glu_bf, b_glu, g_hyena, g_s5, w_out_bf, g_ffn, w_router_p, b_router_p, tm=512):
    T = x2d.shape[0]
    const = lambda i: (0, 0)
    row = lambda w: pl.BlockSpec((tm, w), lambda i: (i, 0))
    return pl.pallas_call(
        functools.partial(_mix_kernel, tm=tm),
        grid=(T // tm,),
        in_specs=[row(D_MODEL), pl.BlockSpec((Q_HY, tm, LANES), lambda i: (0, i, 0)), row(C_S5),
                  pl.BlockSpec((C_S5, C_S5), const), pl.BlockSpec((1, C_S5), const),
                  pl.BlockSpec((1, C_HY), const), pl.BlockSpec((1, C_S5), const),
                  pl.BlockSpec((D_MODEL, D_MODEL), const), pl.BlockSpec((1, D_MODEL), const),
                  pl.BlockSpec((D_MODEL, LANES), const), pl.BlockSpec((D_MODEL, LANES), const),
                  pl.BlockSpec((1, LANES), const)],
        out_specs=[row(D_MODEL), pl.BlockSpec((tm,) + TOKEN_TILE, lambda i: (i, 0, 0)), row(LANES),
                   pl.BlockSpec((N_EXPERTS, 1), const)],
        out_shape=[jax.ShapeDtypeStruct((T, D_MODEL), F32), jax.ShapeDtypeStruct((T,) + TOKEN_TILE, F32),
                   jax.ShapeDtypeStruct((T, LANES), F32), jax.ShapeDtypeStruct((N_EXPERTS, 1), F32)],
        compiler_params=_cparams("arbitrary"),
        name="mix",
    )(x2d, ya4, ybp, w_glu_bf, b_glu, g_hyena, g_s5, w_out_bf, g_ffn, w_router_p[0], w_router_p[1],
      b_router_p)


def _route_kernel(r_ref, cnt_ref, dest_ref, pst_ref, carry, pstart, *, tm):
    i = pl.program_id(0)
    ids_t = r_ref[...].T[TOP_K:2 * TOP_K].astype(I32)
    row = lax.broadcasted_iota(I32, (N_EXPERTS, tm), 0)
    chosen_t = jnp.zeros((N_EXPERTS, tm), F32)
    for k in range(TOP_K):
        chosen_t += (row == ids_t[k:k + 1]).astype(F32)
    oh = jnp.concatenate([chosen_t, jnp.zeros((LANES - N_EXPERTS, tm), F32)], axis=0).T

    @pl.when(i == 0)
    def _():
        padded = jnp.floor((cnt_ref[...] + (MOE_BLOCK - 1)) * (1.0 / MOE_BLOCK)) * MOE_BLOCK
        a = lax.broadcasted_iota(I32, (LANES, LANES), 0)
        b = lax.broadcasted_iota(I32, (LANES, LANES), 1)
        excl = jnp.dot(jnp.broadcast_to(padded, (SUBLANES, LANES)), (a < b).astype(F32),
                       precision=HIGHEST, preferred_element_type=F32)
        pstart[...] = excl[0:1]
        pst_ref[...] = excl[0:1]
        carry[...] = jnp.zeros_like(carry)

    a = lax.broadcasted_iota(I32, (tm, tm), 0)
    b = lax.broadcasted_iota(I32, (tm, tm), 1)
    before = _dot((b < a).astype(BF16), oh.astype(BF16))
    base = before + carry[...] + pstart[...]
    base_t = base.T[:N_EXPERTS]
    picked =[jnp.sum(jnp.where(row == ids_t[k:k + 1], base_t, 0.0), axis=0, keepdims=True)
              for k in range(TOP_K)]
    out_t = jnp.concatenate(picked + [jnp.zeros((LANES - TOP_K, tm), F32)], axis=0)
    dest_ref[...] = out_t.T.astype(I32)
    carry[...] += jnp.sum(oh, axis=0, keepdims=True)


def _route(route, cnt, tm=512):
    T = route.shape[0]
    return pl.pallas_call(
        functools.partial(_route_kernel, tm=tm),
        grid=(T // tm,),
        in_specs=[pl.BlockSpec((tm, LANES), lambda i: (i, 0)),
                  pl.BlockSpec((1, LANES), lambda i: (0, 0))],
        out_specs=[pl.BlockSpec((tm, LANES), lambda i: (i, 0)),
                   pl.BlockSpec((1, LANES), lambda i: (0, 0))],
        out_shape=[jax.ShapeDtypeStruct((T, LANES), I32), jax.ShapeDtypeStruct((1, LANES), F32)],
        scratch_shapes=[pltpu.VMEM((1, LANES), F32), pltpu.VMEM((1, LANES), F32)],
        compiler_params=_cparams("arbitrary"),
        name="route",
    )(route, cnt)


ZERO_ROWS = 64
DMA_UNROLL = 4


RING = 3


def _dispatch_kernel(padpos_ref, padcnt_ref, nu_ref, dest_ref, h_ref, xs_ref, zbuf, hbuf, sem, lsem, zsem,
                     *, td, nblk, nsteps):
    i = pl.program_id(0)

    @pl.when(i == 0)
    def _():
        zbuf[...] = jnp.zeros_like(zbuf)

        def per_expert(e, carry):
            off = padpos_ref[e]

            def zero_row(r):
                return pltpu.make_async_copy(zbuf.at[0], xs_ref.at[off + r], zsem)

            def z_issue(r, c):
                zero_row(r).start()
                return c

            def z_drain(r, c):
                zero_row(r).wait()
                return c

            lax.fori_loop(0, padcnt_ref[e], z_issue, 0)
            lax.fori_loop(0, padcnt_ref[e], z_drain, 0)
            return carry

        lax.fori_loop(0, N_EXPERTS, per_expert, 0)

        def zero_piece(j):
            row = pl.multiple_of(j * ZERO_ROWS, ZERO_ROWS)
            return pltpu.make_async_copy(zbuf, xs_ref.at[pl.ds(row, ZERO_ROWS)], zsem)

        per_blk = MOE_BLOCK // ZERO_ROWS

        def t_issue(j, c):
            zero_piece(j).start()
            return c

        def t_drain(j, c):
            zero_piece(j).wait()
            return c

        lax.fori_loop(nu_ref[0] * per_blk, nblk * per_blk, t_issue, 0)
        lax.fori_loop(nu_ref[0] * per_blk, nblk * per_blk, t_drain, 0)

    def load(step):
        s = lax.rem(step, RING)
        return pltpu.make_async_copy(h_ref.at[pl.ds(step * td, td)], hbuf.at[s], lsem.at[s])

    def wait_rows(step):
        s = lax.rem(step, RING)
        for _ in range(TOP_K):
            pltpu.make_async_copy(hbuf.at[s], xs_ref.at[pl.ds(0, td)], sem.at[s]).wait()

    @pl.when(i == 0)
    def _():
        load(0).start()
        if nsteps > 1:
            load(1).start()

    load(i).wait()
    slot = lax.rem(i, RING)

    def issue(r, carry):
        for k in range(TOP_K):
            pltpu.make_async_copy(hbuf.at[slot, r], xs_ref.at[dest_ref[r * TOP_K + k]],
                                  sem.at[slot]).start(priority=k % 2)
        return carry

    lax.fori_loop(0, td, issue, 0, unroll=DMA_UNROLL)

    @pl.when(i > 0)
    def _():
        wait_rows(i - 1)

    @pl.when(i + 2 < nsteps)
    def _():
        load(i + 2).start()

    @pl.when(i == nsteps - 1)
    def _():
        wait_rows(i)


def _dispatch(hf, dest_flat, padpos, padcnt, n_used, n_slots, td=512):
    T = hf.shape[0]
    nsteps = T // td
    grid_spec = pltpu.PrefetchScalarGridSpec(
        num_scalar_prefetch=3,
        grid=(nsteps,),
        in_specs=[pl.BlockSpec((td * TOP_K,), lambda i, *_: (i,), memory_space=pltpu.SMEM),
                  pl.BlockSpec(memory_space=pl.ANY)],
        out_specs=pl.BlockSpec(memory_space=pl.ANY),
        scratch_shapes=[pltpu.VMEM((ZERO_ROWS,) + TOKEN_TILE, F32), pltpu.VMEM((RING, td) + TOKEN_TILE, F32),
                        pltpu.SemaphoreType.DMA((RING,)), pltpu.SemaphoreType.DMA((RING,)),
                        pltpu.SemaphoreType.DMA(())],
    )
    return pl.pallas_call(
        functools.partial(_dispatch_kernel, td=td, nblk=n_slots // MOE_BLOCK, nsteps=nsteps),
        grid_spec=grid_spec,
        out_shape=jax.ShapeDtypeStruct((n_slots,) + TOKEN_TILE, F32),
        compiler_params=_cparams("arbitrary"),
        name="dispatch",
    )(padpos, padcnt, n_used, dest_flat, hf)


def _moe_kernel(be_ref, nu_ref, x_ref, wg_ref, bg_ref, wu_ref, bu_ref, wd_ref, bd_ref, o_ref):
    used = pl.program_id(0) < nu_ref[0]

    @pl.when(jnp.logical_not(used))
    def _():
        o_ref[...] = jnp.zeros_like(o_ref)

    @pl.when(used)
    def _():
        x = _load_token_tiles(x_ref, MOE_BLOCK).astype(BF16)
        gt = jnp.minimum(_dot(x, wg_ref[...]) + bg_ref[...], SWIGLU_LIMIT)
        up = jnp.clip(_dot(x, wu_ref[...]) + bu_ref[...], -SWIGLU_LIMIT, SWIGLU_LIMIT)
        act = (up + 1.0) * (gt * _sigmoid(SWIGLU_ALPHA * gt))
        _store_token_tiles(o_ref, _dot(act.astype(BF16), wd_ref[...]) + bd_ref[...])


def _moe(xs, block_expert, n_used, wg, bg, wu, bu, wd, bd):
    n_slots = xs.shape[0]
    nblk = n_slots // MOE_BLOCK
    blk = lambda i, be, nu: (jnp.minimum(i, nu[0] - 1), 0, 0)
    exp = lambda i, be, nu: (be[jnp.minimum(i, nu[0] - 1)], 0, 0)
    wspec = pl.BlockSpec((None, D_MODEL, D_MODEL), exp)
    bspec = pl.BlockSpec((None, 1, D_MODEL), exp)
    grid_spec = pltpu.PrefetchScalarGridSpec(
        num_scalar_prefetch=2,
        grid=(nblk,),
        in_specs=[pl.BlockSpec((MOE_BLOCK,) + TOKEN_TILE, blk), wspec, bspec, wspec, bspec, wspec, bspec],
        out_specs=pl.BlockSpec((MOE_BLOCK,) + TOKEN_TILE, lambda i, be, nu: (i, 0, 0)),
    )
    return pl.pallas_call(
        _moe_kernel,
        grid_spec=grid_spec,
        out_shape=jax.ShapeDtypeStruct((n_slots,) + TOKEN_TILE, F32),
        compiler_params=_cparams("arbitrary"),
        name="moe",
    )(block_expert, n_used, xs, wg, bg, wu, bu, wd, bd)


def _combine_kernel(dcur_ref, dnext_ref, x1_ref, r_ref, g_ref, ys_ref, o_ref, buf, sem, *, tc, nsteps):
    i = pl.program_id(0)
    slot = lax.rem(i, 2)

    def gather(d_ref, s):
        def body(r, carry):
            for k in range(TOP_K):
                pltpu.make_async_copy(ys_ref.at[d_ref[r * TOP_K + k]], buf.at[s, k, r],
                                      sem.at[s]).start(priority=k % 2)
            return carry

        lax.fori_loop(0, tc, body, 0, unroll=DMA_UNROLL)

    @pl.when(i == 0)
    def _():
        gather(dcur_ref, slot)

    @pl.when(i + 1 < nsteps)
    def _():
        gather(dnext_ref, 1 - slot)

    for k in range(TOP_K):
        pltpu.make_async_copy(ys_ref.at[pl.ds(0, tc)], buf.at[slot, k], sem.at[slot]).wait()
    gates = r_ref[...]
    acc = x1_ref[...]
    for k in range(TOP_K):
        acc += gates[:, k:k + 1] * _load_token_tiles(buf, tc, (slot * TOP_K + k) * tc)
    o_ref[...] = _rms(acc, g_ref[...])


def _combine(x1, route, dest_flat, ys, g_final, tc=512):
    T = x1.shape[0]
    nsteps = T // tc
    return pl.pallas_call(
        functools.partial(_combine_kernel, tc=tc, nsteps=nsteps),
        grid=(nsteps,),
        in_specs=[pl.BlockSpec((tc * TOP_K,), lambda i: (i,), memory_space=pltpu.SMEM),
                  pl.BlockSpec((tc * TOP_K,), lambda i: (jnp.minimum(i + 1, nsteps - 1),),
                               memory_space=pltpu.SMEM),
                  pl.BlockSpec((tc, D_MODEL), lambda i: (i, 0)),
                  pl.BlockSpec((tc, LANES), lambda i: (i, 0)),
                  pl.BlockSpec((1, D_MODEL), lambda i: (0, 0)),
                  pl.BlockSpec(memory_space=pl.ANY)],
        out_specs=pl.BlockSpec((tc, D_MODEL), lambda i: (i, 0)),
        out_shape=jax.ShapeDtypeStruct((T, D_MODEL), F32),
        scratch_shapes=[pltpu.VMEM((2, TOP_K, tc) + TOKEN_TILE, F32), pltpu.SemaphoreType.DMA((2,))],
        compiler_params=_cparams("arbitrary"),
        name="combine",
    )(dest_flat, dest_flat, x1, route, g_final, ys)


def _moe_layer(x1, hf, route, cnt, ew, g_final):
    T = x1.shape[0]
    n_assign = T * TOP_K
    nblk = n_assign // MOE_BLOCK + N_EXPERTS
    n_slots = nblk * MOE_BLOCK
    dest, pst = _route(route, jnp.pad(cnt[:, 0], (0, LANES - N_EXPERTS))[None, :])
    cnt_i = cnt[:, 0].astype(I32)
    pstart = pst[0, :N_EXPERTS].astype(I32)
    padded = (cnt_i + MOE_BLOCK - 1) // MOE_BLOCK * MOE_BLOCK
    pend = pstart + padded
    block_expert = jnp.minimum(
        jnp.sum(jnp.arange(nblk, dtype=I32)[:, None] * MOE_BLOCK >= pend[None, :], axis=1),
        N_EXPERTS - 1).astype(I32)
    n_used = (pend[-1] // MOE_BLOCK).reshape(1).astype(I32)
    dest_flat = dest[:, :TOP_K].reshape(n_assign)
    xs = _dispatch(hf, dest_flat, pstart + cnt_i, padded - cnt_i, n_used, n_slots)
    ys = _moe(xs, block_expert, n_used, *ew)
    return _combine(x1, route, dest_flat, ys, g_final)


def _trunk(x, p, bb):
    B, L, D = x.shape
    T = B * L
    x2d = x.reshape(T, D)
    v, x1g, x2g, u = _inproj(x2d, L, p["g_mix"], p["w_in"], p["b_in"], p["short_w"], p["short_b"])
    ya = _hyena(v, x1g, x2g, B, L, p["filt"], p["filt_bias"], bb)
    ybp = _s5(u, B, L, p["s5"])
    x1, hf, route, cnt = _mix(x2d, ya, ybp, p["w_glu"], p["b_glu"], p["g_hyena"], p["g_s5"], p["w_out"],
                              p["g_ffn"], p["w_router"], p["b_router"])
    y = _moe_layer(x1, hf, route, cnt, p["experts"], p["g_final"])
    return y.reshape(B, L, D)


def kernel(x_prompt, x_sample, g_mix, w_in, b_in, short_w, short_b, filt_w1, filt_b1, filt_w2, filt_b2, filt_w3, filt_b3, filt_freq, filt_bias, s5_lam_re, s5_lam_im, s5_log_dt, s5_b_re, s5_b_im, s5_c_re, s5_c_im, s5_d, w_glu, b_glu, g_hyena, g_s5, w_out, g_ffn, w_router, b_router, w_gate, b_gate, w_up, b_up, w_down, b_down, g_final):
    assert g_mix.shape[0] == 1, "one encoder layer"
    row = lambda a: a[0][None, :].astype(F32)
    p = dict(
        g_mix=row(g_mix), w_in=w_in[0].astype(BF16), b_in=row(b_in),
        short_w=short_w[0].astype(F32), short_b=row(short_b),
        filt=(filt_w1[0], filt_b1[0], filt_w2[0], filt_b2[0], filt_w3[0], filt_b3[0], filt_freq[0]),
        filt_bias=filt_bias[0].astype(F32),
        s5=_s5_operators(s5_lam_re[0], s5_lam_im[0], s5_log_dt[0], s5_b_re[0], s5_b_im[0],
                         s5_c_re[0], s5_c_im[0], s5_d[0]),
        w_glu=w_glu[0].astype(BF16), b_glu=row(b_glu), g_hyena=row(g_hyena), g_s5=row(g_s5),
        w_out=w_out[0].astype(BF16), g_ffn=row(g_ffn),
        w_router=_split_bf16(jnp.pad(w_router[0].astype(F32), ((0, 0), (0, LANES - N_EXPERTS)))),
        b_router=jnp.pad(b_router[0].astype(F32), (0, LANES - N_EXPERTS))[None, :],
        experts=(w_gate[0].astype(BF16), b_gate[0][:, None, :], w_up[0].astype(BF16),
                 b_up[0][:, None, :], w_down[0].astype(BF16), b_down[0][:, None, :]),
        g_final=g_final[None, :].astype(F32),
    )
    y_prompt = _trunk(x_prompt, p, bb=1)
    y_sample = _trunk(x_sample, p, bb=8)
    return (y_prompt, y_sample)
```

```python
import functools
import math

import numpy as np
import jax
import jax.numpy as jnp
from jax import lax
from jax.experimental import pallas as pl
from jax.experimental.pallas import tpu as pltpu

F32 = jnp.float32
BF16 = jnp.bfloat16
I32 = jnp.int32
HIGHEST = lax.Precision.HIGHEST

LANES = 128
SUBLANES = 8
D_MODEL = 1024
C_HY = 512
C_S5 = 512
D_IN = 3 * C_HY + C_S5
Q_HY = C_HY // LANES
FILTER_BANDS = 16
FILTER_WIDTH = 64
Z_PAD = 128
DECAY_MAX = math.log(1e-2) / 0.3
DECAY_MIN = math.log(1e-2) / 1.5
S5_G, S5_H, S5_P = 32, 16, 64
S5_CHUNK = 16
S5_UW = S5_CHUNK * S5_H
S5_SW = 2 * S5_P
S5_GB = LANES // S5_H
S5_NB = S5_G // S5_GB
S5_KW = S5_CHUNK * LANES
S5_XW = S5_GB * S5_SW
S5_STEP_ROWS = 512
LAMBDA_RE_MAX = -1e-4
N_EXPERTS = 32
TOP_K = 4
SWIGLU_LIMIT = 7.0
SWIGLU_ALPHA = 1.702
MOE_BLOCK = 512
RMS_EPS = 1e-6
VMEM_LIMIT_BYTES = 56 * 1024 * 1024


def _cparams(*sem):
    return pltpu.CompilerParams(dimension_semantics=sem, vmem_limit_bytes=VMEM_LIMIT_BYTES)


def _rms(x, g):
    return x * lax.rsqrt(jnp.mean(x * x, axis=-1, keepdims=True) + RMS_EPS) * g


def _sigmoid(x):
    return 1.0 / (1.0 + jnp.exp(-x))


def _dot(a, b):
    return jnp.dot(a, b, preferred_element_type=F32)


def _split_bf16(w):
    hi = w.astype(BF16)
    return hi, (w - hi.astype(F32)).astype(BF16)


def _lane_block(x, q):
    return x[:, q * LANES:(q + 1) * LANES]


def _as_rows(ref):
    return ref.reshape(math.prod(ref.shape[:-1]), LANES)


TOKEN_TILE = (D_MODEL // LANES, LANES)


def _load_token_tiles(ref, n, row0=0):
    rows = _as_rows(ref)
    nq = TOKEN_TILE[0]
    return jnp.concatenate([rows[pl.ds(row0 * nq + q, n, stride=nq), :] for q in range(nq)], axis=1)


def _store_token_tiles(ref, x):
    rows = _as_rows(ref)
    nq = TOKEN_TILE[0]
    for q in range(nq):
        rows[pl.ds(q, x.shape[0], stride=nq), :] = _lane_block(x, q)


def _inproj_kernel(x_ref, xp_ref, xn_ref, g_ref, w_ref, b_ref, sw_ref, sb_ref,
                   v_ref, x1_ref, x2_ref, u_ref, *, tm, seq):
    i = pl.program_id(0)
    g = g_ref[...]
    rows = jnp.concatenate([x_ref[...], xp_ref[...], xn_ref[...]], axis=0)
    proj = _dot(_rms(rows, g).astype(BF16), w_ref[...]) + b_ref[...]
    u_ref[...] = proj[:tm, 3 * C_HY:]
    z = proj[:tm, :3 * C_HY]
    zh = proj[tm:, :3 * C_HY]
    row0 = i * tm
    has_prev = lax.rem(row0, seq) != 0
    has_next = lax.rem(row0 + tm, seq) != 0
    zp = jnp.where(has_prev, zh[SUBLANES - 1:SUBLANES], 0.0)
    zn = jnp.where(has_next, zh[SUBLANES:SUBLANES + 1], 0.0)
    rid = lax.broadcasted_iota(I32, (tm, 1), 0)
    zm1 = jnp.where(rid == 0, zp, pltpu.roll(z, 1, 0))
    zp1 = jnp.where(rid == tm - 1, zn, pltpu.roll(z, tm - 1, 0))
    sw = sw_ref[...]
    o = zm1 * sw[0:1] + z * sw[1:2] + zp1 * sw[2:3] + sb_ref[...]
    for q in range(Q_HY):
        v_ref[q] = _lane_block(o, q)
        x1_ref[q] = _lane_block(o, Q_HY + q)
        x2_ref[q] = _lane_block(o, 2 * Q_HY + q)


def _inproj(x2d, seq, g_mix, w_in_bf, b_in, short_w, short_b, tm=512):
    T = x2d.shape[0]
    nb8 = T // SUBLANES
    tb = tm // SUBLANES
    hy = jax.ShapeDtypeStruct((Q_HY, T, LANES), F32)
    hy_spec = pl.BlockSpec((Q_HY, tm, LANES), lambda i: (0, i, 0))
    const = lambda i: (0, 0)
    return pl.pallas_call(
        functools.partial(_inproj_kernel, tm=tm, seq=seq),
        grid=(T // tm,),
        in_specs=[
            pl.BlockSpec((tm, D_MODEL), lambda i: (i, 0)),
            pl.BlockSpec((SUBLANES, D_MODEL), lambda i: (jnp.maximum(i * tb - 1, 0), 0)),
            pl.BlockSpec((SUBLANES, D_MODEL), lambda i: (jnp.minimum((i + 1) * tb, nb8 - 1), 0)),
            pl.BlockSpec((1, D_MODEL), const),
            pl.BlockSpec((D_MODEL, D_IN), const),
            pl.BlockSpec((1, D_IN), const),
            pl.BlockSpec((3, 3 * C_HY), const),
            pl.BlockSpec((1, 3 * C_HY), const),
        ],
        out_specs=[hy_spec, hy_spec, hy_spec, pl.BlockSpec((tm, C_S5), lambda i: (i, 0))],
        out_shape=[hy, hy, hy, jax.ShapeDtypeStruct((T, C_S5), F32)],
        compiler_params=_cparams("arbitrary"),
        name="inproj",
    )(x2d, x2d, x2d, g_mix, w_in_bf, b_in, short_w, short_b)


def _filter_kernel(z_ref, w1_ref, b1_ref, w2_ref, b2_ref, w3h_ref, w3l_ref, b3_ref, fr_ref, dl_ref,
                   k_ref, s_ref, *, tr, seq):
    i = pl.program_id(0)
    z = z_ref[...]
    fr = fr_ref[...]
    h = jnp.sin(fr[0:1] * (jnp.dot(z, w1_ref[...], precision=HIGHEST,
                                   preferred_element_type=F32) + b1_ref[...]))
    h = jnp.sin(fr[1:2] * (jnp.dot(h, w2_ref[...], precision=HIGHEST,
                                   preferred_element_type=F32) + b2_ref[...]))
    h_hi = h.astype(BF16)
    h_lo = (h - h_hi.astype(F32)).astype(BF16)
    h = _dot(h_hi, w3h_ref[...]) + _dot(h_lo, w3h_ref[...]) + _dot(h_hi, w3l_ref[...]) + b3_ref[...]
    h = h * jnp.exp(-z[:, 0:1] * dl_ref[...])
    rid = i * tr + lax.broadcasted_iota(I32, (tr, 1), 0)
    h = jnp.where(rid == seq, 0.0, h)
    for q in range(2 * Q_HY):
        k_ref[q] = _lane_block(h, q)

    @pl.when(i == 0)
    def _():
        s_ref[...] = jnp.zeros_like(s_ref)

    s_ref[...] += jnp.sum(jnp.abs(h), axis=0, keepdims=True)


def _filter_taps(seq, w1, b1, w2, b2, w3, b3, freq, tr=512):
    L = seq
    N = 2 * L
    n = jnp.arange(N, dtype=I32)
    pos = jnp.where(n < L, n, jnp.where(n == L, 0, N - n)).astype(F32)
    t = (pos * (1.0 / (L - 1)))[:, None]
    w = 2.0 * math.pi * pos / L
    bands = jnp.linspace(1e-4, FILTER_BANDS - 1, FILTER_BANDS, dtype=F32)
    ang = w[:, None] * bands[None, :]
    zc = jnp.concatenate([t, jnp.cos(ang), -jnp.sin(ang),
                          jnp.zeros((N, Z_PAD - 1 - 2 * FILTER_BANDS), F32)], axis=-1)
    w1p = jnp.pad(w1, ((0, Z_PAD - w1.shape[0]), (0, 0)))
    w3d = w3.reshape(FILTER_WIDTH, 2, 2 * C_HY).transpose(1, 0, 2)
    b3d = b3.reshape(2, 1, 2 * C_HY)
    deltas = jnp.abs(jnp.linspace(DECAY_MIN, DECAY_MAX, C_HY, dtype=F32))
    dl = jnp.tile(deltas, 2)[None, :]
    half = (N // tr) // 2
    const = lambda i: (0, 0)
    taps, sums = pl.pallas_call(
        functools.partial(_filter_kernel, tr=tr, seq=L),
        grid=(N // tr,),
        in_specs=[
            pl.BlockSpec((tr, Z_PAD), lambda i: (i, 0)),
            pl.BlockSpec((Z_PAD, FILTER_WIDTH), const),
            pl.BlockSpec((1, FILTER_WIDTH), const),
            pl.BlockSpec((FILTER_WIDTH, FILTER_WIDTH), const),
            pl.BlockSpec((1, FILTER_WIDTH), const),
            pl.BlockSpec((None, FILTER_WIDTH, 2 * C_HY), lambda i: (i // half, 0, 0)),
            pl.BlockSpec((None, FILTER_WIDTH, 2 * C_HY), lambda i: (i // half, 0, 0)),
            pl.BlockSpec((None, 1, 2 * C_HY), lambda i: (i // half, 0, 0)),
            pl.BlockSpec((2, FILTER_WIDTH), const),
            pl.BlockSpec((1, 2 * C_HY), const),
        ],
        out_specs=[pl.BlockSpec((2 * Q_HY, tr, LANES), lambda i: (0, i, 0)),
                   pl.BlockSpec((1, 2 * C_HY), const)],
        out_shape=[jax.ShapeDtypeStruct((2 * Q_HY, N, LANES), F32),
                   jax.ShapeDtypeStruct((1, 2 * C_HY), F32)],
        compiler_params=_cparams("arbitrary"),
        name="hyena_filter",
    )(zc, w1p, b1[None, :], w2, b2[None, :], *_split_bf16(w3d.astype(F32)), b3d, freq, dl)
    return taps, sums


def _split_n(N):
    n1 = {32768: 128, 4096: 64}.get(N)
    if n1 is None:
        n1 = 1 << (int(math.log2(N)) // 2)
    return n1, N // n1


def _dft_constants(N1, N2):
    N = N1 * N2
    k1 = np.arange(N1)[:, None]
    n1 = np.arange(N1)[None, :]
    ang = 2.0 * np.pi * ((k1 * n1) % N1) / N1
    fa_full = np.concatenate([np.cos(ang), -np.sin(ang)], axis=0)
    fa_half = fa_full[:, :N1 // 2]
    fa_inv = np.concatenate([np.cos(ang), -np.sin(ang)], axis=1)[:N1 // 2] / N
    k2 = np.arange(N2)[:, None]
    n2 = np.arange(N2)[None, :]
    a2 = 2.0 * np.pi * ((k2 * n2) % N2) / N2
    cr = jnp.asarray(np.cos(a2).astype(np.float32))[None]
    ci = jnp.asarray((-np.sin(a2)).astype(np.float32))[None]
    to = lambda a: jnp.asarray(a.astype(np.float32)).astype(BF16)
    kk = lax.broadcasted_iota(I32, (N1, N2), 0)
    nn = lax.broadcasted_iota(I32, (N1, N2), 1)
    ta = (2.0 * math.pi / N) * lax.rem(kk * nn, N).astype(F32)
    tr, ti = jnp.cos(ta), -jnp.sin(ta)
    stack = lambda re, im: jnp.concatenate(
        [jnp.concatenate([re, -im], axis=-1), jnp.concatenate([im, re], axis=-1)], axis=-2).astype(BF16)
    tc_r, tc_i = tr[:, None, :], ti[:, None, :]
    fk = stack(cr * tc_r - ci * tc_i, cr * tc_i + ci * tc_r)
    tr_r, tr_i = tr[:, :, None], ti[:, :, None]
    gk = stack(cr * tr_r - ci * tr_i, -(cr * tr_i + ci * tr_r))
    return dict(fa_full=to(fa_full), fa_half=to(fa_half), fa_inv=to(fa_inv), fk=fk, gk=gk)


U32 = jnp.uint32
_HI16 = 0xFFFF0000
_HALF16 = 0x8000


def _pack_spectrum(re, im):
    rb = lax.bitcast_convert_type(re, U32) + U32(_HALF16)
    ib = lax.bitcast_convert_type(im, U32) + U32(_HALF16)
    return (rb & U32(_HI16)) | (ib >> 16)


def _unpack_spectrum(w):
    return (lax.bitcast_convert_type(w & U32(_HI16), F32), lax.bitcast_convert_type(w << 16, F32))


N2_STEP = 16


def _dft_a_kernel(f_ref, x_ref, o_ref, *, rows):
    nq = x_ref.shape[0]
    x2d = _as_rows(x_ref)
    f = f_ref[...]
    half = f.shape[0] // 2
    for j in range(N2_STEP):
        xs = jnp.concatenate(
            [x2d[pl.ds(q * rows * N2_STEP + j, rows, stride=N2_STEP), :] for q in range(nq)], axis=1)
        r = _dot(f, xs.astype(BF16))
        w = _pack_spectrum(r[:half], r[half:])
        for q in range(nq):
            o_ref[q, j] = _lane_block(w, q)


def _dft_a(f, x):
    Q, Bt, K, N2, _ = x.shape
    R = f.shape[0] // 2
    return pl.pallas_call(
        functools.partial(_dft_a_kernel, rows=K),
        grid=(Bt, N2 // N2_STEP, Q // Q_HY),
        in_specs=[pl.BlockSpec((2 * R, K), lambda b, j, w: (0, 0)),
                  pl.BlockSpec((Q_HY, None, K, N2_STEP, LANES), lambda b, j, w: (w, b, 0, j, 0))],
        out_specs=pl.BlockSpec((Q_HY, None, N2_STEP, R, LANES), lambda b, j, w: (w, b, j, 0, 0)),
        out_shape=jax.ShapeDtypeStruct((Q, Bt, N2, R, LANES), U32),
        compiler_params=_cparams("arbitrary", "arbitrary", "arbitrary"),
        name="dft_a",
    )(f, x)


def _dft_a_inv_kernel(f_ref, z_ref, gate_ref, v_ref, bias_ref, *rest, rows, with_next):
    if with_next:
        fn_ref, o_ref, an_ref = rest
    else:
        (o_ref,) = rest
    g2d = _as_rows(gate_ref)
    v2d = _as_rows(v_ref)
    o2d = _as_rows(o_ref)
    f = f_ref[...]
    bias = bias_ref[...]
    for j in range(N2_STEP):
        zr, zi = _unpack_spectrum(jnp.concatenate([z_ref[q, j] for q in range(Q_HY)], axis=1))
        y = _dot(f, jnp.concatenate([zr, zi], axis=0).astype(BF16))
        outs = []
        for q in range(Q_HY):
            sl = pl.ds(q * rows * N2_STEP + j, rows, stride=N2_STEP)
            outs.append(g2d[sl, :] * (_lane_block(y, q) + v2d[sl, :] * _lane_block(bias, q)))
            o2d[sl, :] = outs[-1]
        if with_next:
            r = _dot(fn_ref[...], jnp.concatenate(outs, axis=1).astype(BF16))
            half = r.shape[0] // 2
            w = _pack_spectrum(r[:half], r[half:])
            for q in range(Q_HY):
                an_ref[q, j] = _lane_block(w, q)


def _dft_a_inv(f, z, gate, v, bias_row, f_next=None):
    Q, Bt, N2, N1, _ = z.shape
    K = f.shape[0]
    nat = pl.BlockSpec((Q, None, K, N2_STEP, LANES), lambda b, j: (0, b, 0, j, 0))
    spec = pl.BlockSpec((Q, None, N2_STEP, N1, LANES), lambda b, j: (0, b, j, 0, 0))
    with_next = f_next is not None
    in_specs = [pl.BlockSpec((K, 2 * N1), lambda b, j: (0, 0)), spec, nat, nat,
                pl.BlockSpec((1, C_HY), lambda b, j: (0, 0))]
    out_specs, out_shape = [nat], [jax.ShapeDtypeStruct((Q, Bt, K, N2, LANES), F32)]
    args = [f, z, gate, v, bias_row]
    if with_next:
        in_specs.append(pl.BlockSpec((2 * N1, K), lambda b, j: (0, 0)))
        out_specs.append(spec)
        out_shape.append(jax.ShapeDtypeStruct(z.shape, U32))
        args.append(f_next)
    return pl.pallas_call(
        functools.partial(_dft_a_inv_kernel, rows=K, with_next=with_next),
        grid=(Bt, N2 // N2_STEP),
        in_specs=in_specs,
        out_specs=out_specs,
        out_shape=out_shape,
        compiler_params=_cparams("arbitrary", "arbitrary"),
        name="dft_a_inv",
    )(*args)


C_STEP_Q = 2
K1_STEP = SUBLANES


def _dft_c_kernel(a_ref, k_ref, f_ref, g_ref, z_ref, *, bb, n2):
    per = n2 * K1_STEP
    a2d = _as_rows(a_ref)
    z2d = _as_rows(z_ref)
    cols = [(b, q) for b in range(bb) for q in range(C_STEP_Q)]

    def rows(b, q, kk):
        return pl.ds((q * bb + b) * per + kk, n2, stride=K1_STEP)

    for kk in range(K1_STEP):
        kr, ki = _unpack_spectrum(jnp.tile(k_ref[kk], (1, bb)))
        ar, ai = _unpack_spectrum(jnp.concatenate([a2d[rows(b, q, kk), :] for b, q in cols], axis=1))
        x = _dot(f_ref[kk], jnp.concatenate([ar, ai], axis=0).astype(BF16))
        xr = x[:n2]
        xi = x[n2:]
        yr = xr * kr - xi * ki
        yi = xr * ki + xi * kr
        zz = _dot(g_ref[kk], jnp.concatenate([yr, yi], axis=0).astype(BF16))
        w = _pack_spectrum(zz[:n2], zz[n2:])
        for c, (b, q) in enumerate(cols):
            z2d[rows(b, q, kk), :] = _lane_block(w, c)


def _dft_c(a5, khat, order, fk, gk, bb):
    Q, Bt, N2, N1, _ = a5.shape
    cw = C_STEP_Q * LANES
    ablk = pl.BlockSpec((C_STEP_Q, bb, N2, K1_STEP, LANES), lambda g, c, b: (c, b, 0, g, 0))
    mats = pl.BlockSpec((K1_STEP, 2 * N2, 2 * N2), lambda g, c, b: (g, 0, 0))
    return pl.pallas_call(
        functools.partial(_dft_c_kernel, bb=bb, n2=N2),
        grid=(N1 // K1_STEP, Q // C_STEP_Q, Bt // bb),
        in_specs=[ablk,
                  pl.BlockSpec((K1_STEP, N2, cw), lambda g, c, b: (g, 0, order * (C_HY // cw) + c)),
                  mats, mats],
        out_specs=ablk,
        out_shape=jax.ShapeDtypeStruct(a5.shape, U32),
        compiler_params=_cparams("arbitrary", "arbitrary", "arbitrary"),
        name="dft_c",
    )(a5, khat, fk, gk)


def _dft_c_filter_kernel(a_ref, f_ref, s_ref, o_ref, *, n2):
    per = n2 * K1_STEP
    a2d = _as_rows(a_ref)
    sc = 1.0 / (s_ref[...] + 1e-6)
    for kk in range(K1_STEP):
        ar, ai = _unpack_spectrum(jnp.concatenate(
            [a2d[pl.ds(q * per + kk, n2, stride=K1_STEP), :] for q in range(C_STEP_Q)], axis=1))
        x = _dot(f_ref[kk], jnp.concatenate([ar, ai], axis=0).astype(BF16))
        o_ref[kk] = _pack_spectrum(x[:n2] * sc, x[n2:] * sc)


def _dft_c_filter(a4, fk, sums):
    Q, N2, N1, _ = a4.shape
    cw = C_STEP_Q * LANES
    return pl.pallas_call(
        functools.partial(_dft_c_filter_kernel, n2=N2),
        grid=(N1 // K1_STEP, Q // C_STEP_Q),
        in_specs=[pl.BlockSpec((C_STEP_Q, N2, K1_STEP, LANES), lambda g, c: (c, 0, g, 0)),
                  pl.BlockSpec((K1_STEP, 2 * N2, 2 * N2), lambda g, c: (g, 0, 0)),
                  pl.BlockSpec((1, cw), lambda g, c: (0, c))],
        out_specs=pl.BlockSpec((K1_STEP, N2, cw), lambda g, c: (g, 0, c)),
        out_shape=jax.ShapeDtypeStruct((N1, N2, Q * LANES), U32),
        compiler_params=_cparams("arbitrary", "arbitrary"),
        name="dft_c_filter",
    )(a4, fk, sums)


def _hyena(v, x1, x2, B, seq, filt, filt_bias, bb):
    L = seq
    N = 2 * L
    N1, N2 = _split_n(N)
    cst = _dft_constants(N1, N2)
    taps, sums = _filter_taps(L, *filt)
    ka = _dft_a(cst["fa_full"], taps.reshape(2 * Q_HY, 1, N1, N2, LANES))
    khat = _dft_c_filter(ka.reshape(2 * Q_HY, N2, N1, LANES), cst["fk"], sums)
    nat = lambda a: a.reshape(Q_HY, B, N1 // 2, N2, LANES)
    cur = nat(v)
    a = _dft_a(cst["fa_half"], cur)
    for o, gate in enumerate((x1, x2)):
        z = _dft_c(a, khat, o, cst["fk"], cst["gk"], bb)
        if o == 0:
            cur, a = _dft_a_inv(cst["fa_inv"], z, nat(gate), cur, filt_bias[o][None, :], cst["fa_half"])
        else:
            (cur,) = _dft_a_inv(cst["fa_inv"], z, nat(gate), cur, filt_bias[o][None, :])
    return cur.reshape(Q_HY, B * L, LANES)


def _s5_operators(lam_re, lam_im, log_dt, b_re, b_im, c_re, c_im, d_skip):
    Tc, G, H, P = S5_CHUNK, S5_G, S5_H, S5_P
    lam = jnp.minimum(lam_re.astype(F32), LAMBDA_RE_MAX) + 1j * lam_im.astype(F32)
    dt = jnp.exp(log_dt.astype(F32))[..., None]
    lam_dt = lam * dt
    lam_bar = jnp.exp(lam_dt)
    b_bar = ((lam_bar - 1.0) / lam)[..., None] * (b_re.astype(F32) + 1j * b_im.astype(F32))
    c = c_re.astype(F32) + 1j * c_im.astype(F32)
    tau = jnp.arange(Tc + 1, dtype=F32)
    pw = jnp.exp(lam_dt[None] * tau[:, None, None, None])
    kk = jnp.einsum('dghp,tdgp,dgpk->dtghk', c, pw[:Tc], b_bar).real
    i = jnp.arange(Tc)
    lag = i[:, None] - i[None, :]
    kf = jnp.where((lag >= 0)[:, :, None, None, None], kk[0][jnp.clip(lag, 0, Tc - 1)], 0.0)
    kb = jnp.where((lag <= 0)[:, :, None, None, None], kk[1][jnp.clip(-lag, 0, Tc - 1)], 0.0)
    m = kf + kb
    eye = (lag == 0)[:, :, None, None, None] * jnp.eye(H, dtype=F32)[None, None, None]
    m = m + eye * d_skip.astype(F32).reshape(G, H)[None, None, :, :, None]
    clf = c[0][None] * pw[1:Tc + 1, 0][:, :, None, :]
    clb = c[1][None] * pw[Tc - i, 1][:, :, None, :]
    sf = pw[Tc - 1 - i, 0][:, :, :, None] * b_bar[0][None]
    sb = pw[i, 1][:, :, :, None] * b_bar[1][None]
    w_intra = m.transpose(2, 1, 4, 0, 3).reshape(G, Tc * H, Tc * H)
    st2y = lambda cl: jnp.concatenate([cl.real, -cl.imag], axis=-1).transpose(1, 3, 0, 2) \
        .reshape(G, 2 * P, Tc * H)
    u2s = lambda s: jnp.concatenate([s.real, s.imag], axis=2).transpose(1, 0, 3, 2) \
        .reshape(G, Tc * H, 2 * P)
    w_state = jnp.concatenate([u2s(sf), u2s(sb)], axis=-1)
    pair = lambda a: a.reshape(2, G // 2, 2 * P)
    lam16 = jnp.concatenate([pair(pw[Tc].real), pair(pw[Tc].imag)], axis=-1).reshape(2, G * 2 * P)
    col = np.arange(S5_KW)
    spread = (np.arange(S5_UW)[:, None] == (col // LANES * S5_H + col % S5_H)[None, :])
    return dict(w_intra=w_intra.astype(BF16), w_xf=st2y(clf).astype(BF16), w_xb=st2y(clb).astype(BF16),
                w_state=w_state.astype(BF16), lam16=lam16,
                spread=jnp.asarray(spread.astype(np.float32)).astype(BF16))


def _s5_rows(i, cl, B, ncc):
    return pl.ds(S5_CHUNK * cl + i, B, stride=S5_CHUNK * ncc)


def _s5_chunk_inputs(u2d, B, ncc):
    cols = []
    for i in range(S5_CHUNK):
        if B == 1:
            cols.append(u2d[pl.ds(i, ncc, stride=S5_CHUNK), :])
        else:
            cols.append(jnp.concatenate([u2d[_s5_rows(i, cl, B, ncc), :] for cl in range(ncc)], axis=0))
    return jnp.concatenate(cols, axis=1).astype(BF16)


def _s5_state_lane(a, part):
    return (a // 2) * 2 * S5_SW + part * S5_SW + (a % 2) * S5_P


def _s5_state_kernel(u_ref, w_ref, sf_ref, sb_ref, lhs_ref, wblk, *, B, ncc):
    @pl.when(pl.program_id(1) == 0)
    def _():
        wblk[...] = jnp.zeros_like(wblk)
        for a in range(S5_GB):
            for j in range(S5_CHUNK):
                r0 = j * LANES + a * S5_H
                for d in range(2):
                    for part in range(2):
                        c0 = d * S5_XW + _s5_state_lane(a, part)
                        s0 = d * S5_SW + part * S5_P
                        wblk[r0:r0 + S5_H, c0:c0 + S5_P] = w_ref[a, j * S5_H:(j + 1) * S5_H, s0:s0 + S5_P]

    lhs = _s5_chunk_inputs(_as_rows(u_ref), B, ncc)
    lhs_ref[...] = lhs
    r = _dot(lhs, wblk[...])
    sf_ref[...] = r[:, :S5_XW]
    sb_ref[...] = r[:, S5_XW:]


def _s5_state(u3, w_state, ncc):
    B, L, _ = u3.shape
    nch = L // S5_CHUNK
    out = jax.ShapeDtypeStruct((nch * B, S5_NB * S5_XW), F32)
    return pl.pallas_call(
        functools.partial(_s5_state_kernel, B=B, ncc=ncc),
        grid=(S5_NB, nch // ncc),
        in_specs=[pl.BlockSpec((B, S5_CHUNK * ncc, LANES), lambda q, t: (0, t, q)),
                  pl.BlockSpec((S5_GB, S5_UW, 2 * S5_SW), lambda q, t: (q, 0, 0))],
        out_specs=[pl.BlockSpec((ncc * B, S5_XW), lambda q, t: (t, q))] * 2
        + [pl.BlockSpec((ncc * B, S5_KW), lambda q, t: (t, q))],
        out_shape=[out, out, jax.ShapeDtypeStruct((nch * B, S5_NB * S5_KW), BF16)],
        scratch_shapes=[pltpu.VMEM((S5_KW, 2 * S5_XW), BF16)],
        compiler_params=_cparams("arbitrary", "arbitrary"),
        name="s5_state",
    )(u3, w_state)


def _s5_scan_kernel(sf_ref, sb_ref, lam_ref, xf_ref, xb_ref, *, nch, rows):
    lb = sf_ref.shape[1]
    vr = max(rows, SUBLANES)
    lam_f, lam_b = lam_ref[0], lam_ref[1]

    def advance(x, lam, inc):
        out = []
        for u in range(0, lb, 2 * S5_SW):
            re, im = x[:, u:u + S5_SW], x[:, u + S5_SW:u + 2 * S5_SW]
            lr, li = lam[:, u:u + S5_SW], lam[:, u + S5_SW:u + 2 * S5_SW]
            out += [lr * re - li * im, lr * im + li * re]
        return jnp.concatenate(out, axis=1) + inc

    def body(c, carry):
        xf, xb = carry
        rf = pl.ds(pl.multiple_of(c * rows, rows), rows)
        xf_ref[rf, :] = xf[:rows]
        xf = advance(xf, lam_f, sf_ref[rf, :])
        rb = pl.ds(pl.multiple_of((nch - 1 - c) * rows, rows), rows)
        xb_ref[rb, :] = xb[:rows]
        xb = advance(xb, lam_b, sb_ref[rb, :])
        return xf, xb

    zero = jnp.zeros((vr, lb), F32)
    lax.fori_loop(0, nch, body, (zero, zero))


def _s5_scan(sf, sb, lam16, nch, rows, lb):
    R, lanes = sf.shape
    blk = pl.BlockSpec((R, lb), lambda j: (0, j))
    out = jax.ShapeDtypeStruct((R, lanes), F32)
    return pl.pallas_call(
        functools.partial(_s5_scan_kernel, nch=nch, rows=rows),
        grid=(lanes // lb,),
        in_specs=[blk, blk, pl.BlockSpec((2, 1, lb), lambda j: (0, 0, j))],
        out_specs=[blk, blk],
        out_shape=[out, out],
        compiler_params=_cparams("arbitrary"),
        name="s5_scan",
    )(sf, sb, lam16[:, None, :])


def _s5_out_kernel(lhs_ref, xf_ref, xb_ref, wm_ref, wf_ref, wb_ref, e_ref, y_ref, wm_blk, wf_blk, wb_blk,
                   *, B, ncc):
    @pl.when(pl.program_id(1) == 0)
    def _():
        lane = lax.broadcasted_iota(I32, (1, S5_KW), 1)
        slot = (lane // S5_H) % S5_GB
        e = e_ref[...]
        for a in range(S5_GB):
            own = slot == a
            ex = jnp.where(own, _dot(wm_ref[a], e), 0.0).astype(BF16)
            for j in range(S5_CHUNK):
                r0 = j * LANES + a * S5_H
                wm_blk[r0:r0 + S5_H, :] = ex[j * S5_H:(j + 1) * S5_H, :]
            for src, dst in ((wf_ref, wf_blk), (wb_ref, wb_blk)):
                ex = jnp.where(own, _dot(src[a], e), 0.0).astype(BF16)
                for part in range(2):
                    r0 = _s5_state_lane(a, part)
                    dst[r0:r0 + S5_P, :] = ex[part * S5_P:(part + 1) * S5_P, :]

    acc = _dot(lhs_ref[...], wm_blk[...])
    acc += _dot(xf_ref[...].astype(BF16), wf_blk[...])
    acc += _dot(xb_ref[...].astype(BF16), wb_blk[...])
    y2d = _as_rows(y_ref)
    for i in range(S5_CHUNK):
        piece = _lane_block(acc, i)
        if B == 1:
            y2d[pl.ds(i, ncc, stride=S5_CHUNK), :] = piece
        else:
            for cl in range(ncc):
                y2d[_s5_rows(i, cl, B, ncc), :] = piece[cl * B:(cl + 1) * B]


def _s5_out(lhs, xf, xb, ops, B, L, ncc):
    nch = L // S5_CHUNK
    tok = pl.BlockSpec((B, S5_CHUNK * ncc, LANES), lambda q, t: (0, t, q))
    st = pl.BlockSpec((ncc * B, S5_XW), lambda q, t: (t, q))
    return pl.pallas_call(
        functools.partial(_s5_out_kernel, B=B, ncc=ncc),
        grid=(S5_NB, nch // ncc),
        in_specs=[pl.BlockSpec((ncc * B, S5_KW), lambda q, t: (t, q)), st, st,
                  pl.BlockSpec((S5_GB, S5_UW, S5_UW), lambda q, t: (q, 0, 0)),
                  pl.BlockSpec((S5_GB, S5_SW, S5_UW), lambda q, t: (q, 0, 0)),
                  pl.BlockSpec((S5_GB, S5_SW, S5_UW), lambda q, t: (q, 0, 0)),
                  pl.BlockSpec((S5_UW, S5_KW), lambda q, t: (0, 0))],
        out_specs=tok,
        out_shape=jax.ShapeDtypeStruct((B, L, C_S5), F32),
        scratch_shapes=[pltpu.VMEM((S5_KW, S5_KW), BF16), pltpu.VMEM((S5_XW, S5_KW), BF16),
                        pltpu.VMEM((S5_XW, S5_KW), BF16)],
        compiler_params=_cparams("arbitrary", "arbitrary"),
        name="s5_out",
    )(lhs, xf, xb, ops["w_intra"], ops["w_xf"], ops["w_xb"], ops["spread"])


def _s5(u, B, seq, ops):
    nch = seq // S5_CHUNK
    u3 = u.reshape(B, seq, C_S5)
    ncc = min(nch, S5_STEP_ROWS // B)
    sf, sb, lhs = _s5_state(u3, ops["w_state"], ncc)
    lb = 1024 if B == 1 else 256
    xf, xb = _s5_scan(sf, sb, ops["lam16"], nch, B, lb)
    return _s5_out(lhs, xf, xb, ops, B, seq, ncc).reshape(B * seq, C_S5)


def _mix_kernel(x_ref, ya_ref, yb_ref, wglu_ref, bglu_ref, gh_ref, gs_ref, wout_ref, gffn_ref,
                wrh_ref, wrl_ref, br_ref, x1_ref, hf_ref, route_ref, cnt_ref, *, tm):
    g = jax.nn.gelu(yb_ref[...])
    yb = g * _sigmoid(_dot(g.astype(BF16), wglu_ref[...]) + bglu_ref[...])
    ya = jnp.concatenate([ya_ref[q] for q in range(Q_HY)], axis=1)
    na = _rms(ya, gh_ref[...]).astype(BF16)
    nb = _rms(yb, gs_ref[...]).astype(BF16)
    mixed = _dot(na, wout_ref[:C_HY, :]) + _dot(nb, wout_ref[C_HY:, :])
    x1 = x_ref[...] + mixed
    x1_ref[...] = x1
    hf = _rms(x1, gffn_ref[...])
    _store_token_tiles(hf_ref, hf)
    hf_hi = hf.astype(BF16)
    hf_lo = (hf - hf_hi.astype(F32)).astype(BF16)
    logits = (_dot(hf_hi, wrh_ref[...]) + _dot(hf_lo, wrh_ref[...]) + _dot(hf_hi, wrl_ref[...])
              + br_ref[...])
    l = logits.T[:N_EXPERTS]
    row = lax.broadcasted_iota(I32, (N_EXPERTS, tm), 0)
    neg = jnp.float32(-jnp.inf)
    vals, idxs = [], []
    for _ in range(TOP_K):
        m = jnp.max(l, axis=0, keepdims=True)
        idx = jnp.min(jnp.where(l == m, row, N_EXPERTS), axis=0, keepdims=True)
        vals.append(m)
        idxs.append(idx)
        l = jnp.where(row == idx, neg, l)
    es = [jnp.exp(v - vals[0]) for v in vals]
    den = es[0] + es[1] + es[2] + es[3]
    packed = jnp.concatenate([e / den for e in es] + [i.astype(F32) for i in idxs]
                             + [jnp.zeros((LANES - 2 * TOP_K, tm), F32)], axis=0)
    route_ref[...] = packed.T

    @pl.when(pl.program_id(0) == 0)
    def _():
        cnt_ref[...] = jnp.zeros_like(cnt_ref)

    chosen = jnp.zeros((N_EXPERTS, tm), F32)
    for idx in idxs:
        chosen += (row == idx).astype(F32)
    cnt_ref[...] += jnp.sum(chosen, axis=1, keepdims=True)


def _mix(x2d, ya4, ybp, w_glu_bf, b_glu, g_hyena, g_s5, w_out_bf, g_ffn, w_router_p, b_router_p, tm=512):
    T = x2d.shape[0]
    const = lambda i: (0, 0)
    row = lambda w: pl.BlockSpec((tm, w), lambda i: (i, 0))
    return pl.pallas_call(
        functools.partial(_mix_kernel, tm=tm),
        grid=(T // tm,),
        in_specs=[row(D_MODEL), pl.BlockSpec((Q_HY, tm, LANES), lambda i: (0, i, 0)), row(C_S5),
                  pl.BlockSpec((C_S5, C_S5), const), pl.BlockSpec((1, C_S5), const),
                  pl.BlockSpec((1, C_HY), const), pl.BlockSpec((1, C_S5), const),
                  pl.BlockSpec((D_MODEL, D_MODEL), const), pl.BlockSpec((1, D_MODEL), const),
                  pl.BlockSpec((D_MODEL, LANES), const), pl.BlockSpec((D_MODEL, LANES), const),
                  pl.BlockSpec((1, LANES), const)],
        out_specs=[row(D_MODEL), pl.BlockSpec((tm,) + TOKEN_TILE, lambda i: (i, 0, 0)), row(LANES),
                   pl.BlockSpec((N_EXPERTS, 1), const)],
        out_shape=[jax.ShapeDtypeStruct((T, D_MODEL), F32), jax.ShapeDtypeStruct((T,) + TOKEN_TILE, F32),
                   jax.ShapeDtypeStruct((T, LANES), F32), jax.ShapeDtypeStruct((N_EXPERTS, 1), F32)],
        compiler_params=_cparams("arbitrary"),
        name="mix",
    )(x2d, ya4, ybp, w_glu_bf, b_glu, g_hyena, g_s5, w_out_bf, g_ffn, w_router_p[0], w_router_p[1],
      b_router_p)


def _route_kernel(r_ref, cnt_ref, dest_ref, pst_ref, carry, pstart, *, tm):
    i = pl.program_id(0)
    ids_t = r_ref[...].T[TOP_K:2 * TOP_K].astype(I32)
    row = lax.broadcasted_iota(I32, (N_EXPERTS, tm), 0)
    chosen_t = jnp.zeros((N_EXPERTS, tm), F32)
    for k in range(TOP_K):
        chosen_t += (row == ids_t[k:k + 1]).astype(F32)
    oh = jnp.concatenate([chosen_t, jnp.zeros((LANES - N_EXPERTS, tm), F32)], axis=0).T

    @pl.when(i == 0)
    def _():
        padded = jnp.floor((cnt_ref[...] + (MOE_BLOCK - 1)) * (1.0 / MOE_BLOCK)) * MOE_BLOCK
        a = lax.broadcasted_iota(I32, (LANES, LANES), 0)
        b = lax.broadcasted_iota(I32, (LANES, LANES), 1)
        excl = jnp.dot(jnp.broadcast_to(padded, (SUBLANES, LANES)), (a < b).astype(F32),
                       precision=HIGHEST, preferred_element_type=F32)
        pstart[...] = excl[0:1]
        pst_ref[...] = excl[0:1]
        carry[...] = jnp.zeros_like(carry)

    a = lax.broadcasted_iota(I32, (tm, tm), 0)
    b = lax.broadcasted_iota(I32, (tm, tm), 1)
    before = _dot((b < a).astype(BF16), oh.astype(BF16))
    base = before + carry[...] + pstart[...]
    base_t = base.T[:N_EXPERTS]
    picked =[jnp.sum(jnp.where(row == ids_t[k:k + 1], base_t, 0.0), axis=0, keepdims=True)
              for k in range(TOP_K)]
    out_t = jnp.concatenate(picked + [jnp.zeros((LANES - TOP_K, tm), F32)], axis=0)
    dest_ref[...] = out_t.T.astype(I32)
    carry[...] += jnp.sum(oh, axis=0, keepdims=True)


def _route(route, cnt, tm=512):
    T = route.shape[0]
    return pl.pallas_call(
        functools.partial(_route_kernel, tm=tm),
        grid=(T // tm,),
        in_specs=[pl.BlockSpec((tm, LANES), lambda i: (i, 0)),
                  pl.BlockSpec((1, LANES), lambda i: (0, 0))],
        out_specs=[pl.BlockSpec((tm, LANES), lambda i: (i, 0)),
                   pl.BlockSpec((1, LANES), lambda i: (0, 0))],
        out_shape=[jax.ShapeDtypeStruct((T, LANES), I32), jax.ShapeDtypeStruct((1, LANES), F32)],
        scratch_shapes=[pltpu.VMEM((1, LANES), F32), pltpu.VMEM((1, LANES), F32)],
        compiler_params=_cparams("arbitrary"),
        name="route",
    )(route, cnt)


ZERO_ROWS = 64
DMA_UNROLL = 4


RING = 3


def _dispatch_kernel(padpos_ref, padcnt_ref, nu_ref, dest_ref, h_ref, xs_ref, zbuf, hbuf, sem, lsem, zsem,
                     *, td, nblk, nsteps):
    i = pl.program_id(0)

    @pl.when(i == 0)
    def _():
        zbuf[...] = jnp.zeros_like(zbuf)

        def per_expert(e, carry):
            off = padpos_ref[e]

            def zero_row(r):
                return pltpu.make_async_copy(zbuf.at[0], xs_ref.at[off + r], zsem)

            def z_issue(r, c):
                zero_row(r).start()
                return c

            def z_drain(r, c):
                zero_row(r).wait()
                return c

            lax.fori_loop(0, padcnt_ref[e], z_issue, 0)
            lax.fori_loop(0, padcnt_ref[e], z_drain, 0)
            return carry

        lax.fori_loop(0, N_EXPERTS, per_expert, 0)

        def zero_piece(j):
            row = pl.multiple_of(j * ZERO_ROWS, ZERO_ROWS)
            return pltpu.make_async_copy(zbuf, xs_ref.at[pl.ds(row, ZERO_ROWS)], zsem)

        per_blk = MOE_BLOCK // ZERO_ROWS

        def t_issue(j, c):
            zero_piece(j).start()
            return c

        def t_drain(j, c):
            zero_piece(j).wait()
            return c

        lax.fori_loop(nu_ref[0] * per_blk, nblk * per_blk, t_issue, 0)
        lax.fori_loop(nu_ref[0] * per_blk, nblk * per_blk, t_drain, 0)

    def load(step):
        s = lax.rem(step, RING)
        return pltpu.make_async_copy(h_ref.at[pl.ds(step * td, td)], hbuf.at[s], lsem.at[s])

    def wait_rows(step):
        s = lax.rem(step, RING)
        for _ in range(TOP_K):
            pltpu.make_async_copy(hbuf.at[s], xs_ref.at[pl.ds(0, td)], sem.at[s]).wait()

    @pl.when(i == 0)
    def _():
        load(0).start()
        if nsteps > 1:
            load(1).start()

    load(i).wait()
    slot = lax.rem(i, RING)

    def issue(r, carry):
        for k in range(TOP_K):
            pltpu.make_async_copy(hbuf.at[slot, r], xs_ref.at[dest_ref[r * TOP_K + k]],
                                  sem.at[slot]).start(priority=k % 2)
        return carry

    lax.fori_loop(0, td, issue, 0, unroll=DMA_UNROLL)

    @pl.when(i > 0)
    def _():
        wait_rows(i - 1)

    @pl.when(i + 2 < nsteps)
    def _():
        load(i + 2).start()

    @pl.when(i == nsteps - 1)
    def _():
        wait_rows(i)


def _dispatch(hf, dest_flat, padpos, padcnt, n_used, n_slots, td=512):
    T = hf.shape[0]
    nsteps = T // td
    grid_spec = pltpu.PrefetchScalarGridSpec(
        num_scalar_prefetch=3,
        grid=(nsteps,),
        in_specs=[pl.BlockSpec((td * TOP_K,), lambda i, *_: (i,), memory_space=pltpu.SMEM),
                  pl.BlockSpec(memory_space=pl.ANY)],
        out_specs=pl.BlockSpec(memory_space=pl.ANY),
        scratch_shapes=[pltpu.VMEM((ZERO_ROWS,) + TOKEN_TILE, F32), pltpu.VMEM((RING, td) + TOKEN_TILE, F32),
                        pltpu.SemaphoreType.DMA((RING,)), pltpu.SemaphoreType.DMA((RING,)),
                        pltpu.SemaphoreType.DMA(())],
    )
    return pl.pallas_call(
        functools.partial(_dispatch_kernel, td=td, nblk=n_slots // MOE_BLOCK, nsteps=nsteps),
        grid_spec=grid_spec,
        out_shape=jax.ShapeDtypeStruct((n_slots,) + TOKEN_TILE, F32),
        compiler_params=_cparams("arbitrary"),
        name="dispatch",
    )(padpos, padcnt, n_used, dest_flat, hf)


def _moe_kernel(be_ref, nu_ref, x_ref, wg_ref, bg_ref, wu_ref, bu_ref, wd_ref, bd_ref, o_ref):
    used = pl.program_id(0) < nu_ref[0]

    @pl.when(jnp.logical_not(used))
    def _():
        o_ref[...] = jnp.zeros_like(o_ref)

    @pl.when(used)
    def _():
        x = _load_token_tiles(x_ref, MOE_BLOCK).astype(BF16)
        gt = jnp.minimum(_dot(x, wg_ref[...]) + bg_ref[...], SWIGLU_LIMIT)
        up = jnp.clip(_dot(x, wu_ref[...]) + bu_ref[...], -SWIGLU_LIMIT, SWIGLU_LIMIT)
        act = (up + 1.0) * (gt * _sigmoid(SWIGLU_ALPHA * gt))
        _store_token_tiles(o_ref, _dot(act.astype(BF16), wd_ref[...]) + bd_ref[...])


def _moe(xs, block_expert, n_used, wg, bg, wu, bu, wd, bd):
    n_slots = xs.shape[0]
    nblk = n_slots // MOE_BLOCK
    blk = lambda i, be, nu: (jnp.minimum(i, nu[0] - 1), 0, 0)
    exp = lambda i, be, nu: (be[jnp.minimum(i, nu[0] - 1)], 0, 0)
    wspec = pl.BlockSpec((None, D_MODEL, D_MODEL), exp)
    bspec = pl.BlockSpec((None, 1, D_MODEL), exp)
    grid_spec = pltpu.PrefetchScalarGridSpec(
        num_scalar_prefetch=2,
        grid=(nblk,),
        in_specs=[pl.BlockSpec((MOE_BLOCK,) + TOKEN_TILE, blk), wspec, bspec, wspec, bspec, wspec, bspec],
        out_specs=pl.BlockSpec((MOE_BLOCK,) + TOKEN_TILE, lambda i, be, nu: (i, 0, 0)),
    )
    return pl.pallas_call(
        _moe_kernel,
        grid_spec=grid_spec,
        out_shape=jax.ShapeDtypeStruct((n_slots,) + TOKEN_TILE, F32),
        compiler_params=_cparams("arbitrary"),
        name="moe",
    )(block_expert, n_used, xs, wg, bg, wu, bu, wd, bd)


def _combine_kernel(dcur_ref, dnext_ref, x1_ref, r_ref, g_ref, ys_ref, o_ref, buf, sem, *, tc, nsteps):
    i = pl.program_id(0)
    slot = lax.rem(i, 2)

    def gather(d_ref, s):
        def body(r, carry):
            for k in range(TOP_K):
                pltpu.make_async_copy(ys_ref.at[d_ref[r * TOP_K + k]], buf.at[s, k, r],
                                      sem.at[s]).start(priority=k % 2)
            return carry

        lax.fori_loop(0, tc, body, 0, unroll=DMA_UNROLL)

    @pl.when(i == 0)
    def _():
        gather(dcur_ref, slot)

    @pl.when(i + 1 < nsteps)
    def _():
        gather(dnext_ref, 1 - slot)

    for k in range(TOP_K):
        pltpu.make_async_copy(ys_ref.at[pl.ds(0, tc)], buf.at[slot, k], sem.at[slot]).wait()
    gates = r_ref[...]
    acc = x1_ref[...]
    for k in range(TOP_K):
        acc += gates[:, k:k + 1] * _load_token_tiles(buf, tc, (slot * TOP_K + k) * tc)
    o_ref[...] = _rms(acc, g_ref[...])


def _combine(x1, route, dest_flat, ys, g_final, tc=512):
    T = x1.shape[0]
    nsteps = T // tc
    return pl.pallas_call(
        functools.partial(_combine_kernel, tc=tc, nsteps=nsteps),
        grid=(nsteps,),
        in_specs=[pl.BlockSpec((tc * TOP_K,), lambda i: (i,), memory_space=pltpu.SMEM),
                  pl.BlockSpec((tc * TOP_K,), lambda i: (jnp.minimum(i + 1, nsteps - 1),),
                               memory_space=pltpu.SMEM),
                  pl.BlockSpec((tc, D_MODEL), lambda i: (i, 0)),
                  pl.BlockSpec((tc, LANES), lambda i: (i, 0)),
                  pl.BlockSpec((1, D_MODEL), lambda i: (0, 0)),
                  pl.BlockSpec(memory_space=pl.ANY)],
        out_specs=pl.BlockSpec((tc, D_MODEL), lambda i: (i, 0)),
        out_shape=jax.ShapeDtypeStruct((T, D_MODEL), F32),
        scratch_shapes=[pltpu.VMEM((2, TOP_K, tc) + TOKEN_TILE, F32), pltpu.SemaphoreType.DMA((2,))],
        compiler_params=_cparams("arbitrary"),
        name="combine",
    )(dest_flat, dest_flat, x1, route, g_final, ys)


def _moe_layer(x1, hf, route, cnt, ew, g_final):
    T = x1.shape[0]
    n_assign = T * TOP_K
    nblk = n_assign // MOE_BLOCK + N_EXPERTS
    n_slots = nblk * MOE_BLOCK
    dest, pst = _route(route, jnp.pad(cnt[:, 0], (0, LANES - N_EXPERTS))[None, :])
    cnt_i = cnt[:, 0].astype(I32)
    pstart = pst[0, :N_EXPERTS].astype(I32)
    padded = (cnt_i + MOE_BLOCK - 1) // MOE_BLOCK * MOE_BLOCK
    pend = pstart + padded
    block_expert = jnp.minimum(
        jnp.sum(jnp.arange(nblk, dtype=I32)[:, None] * MOE_BLOCK >= pend[None, :], axis=1),
        N_EXPERTS - 1).astype(I32)
    n_used = (pend[-1] // MOE_BLOCK).reshape(1).astype(I32)
    dest_flat = dest[:, :TOP_K].reshape(n_assign)
    xs = _dispatch(hf, dest_flat, pstart + cnt_i, padded - cnt_i, n_used, n_slots)
    ys = _moe(xs, block_expert, n_used, *ew)
    return _combine(x1, route, dest_flat, ys, g_final)


def _trunk(x, p, bb):
    B, L, D = x.shape
    T = B * L
    x2d = x.reshape(T, D)
    v, x1g, x2g, u = _inproj(x2d, L, p["g_mix"], p["w_in"], p["b_in"], p["short_w"], p["short_b"])
    ya = _hyena(v, x1g, x2g, B, L, p["filt"], p["filt_bias"], bb)
    ybp = _s5(u, B, L, p["s5"])
    x1, hf, route, cnt = _mix(x2d, ya, ybp, p["w_glu"], p["b_glu"], p["g_hyena"], p["g_s5"], p["w_out"],
                              p["g_ffn"], p["w_router"], p["b_router"])
    y = _moe_layer(x1, hf, route, cnt, p["experts"], p["g_final"])
    return y.reshape(B, L, D)


def kernel(x_prompt, x_sample, g_mix, w_in, b_in, short_w, short_b, filt_w1, filt_b1, filt_w2, filt_b2, filt_w3, filt_b3, filt_freq, filt_bias, s5_lam_re, s5_lam_im, s5_log_dt, s5_b_re, s5_b_im, s5_c_re, s5_c_im, s5_d, w_glu, b_glu, g_hyena, g_s5, w_out, g_ffn, w_router, b_router, w_gate, b_gate, w_up, b_up, w_down, b_down, g_final):
    assert g_mix.shape[0] == 1, "one encoder layer"
    row = lambda a: a[0][None, :].astype(F32)
    p = dict(
        g_mix=row(g_mix), w_in=w_in[0].astype(BF16), b_in=row(b_in),
        short_w=short_w[0].astype(F32), short_b=row(short_b),
        filt=(filt_w1[0], filt_b1[0], filt_w2[0], filt_b2[0], filt_w3[0], filt_b3[0], filt_freq[0]),
        filt_bias=filt_bias[0].astype(F32),
        s5=_s5_operators(s5_lam_re[0], s5_lam_im[0], s5_log_dt[0], s5_b_re[0], s5_b_im[0],
                         s5_c_re[0], s5_c_im[0], s5_d[0]),
        w_glu=w_glu[0].astype(BF16), b_glu=row(b_glu), g_hyena=row(g_hyena), g_s5=row(g_s5),
        w_out=w_out[0].astype(BF16), g_ffn=row(g_ffn),
        w_router=_split_bf16(jnp.pad(w_router[0].astype(F32), ((0, 0), (0, LANES - N_EXPERTS)))),
        b_router=jnp.pad(b_router[0].astype(F32), (0, LANES - N_EXPERTS))[None, :],
        experts=(w_gate[0].astype(BF16), b_gate[0][:, None, :], w_up[0].astype(BF16),
                 b_up[0][:, None, :], w_down[0].astype(BF16), b_down[0][:, None, :]),
        g_final=g_final[None, :].astype(F32),
    )
    y_prompt = _trunk(x_prompt, p, bb=1)
    y_sample = _trunk(x_sample, p, bb=8)
    return (y_prompt, y_sample)
```

```python
import functools
import math

import numpy as np
import jax
import jax.numpy as jnp
from jax import lax
from jax.experimental import pallas as pl
from jax.experimental.pallas import tpu as pltpu

F32 = jnp.float32
BF16 = jnp.bfloat16
I32 = jnp.int32
HIGHEST = lax.Precision.HIGHEST

LANES = 128
SUBLANES = 8
D_MODEL = 1024
C_HY = 512
C_S5 = 512
D_IN = 3 * C_HY + C_S5
Q_HY = C_HY // LANES
FILTER_BANDS = 16
FILTER_WIDTH = 64
Z_PAD = 128
DECAY_MAX = math.log(1e-2) / 0.3
DECAY_MIN = math.log(1e-2) / 1.5
S5_G, S5_H, S5_P = 32, 16, 64
S5_CHUNK = 16
S5_UW = S5_CHUNK * S5_H
S5_SW = 2 * S5_P
S5_GB = LANES // S5_H
S5_NB = S5_G // S5_GB
S5_KW = S5_CHUNK * LANES
S5_XW = S5_GB * S5_SW
S5_STEP_ROWS = 512
LAMBDA_RE_MAX = -1e-4
N_EXPERTS = 32
TOP_K = 4
SWIGLU_LIMIT = 7.0
SWIGLU_ALPHA = 1.702
MOE_BLOCK = 512
RMS_EPS = 1e-6
VMEM_LIMIT_BYTES = 56 * 1024 * 1024


def _cparams(*sem):
    return pltpu.CompilerParams(dimension_semantics=sem, vmem_limit_bytes=VMEM_LIMIT_BYTES)


def _rms(x, g):
    return x * lax.rsqrt(jnp.mean(x * x, axis=-1, keepdims=True) + RMS_EPS) * g


def _sigmoid(x):
    return 1.0 / (1.0 + jnp.exp(-x))


def _dot(a, b):
    return jnp.dot(a, b, preferred_element_type=F32)


def _split_bf16(w):
    hi = w.astype(BF16)
    return hi, (w - hi.astype(F32)).astype(BF16)


def _lane_block(x, q):
    return x[:, q * LANES:(q + 1) * LANES]


def _as_rows(ref):
    return ref.reshape(math.prod(ref.shape[:-1]), LANES)


TOKEN_TILE = (D_MODEL // LANES, LANES)


def _load_token_tiles(ref, n, row0=0):
    rows = _as_rows(ref)
    nq = TOKEN_TILE[0]
    return jnp.concatenate([rows[pl.ds(row0 * nq + q, n, stride=nq), :] for q in range(nq)], axis=1)


def _store_token_tiles(ref, x):
    rows = _as_rows(ref)
    nq = TOKEN_TILE[0]
    for q in range(nq):
        rows[pl.ds(q, x.shape[0], stride=nq), :] = _lane_block(x, q)


def _inproj_kernel(x_ref, xp_ref, xn_ref, g_ref, w_ref, b_ref, sw_ref, sb_ref,
                   v_ref, x1_ref, x2_ref, u_ref, *, tm, seq):
    i = pl.program_id(0)
    g = g_ref[...]
    rows = jnp.concatenate([x_ref[...], xp_ref[...], xn_ref[...]], axis=0)
    proj = _dot(_rms(rows, g).astype(BF16), w_ref[...]) + b_ref[...]
    u_ref[...] = proj[:tm, 3 * C_HY:]
    z = proj[:tm, :3 * C_HY]
    zh = proj[tm:, :3 * C_HY]
    row0 = i * tm
    has_prev = lax.rem(row0, seq) != 0
    has_next = lax.rem(row0 + tm, seq) != 0
    zp = jnp.where(has_prev, zh[SUBLANES - 1:SUBLANES], 0.0)
    zn = jnp.where(has_next, zh[SUBLANES:SUBLANES + 1], 0.0)
    rid = lax.broadcasted_iota(I32, (tm, 1), 0)
    zm1 = jnp.where(rid == 0, zp, pltpu.roll(z, 1, 0))
    zp1 = jnp.where(rid == tm - 1, zn, pltpu.roll(z, tm - 1, 0))
    sw = sw_ref[...]
    o = zm1 * sw[0:1] + z * sw[1:2] + zp1 * sw[2:3] + sb_ref[...]
    for q in range(Q_HY):
        v_ref[q] = _lane_block(o, q)
        x1_ref[q] = _lane_block(o, Q_HY + q)
        x2_ref[q] = _lane_block(o, 2 * Q_HY + q)


def _inproj(x2d, seq, g_mix, w_in_bf, b_in, short_w, short_b, tm=512):
    T = x2d.shape[0]
    nb8 = T // SUBLANES
    tb = tm // SUBLANES
    hy = jax.ShapeDtypeStruct((Q_HY, T, LANES), F32)
    hy_spec = pl.BlockSpec((Q_HY, tm, LANES), lambda i: (0, i, 0))
    const = lambda i: (0, 0)
    return pl.pallas_call(
        functools.partial(_inproj_kernel, tm=tm, seq=seq),
        grid=(T // tm,),
        in_specs=[
            pl.BlockSpec((tm, D_MODEL), lambda i: (i, 0)),
            pl.BlockSpec((SUBLANES, D_MODEL), lambda i: (jnp.maximum(i * tb - 1, 0), 0)),
            pl.BlockSpec((SUBLANES, D_MODEL), lambda i: (jnp.minimum((i + 1) * tb, nb8 - 1), 0)),
            pl.BlockSpec((1, D_MODEL), const),
            pl.BlockSpec((D_MODEL, D_IN), const),
            pl.BlockSpec((1, D_IN), const),
            pl.BlockSpec((3, 3 * C_HY), const),
            pl.BlockSpec((1, 3 * C_HY), const),
        ],
        out_specs=[hy_spec, hy_spec, hy_spec, pl.BlockSpec((tm, C_S5), lambda i: (i, 0))],
        out_shape=[hy, hy, hy, jax.ShapeDtypeStruct((T, C_S5), F32)],
        compiler_params=_cparams("arbitrary"),
        name="inproj",
    )(x2d, x2d, x2d, g_mix, w_in_bf, b_in, short_w, short_b)


def _filter_kernel(z_ref, w1_ref, b1_ref, w2_ref, b2_ref, w3h_ref, w3l_ref, b3_ref, fr_ref, dl_ref,
                   k_ref, s_ref, *, tr, seq):
    i = pl.program_id(0)
    z = z_ref[...]
    fr = fr_ref[...]
    h = jnp.sin(fr[0:1] * (jnp.dot(z, w1_ref[...], precision=HIGHEST,
                                   preferred_element_type=F32) + b1_ref[...]))
    h = jnp.sin(fr[1:2] * (jnp.dot(h, w2_ref[...], precision=HIGHEST,
                                   preferred_element_type=F32) + b2_ref[...]))
    h_hi = h.astype(BF16)
    h_lo = (h - h_hi.astype(F32)).astype(BF16)
    h = _dot(h_hi, w3h_ref[...]) + _dot(h_lo, w3h_ref[...]) + _dot(h_hi, w3l_ref[...]) + b3_ref[...]
    h = h * jnp.exp(-z[:, 0:1] * dl_ref[...])
    rid = i * tr + lax.broadcasted_iota(I32, (tr, 1), 0)
    h = jnp.where(rid == seq, 0.0, h)
    for q in range(2 * Q_HY):
        k_ref[q] = _lane_block(h, q)

    @pl.when(i == 0)
    def _():
        s_ref[...] = jnp.zeros_like(s_ref)

    s_ref[...] += jnp.sum(jnp.abs(h), axis=0, keepdims=True)


def _filter_taps(seq, w1, b1, w2, b2, w3, b3, freq, tr=512):
    L = seq
    N = 2 * L
    n = jnp.arange(N, dtype=I32)
    pos = jnp.where(n < L, n, jnp.where(n == L, 0, N - n)).astype(F32)
    t = (pos * (1.0 / (L - 1)))[:, None]
    w = 2.0 * math.pi * pos / L
    bands = jnp.linspace(1e-4, FILTER_BANDS - 1, FILTER_BANDS, dtype=F32)
    ang = w[:, None] * bands[None, :]
    zc = jnp.concatenate([t, jnp.cos(ang), -jnp.sin(ang),
                          jnp.zeros((N, Z_PAD - 1 - 2 * FILTER_BANDS), F32)], axis=-1)
    w1p = jnp.pad(w1, ((0, Z_PAD - w1.shape[0]), (0, 0)))
    w3d = w3.reshape(FILTER_WIDTH, 2, 2 * C_HY).transpose(1, 0, 2)
    b3d = b3.reshape(2, 1, 2 * C_HY)
    deltas = jnp.abs(jnp.linspace(DECAY_MIN, DECAY_MAX, C_HY, dtype=F32))
    dl = jnp.tile(deltas, 2)[None, :]
    half = (N // tr) // 2
    const = lambda i: (0, 0)
    taps, sums = pl.pallas_call(
        functools.partial(_filter_kernel, tr=tr, seq=L),
        grid=(N // tr,),
        in_specs=[
            pl.BlockSpec((tr, Z_PAD), lambda i: (i, 0)),
            pl.BlockSpec((Z_PAD, FILTER_WIDTH), const),
            pl.BlockSpec((1, FILTER_WIDTH), const),
            pl.BlockSpec((FILTER_WIDTH, FILTER_WIDTH), const),
            pl.BlockSpec((1, FILTER_WIDTH), const),
            pl.BlockSpec((None, FILTER_WIDTH, 2 * C_HY), lambda i: (i // half, 0, 0)),
            pl.BlockSpec((None, FILTER_WIDTH, 2 * C_HY), lambda i: (i // half, 0, 0)),
            pl.BlockSpec((None, 1, 2 * C_HY), lambda i: (i // half, 0, 0)),
            pl.BlockSpec((2, FILTER_WIDTH), const),
            pl.BlockSpec((1, 2 * C_HY), const),
        ],
        out_specs=[pl.BlockSpec((2 * Q_HY, tr, LANES), lambda i: (0, i, 0)),
                   pl.BlockSpec((1, 2 * C_HY), const)],
        out_shape=[jax.ShapeDtypeStruct((2 * Q_HY, N, LANES), F32),
                   jax.ShapeDtypeStruct((1, 2 * C_HY), F32)],
        compiler_params=_cparams("arbitrary"),
        name="hyena_filter",
    )(zc, w1p, b1[None, :], w2, b2[None, :], *_split_bf16(w3d.astype(F32)), b3d, freq, dl)
    return taps, sums


def _split_n(N):
    n1 = {32768: 128, 4096: 64}.get(N)
    if n1 is None:
        n1 = 1 << (int(math.log2(N)) // 2)
    return n1, N // n1


def _dft_constants(N1, N2):
    N = N1 * N2
    k1 = np.arange(N1)[:, None]
    n1 = np.arange(N1)[None, :]
    ang = 2.0 * np.pi * ((k1 * n1) % N1) / N1
    fa_full = np.concatenate([np.cos(ang), -np.sin(ang)], axis=0)
    fa_half = fa_full[:, :N1 // 2]
    fa_inv = np.concatenate([np.cos(ang), -np.sin(ang)], axis=1)[:N1 // 2] / N
    k2 = np.arange(N2)[:, None]
    n2 = np.arange(N2)[None, :]
    a2 = 2.0 * np.pi * ((k2 * n2) % N2) / N2
    cr = jnp.asarray(np.cos(a2).astype(np.float32))[None]
    ci = jnp.asarray((-np.sin(a2)).astype(np.float32))[None]
    to = lambda a: jnp.asarray(a.astype(np.float32)).astype(BF16)
    kk = lax.broadcasted_iota(I32, (N1, N2), 0)
    nn = lax.broadcasted_iota(I32, (N1, N2), 1)
    ta = (2.0 * math.pi / N) * lax.rem(kk * nn, N).astype(F32)
    tr, ti = jnp.cos(ta), -jnp.sin(ta)
    stack = lambda re, im: jnp.concatenate(
        [jnp.concatenate([re, -im], axis=-1), jnp.concatenate([im, re], axis=-1)], axis=-2).astype(BF16)
    tc_r, tc_i = tr[:, None, :], ti[:, None, :]
    fk = stack(cr * tc_r - ci * tc_i, cr * tc_i + ci * tc_r)
    tr_r, tr_i = tr[:, :, None], ti[:, :, None]
    gk = stack(cr * tr_r - ci * tr_i, -(cr * tr_i + ci * tr_r))
    return dict(fa_full=to(fa_full), fa_half=to(fa_half), fa_inv=to(fa_inv), fk=fk, gk=gk)


U32 = jnp.uint32
_HI16 = 0xFFFF0000
_HALF16 = 0x8000


def _pack_spectrum(re, im):
    rb = lax.bitcast_convert_type(re, U32) + U32(_HALF16)
    ib = lax.bitcast_convert_type(im, U32) + U32(_HALF16)
    return (rb & U32(_HI16)) | (ib >> 16)


def _unpack_spectrum(w):
    return (lax.bitcast_convert_type(w & U32(_HI16), F32), lax.bitcast_convert_type(w << 16, F32))


N2_STEP = 16


def _dft_a_kernel(f_ref, x_ref, o_ref, *, rows):
    nq = x_ref.shape[0]
    x2d = _as_rows(x_ref)
    f = f_ref[...]
    half = f.shape[0] // 2
    for j in range(N2_STEP):
        xs = jnp.concatenate(
            [x2d[pl.ds(q * rows * N2_STEP + j, rows, stride=N2_STEP), :] for q in range(nq)], axis=1)
        r = _dot(f, xs.astype(BF16))
        w = _pack_spectrum(r[:half], r[half:])
        for q in range(nq):
            o_ref[q, j] = _lane_block(w, q)


def _dft_a(f, x):
    Q, Bt, K, N2, _ = x.shape
    R = f.shape[0] // 2
    return pl.pallas_call(
        functools.partial(_dft_a_kernel, rows=K),
        grid=(Bt, N2 // N2_STEP, Q // Q_HY),
        in_specs=[pl.BlockSpec((2 * R, K), lambda b, j, w: (0, 0)),
                  pl.BlockSpec((Q_HY, None, K, N2_STEP, LANES), lambda b, j, w: (w, b, 0, j, 0))],
        out_specs=pl.BlockSpec((Q_HY, None, N2_STEP, R, LANES), lambda b, j, w: (w, b, j, 0, 0)),
        out_shape=jax.ShapeDtypeStruct((Q, Bt, N2, R, LANES), U32),
        compiler_params=_cparams("arbitrary", "arbitrary", "arbitrary"),
        name="dft_a",
    )(f, x)


def _dft_a_inv_kernel(f_ref, z_ref, gate_ref, v_ref, bias_ref, *rest, rows, with_next):
    if with_next:
        fn_ref, o_ref, an_ref = rest
    else:
        (o_ref,) = rest
    g2d = _as_rows(gate_ref)
    v2d = _as_rows(v_ref)
    o2d = _as_rows(o_ref)
    f = f_ref[...]
    bias = bias_ref[...]
    for j in range(N2_STEP):
        zr, zi = _unpack_spectrum(jnp.concatenate([z_ref[q, j] for q in range(Q_HY)], axis=1))
        y = _dot(f, jnp.concatenate([zr, zi], axis=0).astype(BF16))
        outs = []
        for q in range(Q_HY):
            sl = pl.ds(q * rows * N2_STEP + j, rows, stride=N2_STEP)
            outs.append(g2d[sl, :] * (_lane_block(y, q) + v2d[sl, :] * _lane_block(bias, q)))
            o2d[sl, :] = outs[-1]
        if with_next:
            r = _dot(fn_ref[...], jnp.concatenate(outs, axis=1).astype(BF16))
            half = r.shape[0] // 2
            w = _pack_spectrum(r[:half], r[half:])
            for q in range(Q_HY):
                an_ref[q, j] = _lane_block(w, q)


def _dft_a_inv(f, z, gate, v, bias_row, f_next=None):
    Q, Bt, N2, N1, _ = z.shape
    K = f.shape[0]
    nat = pl.BlockSpec((Q, None, K, N2_STEP, LANES), lambda b, j: (0, b, 0, j, 0))
    spec = pl.BlockSpec((Q, None, N2_STEP, N1, LANES), lambda b, j: (0, b, j, 0, 0))
    with_next = f_next is not None
    in_specs = [pl.BlockSpec((K, 2 * N1), lambda b, j: (0, 0)), spec, nat, nat,
                pl.BlockSpec((1, C_HY), lambda b, j: (0, 0))]
    out_specs, out_shape = [nat], [jax.ShapeDtypeStruct((Q, Bt, K, N2, LANES), F32)]
    args = [f, z, gate, v, bias_row]
    if with_next:
        in_specs.append(pl.BlockSpec((2 * N1, K), lambda b, j: (0, 0)))
        out_specs.append(spec)
        out_shape.append(jax.ShapeDtypeStruct(z.shape, U32))
        args.append(f_next)
    return pl.pallas_call(
        functools.partial(_dft_a_inv_kernel, rows=K, with_next=with_next),
        grid=(Bt, N2 // N2_STEP),
        in_specs=in_specs,
        out_specs=out_specs,
        out_shape=out_shape,
        compiler_params=_cparams("arbitrary", "arbitrary"),
        name="dft_a_inv",
    )(*args)


C_STEP_Q = 2
K1_STEP = SUBLANES


def _dft_c_kernel(a_ref, k_ref, f_ref, g_ref, z_ref, *, bb, n2):
    per = n2 * K1_STEP
    a2d = _as_rows(a_ref)
    z2d = _as_rows(z_ref)
    cols = [(b, q) for b in range(bb) for q in range(C_STEP_Q)]

    def rows(b, q, kk):
        return pl.ds((q * bb + b) * per + kk, n2, stride=K1_STEP)

    for kk in range(K1_STEP):
        kr, ki = _unpack_spectrum(jnp.tile(k_ref[kk], (1, bb)))
        ar, ai = _unpack_spectrum(jnp.concatenate([a2d[rows(b, q, kk), :] for b, q in cols], axis=1))
        x = _dot(f_ref[kk], jnp.concatenate([ar, ai], axis=0).astype(BF16))
        xr = x[:n2]
        xi = x[n2:]
        yr = xr * kr - xi * ki
        yi = xr * ki + xi * kr
        zz = _dot(g_ref[kk], jnp.concatenate([yr, yi], axis=0).astype(BF16))
        w = _pack_spectrum(zz[:n2], zz[n2:])
        for c, (b, q) in enumerate(cols):
            z2d[rows(b, q, kk), :] = _lane_block(w, c)


def _dft_c(a5, khat, order, fk, gk, bb):
    Q, Bt, N2, N1, _ = a5.shape
    cw = C_STEP_Q * LANES
    ablk = pl.BlockSpec((C_STEP_Q, bb, N2, K1_STEP, LANES), lambda g, c, b: (c, b, 0, g, 0))
    mats = pl.BlockSpec((K1_STEP, 2 * N2, 2 * N2), lambda g, c, b: (g, 0, 0))
    return pl.pallas_call(
        functools.partial(_dft_c_kernel, bb=bb, n2=N2),
        grid=(N1 // K1_STEP, Q // C_STEP_Q, Bt // bb),
        in_specs=[ablk,
                  pl.BlockSpec((K1_STEP, N2, cw), lambda g, c, b: (g, 0, order * (C_HY // cw) + c)),
                  mats, mats],
        out_specs=ablk,
        out_shape=jax.ShapeDtypeStruct(a5.shape, U32),
        compiler_params=_cparams("arbitrary", "arbitrary", "arbitrary"),
        name="dft_c",
    )(a5, khat, fk, gk)


def _dft_c_filter_kernel(a_ref, f_ref, s_ref, o_ref, *, n2):
    per = n2 * K1_STEP
    a2d = _as_rows(a_ref)
    sc = 1.0 / (s_ref[...] + 1e-6)
    for kk in range(K1_STEP):
        ar, ai = _unpack_spectrum(jnp.concatenate(
            [a2d[pl.ds(q * per + kk, n2, stride=K1_STEP), :] for q in range(C_STEP_Q)], axis=1))
        x = _dot(f_ref[kk], jnp.concatenate([ar, ai], axis=0).astype(BF16))
        o_ref[kk] = _pack_spectrum(x[:n2] * sc, x[n2:] * sc)


def _dft_c_filter(a4, fk, sums):
    Q, N2, N1, _ = a4.shape
    cw = C_STEP_Q * LANES
    return pl.pallas_call(
        functools.partial(_dft_c_filter_kernel, n2=N2),
        grid=(N1 // K1_STEP, Q // C_STEP_Q),
        in_specs=[pl.BlockSpec((C_STEP_Q, N2, K1_STEP, LANES), lambda g, c: (c, 0, g, 0)),
                  pl.BlockSpec((K1_STEP, 2 * N2, 2 * N2), lambda g, c: (g, 0, 0)),
                  pl.BlockSpec((1, cw), lambda g, c: (0, c))],
        out_specs=pl.BlockSpec((K1_STEP, N2, cw), lambda g, c: (g, 0, c)),
        out_shape=jax.ShapeDtypeStruct((N1, N2, Q * LANES), U32),
        compiler_params=_cparams("arbitrary", "arbitrary"),
        name="dft_c_filter",
    )(a4, fk, sums)


def _hyena(v, x1, x2, B, seq, filt, filt_bias, bb):
    L = seq
    N = 2 * L
    N1, N2 = _split_n(N)
    cst = _dft_constants(N1, N2)
    taps, sums = _filter_taps(L, *filt)
    ka = _dft_a(cst["fa_full"], taps.reshape(2 * Q_HY, 1, N1, N2, LANES))
    khat = _dft_c_filter(ka.reshape(2 * Q_HY, N2, N1, LANES), cst["fk"], sums)
    nat = lambda a: a.reshape(Q_HY, B, N1 // 2, N2, LANES)
    cur = nat(v)
    a = _dft_a(cst["fa_half"], cur)
    for o, gate in enumerate((x1, x2)):
        z = _dft_c(a, khat, o, cst["fk"], cst["gk"], bb)
        if o == 0:
            cur, a = _dft_a_inv(cst["fa_inv"], z, nat(gate), cur, filt_bias[o][None, :], cst["fa_half"])
        else:
            (cur,) = _dft_a_inv(cst["fa_inv"], z, nat(gate), cur, filt_bias[o][None, :])
    return cur.reshape(Q_HY, B * L, LANES)


def _s5_operators(lam_re, lam_im, log_dt, b_re, b_im, c_re, c_im, d_skip):
    Tc, G, H, P = S5_CHUNK, S5_G, S5_H, S5_P
    lam = jnp.minimum(lam_re.astype(F32), LAMBDA_RE_MAX) + 1j * lam_im.astype(F32)
    dt = jnp.exp(log_dt.astype(F32))[..., None]
    lam_dt = lam * dt
    lam_bar = jnp.exp(lam_dt)
    b_bar = ((lam_bar - 1.0) / lam)[..., None] * (b_re.astype(F32) + 1j * b_im.astype(F32))
    c = c_re.astype(F32) + 1j * c_im.astype(F32)
    tau = jnp.arange(Tc + 1, dtype=F32)
    pw = jnp.exp(lam_dt[None] * tau[:, None, None, None])
    kk = jnp.einsum('dghp,tdgp,dgpk->dtghk', c, pw[:Tc], b_bar).real
    i = jnp.arange(Tc)
    lag = i[:, None] - i[None, :]
    kf = jnp.where((lag >= 0)[:, :, None, None, None], kk[0][jnp.clip(lag, 0, Tc - 1)], 0.0)
    kb = jnp.where((lag <= 0)[:, :, None, None, None], kk[1][jnp.clip(-lag, 0, Tc - 1)], 0.0)
    m = kf + kb
    eye = (lag == 0)[:, :, None, None, None] * jnp.eye(H, dtype=F32)[None, None, None]
    m = m + eye * d_skip.astype(F32).reshape(G, H)[None, None, :, :, None]
    clf = c[0][None] * pw[1:Tc + 1, 0][:, :, None, :]
    clb = c[1][None] * pw[Tc - i, 1][:, :, None, :]
    sf = pw[Tc - 1 - i, 0][:, :, :, None] * b_bar[0][None]
    sb = pw[i, 1][:, :, :, None] * b_bar[1][None]
    w_intra = m.transpose(2, 1, 4, 0, 3).reshape(G, Tc * H, Tc * H)
    st2y = lambda cl: jnp.concatenate([cl.real, -cl.imag], axis=-1).transpose(1, 3, 0, 2) \
        .reshape(G, 2 * P, Tc * H)
    u2s = lambda s: jnp.concatenate([s.real, s.imag], axis=2).transpose(1, 0, 3, 2) \
        .reshape(G, Tc * H, 2 * P)
    w_state = jnp.concatenate([u2s(sf), u2s(sb)], axis=-1)
    pair = lambda a: a.reshape(2, G // 2, 2 * P)
    lam16 = jnp.concatenate([pair(pw[Tc].real), pair(pw[Tc].imag)], axis=-1).reshape(2, G * 2 * P)
    col = np.arange(S5_KW)
    spread = (np.arange(S5_UW)[:, None] == (col // LANES * S5_H + col % S5_H)[None, :])
    return dict(w_intra=w_intra.astype(BF16), w_xf=st2y(clf).astype(BF16), w_xb=st2y(clb).astype(BF16),
                w_state=w_state.astype(BF16), lam16=lam16,
                spread=jnp.asarray(spread.astype(np.float32)).astype(BF16))


def _s5_rows(i, cl, B, ncc):
    return pl.ds(S5_CHUNK * cl + i, B, stride=S5_CHUNK * ncc)


def _s5_chunk_inputs(u2d, B, ncc):
    cols = []
    for i in range(S5_CHUNK):
        if B == 1:
            cols.append(u2d[pl.ds(i, ncc, stride=S5_CHUNK), :])
        else:
            cols.append(jnp.concatenate([u2d[_s5_rows(i, cl, B, ncc), :] for cl in range(ncc)], axis=0))
    return jnp.concatenate(cols, axis=1).astype(BF16)


def _s5_state_lane(a, part):
    return (a // 2) * 2 * S5_SW + part * S5_SW + (a % 2) * S5_P


def _s5_state_kernel(u_ref, w_ref, sf_ref, sb_ref, lhs_ref, wblk, *, B, ncc):
    @pl.when(pl.program_id(1) == 0)
    def _():
        wblk[...] = jnp.zeros_like(wblk)
        for a in range(S5_GB):
            for j in range(S5_CHUNK):
                r0 = j * LANES + a * S5_H
                for d in range(2):
                    for part in range(2):
                        c0 = d * S5_XW + _s5_state_lane(a, part)
                        s0 = d * S5_SW + part * S5_P
                        wblk[r0:r0 + S5_H, c0:c0 + S5_P] = w_ref[a, j * S5_H:(j + 1) * S5_H, s0:s0 + S5_P]

    lhs = _s5_chunk_inputs(_as_rows(u_ref), B, ncc)
    lhs_ref[...] = lhs
    r = _dot(lhs, wblk[...])
    sf_ref[...] = r[:, :S5_XW]
    sb_ref[...] = r[:, S5_XW:]


def _s5_state(u3, w_state, ncc):
    B, L, _ = u3.shape
    nch = L // S5_CHUNK
    out = jax.ShapeDtypeStruct((nch * B, S5_NB * S5_XW), F32)
    return pl.pallas_call(
        functools.partial(_s5_state_kernel, B=B, ncc=ncc),
        grid=(S5_NB, nch // ncc),
        in_specs=[pl.BlockSpec((B, S5_CHUNK * ncc, LANES), lambda q, t: (0, t, q)),
                  pl.BlockSpec((S5_GB, S5_UW, 2 * S5_SW), lambda q, t: (q, 0, 0))],
        out_specs=[pl.BlockSpec((ncc * B, S5_XW), lambda q, t: (t, q))] * 2
        + [pl.BlockSpec((ncc * B, S5_KW), lambda q, t: (t, q))],
        out_shape=[out, out, jax.ShapeDtypeStruct((nch * B, S5_NB * S5_KW), BF16)],
        scratch_shapes=[pltpu.VMEM((S5_KW, 2 * S5_XW), BF16)],
        compiler_params=_cparams("arbitrary", "arbitrary"),
        name="s5_state",
    )(u3, w_state)


def _s5_scan_kernel(sf_ref, sb_ref, lam_ref, xf_ref, xb_ref, *, nch, rows):
    lb = sf_ref.shape[1]
    vr = max(rows, SUBLANES)
    lam_f, lam_b = lam_ref[0], lam_ref[1]

    def advance(x, lam, inc):
        out = []
        for u in range(0, lb, 2 * S5_SW):
            re, im = x[:, u:u + S5_SW], x[:, u + S5_SW:u + 2 * S5_SW]
            lr, li = lam[:, u:u + S5_SW], lam[:, u + S5_SW:u + 2 * S5_SW]
            out += [lr * re - li * im, lr * im + li * re]
        return jnp.concatenate(out, axis=1) + inc

    def body(c, carry):
        xf, xb = carry
        rf = pl.ds(pl.multiple_of(c * rows, rows), rows)
        xf_ref[rf, :] = xf[:rows]
        xf = advance(xf, lam_f, sf_ref[rf, :])
        rb = pl.ds(pl.multiple_of((nch - 1 - c) * rows, rows), rows)
        xb_ref[rb, :] = xb[:rows]
        xb = advance(xb, lam_b, sb_ref[rb, :])
        return xf, xb

    zero = jnp.zeros((vr, lb), F32)
    lax.fori_loop(0, nch, body, (zero, zero))


def _s5_scan(sf, sb, lam16, nch, rows, lb):
    R, lanes = sf.shape
    blk = pl.BlockSpec((R, lb), lambda j: (0, j))
    out = jax.ShapeDtypeStruct((R, lanes), F32)
    return pl.pallas_call(
        functools.partial(_s5_scan_kernel, nch=nch, rows=rows),
        grid=(lanes // lb,),
        in_specs=[blk, blk, pl.BlockSpec((2, 1, lb), lambda j: (0, 0, j))],
        out_specs=[blk, blk],
        out_shape=[out, out],
        compiler_params=_cparams("arbitrary"),
        name="s5_scan",
    )(sf, sb, lam16[:, None, :])


def _s5_out_kernel(lhs_ref, xf_ref, xb_ref, wm_ref, wf_ref, wb_ref, e_ref, y_ref, wm_blk, wf_blk, wb_blk,
                   *, B, ncc):
    @pl.when(pl.program_id(1) == 0)
    def _():
        lane = lax.broadcasted_iota(I32, (1, S5_KW), 1)
        slot = (lane // S5_H) % S5_GB
        e = e_ref[...]
        for a in range(S5_GB):
            own = slot == a
            ex = jnp.where(own, _dot(wm_ref[a], e), 0.0).astype(BF16)
            for j in range(S5_CHUNK):
                r0 = j * LANES + a * S5_H
                wm_blk[r0:r0 + S5_H, :] = ex[j * S5_H:(j + 1) * S5_H, :]
            for src, dst in ((wf_ref, wf_blk), (wb_ref, wb_blk)):
                ex = jnp.where(own, _dot(src[a], e), 0.0).astype(BF16)
                for part in range(2):
                    r0 = _s5_state_lane(a, part)
                    dst[r0:r0 + S5_P, :] = ex[part * S5_P:(part + 1) * S5_P, :]

    acc = _dot(lhs_ref[...], wm_blk[...])
    acc += _dot(xf_ref[...].astype(BF16), wf_blk[...])
    acc += _dot(xb_ref[...].astype(BF16), wb_blk[...])
    y2d = _as_rows(y_ref)
    for i in range(S5_CHUNK):
        piece = _lane_block(acc, i)
        if B == 1:
            y2d[pl.ds(i, ncc, stride=S5_CHUNK), :] = piece
        else:
            for cl in range(ncc):
                y2d[_s5_rows(i, cl, B, ncc), :] = piece[cl * B:(cl + 1) * B]


def _s5_out(lhs, xf, xb, ops, B, L, ncc):
    nch = L // S5_CHUNK
    tok = pl.BlockSpec((B, S5_CHUNK * ncc, LANES), lambda q, t: (0, t, q))
    st = pl.BlockSpec((ncc * B, S5_XW), lambda q, t: (t, q))
    return pl.pallas_call(
        functools.partial(_s5_out_kernel, B=B, ncc=ncc),
        grid=(S5_NB, nch // ncc),
        in_specs=[pl.BlockSpec((ncc * B, S5_KW), lambda q, t: (t, q)), st, st,
                  pl.BlockSpec((S5_GB, S5_UW, S5_UW), lambda q, t: (q, 0, 0)),
                  pl.BlockSpec((S5_GB, S5_SW, S5_UW), lambda q, t: (q, 0, 0)),
                  pl.BlockSpec((S5_GB, S5_SW, S5_UW), lambda q, t: (q, 0, 0)),
                  pl.BlockSpec((S5_UW, S5_KW), lambda q, t: (0, 0))],
        out_specs=tok,
        out_shape=jax.ShapeDtypeStruct((B, L, C_S5), F32),
        scratch_shapes=[pltpu.VMEM((S5_KW, S5_KW), BF16), pltpu.VMEM((S5_XW, S5_KW), BF16),
                        pltpu.VMEM((S5_XW, S5_KW), BF16)],
        compiler_params=_cparams("arbitrary", "arbitrary"),
        name="s5_out",
    )(lhs, xf, xb, ops["w_intra"], ops["w_xf"], ops["w_xb"], ops["spread"])


def _s5(u, B, seq, ops):
    nch = seq // S5_CHUNK
    u3 = u.reshape(B, seq, C_S5)
    ncc = min(nch, S5_STEP_ROWS // B)
    sf, sb, lhs = _s5_state(u3, ops["w_state"], ncc)
    lb = 1024 if B == 1 else 256
    xf, xb = _s5_scan(sf, sb, ops["lam16"], nch, B, lb)
    return _s5_out(lhs, xf, xb, ops, B, seq, ncc).reshape(B * seq, C_S5)


def _mix_kernel(x_ref, ya_ref, yb_ref, wglu_ref, bglu_ref, gh_ref, gs_ref, wout_ref, gffn_ref,
                wrh_ref, wrl_ref, br_ref, x1_ref, hf_ref, route_ref, cnt_ref, *, tm):
    g = jax.nn.gelu(yb_ref[...])
    yb = g * _sigmoid(_dot(g.astype(BF16), wglu_ref[...]) + bglu_ref[...])
    ya = jnp.concatenate([ya_ref[q] for q in range(Q_HY)], axis=1)
    na = _rms(ya, gh_ref[...]).astype(BF16)
    nb = _rms(yb, gs_ref[...]).astype(BF16)
    mixed = _dot(na, wout_ref[:C_HY, :]) + _dot(nb, wout_ref[C_HY:, :])
    x1 = x_ref[...] + mixed
    x1_ref[...] = x1
    hf = _rms(x1, gffn_ref[...])
    _store_token_tiles(hf_ref, hf)
    hf_hi = hf.astype(BF16)
    hf_lo = (hf - hf_hi.astype(F32)).astype(BF16)
    logits = (_dot(hf_hi, wrh_ref[...]) + _dot(hf_lo, wrh_ref[...]) + _dot(hf_hi, wrl_ref[...])
              + br_ref[...])
    l = logits.T[:N_EXPERTS]
    row = lax.broadcasted_iota(I32, (N_EXPERTS, tm), 0)
    neg = jnp.float32(-jnp.inf)
    vals, idxs = [], []
    for _ in range(TOP_K):
        m = jnp.max(l, axis=0, keepdims=True)
        idx = jnp.min(jnp.where(l == m, row, N_EXPERTS), axis=0, keepdims=True)
        vals.append(m)
        idxs.append(idx)
        l = jnp.where(row == idx, neg, l)
    es = [jnp.exp(v - vals[0]) for v in vals]
    den = es[0] + es[1] + es[2] + es[3]
    packed = jnp.concatenate([e / den for e in es] + [i.astype(F32) for i in idxs]
                             + [jnp.zeros((LANES - 2 * TOP_K, tm), F32)], axis=0)
    route_ref[...] = packed.T

    @pl.when(pl.program_id(0) == 0)
    def _():
        cnt_ref[...] = jnp.zeros_like(cnt_ref)

    chosen = jnp.zeros((N_EXPERTS, tm), F32)
    for idx in idxs:
        chosen += (row == idx).astype(F32)
    cnt_ref[...] += jnp.sum(chosen, axis=1, keepdims=True)


def _mix(x2d, ya4, ybp, w_glu_bf, b_glu, g_hyena, g_s5, w_out_bf, g_ffn, w_router_p, b_router_p, tm=512):
    T = x2d.shape[0]
    const = lambda i: (0, 0)
    row = lambda w: pl.BlockSpec((tm, w), lambda i: (i, 0))
    return pl.pallas_call(
        functools.partial(_mix_kernel, tm=tm),
        grid=(T // tm,),
        in_specs=[row(D_MODEL), pl.BlockSpec((Q_HY, tm, LANES), lambda i: (0, i, 0)), row(C_S5),
                  pl.BlockSpec((C_S5, C_S5), const), pl.BlockSpec((1, C_S5), const),
                  pl.BlockSpec((1, C_HY), const), pl.BlockSpec((1, C_S5), const),
                  pl.BlockSpec((D_MODEL, D_MODEL), const), pl.BlockSpec((1, D_MODEL), const),
                  pl.BlockSpec((D_MODEL, LANES), const), pl.BlockSpec((D_MODEL, LANES), const),
                  pl.BlockSpec((1, LANES), const)],
        out_specs=[row(D_MODEL), pl.BlockSpec((tm,) + TOKEN_TILE, lambda i: (i, 0, 0)), row(LANES),
                   pl.BlockSpec((N_EXPERTS, 1), const)],
        out_shape=[jax.ShapeDtypeStruct((T, D_MODEL), F32), jax.ShapeDtypeStruct((T,) + TOKEN_TILE, F32),
                   jax.ShapeDtypeStruct((T, LANES), F32), jax.ShapeDtypeStruct((N_EXPERTS, 1), F32)],
        compiler_params=_cparams("arbitrary"),
        name="mix",
    )(x2d, ya4, ybp, w_glu_bf, b_glu, g_hyena, g_s5, w_out_bf, g_ffn, w_router_p[0], w_router_p[1],
      b_router_p)


def _route_kernel(r_ref, cnt_ref, dest_ref, pst_ref, carry, pstart, *, tm):
    i = pl.program_id(0)
    ids_t = r_ref[...].T[TOP_K:2 * TOP_K].astype(I32)
    row = lax.broadcasted_iota(I32, (N_EXPERTS, tm), 0)
    chosen_t = jnp.zeros((N_EXPERTS, tm), F32)
    for k in range(TOP_K):
        chosen_t += (row == ids_t[k:k + 1]).astype(F32)
    oh = jnp.concatenate([chosen_t, jnp.zeros((LANES - N_EXPERTS, tm), F32)], axis=0).T

    @pl.when(i == 0)
    def _():
        padded = jnp.floor((cnt_ref[...] + (MOE_BLOCK - 1)) * (1.0 / MOE_BLOCK)) * MOE_BLOCK
        a = lax.broadcasted_iota(I32, (LANES, LANES), 0)
        b = lax.broadcasted_iota(I32, (LANES, LANES), 1)
        excl = jnp.dot(jnp.broadcast_to(padded, (SUBLANES, LANES)), (a < b).astype(F32),
                       precision=HIGHEST, preferred_element_type=F32)
        pstart[...] = excl[0:1]
        pst_ref[...] = excl[0:1]
        carry[...] = jnp.zeros_like(carry)

    a = lax.broadcasted_iota(I32, (tm, tm), 0)
    b = lax.broadcasted_iota(I32, (tm, tm), 1)
    before = _dot((b < a).astype(BF16), oh.astype(BF16))
    base = before + carry[...] + pstart[...]
    base_t = base.T[:N_EXPERTS]
    picked =[jnp.sum(jnp.where(row == ids_t[k:k + 1], base_t, 0.0), axis=0, keepdims=True)
              for k in range(TOP_K)]
    out_t = jnp.concatenate(picked + [jnp.zeros((LANES - TOP_K, tm), F32)], axis=0)
    dest_ref[...] = out_t.T.astype(I32)
    carry[...] += jnp.sum(oh, axis=0, keepdims=True)


def _route(route, cnt, tm=512):
    T = route.shape[0]
    return pl.pallas_call(
        functools.partial(_route_kernel, tm=tm),
        grid=(T // tm,),
        in_specs=[pl.BlockSpec((tm, LANES), lambda i: (i, 0)),
                  pl.BlockSpec((1, LANES), lambda i: (0, 0))],
        out_specs=[pl.BlockSpec((tm, LANES), lambda i: (i, 0)),
                   pl.BlockSpec((1, LANES), lambda i: (0, 0))],
        out_shape=[jax.ShapeDtypeStruct((T, LANES), I32), jax.ShapeDtypeStruct((1, LANES), F32)],
        scratch_shapes=[pltpu.VMEM((1, LANES), F32), pltpu.VMEM((1, LANES), F32)],
        compiler_params=_cparams("arbitrary"),
        name="route",
    )(route, cnt)


ZERO_ROWS = 64
DMA_UNROLL = 4


RING = 3


def _dispatch_kernel(padpos_ref, padcnt_ref, nu_ref, dest_ref, h_ref, xs_ref, zbuf, hbuf, sem, lsem, zsem,
                     *, td, nblk, nsteps):
    i = pl.program_id(0)

    @pl.when(i == 0)
    def _():
        zbuf[...] = jnp.zeros_like(zbuf)

        def per_expert(e, carry):
            off = padpos_ref[e]

            def zero_row(r):
                return pltpu.make_async_copy(zbuf.at[0], xs_ref.at[off + r], zsem)

            def z_issue(r, c):
                zero_row(r).start()
                return c

            def z_drain(r, c):
                zero_row(r).wait()
                return c

            lax.fori_loop(0, padcnt_ref[e], z_issue, 0)
            lax.fori_loop(0, padcnt_ref[e], z_drain, 0)
            return carry

        lax.fori_loop(0, N_EXPERTS, per_expert, 0)

        def zero_piece(j):
            row = pl.multiple_of(j * ZERO_ROWS, ZERO_ROWS)
            return pltpu.make_async_copy(zbuf, xs_ref.at[pl.ds(row, ZERO_ROWS)], zsem)

        per_blk = MOE_BLOCK // ZERO_ROWS

        def t_issue(j, c):
            zero_piece(j).start()
            return c

        def t_drain(j, c):
            zero_piece(j).wait()
            return c

        lax.fori_loop(nu_ref[0] * per_blk, nblk * per_blk, t_issue, 0)
        lax.fori_loop(nu_ref[0] * per_blk, nblk * per_blk, t_drain, 0)

    def load(step):
        s = lax.rem(step, RING)
        return pltpu.make_async_copy(h_ref.at[pl.ds(step * td, td)], hbuf.at[s], lsem.at[s])

    def wait_rows(step):
        s = lax.rem(step, RING)
        for _ in range(TOP_K):
            pltpu.make_async_copy(hbuf.at[s], xs_ref.at[pl.ds(0, td)], sem.at[s]).wait()

    @pl.when(i == 0)
    def _():
        load(0).start()
        if nsteps > 1:
            load(1).start()

    load(i).wait()
    slot = lax.rem(i, RING)

    def issue(r, carry):
        for k in range(TOP_K):
            pltpu.make_async_copy(hbuf.at[slot, r], xs_ref.at[dest_ref[r * TOP_K + k]],
                                  sem.at[slot]).start(priority=k % 2)
        return carry

    lax.fori_loop(0, td, issue, 0, unroll=DMA_UNROLL)

    @pl.when(i > 0)
    def _():
        wait_rows(i - 1)

    @pl.when(i + 2 < nsteps)
    def _():
        load(i + 2).start()

    @pl.when(i == nsteps - 1)
    def _():
        wait_rows(i)


def _dispatch(hf, dest_flat, padpos, padcnt, n_used, n_slots, td=512):
    T = hf.shape[0]
    nsteps = T // td
    grid_spec = pltpu.PrefetchScalarGridSpec(
        num_scalar_prefetch=3,
        grid=(nsteps,),
        in_specs=[pl.BlockSpec((td * TOP_K,), lambda i, *_: (i,), memory_space=pltpu.SMEM),
                  pl.BlockSpec(memory_space=pl.ANY)],
        out_specs=pl.BlockSpec(memory_space=pl.ANY),
        scratch_shapes=[pltpu.VMEM((ZERO_ROWS,) + TOKEN_TILE, F32), pltpu.VMEM((RING, td) + TOKEN_TILE, F32),
                        pltpu.SemaphoreType.DMA((RING,)), pltpu.SemaphoreType.DMA((RING,)),
                        pltpu.SemaphoreType.DMA(())],
    )
    return pl.pallas_call(
        functools.partial(_dispatch_kernel, td=td, nblk=n_slots // MOE_BLOCK, nsteps=nsteps),
        grid_spec=grid_spec,
        out_shape=jax.ShapeDtypeStruct((n_slots,) + TOKEN_TILE, F32),
        compiler_params=_cparams("arbitrary"),
        name="dispatch",
    )(padpos, padcnt, n_used, dest_flat, hf)


def _moe_kernel(be_ref, nu_ref, x_ref, wg_ref, bg_ref, wu_ref, bu_ref, wd_ref, bd_ref, o_ref, wbf):
    i = pl.program_id(0)
    last = nu_ref[0] - 1
    used = i <= last
    expert = be_ref[jnp.minimum(i, last)]
    new_expert = (i == 0) | (expert != be_ref[jnp.minimum(jnp.maximum(i - 1, 0), last)])

    @pl.when(jnp.logical_not(used))
    def _():
        o_ref[...] = jnp.zeros_like(o_ref)

    @pl.when(used & new_expert)
    def _():
        for k, w_ref in enumerate((wg_ref, wu_ref, wd_ref)):
            wbf[k] = w_ref[...].astype(BF16)

    @pl.when(used)
    def _():
        x = _load_token_tiles(x_ref, MOE_BLOCK).astype(BF16)
        gt = jnp.minimum(_dot(x, wbf[0]) + bg_ref[...], SWIGLU_LIMIT)
        up = jnp.clip(_dot(x, wbf[1]) + bu_ref[...], -SWIGLU_LIMIT, SWIGLU_LIMIT)
        act = (up + 1.0) * (gt * _sigmoid(SWIGLU_ALPHA * gt))
        _store_token_tiles(o_ref, _dot(act.astype(BF16), wbf[2]) + bd_ref[...])


def _moe(xs, block_expert, n_used, wg, bg, wu, bu, wd, bd):
    n_slots = xs.shape[0]
    nblk = n_slots // MOE_BLOCK
    blk = lambda i, be, nu: (jnp.minimum(i, nu[0] - 1), 0, 0)
    exp = lambda i, be, nu: (be[jnp.minimum(i, nu[0] - 1)], 0, 0)
    wspec = pl.BlockSpec((None, D_MODEL, D_MODEL), exp)
    bspec = pl.BlockSpec((None, 1, D_MODEL), exp)
    grid_spec = pltpu.PrefetchScalarGridSpec(
        num_scalar_prefetch=2,
        grid=(nblk,),
        in_specs=[pl.BlockSpec((MOE_BLOCK,) + TOKEN_TILE, blk), wspec, bspec, wspec, bspec, wspec, bspec],
        out_specs=pl.BlockSpec((MOE_BLOCK,) + TOKEN_TILE, lambda i, be, nu: (i, 0, 0)),
        scratch_shapes=[pltpu.VMEM((3, D_MODEL, D_MODEL), BF16)],
    )
    return pl.pallas_call(
        _moe_kernel,
        grid_spec=grid_spec,
        out_shape=jax.ShapeDtypeStruct((n_slots,) + TOKEN_TILE, F32),
        compiler_params=_cparams("arbitrary"),
        name="moe",
    )(block_expert, n_used, xs, wg, bg, wu, bu, wd, bd)


def _combine_kernel(dcur_ref, dnext_ref, x1_ref, r_ref, g_ref, ys_ref, o_ref, buf, sem, *, tc, nsteps):
    i = pl.program_id(0)
    slot = lax.rem(i, 2)

    def gather(d_ref, s):
        def body(r, carry):
            for k in range(TOP_K):
                pltpu.make_async_copy(ys_ref.at[d_ref[r * TOP_K + k]], buf.at[s, k, r],
                                      sem.at[s]).start(priority=k % 2)
            return carry

        lax.fori_loop(0, tc, body, 0, unroll=DMA_UNROLL)

    @pl.when(i == 0)
    def _():
        gather(dcur_ref, slot)

    @pl.when(i + 1 < nsteps)
    def _():
        gather(dnext_ref, 1 - slot)

    for k in range(TOP_K):
        pltpu.make_async_copy(ys_ref.at[pl.ds(0, tc)], buf.at[slot, k], sem.at[slot]).wait()
    gates = r_ref[...]
    acc = x1_ref[...]
    for k in range(TOP_K):
        acc += gates[:, k:k + 1] * _load_token_tiles(buf, tc, (slot * TOP_K + k) * tc)
    o_ref[...] = _rms(acc, g_ref[...])


def _combine(x1, route, dest_flat, ys, g_final, tc=512):
    T = x1.shape[0]
    nsteps = T // tc
    return pl.pallas_call(
        functools.partial(_combine_kernel, tc=tc, nsteps=nsteps),
        grid=(nsteps,),
        in_specs=[pl.BlockSpec((tc * TOP_K,), lambda i: (i,), memory_space=pltpu.SMEM),
                  pl.BlockSpec((tc * TOP_K,), lambda i: (jnp.minimum(i + 1, nsteps - 1),),
                               memory_space=pltpu.SMEM),
                  pl.BlockSpec((tc, D_MODEL), lambda i: (i, 0)),
                  pl.BlockSpec((tc, LANES), lambda i: (i, 0)),
                  pl.BlockSpec((1, D_MODEL), lambda i: (0, 0)),
                  pl.BlockSpec(memory_space=pl.ANY)],
        out_specs=pl.BlockSpec((tc, D_MODEL), lambda i: (i, 0)),
        out_shape=jax.ShapeDtypeStruct((T, D_MODEL), F32),
        scratch_shapes=[pltpu.VMEM((2, TOP_K, tc) + TOKEN_TILE, F32), pltpu.SemaphoreType.DMA((2,))],
        compiler_params=_cparams("arbitrary"),
        name="combine",
    )(dest_flat, dest_flat, x1, route, g_final, ys)


def _moe_layer(x1, hf, route, cnt, ew, g_final):
    T = x1.shape[0]
    n_assign = T * TOP_K
    nblk = n_assign // MOE_BLOCK + N_EXPERTS
    n_slots = nblk * MOE_BLOCK
    dest, pst = _route(route, jnp.pad(cnt[:, 0], (0, LANES - N_EXPERTS))[None, :])
    cnt_i = cnt[:, 0].astype(I32)
    pstart = pst[0, :N_EXPERTS].astype(I32)
    padded = (cnt_i + MOE_BLOCK - 1) // MOE_BLOCK * MOE_BLOCK
    pend = pstart + padded
    block_expert = jnp.minimum(
        jnp.sum(jnp.arange(nblk, dtype=I32)[:, None] * MOE_BLOCK >= pend[None, :], axis=1),
        N_EXPERTS - 1).astype(I32)
    n_used = (pend[-1] // MOE_BLOCK).reshape(1).astype(I32)
    dest_flat = dest[:, :TOP_K].reshape(n_assign)
    xs = _dispatch(hf, dest_flat, pstart + cnt_i, padded - cnt_i, n_used, n_slots)
    ys = _moe(xs, block_expert, n_used, *ew)
    return _combine(x1, route, dest_flat, ys, g_final)


def _trunk(x, p, bb):
    B, L, D = x.shape
    T = B * L
    x2d = x.reshape(T, D)
    v, x1g, x2g, u = _inproj(x2d, L, p["g_mix"], p["w_in"], p["b_in"], p["short_w"], p["short_b"])
    ya = _hyena(v, x1g, x2g, B, L, p["filt"], p["filt_bias"], bb)
    ybp = _s5(u, B, L, p["s5"])
    x1, hf, route, cnt = _mix(x2d, ya, ybp, p["w_glu"], p["b_glu"], p["g_hyena"], p["g_s5"], p["w_out"],
                              p["g_ffn"], p["w_router"], p["b_router"])
    y = _moe_layer(x1, hf, route, cnt, p["experts"], p["g_final"])
    return y.reshape(B, L, D)


def kernel(x_prompt, x_sample, g_mix, w_in, b_in, short_w, short_b, filt_w1, filt_b1, filt_w2, filt_b2, filt_w3, filt_b3, filt_freq, filt_bias, s5_lam_re, s5_lam_im, s5_log_dt, s5_b_re, s5_b_im, s5_c_re, s5_c_im, s5_d, w_glu, b_glu, g_hyena, g_s5, w_out, g_ffn, w_router, b_router, w_gate, b_gate, w_up, b_up, w_down, b_down, g_final):
    assert g_mix.shape[0] == 1, "one encoder layer"
    row = lambda a: a[0][None, :].astype(F32)
    p = dict(
        g_mix=row(g_mix), w_in=w_in[0].astype(BF16), b_in=row(b_in),
        short_w=short_w[0].astype(F32), short_b=row(short_b),
        filt=(filt_w1[0], filt_b1[0], filt_w2[0], filt_b2[0], filt_w3[0], filt_b3[0], filt_freq[0]),
        filt_bias=filt_bias[0].astype(F32),
        s5=_s5_operators(s5_lam_re[0], s5_lam_im[0], s5_log_dt[0], s5_b_re[0], s5_b_im[0],
                         s5_c_re[0], s5_c_im[0], s5_d[0]),
        w_glu=w_glu[0].astype(BF16), b_glu=row(b_glu), g_hyena=row(g_hyena), g_s5=row(g_s5),
        w_out=w_out[0].astype(BF16), g_ffn=row(g_ffn),
        w_router=_split_bf16(jnp.pad(w_router[0].astype(F32), ((0, 0), (0, LANES - N_EXPERTS)))),
        b_router=jnp.pad(b_router[0].astype(F32), (0, LANES - N_EXPERTS))[None, :],
        experts=(w_gate[0].astype(F32), b_gate[0][:, None, :], w_up[0].astype(F32),
                 b_up[0][:, None, :], w_down[0].astype(F32), b_down[0][:, None, :]),
        g_final=g_final[None, :].astype(F32),
    )
    y_prompt = _trunk(x_prompt, p, bb=1)
    y_sample = _trunk(x_sample, p, bb=8)
    return (y_prompt, y_sample)
```
